```python
import math
import numpy as np
import jax
import jax.numpy as jnp
from jax import lax

D_MODEL = 2048
BATCH = 1
SEQ = 8192
DEPTH = 1
DEC_BATCH = 128
DEC_SEQ = 4
PAST_LEN = 2048
PAGE_SIZE = 128

HEAD_DIM = 128
NSA_HEADS = (D_MODEL // 2) // HEAD_DIM
NSA_KV_HEADS = max(1, NSA_HEADS // 4)
NSA_GROUP = NSA_HEADS // NSA_KV_HEADS
CMP_BLOCK = 32
CMP_STRIDE = CMP_BLOCK // 2
CMP_HIDDEN = HEAD_DIM
SEL_BLOCK = 64
SEL_TOPN = 16
WINDOW = 512
Q_BLOCK = 128
FORCED_SCORE = 1.0e4
GLA_HEADS = 4
GLA_DV = (D_MODEL // 2) // GLA_HEADS
GLA_DK = GLA_DV // 2
GLA_RANK = 16
GLA_TAU = 16.0
GLA_CHUNK = 32
MIX_WIDTH = NSA_HEADS * HEAD_DIM + GLA_HEADS * GLA_DV
MEM_TOKENS = 256
MEM_HEADS = 4
MEM_HEAD_DIM = 128
D_FF = ((8 * D_MODEL // 3 + 255) // 256) * 256
NUM_BUCKETS = 32
MAX_DISTANCE = 128
RMS_EPS = 1e-6
IN_SPLITS = (NSA_HEADS * HEAD_DIM, 6 * NSA_KV_HEADS * HEAD_DIM, 3 * NSA_HEADS,
             GLA_HEADS * GLA_DK, GLA_HEADS * GLA_DK, GLA_HEADS * GLA_DV, GLA_HEADS * GLA_DV, GLA_RANK)
D_IN = sum(IN_SPLITS)

kernel_name = 'hybrid_nsa_gla_macaron_step'


def rmsnorm(x, g):
    x32 = x.astype(jnp.float32)
    y = x32 * lax.rsqrt(jnp.mean(x32 * x32, axis=-1, keepdims=True) + RMS_EPS)
    return (y * g.astype(jnp.float32)).astype(x.dtype)


def swiglu(x, w_gate, w_up, w_down):
    return (jax.nn.silu(x @ w_gate) * (x @ w_up)) @ w_down


def masked_softmax(s, mask):
    s = jnp.where(mask, s.astype(jnp.float32), -jnp.inf)
    m = jnp.max(s, axis=-1, keepdims=True)
    m = jnp.where(jnp.isfinite(m), m, 0.0)
    e = jnp.exp(s - m)
    return e / jnp.maximum(jnp.sum(e, axis=-1, keepdims=True), 1e-30)


def rel_bucket(dist):
    n = jnp.maximum(dist, 0)
    exact = NUM_BUCKETS // 2
    nf = jnp.maximum(n, 1).astype(jnp.float32)
    large = exact + (jnp.log(nf / exact) / math.log(MAX_DISTANCE / exact)
                     * (NUM_BUCKETS - exact)).astype(jnp.int32)
    return jnp.where(n < exact, n, jnp.minimum(large, NUM_BUCKETS - 1))


def compress_rows(rows, pos, w1, w2):
    b, t = rows.shape[:2]
    n_chunk = t // CMP_STRIDE
    ch = rows[:, :n_chunk * CMP_STRIDE].reshape(b, n_chunk, CMP_STRIDE, NSA_KV_HEADS, HEAD_DIM)
    first = jnp.einsum('bcshd,sdf->bchf', ch + pos[:CMP_STRIDE, None, :], w1[:CMP_STRIDE])
    second = jnp.einsum('bcshd,sdf->bchf', ch + pos[CMP_STRIDE:, None, :], w1[CMP_STRIDE:])
    hid = jax.nn.silu(first[:, :-1] + second[:, 1:])
    return jnp.einsum('bchf,fd->bchd', hid, w2)


def cover_matrix(n_cmp, n_sel):
    i = jnp.arange(n_cmp)[:, None] * CMP_STRIDE
    j = jnp.arange(n_sel)[None, :] * SEL_BLOCK
    shared = jnp.minimum(i + CMP_BLOCK, j + SEL_BLOCK) - jnp.maximum(i, j)
    return jnp.maximum(shared, 0).astype(jnp.float32) / CMP_STRIDE


def nsa_mixer(q, k_cmp, v_cmp, k_sel, v_sel, k_win, v_win, gates, pos0, n_buf, p):
    b, l = q.shape[:2]
    t = k_cmp.shape[1]
    kc = compress_rows(k_cmp, p['cmp_pos_k'], p['cmp_w1_k'], p['cmp_w2_k'])
    vc = compress_rows(v_cmp, p['cmp_pos_v'], p['cmp_w1_v'], p['cmp_w2_v'])
    n_cmp = kc.shape[1]
    cmp_end = jnp.arange(n_cmp) * CMP_STRIDE + CMP_BLOCK - 1
    n_sel = -(-t // SEL_BLOCK)
    pad = n_sel * SEL_BLOCK - t

    def to_blocks(r):
        r = jnp.pad(r, ((0, 0), (0, pad), (0, 0), (0, 0)))
        return r.reshape(b, n_sel, SEL_BLOCK, NSA_KV_HEADS, HEAD_DIM).transpose(0, 3, 1, 2, 4)

    ks_blk, vs_blk = to_blocks(k_sel), to_blocks(v_sel)
    cover = cover_matrix(n_cmp, n_sel)
    top_n = min(SEL_TOPN, n_sel)
    tab = p['rel_bias'].astype(jnp.float32).reshape(NUM_BUCKETS, NSA_KV_HEADS, NSA_GROUP)
    kw_pad = jnp.pad(k_win, ((0, 0), (WINDOW, 0), (0, 0), (0, 0)))
    vw_pad = jnp.pad(v_win, ((0, 0), (WINDOW, 0), (0, 0), (0, 0)))
    qb_len = min(Q_BLOCK, l)
    n_qb = l // qb_len
    q_blocks = q.reshape(b, n_qb, qb_len, NSA_KV_HEADS, NSA_GROUP, HEAD_DIM).swapaxes(0, 1)
    g_blocks = gates.reshape(b, n_qb, qb_len, NSA_KV_HEADS, NSA_GROUP, 3).swapaxes(0, 1)
    take = jax.vmap(jax.vmap(lambda rows, ids: rows[ids]))

    def block(args):
        qb, gb, i = args
        start = i * qb_len
        posb = pos0 + start + jnp.arange(qb_len)
        dist_c = posb[:, None] - cmp_end[None, :]
        s = jnp.einsum('bqhgd,bchd->bhgqc', qb, kc).astype(jnp.float32)
        s = s + tab[rel_bucket(dist_c)].transpose(2, 3, 0, 1)
        p_cmp = masked_softmax(s, dist_c >= 0)
        o_cmp = jnp.einsum('bhgqc,bchd->bqhgd', p_cmp.astype(vc.dtype), vc)
        imp = jnp.einsum('bhqc,cj->bhqj', p_cmp.sum(axis=2), cover)
        blk = jnp.arange(n_sel)
        cur = posb[:, None] // SEL_BLOCK
        forced = (blk == 0) | (blk == cur) | (blk == cur - 1)
        valid = blk * SEL_BLOCK <= posb[:, None]
        score = jnp.where(valid, jnp.where(forced, FORCED_SCORE, imp), -jnp.inf)
        _, idx = lax.top_k(score, top_n)
        gk, gv = take(ks_blk, idx), take(vs_blk, idx)
        pos_sel = idx[..., None] * SEL_BLOCK + jnp.arange(SEL_BLOCK)
        dist_s = posb[:, None, None] - pos_sel
        bias_s = jax.vmap(lambda tb, bk: tb[bk], in_axes=(1, 1), out_axes=1)(tab, rel_bucket(dist_s))
        s = jnp.einsum('bqhgd,bhqnsd->bhgqns', qb, gk).astype(jnp.float32) + jnp.moveaxis(bias_s, -1, 2)
        shp = s.shape
        p_sel = masked_softmax(s.reshape(shp[0], shp[1], shp[2], shp[3], -1),
                               (dist_s >= 0).reshape(shp[0], shp[1], 1, shp[3], -1)).reshape(shp)
        o_sel = jnp.einsum('bhgqns,bhqnsd->bqhgd', p_sel.astype(gv.dtype), gv)
        off = start + n_buf
        kw = lax.dynamic_slice_in_dim(kw_pad, off, WINDOW + qb_len, axis=1)
        vw = lax.dynamic_slice_in_dim(vw_pad, off, WINDOW + qb_len, axis=1)
        pos_w = posb[0] - WINDOW + jnp.arange(WINDOW + qb_len)
        dist_w = posb[:, None] - pos_w[None, :]
        mask_w = (dist_w >= 0) & (dist_w < WINDOW) & (pos_w[None, :] >= 0)
        s = jnp.einsum('bqhgd,bwhd->bhgqw', qb, kw).astype(jnp.float32)
        s = s + tab[rel_bucket(dist_w)].transpose(2, 3, 0, 1)
        p_win = masked_softmax(s, mask_w)
        o_win = jnp.einsum('bhgqw,bwhd->bqhgd', p_win.astype(vw.dtype), vw)
        return gb[..., 0:1] * o_cmp + gb[..., 1:2] * o_sel + gb[..., 2:3] * o_win

    o = lax.map(block, (q_blocks, g_blocks, jnp.arange(n_qb)))
    return o.swapaxes(0, 1).reshape(b, l, NSA_HEADS * HEAD_DIM)


def gla_recurrence(q, k, v, log_a, s0):
    b, l, h, _ = q.shape
    c = GLA_CHUNK if l % GLA_CHUNK == 0 else l
    n = l // c

    def chunks(t):
        return t.astype(jnp.float32).reshape(b, n, c, h, t.shape[-1]).transpose(1, 0, 3, 2, 4)

    tril = jnp.tril(jnp.ones((c, c), dtype=bool))

    def step(state, inp):
        qc, kc, vc, gc = inp
        cum = jnp.cumsum(gc, axis=2)
        last = cum[:, :, -1:, :]
        qe = qc * jnp.exp(cum)
        att = jnp.where(tril, jnp.einsum('bhid,bhjd->bhij', qe, kc * jnp.exp(-cum)), 0.0)
        o = jnp.einsum('bhid,bhdv->bhiv', qe, state) + jnp.einsum('bhij,bhjv->bhiv', att, vc)
        state = (jnp.exp(last[:, :, 0, :])[..., None] * state
                 + jnp.einsum('bhjd,bhjv->bhdv', kc * jnp.exp(last - cum), vc))
        return state, o

    s_fin, o = lax.scan(step, s0.astype(jnp.float32),
                        (chunks(q * GLA_DK ** -0.5), chunks(k), chunks(v), chunks(log_a)))
    return o.transpose(1, 0, 3, 2, 4).reshape(b, l, h, v.shape[-1]), s_fin


def memory_kv(mem, norm_g, w_k, w_v):
    m = rmsnorm(mem, norm_g)
    b = mem.shape[0]
    return ((m @ w_k).reshape(b, -1, MEM_HEADS, MEM_HEAD_DIM),
            (m @ w_v).reshape(b, -1, MEM_HEADS, MEM_HEAD_DIM))


def gather_pages(pool, page_table):
    rows = pool[page_table]
    return rows.reshape(page_table.shape[0], -1, *pool.shape[2:])


def decoder_layer(x, past, win_buf, gla_state, mem_k, mem_v, p):
    b, l, _ = x.shape
    pos0 = past[0].shape[1]
    n_buf = win_buf[0].shape[1]
    h = x + 0.5 * swiglu(rmsnorm(x, p['norm_ffn1']), p['ffn1_w_gate'], p['ffn1_w_up'], p['ffn1_w_down'])
    z = rmsnorm(h, p['norm_mix']) @ p['w_in']
    q_nsa, kv_nsa, g_nsa, q_gla, k_gla, v_gla, r_gla, a_gla = jnp.split(
        z, np.cumsum(IN_SPLITS)[:-1].tolist(), axis=-1)
    kv = kv_nsa.reshape(b, l, 6, NSA_KV_HEADS, HEAD_DIM)
    new_rows = tuple(kv[:, :, j] for j in range(6))
    full = tuple(jnp.concatenate([r_past, r_new], axis=1) for r_past, r_new in zip(past, new_rows[:4]))
    k_win = jnp.concatenate([win_buf[0], new_rows[4]], axis=1)
    v_win = jnp.concatenate([win_buf[1], new_rows[5]], axis=1)
    q_nsa = q_nsa.reshape(b, l, NSA_KV_HEADS, NSA_GROUP, HEAD_DIM) * HEAD_DIM ** -0.5
    gates = jax.nn.sigmoid(g_nsa.reshape(b, l, NSA_KV_HEADS, NSA_GROUP, 3))
    o_nsa = nsa_mixer(q_nsa, full[0], full[1], full[2], full[3], k_win, v_win, gates, pos0, n_buf, p)
    log_a = jax.nn.log_sigmoid((a_gla @ p['gla_w_a2'] + p['gla_b_a']).astype(jnp.float32)) / GLA_TAU
    o_gla, gla_new = gla_recurrence(q_gla.reshape(b, l, GLA_HEADS, GLA_DK), k_gla.reshape(b, l, GLA_HEADS, GLA_DK),
                                    v_gla.reshape(b, l, GLA_HEADS, GLA_DV), log_a.reshape(b, l, GLA_HEADS, GLA_DK),
                                    gla_state)
    o_gla = rmsnorm(o_gla, p['gla_norm']).astype(x.dtype) * jax.nn.silu(r_gla.reshape(b, l, GLA_HEADS, GLA_DV))
    h = h + jnp.concatenate([o_nsa, o_gla.reshape(b, l, -1)], axis=-1) @ p['w_out']
    qm = (rmsnorm(h, p['norm_mem']) @ p['w_mem_q']).reshape(b, l, MEM_HEADS, MEM_HEAD_DIM) * MEM_HEAD_DIM ** -0.5
    pm = jax.nn.softmax(jnp.einsum('bqhd,bmhd->bhqm', qm, mem_k).astype(jnp.float32), axis=-1)
    om = jnp.einsum('bhqm,bmhd->bqhd', pm.astype(mem_v.dtype), mem_v).reshape(b, l, -1)
    h = h + om @ p['w_mem_o']
    h = h + 0.5 * swiglu(rmsnorm(h, p['norm_ffn2']), p['ffn2_w_gate'], p['ffn2_w_up'], p['ffn2_w_down'])
    keep = n_buf if n_buf > 0 else min(WINDOW, l)
    return h, (new_rows[0], new_rows[1], new_rows[2], new_rows[3],
               k_win[:, -keep:], v_win[:, -keep:], gla_new)


def setup_inputs(seed: int = 0) -> dict:
    key = jax.random.key(seed)
    ks = iter(jax.random.split(key, 64))

    def nrm(shape, scale):
        return jax.random.normal(next(ks), shape, jnp.float32) * scale

    def gain(shape):
        return 1.0 + 0.05 * jax.random.normal(next(ks), shape, jnp.float32)

    n_pages = PAST_LEN // PAGE_SIZE
    n_pool = (5 * DEC_BATCH * n_pages) // 4
    w_buf = min(WINDOW, PAST_LEN)
    pool_shape = (DEPTH, n_pool, PAGE_SIZE, NSA_KV_HEADS, HEAD_DIM)
    win_shape = (DEPTH, DEC_BATCH, w_buf, NSA_KV_HEADS, HEAD_DIM)
    mem_shape = (DEPTH, DEC_BATCH, MEM_TOKENS, MEM_HEADS, MEM_HEAD_DIM)
    perm = jax.random.permutation(next(ks), n_pool)
    page_table = perm[:DEC_BATCH * n_pages].reshape(DEC_BATCH, n_pages).astype(jnp.int32)
    mem_w = MEM_HEADS * MEM_HEAD_DIM
    return {
        'x_prompt': nrm((BATCH, SEQ, D_MODEL), 1.0),
        'x_sample': nrm((DEC_BATCH, DEC_SEQ, D_MODEL), 1.0),
        'mem_prompt': nrm((BATCH, MEM_TOKENS, D_MODEL), 1.0),
        'cache_k_cmp': nrm(pool_shape, 1.0),
        'cache_v_cmp': nrm(pool_shape, 1.0),
        'cache_k_sel': nrm(pool_shape, 1.0),
        'cache_v_sel': nrm(pool_shape, 1.0),
        'cache_k_win': nrm(win_shape, 1.0),
        'cache_v_win': nrm(win_shape, 1.0),
        'state_gla': nrm((DEPTH, DEC_BATCH, GLA_HEADS, GLA_DK, GLA_DV), 0.3),
        'cache_k_mem': nrm(mem_shape, 1.0),
        'cache_v_mem': nrm(mem_shape, 1.0),
        'page_table': page_table,
        'norm_ffn1': gain((DEPTH, D_MODEL)),
        'ffn1_w_gate': nrm((DEPTH, D_MODEL, D_FF), D_MODEL ** -0.5),
        'ffn1_w_up': nrm((DEPTH, D_MODEL, D_FF), D_MODEL ** -0.5),
        'ffn1_w_down': nrm((DEPTH, D_FF, D_MODEL), D_FF ** -0.5),
        'norm_mix': gain((DEPTH, D_MODEL)),
        'w_in': nrm((DEPTH, D_MODEL, D_IN), D_MODEL ** -0.5),
        'w_out': nrm((DEPTH, MIX_WIDTH, D_MODEL), MIX_WIDTH ** -0.5),
        'cmp_pos_k': nrm((DEPTH, CMP_BLOCK, HEAD_DIM), 0.1),
        'cmp_w1_k': nrm((DEPTH, CMP_BLOCK, HEAD_DIM, CMP_HIDDEN), (CMP_BLOCK * HEAD_DIM) ** -0.5),
        'cmp_w2_k': nrm((DEPTH, CMP_HIDDEN, HEAD_DIM), CMP_HIDDEN ** -0.5),
        'cmp_pos_v': nrm((DEPTH, CMP_BLOCK, HEAD_DIM), 0.1),
        'cmp_w1_v': nrm((DEPTH, CMP_BLOCK, HEAD_DIM, CMP_HIDDEN), (CMP_BLOCK * HEAD_DIM) ** -0.5),
        'cmp_w2_v': nrm((DEPTH, CMP_HIDDEN, HEAD_DIM), CMP_HIDDEN ** -0.5),
        'rel_bias': nrm((NUM_BUCKETS, NSA_HEADS), 0.5),
        'gla_w_a2': nrm((DEPTH, GLA_RANK, GLA_HEADS * GLA_DK), GLA_RANK ** -0.5),
        'gla_b_a': nrm((DEPTH, GLA_HEADS * GLA_DK), 0.1),
        'gla_norm': gain((DEPTH, GLA_DV)),
        'norm_mem': gain((DEPTH, D_MODEL)),
        'norm_mem_src': gain((DEPTH, D_MODEL)),
        'w_mem_q': nrm((DEPTH, D_MODEL, mem_w), D_MODEL ** -0.5),
        'w_mem_k': nrm((DEPTH, D_MODEL, mem_w), D_MODEL ** -0.5),
        'w_mem_v': nrm((DEPTH, D_MODEL, mem_w), D_MODEL ** -0.5),
        'w_mem_o': nrm((DEPTH, mem_w, D_MODEL), mem_w ** -0.5),
        'norm_ffn2': gain((DEPTH, D_MODEL)),
        'ffn2_w_gate': nrm((DEPTH, D_MODEL, D_FF), D_MODEL ** -0.5),
        'ffn2_w_up': nrm((DEPTH, D_MODEL, D_FF), D_MODEL ** -0.5),
        'ffn2_w_down': nrm((DEPTH, D_FF, D_MODEL), D_FF ** -0.5),
        'norm_final': gain((D_MODEL,)),
    }


def reference(x_prompt, x_sample, mem_prompt, cache_k_cmp, cache_v_cmp, cache_k_sel, cache_v_sel,
              cache_k_win, cache_v_win, state_gla, cache_k_mem, cache_v_mem, page_table,
              norm_ffn1, ffn1_w_gate, ffn1_w_up, ffn1_w_down, norm_mix, w_in, w_out,
              cmp_pos_k, cmp_w1_k, cmp_w2_k, cmp_pos_v, cmp_w1_v, cmp_w2_v, rel_bias,
              gla_w_a2, gla_b_a, gla_norm, norm_mem, norm_mem_src, w_mem_q, w_mem_k, w_mem_v, w_mem_o,
              norm_ffn2, ffn2_w_gate, ffn2_w_up, ffn2_w_down, norm_final):
    bp = x_prompt.shape[0]
    empty = jnp.zeros((bp, 0, NSA_KV_HEADS, HEAD_DIM), x_prompt.dtype)
    gla0 = jnp.zeros((bp, GLA_HEADS, GLA_DK, GLA_DV), jnp.float32)
    h_p, h_s = x_prompt, x_sample
    st_p, st_s = [], []
    for layer in range(DEPTH):
        p = dict(norm_ffn1=norm_ffn1[layer], ffn1_w_gate=ffn1_w_gate[layer], ffn1_w_up=ffn1_w_up[layer],
                 ffn1_w_down=ffn1_w_down[layer], norm_mix=norm_mix[layer], w_in=w_in[layer], w_out=w_out[layer],
                 cmp_pos_k=cmp_pos_k[layer], cmp_w1_k=cmp_w1_k[layer], cmp_w2_k=cmp_w2_k[layer],
                 cmp_pos_v=cmp_pos_v[layer], cmp_w1_v=cmp_w1_v[layer], cmp_w2_v=cmp_w2_v[layer],
                 rel_bias=rel_bias, gla_w_a2=gla_w_a2[layer], gla_b_a=gla_b_a[layer], gla_norm=gla_norm[layer],
                 norm_mem=norm_mem[layer], w_mem_q=w_mem_q[layer], w_mem_o=w_mem_o[layer],
                 norm_ffn2=norm_ffn2[layer], ffn2_w_gate=ffn2_w_gate[layer], ffn2_w_up=ffn2_w_up[layer],
                 ffn2_w_down=ffn2_w_down[layer])
        mem_k_p, mem_v_p = memory_kv(mem_prompt, norm_mem_src[layer], w_mem_k[layer], w_mem_v[layer])
        h_p, new_p = decoder_layer(h_p, (empty, empty, empty, empty), (empty, empty), gla0, mem_k_p, mem_v_p, p)
        st_p.append(new_p + (mem_k_p, mem_v_p))
        past = tuple(gather_pages(c[layer], page_table) for c in (cache_k_cmp, cache_v_cmp, cache_k_sel, cache_v_sel))
        h_s, new_s = decoder_layer(h_s, past, (cache_k_win[layer], cache_v_win[layer]), state_gla[layer],
                                   cache_k_mem[layer], cache_v_mem[layer], p)
        st_s.append(new_s)
    sp = [jnp.stack(a) for a in zip(*st_p)]
    ss = [jnp.stack(a) for a in zip(*st_s)]
    y_prompt = rmsnorm(h_p, norm_final)
    y_sample = rmsnorm(h_s, norm_final)
    return (y_prompt, y_sample, sp[0], sp[1], sp[2], sp[3], sp[4], sp[5], sp[6], sp[7], sp[8],
            ss[0], ss[1], ss[2], ss[3], ss[4], ss[5], ss[6])
```

```python
import functools
import math

import numpy as np
import jax
import jax.numpy as jnp
from jax import lax
from jax.experimental import pallas as pl
from jax.experimental.pallas import tpu as pltpu

F32 = jnp.float32
BF16 = jnp.bfloat16

D_MODEL = 2048
SEQ = 8192
DEC_BATCH = 128
DEC_SEQ = 4
PAST_LEN = 2048
PAGE_SIZE = 128
N_PAGES = PAST_LEN // PAGE_SIZE
HEAD_DIM = 128
NSA_HEADS = 8
NSA_KV_HEADS = 2
NSA_GROUP = 4
CMP_BLOCK = 32
CMP_STRIDE = 16
SEL_BLOCK = 64
SEL_TOPN = 16
WINDOW = 512
FORCED_SCORE = 1.0e4
GLA_HEADS = 4
GLA_DV = 256
GLA_DK = 128
GLA_RANK = 16
GLA_TAU = 16.0
GLA_CHUNK = 32
MEM_TOKENS = 256
MEM_HEADS = 4
MEM_HEAD_DIM = 128
D_FF = 5632
NUM_BUCKETS = 32
MAX_DISTANCE = 128
RMS_EPS = 1e-6

N_TOK = SEQ + DEC_BATCH * DEC_SEQ
Z_QN, Z_VG, Z_RG, Z_KV, Z_QG, Z_KG, Z_MISC = 0, 1024, 2048, 3072, 4608, 5120, 5632
Z_W = 5760
MISC_GATES, MISC_A = 0, 24

NEG = -1e30
M_INIT = -1e29

Q_TILE = 128
N_QT = SEQ // Q_TILE
N_CMP_PAD = SEQ // CMP_STRIDE
SLAB = N_CMP_PAD + 128
SLAB_OFF = SLAB - 16
KC_ROWS = SLAB_OFF + N_CMP_PAD + 16
N_SEL = SEQ // SEL_BLOCK
REL0 = N_SEL - 2

VMEM_LIMIT = 56 * 1024 * 1024


def _cparams(sem):
    return pltpu.CompilerParams(dimension_semantics=sem, vmem_limit_bytes=VMEM_LIMIT)


def _dot(a, b):
    return jnp.dot(a, b, preferred_element_type=F32)


def _dot_nt(a, b):
    return lax.dot_general(a, b, (((1,), (1,)), ((), ())), preferred_element_type=F32)


def _dot_tn(a, b):
    return lax.dot_general(a, b, (((0,), (0,)), ((), ())), preferred_element_type=F32)


def _rms(x, g):
    return x * lax.rsqrt(jnp.mean(x * x, axis=-1, keepdims=True) + RMS_EPS) * g


def _silu(x):
    return x * jax.nn.sigmoid(x)


def _ffn_kernel(x_ref, g_ref, wg_ref, wu_ref, wd_ref, gf_ref, o_ref, xn_ref, *, n_ff, final_norm):
    j = pl.program_id(1)

    @pl.when(j == 0)
    def _():
        xn_ref[...] = _rms(x_ref[...], g_ref[...]).astype(BF16)
        o_ref[...] = jnp.zeros_like(o_ref)

    xn = xn_ref[...]
    hid = _silu(_dot(xn, wg_ref[...])) * _dot(xn, wu_ref[...])
    o_ref[...] += _dot(hid.astype(BF16), wd_ref[...])

    @pl.when(j == n_ff - 1)
    def _():
        h = x_ref[...] + 0.5 * o_ref[...]
        o_ref[...] = _rms(h, gf_ref[...]) if final_norm else h


def _ffn(x, g, wg, wu, wd, gf, final_norm, tm=512, tf=512):
    n, d = x.shape
    n_ff = D_FF // tf
    return pl.pallas_call(
        functools.partial(_ffn_kernel, n_ff=n_ff, final_norm=final_norm),
        grid=(n // tm, n_ff),
        in_specs=[pl.BlockSpec((tm, d), lambda i, j: (i, 0)),
                  pl.BlockSpec((1, d), lambda i, j: (0, 0)),
                  pl.BlockSpec((d, tf), lambda i, j: (0, j)),
                  pl.BlockSpec((d, tf), lambda i, j: (0, j)),
                  pl.BlockSpec((tf, d), lambda i, j: (j, 0)),
                  pl.BlockSpec((1, d), lambda i, j: (0, 0))],
        out_specs=pl.BlockSpec((tm, d), lambda i, j: (i, 0)),
        out_shape=jax.ShapeDtypeStruct((n, d), F32),
        scratch_shapes=[pltpu.VMEM((tm, d), BF16)],
        compiler_params=_cparams(("parallel", "arbitrary")),
        name="ffn",
    )(x, g, wg, wu, wd, gf)


def _norm_matmul_kernel(x_ref, g_ref, w_ref, o_ref, xn_ref):
    @pl.when(pl.program_id(1) == 0)
    def _():
        xn_ref[...] = _rms(x_ref[...], g_ref[...]).astype(BF16)

    o_ref[...] = _dot(xn_ref[...], w_ref[...])


def _norm_matmul(x, g, w, tm, tn, name):
    n, d = x.shape
    dout = w.shape[1]
    return pl.pallas_call(
        _norm_matmul_kernel,
        grid=(n // tm, dout // tn),
        in_specs=[pl.BlockSpec((tm, d), lambda i, j: (i, 0)),
                  pl.BlockSpec((1, d), lambda i, j: (0, 0)),
                  pl.BlockSpec((d, tn), lambda i, j: (0, j))],
        out_specs=pl.BlockSpec((tm, tn), lambda i, j: (i, j)),
        out_shape=jax.ShapeDtypeStruct((n, dout), F32),
        scratch_shapes=[pltpu.VMEM((tm, d), BF16)],
        compiler_params=_cparams(("parallel", "arbitrary")),
        name=name,
    )(x, g, w)


def _matmul_res_kernel(*refs, n_lhs):
    res_ref = refs[0]
    o_ref = refs[1 + 2 * n_lhs]
    acc = res_ref[...]
    for i in range(n_lhs):
        acc = acc + _dot(refs[1 + i][...].astype(BF16), refs[1 + n_lhs + i][...])
    o_ref[...] = acc


def _matmul_res(res, lhs, ws, tm, name):
    n, d = res.shape
    n_lhs = len(lhs)
    return pl.pallas_call(
        functools.partial(_matmul_res_kernel, n_lhs=n_lhs),
        grid=(n // tm,),
        in_specs=([pl.BlockSpec((tm, d), lambda i: (i, 0))]
                  + [pl.BlockSpec((tm, a.shape[1]), lambda i: (i, 0)) for a in lhs]
                  + [pl.BlockSpec(w.shape, lambda i: (0, 0)) for w in ws]),
        out_specs=pl.BlockSpec((tm, d), lambda i: (i, 0)),
        out_shape=jax.ShapeDtypeStruct((n, d), F32),
        compiler_params=_cparams(("parallel",)),
        name=name,
    )(res, *lhs, *ws)


def _rel_bucket_np(dist):
    n = np.maximum(dist, 0)
    exact = NUM_BUCKETS // 2
    nf = np.maximum(n, 1).astype(np.float32)
    large = exact + (np.log(nf / np.float32(exact)) / np.float32(math.log(MAX_DISTANCE / exact))
                     * np.float32(NUM_BUCKETS - exact)).astype(np.int32)
    return np.where(n < exact, n, np.minimum(large, NUM_BUCKETS - 1)).astype(np.int32)


def _bucket_or_masked(dist, valid):
    return np.where(valid, _rel_bucket_np(dist), -1).astype(np.int32)


def _prompt_bucket_table():
    i = np.arange(Q_TILE)[:, None]
    j = np.arange(Q_TILE)[None, :]
    u = np.arange(SLAB)[None, :]
    dist_c = i - CMP_STRIDE * u + (CMP_STRIDE * SLAB_OFF - (CMP_BLOCK - 1))
    diag = _bucket_or_masked(i - j, i - j >= 0)
    prev = _bucket_or_masked(Q_TILE + i - j, np.ones((Q_TILE, Q_TILE), bool))
    first = _bucket_or_masked(WINDOW + i - j, j > i)
    far = np.full((Q_TILE, Q_TILE), NUM_BUCKETS - 1, np.int32)
    cmp_ = _bucket_or_masked(dist_c, dist_c >= 0)
    return np.concatenate([diag, prev, first, far, cmp_], axis=1)


S_KEYS = PAST_LEN + PAGE_SIZE
S_CMP = PAST_LEN // CMP_STRIDE


def _sample_bucket_table():
    i = np.arange(DEC_SEQ)[:, None]
    pos = PAST_LEN + i
    c = np.arange(S_CMP)[None, :]
    dist_c = pos - (c * CMP_STRIDE + CMP_BLOCK - 1)
    cmp_ = _bucket_or_masked(dist_c, (dist_c >= 0) & (c < S_CMP - 1))
    k = np.arange(S_KEYS)[None, :]
    sel = _bucket_or_masked(pos - k, k <= pos)
    j = np.arange(WINDOW)[None, :]
    dist_w = pos - (PAST_LEN - WINDOW + j)
    win = _bucket_or_masked(dist_w, dist_w < WINDOW)
    return np.concatenate([cmp_, sel, win], axis=1)


def _bias_table_kernel(tab_ref, idx_ref, o_ref):
    h = pl.program_id(0)
    idx = idx_ref[...]
    out = jnp.full(idx.shape, NEG, F32)
    for b in range(NUM_BUCKETS):
        out = jnp.where(idx == b, tab_ref[b, h], out)
    o_ref[0] = out


def _bias_tables(rel_bias, idx, name):
    r, c = idx.shape
    return pl.pallas_call(
        _bias_table_kernel,
        grid=(NSA_HEADS,),
        in_specs=[pl.BlockSpec(memory_space=pltpu.SMEM),
                  pl.BlockSpec((r, c), lambda h: (0, 0))],
        out_specs=pl.BlockSpec((1, r, c), lambda h: (h, 0, 0)),
        out_shape=jax.ShapeDtypeStruct((NSA_HEADS, r, c), F32),
        compiler_params=_cparams(("arbitrary",)),
        name=name,
    )(rel_bias, jnp.asarray(idx))


def _cover_np(n_cmp_cols, n_blk_cols, delta_of):
    u = np.arange(n_cmp_cols)[:, None]
    j = np.arange(n_blk_cols)[None, :]
    delta = delta_of(u, j)
    shared = np.minimum(CMP_STRIDE * delta + CMP_BLOCK, SEL_BLOCK) - np.maximum(CMP_STRIDE * delta, 0)
    return (np.maximum(shared, 0) / CMP_STRIDE).astype(np.float32)


def _compress(rows_ref, n_chunk, pos_ref, w1_ref, w2_ref):
    first = jnp.zeros((n_chunk, HEAD_DIM), F32)
    second = jnp.zeros((n_chunk, HEAD_DIM), F32)
    for s in range(CMP_STRIDE):
        r = rows_ref[pl.ds(s, n_chunk, stride=CMP_STRIDE), :]
        first = first + _dot((r + pos_ref[s:s + 1, :]).astype(BF16), w1_ref[s])
        second = second + _dot((r + pos_ref[CMP_STRIDE + s:CMP_STRIDE + s + 1, :]).astype(BF16),
                               w1_ref[CMP_STRIDE + s])
    nxt = pltpu.roll(second, n_chunk - 1, axis=0)
    hid = _silu(first + nxt)
    return _dot(hid.astype(BF16), w2_ref[...])


def _flash_step(carry, s, v):
    m, l, acc = carry
    m_new = jnp.maximum(m, jnp.max(s, axis=-1, keepdims=True))
    alpha = jnp.exp(m - m_new)
    e = jnp.exp(s - m_new)
    l = alpha * l + jnp.sum(e, axis=-1, keepdims=True)
    acc = alpha * acc + _dot(e.astype(BF16), v)
    return m_new, l, acc


def _flash_init(rows):
    return (jnp.full((rows, 1), M_INIT, F32), jnp.zeros((rows, 1), F32), jnp.zeros((rows, HEAD_DIM), F32))


def _masked_softmax(s, valid):
    s = jnp.where(valid, s, NEG)
    m = jnp.max(s, axis=-1, keepdims=True)
    e = jnp.where(valid, jnp.exp(s - m), 0.0)
    return e / jnp.maximum(jnp.sum(e, axis=-1, keepdims=True), 1e-30)


def _split_dot(x, w):
    hi = x.astype(BF16)
    lo = (x - hi.astype(F32)).astype(BF16)
    return _dot(hi, w) + _dot(lo, w)


def _top_n_mask(score, index, axis):
    sel = jnp.zeros(score.shape, F32)
    for _ in range(SEL_TOPN):
        mx = jnp.max(score, axis=axis, keepdims=True)
        first = jnp.min(jnp.where(score == mx, index, 1e9), axis=axis, keepdims=True)
        hit = index == first
        sel = jnp.where(hit, 1.0, sel)
        score = jnp.where(hit, -jnp.inf, score)
    return sel


def _compress_prompt_kernel(rows_ref, pos_ref, w1_ref, w2_ref, o_ref):
    out = _compress(rows_ref, N_CMP_PAD, pos_ref.at[0], w1_ref.at[0], w2_ref.at[0])
    real = lax.broadcasted_iota(jnp.int32, (N_CMP_PAD, HEAD_DIM), 0) < N_CMP_PAD - 1
    o_ref[0, 0, 0:SLAB_OFF, :] = jnp.zeros((SLAB_OFF, HEAD_DIM), F32)
    o_ref[0, 0, SLAB_OFF:SLAB_OFF + N_CMP_PAD, :] = jnp.where(real, out, 0.0)
    o_ref[0, 0, SLAB_OFF + N_CMP_PAD:KC_ROWS, :] = jnp.zeros((KC_ROWS - SLAB_OFF - N_CMP_PAD, HEAD_DIM), F32)


def _compress_prompt(z, pos, w1, w2):
    kv_blk = Z_KV // HEAD_DIM
    return pl.pallas_call(
        _compress_prompt_kernel,
        grid=(2, NSA_KV_HEADS),
        in_specs=[pl.BlockSpec((SEQ, HEAD_DIM), lambda i, h: (0, kv_blk + NSA_KV_HEADS * i + h)),
                  pl.BlockSpec((1, CMP_BLOCK, HEAD_DIM), lambda i, h: (i, 0, 0)),
                  pl.BlockSpec((1, CMP_BLOCK, HEAD_DIM, HEAD_DIM), lambda i, h: (i, 0, 0, 0)),
                  pl.BlockSpec((1, HEAD_DIM, HEAD_DIM), lambda i, h: (i, 0, 0))],
        out_specs=pl.BlockSpec((1, 1, KC_ROWS, HEAD_DIM), lambda i, h: (i, h, 0, 0)),
        out_shape=jax.ShapeDtypeStruct((2, NSA_KV_HEADS, KC_ROWS, HEAD_DIM), F32),
        compiler_params=_cparams(("parallel", "parallel")),
        name="compress_prompt",
    )(z, pos, w1, w2)


def _nsa_prompt_kernel(q_ref, misc_ref, ksel_ref, vsel_ref, kwin_ref, vwin_ref, kc_ref, vc_ref,
                       bias_ref, cover_ref, o_ref):
    kvh = pl.program_id(0)
    t = pl.program_id(1)
    rows = NSA_GROUP * Q_TILE
    q = q_ref[...] * (HEAD_DIM ** -0.5)
    qs = jnp.concatenate([q[:, g * HEAD_DIM:(g + 1) * HEAD_DIM] for g in range(NSA_GROUP)], axis=0).astype(BF16)

    def bias_tile(k):
        return bias_ref[0, :, k * Q_TILE:(k + 1) * Q_TILE]
    b_diag, b_prev, b_first, b_far = bias_tile(0), bias_tile(1), bias_tile(2), bias_tile(3)

    start = pl.multiple_of(t * (Q_TILE // CMP_STRIDE), 8)
    kslab = kc_ref[0, 0, pl.ds(start, SLAB), :].astype(BF16)
    vslab = vc_ref[0, 0, pl.ds(start, SLAB), :].astype(BF16)
    b_cmp = bias_ref[0, :, 4 * Q_TILE:4 * Q_TILE + SLAB]
    u = lax.broadcasted_iota(jnp.int32, (1, SLAB), 1)
    valid = (b_cmp > M_INIT) & (u >= SLAB_OFF - (Q_TILE // CMP_STRIDE) * t)
    p_cmp = _masked_softmax(_dot_nt(qs, kslab) + b_cmp, valid)
    o_cmp = _dot(p_cmp.astype(BF16), vslab)

    p_sum = p_cmp[0:Q_TILE]
    for g in range(1, NSA_GROUP):
        p_sum = p_sum + p_cmp[g * Q_TILE:(g + 1) * Q_TILE]
    imp = _split_dot(p_sum, cover_ref[...]).T
    jr = lax.broadcasted_iota(jnp.int32, (N_SEL, Q_TILE), 0)
    qi = lax.broadcasted_iota(jnp.int32, (N_SEL, Q_TILE), 1)
    cur = REL0 + (qi >= SEL_BLOCK).astype(jnp.int32)
    first_blk = REL0 - 2 * t
    forced = (jr == first_blk) | (jr == cur) | (jr == cur - 1)
    in_range = (jr <= cur) & (jr >= first_blk)
    score = jnp.where(in_range, jnp.where(forced, FORCED_SCORE, imp), NEG)
    sel = _top_n_mask(score, jr.astype(F32), 0).T.astype(BF16)

    def sel_mask(kt):
        blk = lax.broadcasted_iota(jnp.int32, (N_SEL, Q_TILE), 0)
        key = lax.broadcasted_iota(jnp.int32, (N_SEL, Q_TILE), 1)
        expand = (blk == 2 * (kt - t) + REL0 + (key >= SEL_BLOCK).astype(jnp.int32)).astype(BF16)
        add = (_dot(sel, expand) - 1.0) * (-NEG)
        return jnp.concatenate([add] * NSA_GROUP, axis=0)

    def key_tile(ref, kt):
        return ref[pl.ds(pl.multiple_of(kt * Q_TILE, Q_TILE), Q_TILE), :]

    def far_body(kt, carry):
        s = _dot_nt(qs, key_tile(ksel_ref, kt)) + b_far + sel_mask(kt)
        return _flash_step(carry, s, key_tile(vsel_ref, kt))

    carry = lax.fori_loop(0, jnp.maximum(t - 1, 0), far_body, _flash_init(rows))
    kp = jnp.maximum(t - 1, 0)
    pen = jnp.where(t >= 1, 0.0, NEG)
    s = _dot_nt(qs, key_tile(ksel_ref, kp)) + b_prev + sel_mask(t - 1) + pen
    carry = _flash_step(carry, s, key_tile(vsel_ref, kp))
    s = _dot_nt(qs, key_tile(ksel_ref, t)) + b_diag + sel_mask(t)
    _, l, acc = _flash_step(carry, s, key_tile(vsel_ref, t))
    o_sel = acc / l

    carry = _flash_init(rows)
    n_wt = WINDOW // Q_TILE + 1
    for w, b_tile in enumerate((b_first,) + (b_far,) * (n_wt - 3) + (b_prev, b_diag)):
        kt = t - (n_wt - 1) + w
        kc = jnp.maximum(kt, 0)
        s = _dot_nt(qs, key_tile(kwin_ref, kc)) + b_tile + jnp.where(kt >= 0, 0.0, NEG)
        carry = _flash_step(carry, s, key_tile(vwin_ref, kc))
    _, l, acc = carry
    o_win = acc / l

    gates = jax.nn.sigmoid(misc_ref[...])
    gk = jnp.where(kvh == 0, gates[:, MISC_GATES:MISC_GATES + 3 * NSA_GROUP],
                   gates[:, MISC_GATES + 3 * NSA_GROUP:MISC_GATES + 6 * NSA_GROUP])
    for g in range(NSA_GROUP):
        r = slice(g * Q_TILE, (g + 1) * Q_TILE)
        o_ref[:, g * HEAD_DIM:(g + 1) * HEAD_DIM] = (gk[:, 3 * g:3 * g + 1] * o_cmp[r]
                                                     + gk[:, 3 * g + 1:3 * g + 2] * o_sel[r]
                                                     + gk[:, 3 * g + 2:3 * g + 3] * o_win[r])


def _nsa_prompt(z, kvb, kcp, bias, cover):
    gw = NSA_GROUP * HEAD_DIM

    def kv_spec(j):
        return pl.BlockSpec((SEQ, HEAD_DIM), lambda h, t, j=j: (0, 2 * j + h))
    return pl.pallas_call(
        _nsa_prompt_kernel,
        grid=(NSA_KV_HEADS, N_QT),
        in_specs=[pl.BlockSpec((Q_TILE, gw), lambda h, t: (t, h)),
                  pl.BlockSpec((Q_TILE, 128), lambda h, t: (t, Z_MISC // 128)),
                  kv_spec(2), kv_spec(3), kv_spec(4), kv_spec(5),
                  pl.BlockSpec((1, 1, KC_ROWS, HEAD_DIM), lambda h, t: (0, h, 0, 0)),
                  pl.BlockSpec((1, 1, KC_ROWS, HEAD_DIM), lambda h, t: (1, h, 0, 0)),
                  pl.BlockSpec((1,) + bias.shape[1:], lambda h, t: (h, 0, 0)),
                  pl.BlockSpec(cover.shape, lambda h, t: (0, 0))],
        out_specs=pl.BlockSpec((Q_TILE, gw), lambda h, t: (t, h)),
        out_shape=jax.ShapeDtypeStruct((SEQ, NSA_HEADS * HEAD_DIM), F32),
        compiler_params=_cparams(("arbitrary", "arbitrary")),
        name="nsa_prompt",
    )(z, z, kvb, kvb, kvb, kvb, kcp, kcp, bias, cover)


S_ROWS = NSA_GROUP * DEC_SEQ


def _nsa_sample_kernel(pt_ref, *refs):
    del pt_ref
    n_pg = N_PAGES
    q_ref, gate_ref, new_ref, kwin_ref, vwin_ref = refs[0:5]
    pages = [refs[5 + k * n_pg:5 + (k + 1) * n_pg] for k in range(4)]
    (posk_ref, w1k_ref, w2k_ref, posv_ref, w1v_ref, w2v_ref,
     bias_ref, cover_ref, expand_ref) = refs[5 + 4 * n_pg:14 + 4 * n_pg]
    o_ref, kwin_o_ref, vwin_o_ref = refs[14 + 4 * n_pg:17 + 4 * n_pg]
    past_ref, newk_ref, newv_ref = refs[17 + 4 * n_pg:20 + 4 * n_pg]
    kvw = NSA_KV_HEADS * HEAD_DIM
    new = new_ref[0]

    def new_rows(j):
        return new[:, j * kvw:(j + 1) * kvw]

    comp = []
    for k, (pos_ref, w1_ref, w2_ref) in enumerate(((posk_ref, w1k_ref, w2k_ref), (posv_ref, w1v_ref, w2v_ref))):
        per_head = []
        for h in range(NSA_KV_HEADS):
            for p in range(n_pg):
                past_ref[p * PAGE_SIZE:(p + 1) * PAGE_SIZE, :] = pages[k][p][0, :, h * HEAD_DIM:(h + 1) * HEAD_DIM]
            per_head.append(_compress(past_ref, S_CMP, pos_ref, w1_ref, w2_ref))
        comp.append(per_head)

    keep = WINDOW - DEC_SEQ
    kwin_o_ref[0, 0:keep, :] = kwin_ref[0, DEC_SEQ:WINDOW, :]
    kwin_o_ref[0, keep:WINDOW, :] = new_rows(4)
    vwin_o_ref[0, 0:keep, :] = vwin_ref[0, DEC_SEQ:WINDOW, :]
    vwin_o_ref[0, keep:WINDOW, :] = new_rows(5)

    gates = jax.nn.sigmoid(gate_ref[0])
    o_cols = S_CMP
    for h in range(NSA_KV_HEADS):
        lanes = slice(h * HEAD_DIM, (h + 1) * HEAD_DIM)
        qs = (q_ref[0, h] * (HEAD_DIM ** -0.5)).astype(BF16)
        b_cmp = bias_ref[h, :, 0:S_CMP]
        b_sel = bias_ref[h, :, o_cols:o_cols + S_KEYS]
        b_win = bias_ref[h, :, o_cols + S_KEYS:o_cols + S_KEYS + WINDOW]
        b_new = b_sel[:, PAST_LEN:S_KEYS]

        kc, vc = comp[0][h].astype(BF16), comp[1][h].astype(BF16)
        p_cmp = _masked_softmax(_dot_nt(qs, kc) + b_cmp, b_cmp > M_INIT)
        o_cmp = _dot(p_cmp.astype(BF16), vc)

        imp = _split_dot(p_cmp, cover_ref[...])
        imp = imp + pltpu.roll(imp, 4, axis=0) + pltpu.roll(imp, 8, axis=0) + pltpu.roll(imp, 12, axis=0)
        blk = lax.broadcasted_iota(jnp.int32, (S_ROWS, 128), 1)
        cur = PAST_LEN // SEL_BLOCK
        forced = (blk == 0) | (blk == cur) | (blk == cur - 1)
        score = jnp.where(blk <= cur, jnp.where(forced, FORCED_SCORE, imp), NEG)
        sel = _top_n_mask(score, blk.astype(F32), 1).astype(BF16)
        mask_add = (_dot(sel, expand_ref[...]) - 1.0) * (-NEG)

        newk_ref[...] = jnp.zeros_like(newk_ref)
        newv_ref[...] = jnp.zeros_like(newv_ref)
        newk_ref[0:DEC_SEQ, :] = new_rows(2)[:, lanes]
        newv_ref[0:DEC_SEQ, :] = new_rows(3)[:, lanes]
        carry = _flash_init(S_ROWS)
        for p in range(n_pg + 1):
            cols = slice(p * PAGE_SIZE, (p + 1) * PAGE_SIZE)
            if p < n_pg:
                k_t, v_t = pages[2][p][0, :, lanes].astype(BF16), pages[3][p][0, :, lanes].astype(BF16)
            else:
                k_t, v_t = newk_ref[...].astype(BF16), newv_ref[...].astype(BF16)
            s = _dot_nt(qs, k_t) + b_sel[:, cols] + mask_add[:, cols]
            carry = _flash_step(carry, s, v_t)
        _, l, acc = carry
        o_sel = acc / l

        newk_ref[0:DEC_SEQ, :] = new_rows(4)[:, lanes]
        newv_ref[0:DEC_SEQ, :] = new_rows(5)[:, lanes]
        carry = _flash_init(S_ROWS)
        s = _dot_nt(qs, kwin_ref[0, :, lanes].astype(BF16)) + b_win
        carry = _flash_step(carry, s, vwin_ref[0, :, lanes].astype(BF16))
        s = _dot_nt(qs, newk_ref[...].astype(BF16)) + b_new
        _, l, acc = _flash_step(carry, s, newv_ref[...].astype(BF16))
        o_win = acc / l

        g = gates[h]
        o_ref[0, h] = g[:, 0:1] * o_cmp + g[:, 1:2] * o_sel + g[:, 2:3] * o_win


def _nsa_sample(page_table, q_s, gate_s, new_s, kwin, vwin, pools, cmp_w, bias, cover, expand):
    kvw = NSA_KV_HEADS * HEAD_DIM

    def full(a):
        return pl.BlockSpec(a.shape, lambda b, pt, n=a.ndim: (0,) * n)

    def per_b(a):
        return pl.BlockSpec((1,) + a.shape[1:], lambda b, pt, n=a.ndim: (b,) + (0,) * (n - 1))
    page_specs = [pl.BlockSpec((1, PAGE_SIZE, kvw), lambda b, pt, p=p: (pt[b, p], 0, 0))
                  for _ in range(4) for p in range(N_PAGES)]
    page_args = [pool for pool in pools for _ in range(N_PAGES)]
    consts = list(cmp_w) + [bias, cover, expand]
    grid_spec = pltpu.PrefetchScalarGridSpec(
        num_scalar_prefetch=1,
        grid=(DEC_BATCH,),
        in_specs=[per_b(q_s), per_b(gate_s), per_b(new_s), per_b(kwin), per_b(vwin)]
        + page_specs + [full(a) for a in consts],
        out_specs=[per_b(q_s), per_b(kwin), per_b(vwin)],
        scratch_shapes=[pltpu.VMEM((PAST_LEN, HEAD_DIM), F32),
                        pltpu.VMEM((PAGE_SIZE, HEAD_DIM), F32),
                        pltpu.VMEM((PAGE_SIZE, HEAD_DIM), F32)],
    )
    return pl.pallas_call(
        _nsa_sample_kernel,
        grid_spec=grid_spec,
        out_shape=[jax.ShapeDtypeStruct(q_s.shape, F32),
                   jax.ShapeDtypeStruct(kwin.shape, F32),
                   jax.ShapeDtypeStruct(vwin.shape, F32)],
        compiler_params=_cparams(("arbitrary",)),
        name="nsa_sample",
    )(page_table, q_s, gate_s, new_s, kwin, vwin, *page_args, *consts)


def _log_decay(a_blk, wa_ref, ba_ref):
    x = _dot(a_blk.astype(BF16), wa_ref[...]) + ba_ref[...]
    return (jnp.minimum(x, 0.0) - jnp.log1p(jnp.exp(-jnp.abs(x)))) * (1.0 / GLA_TAU)


def _segment_cumsum(g, seg):
    pos = lax.broadcasted_iota(jnp.int32, g.shape, 0) % seg
    cum = g
    sh = 1
    while sh < seg:
        cum = cum + jnp.where(pos >= sh, pltpu.roll(cum, sh, axis=0), 0.0)
        sh *= 2
    return cum


def _gla_prompt_kernel(q_ref, k_ref, v_ref, r_ref, a_ref, wa_ref, ba_ref, gn_ref, o_ref, st_o_ref, st_ref,
                       *, n_blk, tb):
    tbi = pl.program_id(1)

    @pl.when(tbi == 0)
    def _():
        st_ref[...] = jnp.zeros_like(st_ref)

    c = GLA_CHUNK
    cum = _segment_cumsum(_log_decay(a_ref[...], wa_ref, ba_ref), c)
    q = q_ref[...] * (GLA_DK ** -0.5)
    k = k_ref[...]
    v = v_ref[...].astype(BF16)
    qe = (q * jnp.exp(cum)).astype(BF16)
    kd = (k * jnp.exp(-cum)).astype(BF16)
    tril = lax.broadcasted_iota(jnp.int32, (c, c), 0) >= lax.broadcasted_iota(jnp.int32, (c, c), 1)
    st = st_ref[...]
    outs = []
    for ci in range(tb // c):
        r = slice(ci * c, (ci + 1) * c)
        last = cum[ci * c + c - 1:ci * c + c, :]
        att = jnp.where(tril, _dot_nt(qe[r], kd[r]), 0.0)
        outs.append(_dot_nt(qe[r], st.astype(BF16)) + _dot(att.astype(BF16), v[r]))
        kl = (k[r] * jnp.exp(last - cum[r])).astype(BF16)
        st = jnp.exp(last) * st + _dot_tn(v[r], kl)
    st_ref[...] = st
    o = jnp.concatenate(outs, axis=0)
    o_ref[...] = _rms(o, gn_ref[...]) * _silu(r_ref[...])

    @pl.when(tbi == n_blk - 1)
    def _():
        st_o_ref[0] = st


def _gla_prompt(z, wa, ba, gn, tb=256):
    n_blk = SEQ // tb
    return pl.pallas_call(
        functools.partial(_gla_prompt_kernel, n_blk=n_blk, tb=tb),
        grid=(GLA_HEADS, n_blk),
        in_specs=[pl.BlockSpec((tb, GLA_DK), lambda h, i: (i, Z_QG // GLA_DK + h)),
                  pl.BlockSpec((tb, GLA_DK), lambda h, i: (i, Z_KG // GLA_DK + h)),
                  pl.BlockSpec((tb, GLA_DV), lambda h, i: (i, Z_VG // GLA_DV + h)),
                  pl.BlockSpec((tb, GLA_DV), lambda h, i: (i, Z_RG // GLA_DV + h)),
                  pl.BlockSpec((tb, 128), lambda h, i: (i, Z_MISC // 128)),
                  pl.BlockSpec((128, GLA_DK), lambda h, i: (0, h)),
                  pl.BlockSpec((1, GLA_DK), lambda h, i: (0, h)),
                  pl.BlockSpec((1, GLA_DV), lambda h, i: (0, 0))],
        out_specs=[pl.BlockSpec((tb, GLA_DV), lambda h, i: (i, h)),
                   pl.BlockSpec((1, GLA_DV, GLA_DK), lambda h, i: (h, 0, 0))],
        out_shape=[jax.ShapeDtypeStruct((SEQ, GLA_HEADS * GLA_DV), F32),
                   jax.ShapeDtypeStruct((GLA_HEADS, GLA_DV, GLA_DK), F32)],
        scratch_shapes=[pltpu.VMEM((GLA_DV, GLA_DK), F32)],
        compiler_params=_cparams(("parallel", "arbitrary")),
        name="gla_prompt",
    )(z, z, z, z, z, wa, ba, gn)


GS_B = 4


def _gla_sample_kernel(q_ref, k_ref, v_ref, r_ref, a_ref, wa_ref, ba_ref, gn_ref, s_ref, o_ref, s_o_ref):
    rows = GS_B * DEC_SEQ
    a = a_ref[...]
    ri = lax.broadcasted_iota(jnp.int32, (rows, rows), 0)
    ci = lax.broadcasted_iota(jnp.int32, (rows, rows), 1)
    same_causal = (ri // DEC_SEQ == ci // DEC_SEQ) & (ri >= ci)
    row_b = lax.broadcasted_iota(jnp.int32, (rows, 1), 0) // DEC_SEQ
    ones = jnp.ones((rows, 128), BF16)
    for h in range(GLA_HEADS):
        dk = slice(h * GLA_DK, (h + 1) * GLA_DK)
        dv = slice(h * GLA_DV, (h + 1) * GLA_DV)
        g = _log_decay(a, wa_ref.at[:, dk], ba_ref.at[:, dk])
        cum = _segment_cumsum(g, DEC_SEQ)
        q = q_ref[:, dk] * (GLA_DK ** -0.5)
        k = k_ref[:, dk]
        v = v_ref[:, dv].astype(BF16)
        qe = (q * jnp.exp(cum)).astype(BF16)
        kd = (k * jnp.exp(-cum)).astype(BF16)
        att = jnp.where(same_causal, _dot_nt(qe, kd), 0.0)
        o = _dot(att.astype(BF16), v)
        for b in range(GS_B):
            mine = row_b == b
            last = cum[b * DEC_SEQ + DEC_SEQ - 1:(b + 1) * DEC_SEQ, :]
            s = s_ref[b, h]
            o = o + jnp.where(mine, _dot(qe, s.astype(BF16)), 0.0)
            kl = jnp.where(mine, k * jnp.exp(last - cum), 0.0)
            hi = jnp.where(mine, g, 0.0).astype(BF16)
            lo = (jnp.where(mine, g, 0.0) - hi.astype(F32)).astype(BF16)
            last_col = (_dot_tn(hi, ones) + _dot_tn(lo, ones))[:, 0:1]
            s_o_ref[b, h] = jnp.exp(last_col) * s + _dot_tn(kl.astype(BF16), v)
        o_ref[:, dv] = _rms(o, gn_ref[...]) * _silu(r_ref[:, dv])


def _gla_sample(zs, state, wa, ba, gn):
    rows = GS_B * DEC_SEQ
    n = DEC_BATCH * DEC_SEQ
    hk, hv = GLA_HEADS * GLA_DK, GLA_HEADS * GLA_DV
    st_spec = pl.BlockSpec((GS_B, GLA_HEADS, GLA_DK, GLA_DV), lambda i: (i, 0, 0, 0))
    return pl.pallas_call(
        _gla_sample_kernel,
        grid=(DEC_BATCH // GS_B,),
        in_specs=[pl.BlockSpec((rows, hk), lambda i: (i, Z_QG // hk)),
                  pl.BlockSpec((rows, hk), lambda i: (i, Z_KG // hk)),
                  pl.BlockSpec((rows, hv), lambda i: (i, Z_VG // hv)),
                  pl.BlockSpec((rows, hv), lambda i: (i, Z_RG // hv)),
                  pl.BlockSpec((rows, 128), lambda i: (i, Z_MISC // 128)),
                  pl.BlockSpec((128, hk), lambda i: (0, 0)),
                  pl.BlockSpec((1, hk), lambda i: (0, 0)),
                  pl.BlockSpec((1, GLA_DV), lambda i: (0, 0)),
                  st_spec],
        out_specs=[pl.BlockSpec((rows, hv), lambda i: (i, 0)), st_spec],
        out_shape=[jax.ShapeDtypeStruct((n, hv), F32), jax.ShapeDtypeStruct(state.shape, F32)],
        compiler_params=_cparams(("parallel",)),
        name="gla_sample",
    )(zs, zs, zs, zs, zs, wa, ba, gn, state)


def _softmax_rows(s):
    m = jnp.max(s, axis=-1, keepdims=True)
    e = jnp.exp(s - m)
    return e / jnp.sum(e, axis=-1, keepdims=True)


def _mem_prompt_kernel(q_ref, k_ref, v_ref, o_ref):
    for h in range(MEM_HEADS):
        d = slice(h * MEM_HEAD_DIM, (h + 1) * MEM_HEAD_DIM)
        q = (q_ref[:, d] * (MEM_HEAD_DIM ** -0.5)).astype(BF16)
        p = _softmax_rows(_dot_nt(q, k_ref[:, d].astype(BF16)))
        o_ref[:, d] = _dot(p.astype(BF16), v_ref[:, d].astype(BF16))


def _mem_prompt(qm, memkv, tq=256):
    w = MEM_HEADS * MEM_HEAD_DIM
    return pl.pallas_call(
        _mem_prompt_kernel,
        grid=(SEQ // tq,),
        in_specs=[pl.BlockSpec((tq, w), lambda i: (i, 0)),
                  pl.BlockSpec((MEM_TOKENS, w), lambda i: (0, 0)),
                  pl.BlockSpec((MEM_TOKENS, w), lambda i: (0, 1))],
        out_specs=pl.BlockSpec((tq, w), lambda i: (i, 0)),
        out_shape=jax.ShapeDtypeStruct((SEQ, w), F32),
        compiler_params=_cparams(("parallel",)),
        name="mem_prompt",
    )(qm, memkv, memkv)


def _mem_sample_kernel(q_ref, k_ref, v_ref, o_ref):
    rows = MEM_HEADS * DEC_SEQ
    q = (q_ref[0] * (MEM_HEAD_DIM ** -0.5)).astype(BF16)
    row_h = lax.broadcasted_iota(jnp.int32, (rows, 1), 0) // DEC_SEQ
    s = jnp.zeros((rows, MEM_TOKENS), F32)
    for h in range(MEM_HEADS):
        d = slice(h * MEM_HEAD_DIM, (h + 1) * MEM_HEAD_DIM)
        s = s + jnp.where(row_h == h, _dot_nt(q, k_ref[0, :, d].astype(BF16)), 0.0)
    p = _softmax_rows(s).astype(BF16)
    o = jnp.zeros((rows, MEM_HEAD_DIM), F32)
    for h in range(MEM_HEADS):
        d = slice(h * MEM_HEAD_DIM, (h + 1) * MEM_HEAD_DIM)
        o = o + jnp.where(row_h == h, _dot(p, v_ref[0, :, d].astype(BF16)), 0.0)
    o_ref[0] = o


def _mem_sample(q_s, k_mem, v_mem):
    w = MEM_HEADS * MEM_HEAD_DIM
    rows = MEM_HEADS * DEC_SEQ
    return pl.pallas_call(
        _mem_sample_kernel,
        grid=(DEC_BATCH,),
        in_specs=[pl.BlockSpec((1, rows, MEM_HEAD_DIM), lambda b: (b, 0, 0)),
                  pl.BlockSpec((1, MEM_TOKENS, w), lambda b: (b, 0, 0)),
                  pl.BlockSpec((1, MEM_TOKENS, w), lambda b: (b, 0, 0))],
        out_specs=pl.BlockSpec((1, rows, MEM_HEAD_DIM), lambda b: (b, 0, 0)),
        out_shape=jax.ShapeDtypeStruct((DEC_BATCH, rows, MEM_HEAD_DIM), F32),
        compiler_params=_cparams(("parallel",)),
        name="mem_sample",
    )(q_s, k_mem, v_mem)


def _permute_w_in(w_in):
    qn, kv, gt, qg, kg, vg, rg, ag = jnp.split(
        w_in, np.cumsum([1024, 1536, 24, 512, 512, 1024, 1024, 16])[:-1].tolist(), axis=1)
    pad = jnp.zeros((w_in.shape[0], Z_W - Z_MISC - 40), w_in.dtype)
    return jnp.concatenate([qn, vg, rg, kv, qg, kg, gt, ag, pad], axis=1)


def kernel(x_prompt, x_sample, mem_prompt, cache_k_cmp, cache_v_cmp, cache_k_sel, cache_v_sel, cache_k_win,
           cache_v_win, state_gla, cache_k_mem, cache_v_mem, page_table, norm_ffn1, ffn1_w_gate, ffn1_w_up,
           ffn1_w_down, norm_mix, w_in, w_out, cmp_pos_k, cmp_w1_k, cmp_w2_k, cmp_pos_v, cmp_w1_v, cmp_w2_v,
           rel_bias, gla_w_a2, gla_b_a, gla_norm, norm_mem, norm_mem_src, w_mem_q, w_mem_k, w_mem_v, w_mem_o,
           norm_ffn2, ffn2_w_gate, ffn2_w_up, ffn2_w_down, norm_final):
    bf = lambda a: a.astype(BF16)
    row = lambda a: a.reshape(1, -1)
    nb, ns = DEC_BATCH, DEC_SEQ
    kvw = NSA_KV_HEADS * HEAD_DIM

    x = jnp.concatenate([x_prompt[0], x_sample.reshape(nb * ns, D_MODEL)], axis=0)
    ones = jnp.ones((1, D_MODEL), F32)
    h1 = _ffn(x, row(norm_ffn1[0]), bf(ffn1_w_gate[0]), bf(ffn1_w_up[0]), bf(ffn1_w_down[0]), ones, False)
    z = _norm_matmul(h1, row(norm_mix[0]), bf(_permute_w_in(w_in[0])), 512, Z_W // 5, "proj_in")

    kv_p = z[:SEQ, Z_KV:Z_QG]
    kv_s = z[SEQ:, Z_KV:Z_QG].reshape(nb, ns, 6 * kvw)
    rows_p = [kv_p[:, j * kvw:(j + 1) * kvw].reshape(1, 1, SEQ, NSA_KV_HEADS, HEAD_DIM) for j in range(6)]
    rows_s = [kv_s[:, :, j * kvw:(j + 1) * kvw].reshape(1, nb, ns, NSA_KV_HEADS, HEAD_DIM) for j in range(4)]

    tab_p = _bias_tables(rel_bias, _prompt_bucket_table(), "bias_prompt")
    tab_p = tab_p.reshape(NSA_KV_HEADS, NSA_GROUP * Q_TILE, -1)
    tab_s = _bias_tables(rel_bias, np.tile(_sample_bucket_table(), (2, 1)), "bias_sample")[:, :ns]
    tab_s = tab_s.reshape(NSA_KV_HEADS, S_ROWS, -1)

    cmp_pos = jnp.stack([cmp_pos_k[0], cmp_pos_v[0]])
    cmp_w1 = bf(jnp.stack([cmp_w1_k[0], cmp_w1_v[0]]))
    cmp_w2 = bf(jnp.stack([cmp_w2_k[0], cmp_w2_v[0]]))
    kcp = _compress_prompt(z, cmp_pos, cmp_w1, cmp_w2)
    cover_p = jnp.asarray(_cover_np(SLAB, N_SEL, lambda u, j: u - 4 * j - SLAB_OFF + 4 * REL0), BF16)
    o_nsa_p = _nsa_prompt(z, bf(kv_p), kcp, tab_p, cover_p)

    zs = z[SEQ:]
    q_s = zs[:, Z_QN:Z_QN + NSA_HEADS * HEAD_DIM].reshape(nb, ns, NSA_KV_HEADS, NSA_GROUP, HEAD_DIM)
    q_s = q_s.transpose(0, 2, 3, 1, 4).reshape(nb, NSA_KV_HEADS, S_ROWS, HEAD_DIM)
    gate_s = zs[:, Z_MISC + MISC_GATES:Z_MISC + MISC_GATES + 3 * NSA_HEADS]
    gate_s = gate_s.reshape(nb, ns, NSA_KV_HEADS, NSA_GROUP, 3).transpose(0, 2, 3, 1, 4)
    gate_s = gate_s.reshape(nb, NSA_KV_HEADS, S_ROWS, 3)
    pools = [c[0].reshape(-1, PAGE_SIZE, kvw) for c in (cache_k_cmp, cache_v_cmp, cache_k_sel, cache_v_sel)]
    cover_s = jnp.asarray(_cover_np(S_CMP, 128, lambda c, j: c - 4 * j), BF16)
    expand_s = jnp.asarray(np.arange(128)[:, None] == (np.arange(S_KEYS)[None, :] // SEL_BLOCK), BF16)
    o_nsa_s, kwin_s, vwin_s = _nsa_sample(
        page_table, q_s, gate_s, kv_s, cache_k_win[0].reshape(nb, WINDOW, kvw),
        cache_v_win[0].reshape(nb, WINDOW, kvw), pools,
        (cmp_pos_k[0], bf(cmp_w1_k[0]), bf(cmp_w2_k[0]), cmp_pos_v[0], bf(cmp_w1_v[0]), bf(cmp_w2_v[0])),
        tab_s, cover_s, expand_s)
    o_nsa_s = o_nsa_s.reshape(nb, NSA_KV_HEADS, NSA_GROUP, ns, HEAD_DIM).transpose(0, 3, 1, 2, 4)
    o_nsa = jnp.concatenate([o_nsa_p, o_nsa_s.reshape(nb * ns, NSA_HEADS * HEAD_DIM)], axis=0)

    wa = bf(jnp.zeros((128, GLA_HEADS * GLA_DK), F32).at[MISC_A:MISC_A + GLA_RANK].set(gla_w_a2[0]))
    ba, gn = row(gla_b_a[0]), row(gla_norm[0])
    o_gla_p, st_p = _gla_prompt(z, wa, ba, gn)
    o_gla_s, st_s = _gla_sample(zs, state_gla[0], wa, ba, gn)
    o_gla = jnp.concatenate([o_gla_p, o_gla_s], axis=0)

    half = NSA_HEADS * HEAD_DIM
    h2 = _matmul_res(h1, [o_nsa, o_gla], [bf(w_out[0][:half]), bf(w_out[0][half:])], 512, "proj_out")

    memkv = _norm_matmul(mem_prompt[0], row(norm_mem_src[0]),
                         bf(jnp.concatenate([w_mem_k[0], w_mem_v[0]], axis=1)), MEM_TOKENS, 512, "mem_kv")
    mw = MEM_HEADS * MEM_HEAD_DIM
    qm = _norm_matmul(h2, row(norm_mem[0]), bf(w_mem_q[0]), 512, mw, "mem_q")
    om_p = _mem_prompt(qm, memkv)
    qm_s = qm[SEQ:].reshape(nb, ns, MEM_HEADS, MEM_HEAD_DIM).transpose(0, 2, 1, 3)
    om_s = _mem_sample(qm_s.reshape(nb, MEM_HEADS * ns, MEM_HEAD_DIM),
                       cache_k_mem[0].reshape(nb, MEM_TOKENS, mw), cache_v_mem[0].reshape(nb, MEM_TOKENS, mw))
    om_s = om_s.reshape(nb, MEM_HEADS, ns, MEM_HEAD_DIM).transpose(0, 2, 1, 3).reshape(nb * ns, mw)
    h3 = _matmul_res(h2, [jnp.concatenate([om_p, om_s], axis=0)], [bf(w_mem_o[0])], 512, "mem_out")

    y = _ffn(h3, row(norm_ffn2[0]), bf(ffn2_w_gate[0]), bf(ffn2_w_up[0]), bf(ffn2_w_down[0]),
             row(norm_final), True)

    mem_shape = (1, 1, MEM_TOKENS, MEM_HEADS, MEM_HEAD_DIM)
    win_shape = (1, nb, WINDOW, NSA_KV_HEADS, HEAD_DIM)
    return (y[:SEQ].reshape(1, SEQ, D_MODEL), y[SEQ:].reshape(nb, ns, D_MODEL),
            rows_p[0], rows_p[1], rows_p[2], rows_p[3],
            rows_p[4][:, :, SEQ - WINDOW:], rows_p[5][:, :, SEQ - WINDOW:],
            st_p.transpose(0, 2, 1).reshape(1, 1, GLA_HEADS, GLA_DK, GLA_DV),
            memkv[:, :mw].reshape(mem_shape), memkv[:, mw:].reshape(mem_shape),
            rows_s[0], rows_s[1], rows_s[2], rows_s[3],
            kwin_s.reshape(win_shape), vwin_s.reshape(win_shape),
            st_s.reshape(1, nb, GLA_HEADS, GLA_DK, GLA_DV))
```

```python
import functools
import math

import numpy as np
import jax
import jax.numpy as jnp
from jax import lax
from jax.experimental import pallas as pl
from jax.experimental.pallas import tpu as pltpu

F32 = jnp.float32
BF16 = jnp.bfloat16

D_MODEL = 2048
SEQ = 8192
DEC_BATCH = 128
DEC_SEQ = 4
PAST_LEN = 2048
PAGE_SIZE = 128
N_PAGES = PAST_LEN // PAGE_SIZE
HEAD_DIM = 128
NSA_HEADS = 8
NSA_KV_HEADS = 2
NSA_GROUP = 4
CMP_BLOCK = 32
CMP_STRIDE = 16
SEL_BLOCK = 64
SEL_TOPN = 16
WINDOW = 512
FORCED_SCORE = 1.0e4
GLA_HEADS = 4
GLA_DV = 256
GLA_DK = 128
GLA_RANK = 16
GLA_TAU = 16.0
GLA_CHUNK = 32
MEM_TOKENS = 256
MEM_HEADS = 4
MEM_HEAD_DIM = 128
D_FF = 5632
NUM_BUCKETS = 32
MAX_DISTANCE = 128
RMS_EPS = 1e-6

N_TOK = SEQ + DEC_BATCH * DEC_SEQ
Z_QN, Z_VG, Z_RG, Z_KV, Z_QG, Z_KG, Z_MISC = 0, 1024, 2048, 3072, 4608, 5120, 5632
Z_W = 5760
MISC_GATES, MISC_A = 0, 24

NEG = -1e30
M_INIT = -1e29

Q_TILE = 128
N_QT = SEQ // Q_TILE
N_CMP_PAD = SEQ // CMP_STRIDE
SLAB = N_CMP_PAD + 128
SLAB_OFF = SLAB - 16
KC_ROWS = SLAB_OFF + N_CMP_PAD + 16
N_SEL = SEQ // SEL_BLOCK
REL0 = N_SEL - 2

VMEM_LIMIT = 56 * 1024 * 1024


def _cparams(sem):
    return pltpu.CompilerParams(dimension_semantics=sem, vmem_limit_bytes=VMEM_LIMIT)


def _dot(a, b):
    return jnp.dot(a, b, preferred_element_type=F32)


def _dot_nt(a, b):
    return lax.dot_general(a, b, (((1,), (1,)), ((), ())), preferred_element_type=F32)


def _dot_tn(a, b):
    return lax.dot_general(a, b, (((0,), (0,)), ((), ())), preferred_element_type=F32)


def _rms(x, g):
    return x * lax.rsqrt(jnp.mean(x * x, axis=-1, keepdims=True) + RMS_EPS) * g


def _silu(x):
    return x * jax.nn.sigmoid(x)


def _ffn_kernel(x_ref, g_ref, wg_ref, wu_ref, wd_ref, gf_ref, o_ref, xn_ref, *, n_ff, final_norm):
    j = pl.program_id(1)

    @pl.when(j == 0)
    def _():
        xn_ref[...] = _rms(x_ref[...], g_ref[...]).astype(BF16)
        o_ref[...] = jnp.zeros_like(o_ref)

    xn = xn_ref[...]
    hid = _silu(_dot(xn, wg_ref[...])) * _dot(xn, wu_ref[...])
    o_ref[...] += _dot(hid.astype(BF16), wd_ref[...])

    @pl.when(j == n_ff - 1)
    def _():
        h = x_ref[...] + 0.5 * o_ref[...]
        o_ref[...] = _rms(h, gf_ref[...]) if final_norm else h


def _ffn(x, g, wg, wu, wd, gf, final_norm, tm=512, tf=512):
    n, d = x.shape
    n_ff = D_FF // tf
    return pl.pallas_call(
        functools.partial(_ffn_kernel, n_ff=n_ff, final_norm=final_norm),
        grid=(n // tm, n_ff),
        in_specs=[pl.BlockSpec((tm, d), lambda i, j: (i, 0)),
                  pl.BlockSpec((1, d), lambda i, j: (0, 0)),
                  pl.BlockSpec((d, tf), lambda i, j: (0, j)),
                  pl.BlockSpec((d, tf), lambda i, j: (0, j)),
                  pl.BlockSpec((tf, d), lambda i, j: (j, 0)),
                  pl.BlockSpec((1, d), lambda i, j: (0, 0))],
        out_specs=pl.BlockSpec((tm, d), lambda i, j: (i, 0)),
        out_shape=jax.ShapeDtypeStruct((n, d), F32),
        scratch_shapes=[pltpu.VMEM((tm, d), BF16)],
        compiler_params=_cparams(("parallel", "arbitrary")),
        name="ffn",
    )(x, g, wg, wu, wd, gf)


def _norm_matmul_kernel(x_ref, g_ref, w_ref, o_ref, xn_ref):
    @pl.when(pl.program_id(1) == 0)
    def _():
        xn_ref[...] = _rms(x_ref[...], g_ref[...]).astype(BF16)

    o_ref[...] = _dot(xn_ref[...], w_ref[...])


def _norm_matmul(x, g, w, tm, tn, name):
    n, d = x.shape
    dout = w.shape[1]
    return pl.pallas_call(
        _norm_matmul_kernel,
        grid=(n // tm, dout // tn),
        in_specs=[pl.BlockSpec((tm, d), lambda i, j: (i, 0)),
                  pl.BlockSpec((1, d), lambda i, j: (0, 0)),
                  pl.BlockSpec((d, tn), lambda i, j: (0, j))],
        out_specs=pl.BlockSpec((tm, tn), lambda i, j: (i, j)),
        out_shape=jax.ShapeDtypeStruct((n, dout), F32),
        scratch_shapes=[pltpu.VMEM((tm, d), BF16)],
        compiler_params=_cparams(("parallel", "arbitrary")),
        name=name,
    )(x, g, w)


def _matmul_res_kernel(*refs, n_lhs):
    res_ref = refs[0]
    o_ref = refs[1 + 2 * n_lhs]
    acc = res_ref[...]
    for i in range(n_lhs):
        acc = acc + _dot(refs[1 + i][...].astype(BF16), refs[1 + n_lhs + i][...])
    o_ref[...] = acc


def _matmul_res(res, lhs, ws, tm, name):
    n, d = res.shape
    n_lhs = len(lhs)
    return pl.pallas_call(
        functools.partial(_matmul_res_kernel, n_lhs=n_lhs),
        grid=(n // tm,),
        in_specs=([pl.BlockSpec((tm, d), lambda i: (i, 0))]
                  + [pl.BlockSpec((tm, a.shape[1]), lambda i: (i, 0)) for a in lhs]
                  + [pl.BlockSpec(w.shape, lambda i: (0, 0)) for w in ws]),
        out_specs=pl.BlockSpec((tm, d), lambda i: (i, 0)),
        out_shape=jax.ShapeDtypeStruct((n, d), F32),
        compiler_params=_cparams(("parallel",)),
        name=name,
    )(res, *lhs, *ws)


def _rel_bucket_np(dist):
    n = np.maximum(dist, 0)
    exact = NUM_BUCKETS // 2
    nf = np.maximum(n, 1).astype(np.float32)
    large = exact + (np.log(nf / np.float32(exact)) / np.float32(math.log(MAX_DISTANCE / exact))
                     * np.float32(NUM_BUCKETS - exact)).astype(np.int32)
    return np.where(n < exact, n, np.minimum(large, NUM_BUCKETS - 1)).astype(np.int32)


def _bucket_or_masked(dist, valid):
    return np.where(valid, _rel_bucket_np(dist), -1).astype(np.int32)


def _prompt_bucket_table():
    i = np.arange(Q_TILE)[:, None]
    j = np.arange(Q_TILE)[None, :]
    u = np.arange(SLAB)[None, :]
    dist_c = i - CMP_STRIDE * u + (CMP_STRIDE * SLAB_OFF - (CMP_BLOCK - 1))
    diag = _bucket_or_masked(i - j, i - j >= 0)
    prev = _bucket_or_masked(Q_TILE + i - j, np.ones((Q_TILE, Q_TILE), bool))
    first = _bucket_or_masked(WINDOW + i - j, j > i)
    far = np.full((Q_TILE, Q_TILE), NUM_BUCKETS - 1, np.int32)
    cmp_ = _bucket_or_masked(dist_c, dist_c >= 0)
    return np.concatenate([diag, prev, first, far, cmp_], axis=1)


S_CMP = PAST_LEN // CMP_STRIDE
S_CMP_COLS = NSA_KV_HEADS * S_CMP
S_NEW_COLS = 128
S_SEL_COLS = NSA_KV_HEADS * PAST_LEN + S_NEW_COLS
S_WIN_COLS = NSA_KV_HEADS * WINDOW


def _sample_bucket_table():
    i = np.arange(DEC_SEQ)[:, None]
    pos = PAST_LEN + i
    rows = []
    for h in range(NSA_KV_HEADS):
        col = np.arange(S_CMP_COLS)[None, :]
        c = col % S_CMP
        dist_c = pos - (c * CMP_STRIDE + CMP_BLOCK - 1)
        cmp_ = _bucket_or_masked(dist_c, (col // S_CMP == h) & (c < S_CMP - 1) & (dist_c >= 0))
        col = np.arange(S_SEL_COLS)[None, :]
        key = col // NSA_KV_HEADS
        sel = _bucket_or_masked(pos - key, (col % NSA_KV_HEADS == h) & (key <= pos))
        col = np.arange(S_WIN_COLS)[None, :]
        dist_w = pos - (PAST_LEN - WINDOW + col // NSA_KV_HEADS)
        win = _bucket_or_masked(dist_w, (col % NSA_KV_HEADS == h) & (dist_w < WINDOW))
        rows.append(np.concatenate([cmp_, sel, win], axis=1))
    return np.concatenate(rows, axis=0)


def _bias_table_kernel(tab_ref, idx_ref, o_ref):
    h = pl.program_id(0)
    idx = idx_ref[...]
    out = jnp.full(idx.shape, NEG, F32)
    for b in range(NUM_BUCKETS):
        out = jnp.where(idx == b, tab_ref[b, h], out)
    o_ref[0] = out


def _bias_tables(rel_bias, idx, name):
    r, c = idx.shape
    return pl.pallas_call(
        _bias_table_kernel,
        grid=(NSA_HEADS,),
        in_specs=[pl.BlockSpec(memory_space=pltpu.SMEM),
                  pl.BlockSpec((r, c), lambda h: (0, 0))],
        out_specs=pl.BlockSpec((1, r, c), lambda h: (h, 0, 0)),
        out_shape=jax.ShapeDtypeStruct((NSA_HEADS, r, c), F32),
        compiler_params=_cparams(("arbitrary",)),
        name=name,
    )(rel_bias, jnp.asarray(idx))


def _cover_np(n_cmp_cols, n_blk_cols, delta_of):
    u = np.arange(n_cmp_cols)[:, None]
    j = np.arange(n_blk_cols)[None, :]
    delta = delta_of(u, j)
    shared = np.minimum(CMP_STRIDE * delta + CMP_BLOCK, SEL_BLOCK) - np.maximum(CMP_STRIDE * delta, 0)
    return (np.maximum(shared, 0) / CMP_STRIDE).astype(np.float32)


def _compress(load_rows, n_chunk, pos_ref, w1_ref, w2_ref):
    first = jnp.zeros((n_chunk, HEAD_DIM), F32)
    second = jnp.zeros((n_chunk, HEAD_DIM), F32)
    for s in range(CMP_STRIDE):
        r = load_rows(s)
        first = first + _dot((r + pos_ref[s:s + 1, :]).astype(BF16), w1_ref[s])
        second = second + _dot((r + pos_ref[CMP_STRIDE + s:CMP_STRIDE + s + 1, :]).astype(BF16),
                               w1_ref[CMP_STRIDE + s])
    nxt = pltpu.roll(second, n_chunk - 1, axis=0)
    hid = _silu(first + nxt)
    return _dot(hid.astype(BF16), w2_ref[...])


def _flash_step(carry, s, v):
    m, l, acc = carry
    m_new = jnp.maximum(m, jnp.max(s, axis=-1, keepdims=True))
    alpha = jnp.exp(m - m_new)
    e = jnp.exp(s - m_new)
    l = alpha * l + jnp.sum(e, axis=-1, keepdims=True)
    acc = alpha * acc + _dot(e.astype(BF16), v)
    return m_new, l, acc


def _flash_init(rows):
    return (jnp.full((rows, 1), M_INIT, F32), jnp.zeros((rows, 1), F32), jnp.zeros((rows, HEAD_DIM), F32))


def _masked_softmax(s, valid):
    s = jnp.where(valid, s, NEG)
    m = jnp.max(s, axis=-1, keepdims=True)
    e = jnp.where(valid, jnp.exp(s - m), 0.0)
    return e / jnp.maximum(jnp.sum(e, axis=-1, keepdims=True), 1e-30)


def _split_dot(x, w):
    hi = x.astype(BF16)
    lo = (x - hi.astype(F32)).astype(BF16)
    return _dot(hi, w) + _dot(lo, w)


def _top_n_mask(score, index, axis):
    sel = jnp.zeros(score.shape, F32)
    for _ in range(SEL_TOPN):
        mx = jnp.max(score, axis=axis, keepdims=True)
        first = jnp.min(jnp.where(score == mx, index, 1e9), axis=axis, keepdims=True)
        hit = index == first
        sel = jnp.where(hit, 1.0, sel)
        score = jnp.where(hit, -jnp.inf, score)
    return sel


def _top_n_mask_by_rank(score, n_cand):
    lane = lax.broadcasted_iota(jnp.int32, score.shape, 1)
    rank = jnp.zeros(score.shape, F32)
    for j in range(n_cand):
        col = score[:, j:j + 1]
        ahead = (col > score) | ((col == score) & (lane > j))
        rank = rank + ahead.astype(F32)
    return ((rank < SEL_TOPN) & (lane < n_cand)).astype(F32)


def _compress_prompt_kernel(rows_ref, pos_ref, w1_ref, w2_ref, o_ref):
    out = _compress(lambda s: rows_ref[pl.ds(s, N_CMP_PAD, stride=CMP_STRIDE), :], N_CMP_PAD,
                    pos_ref.at[0], w1_ref.at[0], w2_ref.at[0])
    real = lax.broadcasted_iota(jnp.int32, (N_CMP_PAD, HEAD_DIM), 0) < N_CMP_PAD - 1
    o_ref[0, 0, 0:SLAB_OFF, :] = jnp.zeros((SLAB_OFF, HEAD_DIM), F32)
    o_ref[0, 0, SLAB_OFF:SLAB_OFF + N_CMP_PAD, :] = jnp.where(real, out, 0.0)
    o_ref[0, 0, SLAB_OFF + N_CMP_PAD:KC_ROWS, :] = jnp.zeros((KC_ROWS - SLAB_OFF - N_CMP_PAD, HEAD_DIM), F32)


def _compress_prompt(z, pos, w1, w2):
    kv_blk = Z_KV // HEAD_DIM
    return pl.pallas_call(
        _compress_prompt_kernel,
        grid=(2, NSA_KV_HEADS),
        in_specs=[pl.BlockSpec((SEQ, HEAD_DIM), lambda i, h: (0, kv_blk + NSA_KV_HEADS * i + h)),
                  pl.BlockSpec((1, CMP_BLOCK, HEAD_DIM), lambda i, h: (i, 0, 0)),
                  pl.BlockSpec((1, CMP_BLOCK, HEAD_DIM, HEAD_DIM), lambda i, h: (i, 0, 0, 0)),
                  pl.BlockSpec((1, HEAD_DIM, HEAD_DIM), lambda i, h: (i, 0, 0))],
        out_specs=pl.BlockSpec((1, 1, KC_ROWS, HEAD_DIM), lambda i, h: (i, h, 0, 0)),
        out_shape=jax.ShapeDtypeStruct((2, NSA_KV_HEADS, KC_ROWS, HEAD_DIM), F32),
        compiler_params=_cparams(("parallel", "parallel")),
        name="compress_prompt",
    )(z, pos, w1, w2)


def _nsa_prompt_kernel(q_ref, misc_ref, ksel_ref, vsel_ref, kwin_ref, vwin_ref, kc_ref, vc_ref,
                       bias_ref, cover_ref, o_ref):
    kvh = pl.program_id(0)
    t = pl.program_id(1)
    rows = NSA_GROUP * Q_TILE
    q = q_ref[...] * (HEAD_DIM ** -0.5)
    qs = jnp.concatenate([q[:, g * HEAD_DIM:(g + 1) * HEAD_DIM] for g in range(NSA_GROUP)], axis=0).astype(BF16)

    def bias_tile(k):
        return bias_ref[0, :, k * Q_TILE:(k + 1) * Q_TILE]
    b_diag, b_prev, b_first, b_far = bias_tile(0), bias_tile(1), bias_tile(2), bias_tile(3)

    start = pl.multiple_of(t * (Q_TILE // CMP_STRIDE), 8)
    kslab = kc_ref[0, 0, pl.ds(start, SLAB), :].astype(BF16)
    vslab = vc_ref[0, 0, pl.ds(start, SLAB), :].astype(BF16)
    b_cmp = bias_ref[0, :, 4 * Q_TILE:4 * Q_TILE + SLAB]
    u = lax.broadcasted_iota(jnp.int32, (1, SLAB), 1)
    valid = (b_cmp > M_INIT) & (u >= SLAB_OFF - (Q_TILE // CMP_STRIDE) * t)
    p_cmp = _masked_softmax(_dot_nt(qs, kslab) + b_cmp, valid)
    o_cmp = _dot(p_cmp.astype(BF16), vslab)

    p_sum = p_cmp[0:Q_TILE]
    for g in range(1, NSA_GROUP):
        p_sum = p_sum + p_cmp[g * Q_TILE:(g + 1) * Q_TILE]
    imp = _split_dot(p_sum, cover_ref[...]).T
    jr = lax.broadcasted_iota(jnp.int32, (N_SEL, Q_TILE), 0)
    qi = lax.broadcasted_iota(jnp.int32, (N_SEL, Q_TILE), 1)
    cur = REL0 + (qi >= SEL_BLOCK).astype(jnp.int32)
    first_blk = REL0 - 2 * t
    forced = (jr == first_blk) | (jr == cur) | (jr == cur - 1)
    in_range = (jr <= cur) & (jr >= first_blk)
    score = jnp.where(in_range, jnp.where(forced, FORCED_SCORE, imp), NEG)
    sel = _top_n_mask(score, jr.astype(F32), 0).T.astype(BF16)

    def sel_mask(kt):
        blk = lax.broadcasted_iota(jnp.int32, (N_SEL, Q_TILE), 0)
        key = lax.broadcasted_iota(jnp.int32, (N_SEL, Q_TILE), 1)
        expand = (blk == 2 * (kt - t) + REL0 + (key >= SEL_BLOCK).astype(jnp.int32)).astype(BF16)
        add = (_dot(sel, expand) - 1.0) * (-NEG)
        return jnp.concatenate([add] * NSA_GROUP, axis=0)

    def key_tile(ref, kt):
        return ref[pl.ds(pl.multiple_of(kt * Q_TILE, Q_TILE), Q_TILE), :]

    def far_body(kt, carry):
        s = _dot_nt(qs, key_tile(ksel_ref, kt)) + b_far + sel_mask(kt)
        return _flash_step(carry, s, key_tile(vsel_ref, kt))

    carry = lax.fori_loop(0, jnp.maximum(t - 1, 0), far_body, _flash_init(rows))
    kp = jnp.maximum(t - 1, 0)
    pen = jnp.where(t >= 1, 0.0, NEG)
    s = _dot_nt(qs, key_tile(ksel_ref, kp)) + b_prev + sel_mask(t - 1) + pen
    carry = _flash_step(carry, s, key_tile(vsel_ref, kp))
    s = _dot_nt(qs, key_tile(ksel_ref, t)) + b_diag + sel_mask(t)
    _, l, acc = _flash_step(carry, s, key_tile(vsel_ref, t))
    o_sel = acc / l

    carry = _flash_init(rows)
    n_wt = WINDOW // Q_TILE + 1
    for w, b_tile in enumerate((b_first,) + (b_far,) * (n_wt - 3) + (b_prev, b_diag)):
        kt = t - (n_wt - 1) + w
        kc = jnp.maximum(kt, 0)
        s = _dot_nt(qs, key_tile(kwin_ref, kc)) + b_tile + jnp.where(kt >= 0, 0.0, NEG)
        carry = _flash_step(carry, s, key_tile(vwin_ref, kc))
    _, l, acc = carry
    o_win = acc / l

    gates = jax.nn.sigmoid(misc_ref[...])
    gk = jnp.where(kvh == 0, gates[:, MISC_GATES:MISC_GATES + 3 * NSA_GROUP],
                   gates[:, MISC_GATES + 3 * NSA_GROUP:MISC_GATES + 6 * NSA_GROUP])
    for g in range(NSA_GROUP):
        r = slice(g * Q_TILE, (g + 1) * Q_TILE)
        o_ref[:, g * HEAD_DIM:(g + 1) * HEAD_DIM] = (gk[:, 3 * g:3 * g + 1] * o_cmp[r]
                                                     + gk[:, 3 * g + 1:3 * g + 2] * o_sel[r]
                                                     + gk[:, 3 * g + 2:3 * g + 3] * o_win[r])


def _nsa_prompt(z, kvb, kcp, bias, cover):
    gw = NSA_GROUP * HEAD_DIM

    def kv_spec(j):
        return pl.BlockSpec((SEQ, HEAD_DIM), lambda h, t, j=j: (0, 2 * j + h))
    return pl.pallas_call(
        _nsa_prompt_kernel,
        grid=(NSA_KV_HEADS, N_QT),
        in_specs=[pl.BlockSpec((Q_TILE, gw), lambda h, t: (t, h)),
                  pl.BlockSpec((Q_TILE, 128), lambda h, t: (t, Z_MISC // 128)),
                  kv_spec(2), kv_spec(3), kv_spec(4), kv_spec(5),
                  pl.BlockSpec((1, 1, KC_ROWS, HEAD_DIM), lambda h, t: (0, h, 0, 0)),
                  pl.BlockSpec((1, 1, KC_ROWS, HEAD_DIM), lambda h, t: (1, h, 0, 0)),
                  pl.BlockSpec((1,) + bias.shape[1:], lambda h, t: (h, 0, 0)),
                  pl.BlockSpec(cover.shape, lambda h, t: (0, 0))],
        out_specs=pl.BlockSpec((Q_TILE, gw), lambda h, t: (t, h)),
        out_shape=jax.ShapeDtypeStruct((SEQ, NSA_HEADS * HEAD_DIM), F32),
        compiler_params=_cparams(("arbitrary", "arbitrary")),
        name="nsa_prompt",
    )(z, z, kvb, kvb, kvb, kvb, kcp, kcp, bias, cover)


S_ROWS = NSA_HEADS * DEC_SEQ
PAGE_ROWS = NSA_KV_HEADS * PAGE_SIZE
NEW_ROWS = NSA_KV_HEADS * DEC_SEQ


def _nsa_sample_kernel(pt_ref, *refs):
    del pt_ref
    n_pg = N_PAGES
    q_ref, gate_ref, new_ref, kwin_ref, vwin_ref = refs[0:5]
    pages = [refs[5 + k * n_pg:5 + (k + 1) * n_pg] for k in range(4)]
    (posk_ref, w1k_ref, w2k_ref, posv_ref, w1v_ref, w2v_ref,
     bias_ref, cover_ref, expand_ref) = refs[5 + 4 * n_pg:14 + 4 * n_pg]
    o_ref, kwin_o_ref, vwin_o_ref = refs[14 + 4 * n_pg:17 + 4 * n_pg]
    past_ref, newk_ref, newv_ref = refs[17 + 4 * n_pg:20 + 4 * n_pg]

    comp = []
    for k, (pos_ref, w1_ref, w2_ref) in enumerate(((posk_ref, w1k_ref, w2k_ref), (posv_ref, w1v_ref, w2v_ref))):
        for p in range(n_pg):
            past_ref[p * PAGE_ROWS:(p + 1) * PAGE_ROWS, :] = pages[k][p][...]

        def load_rows(s):
            return jnp.concatenate(
                [past_ref[pl.ds(NSA_KV_HEADS * s + h, S_CMP, stride=NSA_KV_HEADS * CMP_STRIDE), :]
                 for h in range(NSA_KV_HEADS)], axis=0)
        comp.append(_compress(load_rows, S_CMP_COLS, pos_ref, w1_ref, w2_ref).astype(BF16))

    keep = NSA_KV_HEADS * WINDOW - NEW_ROWS
    kwin_o_ref[0:keep, :] = kwin_ref[NEW_ROWS:NSA_KV_HEADS * WINDOW, :]
    kwin_o_ref[keep:keep + NEW_ROWS, :] = new_ref[0, 4]
    vwin_o_ref[0:keep, :] = vwin_ref[NEW_ROWS:NSA_KV_HEADS * WINDOW, :]
    vwin_o_ref[keep:keep + NEW_ROWS, :] = new_ref[0, 5]

    qs = (q_ref[0] * (HEAD_DIM ** -0.5)).astype(BF16)
    o0, o1 = S_CMP_COLS, S_CMP_COLS + S_SEL_COLS
    b_cmp = bias_ref[:, 0:o0]
    b_new = bias_ref[:, o1 - S_NEW_COLS:o1]

    def attend(scores, bias, values):
        s = jnp.concatenate(scores, axis=1) + bias
        e = jnp.exp(s - jnp.max(s, axis=-1, keepdims=True))
        acc = jnp.zeros((S_ROWS, HEAD_DIM), F32)
        c0 = 0
        for v in values:
            acc = acc + _dot(e[:, c0:c0 + v.shape[0]].astype(BF16), v)
            c0 += v.shape[0]
        return acc / jnp.sum(e, axis=-1, keepdims=True)

    p_cmp = _masked_softmax(_dot_nt(qs, comp[0]) + b_cmp, b_cmp > M_INIT)
    o_cmp = _dot(p_cmp.astype(BF16), comp[1])

    imp = _split_dot(p_cmp, cover_ref[...])
    imp = imp + pltpu.roll(imp, 8, axis=0) + pltpu.roll(imp, 16, axis=0) + pltpu.roll(imp, 24, axis=0)
    blk = lax.broadcasted_iota(jnp.int32, (S_ROWS, 128), 1)
    cur = PAST_LEN // SEL_BLOCK
    forced = (blk == 0) | (blk == cur) | (blk == cur - 1)
    score = jnp.where(blk <= cur, jnp.where(forced, FORCED_SCORE, imp), NEG)
    sel = _top_n_mask_by_rank(score, cur + 1).astype(BF16)
    mask_add = (_dot(sel, expand_ref[...]) - 1.0) * (-NEG)

    def new_tile(k_ref, v_ref, jk, jv):
        k_ref[...] = jnp.zeros_like(k_ref)
        v_ref[...] = jnp.zeros_like(v_ref)
        k_ref[0:NEW_ROWS, :] = new_ref[0, jk]
        v_ref[0:NEW_ROWS, :] = new_ref[0, jv]
        return k_ref[...].astype(BF16), v_ref[...].astype(BF16)

    nk, nv = new_tile(newk_ref, newv_ref, 2, 3)
    scores = [_dot_nt(qs, pages[2][p][...].astype(BF16)) for p in range(n_pg)] + [_dot_nt(qs, nk)]
    values = [pages[3][p][...].astype(BF16) for p in range(n_pg)] + [nv]
    o_sel = attend(scores, bias_ref[:, o0:o1] + mask_add, values)

    nk, nv = new_tile(newk_ref, newv_ref, 4, 5)
    scores = [_dot_nt(qs, kwin_ref[...].astype(BF16)), _dot_nt(qs, nk)]
    bias_w = jnp.concatenate([bias_ref[:, o1:o1 + S_WIN_COLS], b_new], axis=1)
    o_win = attend(scores, bias_w, [vwin_ref[...].astype(BF16), nv])

    g = jax.nn.sigmoid(gate_ref[0])
    o_ref[0] = g[:, 0:1] * o_cmp + g[:, 1:2] * o_sel + g[:, 2:3] * o_win


def _nsa_sample(page_table, q_s, gate_s, new_s, kwin, vwin, pools, cmp_w, bias, cover, expand):
    win_rows = NSA_KV_HEADS * WINDOW

    def full(a):
        return pl.BlockSpec(a.shape, lambda b, pt, n=a.ndim: (0,) * n)

    def per_b(a):
        return pl.BlockSpec((1,) + a.shape[1:], lambda b, pt, n=a.ndim: (b,) + (0,) * (n - 1))
    win_spec = pl.BlockSpec((win_rows, HEAD_DIM), lambda b, pt: (b, 0))
    page_specs = [pl.BlockSpec((PAGE_ROWS, HEAD_DIM), lambda b, pt, p=p: (pt[b, p], 0))
                  for _ in range(4) for p in range(N_PAGES)]
    page_args = [pool for pool in pools for _ in range(N_PAGES)]
    consts = list(cmp_w) + [bias, cover, expand]
    grid_spec = pltpu.PrefetchScalarGridSpec(
        num_scalar_prefetch=1,
        grid=(DEC_BATCH,),
        in_specs=[per_b(q_s), per_b(gate_s), per_b(new_s), win_spec, win_spec]
        + page_specs + [full(a) for a in consts],
        out_specs=[per_b(q_s), win_spec, win_spec],
        scratch_shapes=[pltpu.VMEM((NSA_KV_HEADS * PAST_LEN, HEAD_DIM), F32),
                        pltpu.VMEM((S_NEW_COLS, HEAD_DIM), F32),
                        pltpu.VMEM((S_NEW_COLS, HEAD_DIM), F32)],
    )
    return pl.pallas_call(
        _nsa_sample_kernel,
        grid_spec=grid_spec,
        out_shape=[jax.ShapeDtypeStruct(q_s.shape, F32),
                   jax.ShapeDtypeStruct(kwin.shape, F32),
                   jax.ShapeDtypeStruct(vwin.shape, F32)],
        compiler_params=_cparams(("arbitrary",)),
        name="nsa_sample",
    )(page_table, q_s, gate_s, new_s, kwin, vwin, *page_args, *consts)


def _log_decay(a_blk, wa_ref, ba_ref):
    x = _dot(a_blk.astype(BF16), wa_ref[...]) + ba_ref[...]
    return (jnp.minimum(x, 0.0) - jnp.log1p(jnp.exp(-jnp.abs(x)))) * (1.0 / GLA_TAU)


def _segment_cumsum(g, seg):
    pos = lax.broadcasted_iota(jnp.int32, g.shape, 0) % seg
    cum = g
    sh = 1
    while sh < seg:
        cum = cum + jnp.where(pos >= sh, pltpu.roll(cum, sh, axis=0), 0.0)
        sh *= 2
    return cum


def _gla_prompt_kernel(q_ref, k_ref, v_ref, r_ref, a_ref, wa_ref, ba_ref, gn_ref, o_ref, st_o_ref, st_ref,
                       *, n_blk, tb):
    tbi = pl.program_id(1)

    @pl.when(tbi == 0)
    def _():
        st_ref[...] = jnp.zeros_like(st_ref)

    c = GLA_CHUNK
    cum = _segment_cumsum(_log_decay(a_ref[...], wa_ref, ba_ref), c)
    q = q_ref[...] * (GLA_DK ** -0.5)
    k = k_ref[...]
    v = v_ref[...].astype(BF16)
    qe = (q * jnp.exp(cum)).astype(BF16)
    kd = (k * jnp.exp(-cum)).astype(BF16)
    tril = lax.broadcasted_iota(jnp.int32, (c, c), 0) >= lax.broadcasted_iota(jnp.int32, (c, c), 1)
    st = st_ref[...]
    outs = []
    for ci in range(tb // c):
        r = slice(ci * c, (ci + 1) * c)
        last = cum[ci * c + c - 1:ci * c + c, :]
        att = jnp.where(tril, _dot_nt(qe[r], kd[r]), 0.0)
        outs.append(_dot_nt(qe[r], st.astype(BF16)) + _dot(att.astype(BF16), v[r]))
        kl = (k[r] * jnp.exp(last - cum[r])).astype(BF16)
        st = jnp.exp(last) * st + _dot_tn(v[r], kl)
    st_ref[...] = st
    o = jnp.concatenate(outs, axis=0)
    o_ref[...] = _rms(o, gn_ref[...]) * _silu(r_ref[...])

    @pl.when(tbi == n_blk - 1)
    def _():
        st_o_ref[0] = st


def _gla_prompt(z, wa, ba, gn, tb=256):
    n_blk = SEQ // tb
    return pl.pallas_call(
        functools.partial(_gla_prompt_kernel, n_blk=n_blk, tb=tb),
        grid=(GLA_HEADS, n_blk),
        in_specs=[pl.BlockSpec((tb, GLA_DK), lambda h, i: (i, Z_QG // GLA_DK + h)),
                  pl.BlockSpec((tb, GLA_DK), lambda h, i: (i, Z_KG // GLA_DK + h)),
                  pl.BlockSpec((tb, GLA_DV), lambda h, i: (i, Z_VG // GLA_DV + h)),
                  pl.BlockSpec((tb, GLA_DV), lambda h, i: (i, Z_RG // GLA_DV + h)),
                  pl.BlockSpec((tb, 128), lambda h, i: (i, Z_MISC // 128)),
                  pl.BlockSpec((128, GLA_DK), lambda h, i: (0, h)),
                  pl.BlockSpec((1, GLA_DK), lambda h, i: (0, h)),
                  pl.BlockSpec((1, GLA_DV), lambda h, i: (0, 0))],
        out_specs=[pl.BlockSpec((tb, GLA_DV), lambda h, i: (i, h)),
                   pl.BlockSpec((1, GLA_DV, GLA_DK), lambda h, i: (h, 0, 0))],
        out_shape=[jax.ShapeDtypeStruct((SEQ, GLA_HEADS * GLA_DV), F32),
                   jax.ShapeDtypeStruct((GLA_HEADS, GLA_DV, GLA_DK), F32)],
        scratch_shapes=[pltpu.VMEM((GLA_DV, GLA_DK), F32)],
        compiler_params=_cparams(("parallel", "arbitrary")),
        name="gla_prompt",
    )(z, z, z, z, z, wa, ba, gn)


GS_B = 4


def _gla_sample_kernel(q_ref, k_ref, v_ref, r_ref, a_ref, wa_ref, ba_ref, gn_ref, s_ref, o_ref, s_o_ref):
    rows = GS_B * DEC_SEQ
    a = a_ref[...]
    ri = lax.broadcasted_iota(jnp.int32, (rows, rows), 0)
    ci = lax.broadcasted_iota(jnp.int32, (rows, rows), 1)
    same_causal = (ri // DEC_SEQ == ci // DEC_SEQ) & (ri >= ci)
    row_b = lax.broadcasted_iota(jnp.int32, (rows, 1), 0) // DEC_SEQ
    ones = jnp.ones((rows, 128), BF16)
    for h in range(GLA_HEADS):
        dk = slice(h * GLA_DK, (h + 1) * GLA_DK)
        dv = slice(h * GLA_DV, (h + 1) * GLA_DV)
        g = _log_decay(a, wa_ref.at[:, dk], ba_ref.at[:, dk])
        cum = _segment_cumsum(g, DEC_SEQ)
        q = q_ref[:, dk] * (GLA_DK ** -0.5)
        k = k_ref[:, dk]
        v = v_ref[:, dv].astype(BF16)
        qe = (q * jnp.exp(cum)).astype(BF16)
        kd = (k * jnp.exp(-cum)).astype(BF16)
        att = jnp.where(same_causal, _dot_nt(qe, kd), 0.0)
        o = _dot(att.astype(BF16), v)
        for b in range(GS_B):
            mine = row_b == b
            last = cum[b * DEC_SEQ + DEC_SEQ - 1:(b + 1) * DEC_SEQ, :]
            s = s_ref[b, h]
            o = o + jnp.where(mine, _dot(qe, s.astype(BF16)), 0.0)
            kl = jnp.where(mine, k * jnp.exp(last - cum), 0.0)
            hi = jnp.where(mine, g, 0.0).astype(BF16)
            lo = (jnp.where(mine, g, 0.0) - hi.astype(F32)).astype(BF16)
            last_col = (_dot_tn(hi, ones) + _dot_tn(lo, ones))[:, 0:1]
            s_o_ref[b, h] = jnp.exp(last_col) * s + _dot_tn(kl.astype(BF16), v)
        o_ref[:, dv] = _rms(o, gn_ref[...]) * _silu(r_ref[:, dv])


def _gla_sample(zs, state, wa, ba, gn):
    rows = GS_B * DEC_SEQ
    n = DEC_BATCH * DEC_SEQ
    hk, hv = GLA_HEADS * GLA_DK, GLA_HEADS * GLA_DV
    st_spec = pl.BlockSpec((GS_B, GLA_HEADS, GLA_DK, GLA_DV), lambda i: (i, 0, 0, 0))
    return pl.pallas_call(
        _gla_sample_kernel,
        grid=(DEC_BATCH // GS_B,),
        in_specs=[pl.BlockSpec((rows, hk), lambda i: (i, Z_QG // hk)),
                  pl.BlockSpec((rows, hk), lambda i: (i, Z_KG // hk)),
                  pl.BlockSpec((rows, hv), lambda i: (i, Z_VG // hv)),
                  pl.BlockSpec((rows, hv), lambda i: (i, Z_RG // hv)),
                  pl.BlockSpec((rows, 128), lambda i: (i, Z_MISC // 128)),
                  pl.BlockSpec((128, hk), lambda i: (0, 0)),
                  pl.BlockSpec((1, hk), lambda i: (0, 0)),
                  pl.BlockSpec((1, GLA_DV), lambda i: (0, 0)),
                  st_spec],
        out_specs=[pl.BlockSpec((rows, hv), lambda i: (i, 0)), st_spec],
        out_shape=[jax.ShapeDtypeStruct((n, hv), F32), jax.ShapeDtypeStruct(state.shape, F32)],
        compiler_params=_cparams(("parallel",)),
        name="gla_sample",
    )(zs, zs, zs, zs, zs, wa, ba, gn, state)


def _softmax_rows(s):
    m = jnp.max(s, axis=-1, keepdims=True)
    e = jnp.exp(s - m)
    return e / jnp.sum(e, axis=-1, keepdims=True)


def _mem_prompt_kernel(q_ref, k_ref, v_ref, o_ref):
    for h in range(MEM_HEADS):
        d = slice(h * MEM_HEAD_DIM, (h + 1) * MEM_HEAD_DIM)
        q = (q_ref[:, d] * (MEM_HEAD_DIM ** -0.5)).astype(BF16)
        p = _softmax_rows(_dot_nt(q, k_ref[:, d].astype(BF16)))
        o_ref[:, d] = _dot(p.astype(BF16), v_ref[:, d].astype(BF16))


def _mem_prompt(qm, memkv, tq=256):
    w = MEM_HEADS * MEM_HEAD_DIM
    return pl.pallas_call(
        _mem_prompt_kernel,
        grid=(SEQ // tq,),
        in_specs=[pl.BlockSpec((tq, w), lambda i: (i, 0)),
                  pl.BlockSpec((MEM_TOKENS, w), lambda i: (0, 0)),
                  pl.BlockSpec((MEM_TOKENS, w), lambda i: (0, 1))],
        out_specs=pl.BlockSpec((tq, w), lambda i: (i, 0)),
        out_shape=jax.ShapeDtypeStruct((SEQ, w), F32),
        compiler_params=_cparams(("parallel",)),
        name="mem_prompt",
    )(qm, memkv, memkv)


def _mem_sample_kernel(q_ref, k_ref, v_ref, o_ref):
    rows = MEM_HEADS * DEC_SEQ
    cols = MEM_HEADS * MEM_TOKENS
    q = (q_ref[0] * (MEM_HEAD_DIM ** -0.5)).astype(BF16)
    row_h = lax.broadcasted_iota(jnp.int32, (rows, cols), 0) // DEC_SEQ
    col_h = lax.broadcasted_iota(jnp.int32, (rows, cols), 1) % MEM_HEADS
    s = jnp.where(row_h == col_h, _dot_nt(q, k_ref[...].astype(BF16)), NEG)
    o_ref[0] = _dot(_softmax_rows(s).astype(BF16), v_ref[...].astype(BF16))


def _mem_sample(q_s, k_mem, v_mem):
    rows = MEM_HEADS * DEC_SEQ
    kv_spec = pl.BlockSpec((MEM_HEADS * MEM_TOKENS, MEM_HEAD_DIM), lambda b: (b, 0))
    return pl.pallas_call(
        _mem_sample_kernel,
        grid=(DEC_BATCH,),
        in_specs=[pl.BlockSpec((1, rows, MEM_HEAD_DIM), lambda b: (b, 0, 0)), kv_spec, kv_spec],
        out_specs=pl.BlockSpec((1, rows, MEM_HEAD_DIM), lambda b: (b, 0, 0)),
        out_shape=jax.ShapeDtypeStruct((DEC_BATCH, rows, MEM_HEAD_DIM), F32),
        compiler_params=_cparams(("parallel",)),
        name="mem_sample",
    )(q_s, k_mem, v_mem)


def _permute_w_in(w_in):
    qn, kv, gt, qg, kg, vg, rg, ag = jnp.split(
        w_in, np.cumsum([1024, 1536, 24, 512, 512, 1024, 1024, 16])[:-1].tolist(), axis=1)
    pad = jnp.zeros((w_in.shape[0], Z_W - Z_MISC - 40), w_in.dtype)
    return jnp.concatenate([qn, vg, rg, kv, qg, kg, gt, ag, pad], axis=1)


def kernel(x_prompt, x_sample, mem_prompt, cache_k_cmp, cache_v_cmp, cache_k_sel, cache_v_sel, cache_k_win,
           cache_v_win, state_gla, cache_k_mem, cache_v_mem, page_table, norm_ffn1, ffn1_w_gate, ffn1_w_up,
           ffn1_w_down, norm_mix, w_in, w_out, cmp_pos_k, cmp_w1_k, cmp_w2_k, cmp_pos_v, cmp_w1_v, cmp_w2_v,
           rel_bias, gla_w_a2, gla_b_a, gla_norm, norm_mem, norm_mem_src, w_mem_q, w_mem_k, w_mem_v, w_mem_o,
           norm_ffn2, ffn2_w_gate, ffn2_w_up, ffn2_w_down, norm_final):
    bf = lambda a: a.astype(BF16)
    row = lambda a: a.reshape(1, -1)
    nb, ns = DEC_BATCH, DEC_SEQ
    kvw = NSA_KV_HEADS * HEAD_DIM

    x = jnp.concatenate([x_prompt[0], x_sample.reshape(nb * ns, D_MODEL)], axis=0)
    ones = jnp.ones((1, D_MODEL), F32)
    h1 = _ffn(x, row(norm_ffn1[0]), bf(ffn1_w_gate[0]), bf(ffn1_w_up[0]), bf(ffn1_w_down[0]), ones, False)
    z = _norm_matmul(h1, row(norm_mix[0]), bf(_permute_w_in(w_in[0])), 512, Z_W // 5, "proj_in")

    kv_p = z[:SEQ, Z_KV:Z_QG]
    kv_s = z[SEQ:, Z_KV:Z_QG].reshape(nb, ns, 6 * kvw)
    rows_p = [kv_p[:, j * kvw:(j + 1) * kvw].reshape(1, 1, SEQ, NSA_KV_HEADS, HEAD_DIM) for j in range(6)]
    rows_s = [kv_s[:, :, j * kvw:(j + 1) * kvw].reshape(1, nb, ns, NSA_KV_HEADS, HEAD_DIM) for j in range(4)]

    tab_p = _bias_tables(rel_bias, _prompt_bucket_table(), "bias_prompt")
    tab_p = tab_p.reshape(NSA_KV_HEADS, NSA_GROUP * Q_TILE, -1)
    tab_s = _bias_tables(rel_bias, _sample_bucket_table(), "bias_sample")
    tab_s = tab_s.reshape(NSA_KV_HEADS, NSA_GROUP, NSA_KV_HEADS, ns, -1)
    tab_s = jnp.stack([tab_s[h, :, h] for h in range(NSA_KV_HEADS)], axis=1).reshape(S_ROWS, -1)

    cmp_pos = jnp.stack([cmp_pos_k[0], cmp_pos_v[0]])
    cmp_w1 = bf(jnp.stack([cmp_w1_k[0], cmp_w1_v[0]]))
    cmp_w2 = bf(jnp.stack([cmp_w2_k[0], cmp_w2_v[0]]))
    kcp = _compress_prompt(z, cmp_pos, cmp_w1, cmp_w2)
    cover_p = jnp.asarray(_cover_np(SLAB, N_SEL, lambda u, j: u - 4 * j - SLAB_OFF + 4 * REL0), BF16)
    o_nsa_p = _nsa_prompt(z, bf(kv_p), kcp, tab_p, cover_p)

    zs = z[SEQ:]
    q_s = zs[:, Z_QN:Z_QN + NSA_HEADS * HEAD_DIM].reshape(nb, ns, NSA_KV_HEADS, NSA_GROUP, HEAD_DIM)
    q_s = q_s.transpose(0, 3, 2, 1, 4).reshape(nb, S_ROWS, HEAD_DIM)
    gate_s = zs[:, Z_MISC + MISC_GATES:Z_MISC + MISC_GATES + 3 * NSA_HEADS]
    gate_s = gate_s.reshape(nb, ns, NSA_KV_HEADS, NSA_GROUP, 3).transpose(0, 3, 2, 1, 4).reshape(nb, S_ROWS, 3)
    new_s = kv_s.reshape(nb, ns, 6, NSA_KV_HEADS, HEAD_DIM).transpose(0, 2, 1, 3, 4).reshape(nb, 6, NEW_ROWS, HEAD_DIM)
    as_rows = lambda c: c.reshape(-1, HEAD_DIM)
    pools = [as_rows(c) for c in (cache_k_cmp, cache_v_cmp, cache_k_sel, cache_v_sel)]
    cover_s = _cover_np(S_CMP, 128, lambda c, j: c - 4 * j)
    cover_s = jnp.asarray(np.concatenate([cover_s] * NSA_KV_HEADS, axis=0), BF16)
    expand_s = jnp.asarray(np.arange(128)[:, None]
                           == (np.arange(S_SEL_COLS)[None, :] // (NSA_KV_HEADS * SEL_BLOCK)), BF16)
    o_nsa_s, kwin_s, vwin_s = _nsa_sample(
        page_table, q_s, gate_s, new_s, as_rows(cache_k_win), as_rows(cache_v_win), pools,
        (cmp_pos_k[0], bf(cmp_w1_k[0]), bf(cmp_w2_k[0]), cmp_pos_v[0], bf(cmp_w1_v[0]), bf(cmp_w2_v[0])),
        tab_s, cover_s, expand_s)
    o_nsa_s = o_nsa_s.reshape(nb, NSA_GROUP, NSA_KV_HEADS, ns, HEAD_DIM).transpose(0, 3, 2, 1, 4)
    o_nsa = jnp.concatenate([o_nsa_p, o_nsa_s.reshape(nb * ns, NSA_HEADS * HEAD_DIM)], axis=0)

    wa = bf(jnp.zeros((128, GLA_HEADS * GLA_DK), F32).at[MISC_A:MISC_A + GLA_RANK].set(gla_w_a2[0]))
    ba, gn = row(gla_b_a[0]), row(gla_norm[0])
    o_gla_p, st_p = _gla_prompt(z, wa, ba, gn)
    o_gla_s, st_s = _gla_sample(zs, state_gla[0], wa, ba, gn)
    o_gla = jnp.concatenate([o_gla_p, o_gla_s], axis=0)

    half = NSA_HEADS * HEAD_DIM
    h2 = _matmul_res(h1, [o_nsa, o_gla], [bf(w_out[0][:half]), bf(w_out[0][half:])], 512, "proj_out")

    memkv = _norm_matmul(mem_prompt[0], row(norm_mem_src[0]),
                         bf(jnp.concatenate([w_mem_k[0], w_mem_v[0]], axis=1)), MEM_TOKENS, 512, "mem_kv")
    mw = MEM_HEADS * MEM_HEAD_DIM
    qm = _norm_matmul(h2, row(norm_mem[0]), bf(w_mem_q[0]), 512, mw, "mem_q")
    om_p = _mem_prompt(qm, memkv)
    qm_s = qm[SEQ:].reshape(nb, ns, MEM_HEADS, MEM_HEAD_DIM).transpose(0, 2, 1, 3)
    om_s = _mem_sample(qm_s.reshape(nb, MEM_HEADS * ns, MEM_HEAD_DIM),
                       as_rows(cache_k_mem), as_rows(cache_v_mem))
    om_s = om_s.reshape(nb, MEM_HEADS, ns, MEM_HEAD_DIM).transpose(0, 2, 1, 3).reshape(nb * ns, mw)
    h3 = _matmul_res(h2, [jnp.concatenate([om_p, om_s], axis=0)], [bf(w_mem_o[0])], 512, "mem_out")

    y = _ffn(h3, row(norm_ffn2[0]), bf(ffn2_w_gate[0]), bf(ffn2_w_up[0]), bf(ffn2_w_down[0]),
             row(norm_final), True)

    mem_shape = (1, 1, MEM_TOKENS, MEM_HEADS, MEM_HEAD_DIM)
    win_shape = (1, nb, WINDOW, NSA_KV_HEADS, HEAD_DIM)
    return (y[:SEQ].reshape(1, SEQ, D_MODEL), y[SEQ:].reshape(nb, ns, D_MODEL),
            rows_p[0], rows_p[1], rows_p[2], rows_p[3],
            rows_p[4][:, :, SEQ - WINDOW:], rows_p[5][:, :, SEQ - WINDOW:],
            st_p.transpose(0, 2, 1).reshape(1, 1, GLA_HEADS, GLA_DK, GLA_DV),
            memkv[:, :mw].reshape(mem_shape), memkv[:, mw:].reshape(mem_shape),
            rows_s[0], rows_s[1], rows_s[2], rows_s[3],
            kwin_s.reshape(win_shape), vwin_s.reshape(win_shape),
            st_s.reshape(1, nb, GLA_HEADS, GLA_DK, GLA_DV))
```

```python
import functools
import math

import numpy as np
import jax
import jax.numpy as jnp
from jax import lax
from jax.experimental import pallas as pl
from jax.experimental.pallas import tpu as pltpu

F32 = jnp.float32
BF16 = jnp.bfloat16

D_MODEL = 2048
SEQ = 8192
DEC_BATCH = 128
DEC_SEQ = 4
PAST_LEN = 2048
PAGE_SIZE = 128
N_PAGES = PAST_LEN // PAGE_SIZE
HEAD_DIM = 128
NSA_HEADS = 8
NSA_KV_HEADS = 2
NSA_GROUP = 4
CMP_BLOCK = 32
CMP_STRIDE = 16
SEL_BLOCK = 64
SEL_TOPN = 16
WINDOW = 512
FORCED_SCORE = 1.0e4
GLA_HEADS = 4
GLA_DV = 256
GLA_DK = 128
GLA_RANK = 16
GLA_TAU = 16.0
GLA_CHUNK = 32
MEM_TOKENS = 256
MEM_HEADS = 4
MEM_HEAD_DIM = 128
D_FF = 5632
NUM_BUCKETS = 32
MAX_DISTANCE = 128
RMS_EPS = 1e-6

N_TOK = SEQ + DEC_BATCH * DEC_SEQ
Z_QN, Z_VG, Z_RG, Z_KV, Z_QG, Z_KG, Z_MISC = 0, 1024, 2048, 3072, 4608, 5120, 5632
Z_W = 5760
MISC_GATES, MISC_A = 0, 24

NEG = -1e30
M_INIT = -1e29

Q_TILE = 128
N_QT = SEQ // Q_TILE
N_CMP_PAD = SEQ // CMP_STRIDE
SLAB = N_CMP_PAD + 128
SLAB_OFF = SLAB - 16
KC_ROWS = SLAB_OFF + N_CMP_PAD + 16
N_SEL = SEQ // SEL_BLOCK
REL0 = N_SEL - 2
FAR_TILES = 4
FLASH_SPLIT = 2

VMEM_LIMIT = 56 * 1024 * 1024


def _cparams(sem):
    return pltpu.CompilerParams(dimension_semantics=sem, vmem_limit_bytes=VMEM_LIMIT)


def _dot(a, b):
    return jnp.dot(a, b, preferred_element_type=F32)


def _dot_nt(a, b):
    return lax.dot_general(a, b, (((1,), (1,)), ((), ())), preferred_element_type=F32)


def _dot_tn(a, b):
    return lax.dot_general(a, b, (((0,), (0,)), ((), ())), preferred_element_type=F32)


def _rms(x, g):
    return x * lax.rsqrt(jnp.mean(x * x, axis=-1, keepdims=True) + RMS_EPS) * g


def _silu(x):
    return x * jax.nn.sigmoid(x)


def _ffn_kernel(x_ref, g_ref, wg_ref, wu_ref, wd_ref, gf_ref, o_ref, xn_ref, *, n_ff, final_norm):
    j = pl.program_id(1)

    @pl.when(j == 0)
    def _():
        xn_ref[...] = _rms(x_ref[...], g_ref[...]).astype(BF16)
        o_ref[...] = jnp.zeros_like(o_ref)

    xn = xn_ref[...]
    hid = _silu(_dot(xn, wg_ref[...])) * _dot(xn, wu_ref[...])
    o_ref[...] += _dot(hid.astype(BF16), wd_ref[...])

    @pl.when(j == n_ff - 1)
    def _():
        h = x_ref[...] + 0.5 * o_ref[...]
        o_ref[...] = _rms(h, gf_ref[...]) if final_norm else h


def _ffn(x, g, wg, wu, wd, gf, final_norm, tm=512, tf=512):
    n, d = x.shape
    n_ff = D_FF // tf
    return pl.pallas_call(
        functools.partial(_ffn_kernel, n_ff=n_ff, final_norm=final_norm),
        grid=(n // tm, n_ff),
        in_specs=[pl.BlockSpec((tm, d), lambda i, j: (i, 0)),
                  pl.BlockSpec((1, d), lambda i, j: (0, 0)),
                  pl.BlockSpec((d, tf), lambda i, j: (0, j)),
                  pl.BlockSpec((d, tf), lambda i, j: (0, j)),
                  pl.BlockSpec((tf, d), lambda i, j: (j, 0)),
                  pl.BlockSpec((1, d), lambda i, j: (0, 0))],
        out_specs=pl.BlockSpec((tm, d), lambda i, j: (i, 0)),
        out_shape=jax.ShapeDtypeStruct((n, d), F32),
        scratch_shapes=[pltpu.VMEM((tm, d), BF16)],
        compiler_params=_cparams(("parallel", "arbitrary")),
        name="ffn",
    )(x, g, wg, wu, wd, gf)


def _norm_matmul_kernel(x_ref, g_ref, w_ref, o_ref, xn_ref):
    @pl.when(pl.program_id(1) == 0)
    def _():
        xn_ref[...] = _rms(x_ref[...], g_ref[...]).astype(BF16)

    o_ref[...] = _dot(xn_ref[...], w_ref[...])


def _norm_matmul(x, g, w, tm, tn, name):
    n, d = x.shape
    dout = w.shape[1]
    return pl.pallas_call(
        _norm_matmul_kernel,
        grid=(n // tm, dout // tn),
        in_specs=[pl.BlockSpec((tm, d), lambda i, j: (i, 0)),
                  pl.BlockSpec((1, d), lambda i, j: (0, 0)),
                  pl.BlockSpec((d, tn), lambda i, j: (0, j))],
        out_specs=pl.BlockSpec((tm, tn), lambda i, j: (i, j)),
        out_shape=jax.ShapeDtypeStruct((n, dout), F32),
        scratch_shapes=[pltpu.VMEM((tm, d), BF16)],
        compiler_params=_cparams(("parallel", "arbitrary")),
        name=name,
    )(x, g, w)


def _matmul_res_kernel(*refs, n_lhs):
    res_ref = refs[0]
    o_ref = refs[1 + 2 * n_lhs]
    acc = res_ref[...]
    for i in range(n_lhs):
        acc = acc + _dot(refs[1 + i][...].astype(BF16), refs[1 + n_lhs + i][...])
    o_ref[...] = acc


def _matmul_res(res, lhs, ws, tm, name):
    n, d = res.shape
    n_lhs = len(lhs)
    return pl.pallas_call(
        functools.partial(_matmul_res_kernel, n_lhs=n_lhs),
        grid=(n // tm,),
        in_specs=([pl.BlockSpec((tm, d), lambda i: (i, 0))]
                  + [pl.BlockSpec((tm, a.shape[1]), lambda i: (i, 0)) for a in lhs]
                  + [pl.BlockSpec(w.shape, lambda i: (0, 0)) for w in ws]),
        out_specs=pl.BlockSpec((tm, d), lambda i: (i, 0)),
        out_shape=jax.ShapeDtypeStruct((n, d), F32),
        compiler_params=_cparams(("parallel",)),
        name=name,
    )(res, *lhs, *ws)


def _rel_bucket_np(dist):
    n = np.maximum(dist, 0)
    exact = NUM_BUCKETS // 2
    nf = np.maximum(n, 1).astype(np.float32)
    large = exact + (np.log(nf / np.float32(exact)) / np.float32(math.log(MAX_DISTANCE / exact))
                     * np.float32(NUM_BUCKETS - exact)).astype(np.int32)
    return np.where(n < exact, n, np.minimum(large, NUM_BUCKETS - 1)).astype(np.int32)


def _bucket_or_masked(dist, valid):
    return np.where(valid, _rel_bucket_np(dist), -1).astype(np.int32)


def _prompt_bucket_table():
    i = np.arange(Q_TILE)[:, None]
    j = np.arange(Q_TILE)[None, :]
    u = np.arange(SLAB)[None, :]
    dist_c = i - CMP_STRIDE * u + (CMP_STRIDE * SLAB_OFF - (CMP_BLOCK - 1))
    diag = _bucket_or_masked(i - j, i - j >= 0)
    prev = _bucket_or_masked(Q_TILE + i - j, np.ones((Q_TILE, Q_TILE), bool))
    first = _bucket_or_masked(WINDOW + i - j, j > i)
    cmp_ = _bucket_or_masked(dist_c, dist_c >= 0)
    return np.concatenate([diag, prev, first, cmp_], axis=1).T


S_CMP = PAST_LEN // CMP_STRIDE
S_CMP_COLS = NSA_KV_HEADS * S_CMP
S_NEW_COLS = 128
S_SEL_COLS = NSA_KV_HEADS * PAST_LEN + S_NEW_COLS
S_WIN_COLS = NSA_KV_HEADS * WINDOW


def _sample_bucket_table():
    i = np.arange(DEC_SEQ)[:, None]
    pos = PAST_LEN + i
    rows = []
    for h in range(NSA_KV_HEADS):
        col = np.arange(S_CMP_COLS)[None, :]
        c = col % S_CMP
        dist_c = pos - (c * CMP_STRIDE + CMP_BLOCK - 1)
        cmp_ = _bucket_or_masked(dist_c, (col // S_CMP == h) & (c < S_CMP - 1) & (dist_c >= 0))
        col = np.arange(S_SEL_COLS)[None, :]
        key = col // NSA_KV_HEADS
        sel = _bucket_or_masked(pos - key, (col % NSA_KV_HEADS == h) & (key <= pos))
        col = np.arange(S_WIN_COLS)[None, :]
        dist_w = pos - (PAST_LEN - WINDOW + col // NSA_KV_HEADS)
        win = _bucket_or_masked(dist_w, (col % NSA_KV_HEADS == h) & (dist_w < WINDOW))
        rows.append(np.concatenate([cmp_, sel, win], axis=1))
    return np.concatenate(rows, axis=0)


def _bias_table_kernel(tab_ref, idx_ref, o_ref):
    h = pl.program_id(0)
    idx = idx_ref[...]
    out = jnp.full(idx.shape, NEG, F32)
    for b in range(NUM_BUCKETS):
        out = jnp.where(idx == b, tab_ref[b, h], out)
    o_ref[0] = out


def _bias_tables(rel_bias, idx, name):
    r, c = idx.shape
    return pl.pallas_call(
        _bias_table_kernel,
        grid=(NSA_HEADS,),
        in_specs=[pl.BlockSpec(memory_space=pltpu.SMEM),
                  pl.BlockSpec((r, c), lambda h: (0, 0))],
        out_specs=pl.BlockSpec((1, r, c), lambda h: (h, 0, 0)),
        out_shape=jax.ShapeDtypeStruct((NSA_HEADS, r, c), F32),
        compiler_params=_cparams(("arbitrary",)),
        name=name,
    )(rel_bias, jnp.asarray(idx))


def _cover_np(n_cmp_cols, n_blk_cols, delta_of):
    u = np.arange(n_cmp_cols)[:, None]
    j = np.arange(n_blk_cols)[None, :]
    delta = delta_of(u, j)
    shared = np.minimum(CMP_STRIDE * delta + CMP_BLOCK, SEL_BLOCK) - np.maximum(CMP_STRIDE * delta, 0)
    return (np.maximum(shared, 0) / CMP_STRIDE).astype(np.float32)


def _compress(load_rows, n_chunk, pos_ref, w1_ref, w2_ref):
    first = jnp.zeros((n_chunk, HEAD_DIM), F32)
    second = jnp.zeros((n_chunk, HEAD_DIM), F32)
    for s in range(CMP_STRIDE):
        r = load_rows(s)
        first = first + _dot((r + pos_ref[s:s + 1, :]).astype(BF16), w1_ref[s])
        second = second + _dot((r + pos_ref[CMP_STRIDE + s:CMP_STRIDE + s + 1, :]).astype(BF16),
                               w1_ref[CMP_STRIDE + s])
    nxt = pltpu.roll(second, n_chunk - 1, axis=0)
    hid = _silu(first + nxt)
    return _dot(hid.astype(BF16), w2_ref[...])


def _flash_step(carry, s, v_t):
    outs = []
    w = s.shape[1] // FLASH_SPLIT
    for c in range(FLASH_SPLIT):
        m, l, acc = (x[:, c * w:(c + 1) * w] for x in carry)
        sc = s[:, c * w:(c + 1) * w]
        m_new = jnp.maximum(m, jnp.max(sc, axis=0, keepdims=True))
        alpha = jnp.exp(m - m_new)
        e = jnp.exp(sc - m_new)
        l = alpha * l + jnp.sum(e, axis=0, keepdims=True)
        acc = alpha * acc + _dot(v_t, e.astype(BF16))
        outs.append((m_new, l, acc))
    return tuple(jnp.concatenate([o[i] for o in outs], axis=1) for i in range(3))


def _flash_init(cols):
    return (jnp.full((1, cols), M_INIT, F32), jnp.zeros((1, cols), F32), jnp.zeros((HEAD_DIM, cols), F32))


def _masked_softmax(s, valid):
    s = jnp.where(valid, s, NEG)
    m = jnp.max(s, axis=-1, keepdims=True)
    e = jnp.where(valid, jnp.exp(s - m), 0.0)
    return e / jnp.maximum(jnp.sum(e, axis=-1, keepdims=True), 1e-30)


def _split_dot(x, w):
    hi = x.astype(BF16)
    lo = (x - hi.astype(F32)).astype(BF16)
    return _dot(hi, w) + _dot(lo, w)


def _top_n_mask(score, index, axis):
    sel = jnp.zeros(score.shape, F32)
    for _ in range(SEL_TOPN):
        mx = jnp.max(score, axis=axis, keepdims=True)
        first = jnp.min(jnp.where(score == mx, index, 1e9), axis=axis, keepdims=True)
        hit = index == first
        sel = jnp.where(hit, 1.0, sel)
        score = jnp.where(hit, -jnp.inf, score)
    return sel


def _top_n_mask_by_rank(score, n_cand):
    lane = lax.broadcasted_iota(jnp.int32, score.shape, 1)
    rank = jnp.zeros(score.shape, F32)
    for j in range(n_cand):
        col = score[:, j:j + 1]
        ahead = (col > score) | ((col == score) & (lane > j))
        rank = rank + ahead.astype(F32)
    return ((rank < SEL_TOPN) & (lane < n_cand)).astype(F32)


def _compress_prompt_kernel(rows_ref, pos_ref, w1_ref, w2_ref, o_ref):
    out = _compress(lambda s: rows_ref[pl.ds(s, N_CMP_PAD, stride=CMP_STRIDE), :], N_CMP_PAD,
                    pos_ref.at[0], w1_ref.at[0], w2_ref.at[0])
    real = lax.broadcasted_iota(jnp.int32, (N_CMP_PAD, HEAD_DIM), 0) < N_CMP_PAD - 1
    o_ref[0, 0, 0:SLAB_OFF, :] = jnp.zeros((SLAB_OFF, HEAD_DIM), F32)
    o_ref[0, 0, SLAB_OFF:SLAB_OFF + N_CMP_PAD, :] = jnp.where(real, out, 0.0)
    o_ref[0, 0, SLAB_OFF + N_CMP_PAD:KC_ROWS, :] = jnp.zeros((KC_ROWS - SLAB_OFF - N_CMP_PAD, HEAD_DIM), F32)


def _compress_prompt(z, pos, w1, w2):
    kv_blk = Z_KV // HEAD_DIM
    return pl.pallas_call(
        _compress_prompt_kernel,
        grid=(2, NSA_KV_HEADS),
        in_specs=[pl.BlockSpec((SEQ, HEAD_DIM), lambda i, h: (0, kv_blk + NSA_KV_HEADS * i + h)),
                  pl.BlockSpec((1, CMP_BLOCK, HEAD_DIM), lambda i, h: (i, 0, 0)),
                  pl.BlockSpec((1, CMP_BLOCK, HEAD_DIM, HEAD_DIM), lambda i, h: (i, 0, 0, 0)),
                  pl.BlockSpec((1, HEAD_DIM, HEAD_DIM), lambda i, h: (i, 0, 0))],
        out_specs=pl.BlockSpec((1, 1, KC_ROWS, HEAD_DIM), lambda i, h: (i, h, 0, 0)),
        out_shape=jax.ShapeDtypeStruct((2, NSA_KV_HEADS, KC_ROWS, HEAD_DIM), F32),
        compiler_params=_cparams(("parallel", "parallel")),
        name="compress_prompt",
    )(z, pos, w1, w2)


def _nsa_prompt_kernel(q_ref, misc_ref, ksel_ref, vsel_ref, kwin_ref, vwin_ref, kc_ref, vc_ref,
                       bias_ref, cover_ref, erel_ref, o_ref):
    kvh = pl.program_id(0)
    t = pl.program_id(1)
    cols = NSA_GROUP * Q_TILE
    q = q_ref[...] * (HEAD_DIM ** -0.5)
    qt = jnp.concatenate([q[:, g * HEAD_DIM:(g + 1) * HEAD_DIM].T for g in range(NSA_GROUP)], axis=1).astype(BF16)

    def bias_tile(k):
        return bias_ref[0, k * Q_TILE:(k + 1) * Q_TILE, :]
    b_diag, b_prev, b_first = bias_tile(0), bias_tile(1), bias_tile(2)

    start = pl.multiple_of(t * (Q_TILE // CMP_STRIDE), 8)
    kslab = kc_ref[0, 0, pl.ds(start, SLAB), :].astype(BF16)
    vslab = vc_ref[0, 0, pl.ds(start, SLAB), :].astype(BF16)
    b_cmp = bias_ref[0, 3 * Q_TILE:3 * Q_TILE + SLAB, :]
    u = lax.broadcasted_iota(jnp.int32, (SLAB, 1), 0)
    valid = (b_cmp > M_INIT) & (u >= SLAB_OFF - (Q_TILE // CMP_STRIDE) * t)
    s = jnp.where(valid, _dot(kslab, qt) + b_cmp, NEG)
    e = jnp.where(valid, jnp.exp(s - jnp.max(s, axis=0, keepdims=True)), 0.0)
    p_cmp = e / jnp.maximum(jnp.sum(e, axis=0, keepdims=True), 1e-30)
    o_cmp = _dot_tn(vslab, p_cmp.astype(BF16))

    p_sum = p_cmp[:, 0:Q_TILE]
    for g in range(1, NSA_GROUP):
        p_sum = p_sum + p_cmp[:, g * Q_TILE:(g + 1) * Q_TILE]
    hi = p_sum.astype(BF16)
    lo = (p_sum - hi.astype(F32)).astype(BF16)
    imp = _dot(cover_ref[...], hi) + _dot(cover_ref[...], lo)
    jr = lax.broadcasted_iota(jnp.int32, (N_SEL, Q_TILE), 0)
    qi = lax.broadcasted_iota(jnp.int32, (N_SEL, Q_TILE), 1)
    cur = REL0 + (qi >= SEL_BLOCK).astype(jnp.int32)
    first_blk = REL0 - 2 * t
    forced = (jr == first_blk) | (jr == cur) | (jr == cur - 1)
    in_range = (jr <= cur) & (jr >= first_blk)
    score = jnp.where(in_range, jnp.where(forced, FORCED_SCORE, imp), NEG)
    sel = _top_n_mask(score, jr.astype(F32), 0)

    unsel = ((sel - 1.0) * (-NEG)).astype(BF16)
    q_aug = jnp.concatenate([qt, jnp.concatenate([unsel] * NSA_GROUP, axis=1)], axis=0)

    def key_tile(ref, kt):
        return ref[pl.ds(pl.multiple_of(kt * Q_TILE, Q_TILE), Q_TILE), :]

    def value_tile(ref, kt):
        return ref[:, pl.ds(pl.multiple_of(kt * Q_TILE, Q_TILE), Q_TILE)]

    def sel_scores(kt, back):
        return _dot(jnp.concatenate([key_tile(ksel_ref, kt), erel_ref[back]], axis=1), q_aug)

    def far_step(kts, carry):
        s = jnp.concatenate([sel_scores(kt, t - kt) for kt in kts], axis=0)
        return _flash_step(carry, s, jnp.concatenate([value_tile(vsel_ref, kt) for kt in kts], axis=1))

    n_far = jnp.maximum(t - 1, 0)
    n_multi = n_far // FAR_TILES
    carry = lax.fori_loop(0, n_multi, lambda i, c: far_step([FAR_TILES * i + j for j in range(FAR_TILES)], c),
                          _flash_init(cols))
    carry = lax.fori_loop(FAR_TILES * n_multi, n_far, lambda kt, c: far_step([kt], c), carry)
    kp = jnp.maximum(t - 1, 0)
    pen = jnp.where(t >= 1, 0.0, NEG)
    s = jnp.concatenate([sel_scores(kp, 1) + b_prev + pen, sel_scores(t, 0) + b_diag], axis=0)
    v = jnp.concatenate([value_tile(vsel_ref, kp), value_tile(vsel_ref, t)], axis=1)
    _, l, acc = _flash_step(carry, s, v)
    o_sel = acc / l

    n_wt = WINDOW // Q_TILE + 1
    scores, values = [], []
    for w, b_tile in enumerate((b_first,) + (None,) * (n_wt - 3) + (b_prev, b_diag)):
        kt = t - (n_wt - 1) + w
        kc = jnp.maximum(kt, 0)
        s = _dot(key_tile(kwin_ref, kc), qt) + jnp.where(kt >= 0, 0.0, NEG)
        scores.append(s if b_tile is None else s + b_tile)
        values.append(value_tile(vwin_ref, kc))
    _, l, acc = _flash_step(_flash_init(cols), jnp.concatenate(scores, axis=0), jnp.concatenate(values, axis=1))
    o_win = acc / l

    gates = jax.nn.sigmoid(misc_ref[...]).T
    n_g = 3 * NSA_GROUP
    gk = jnp.where(kvh == 0, gates, pltpu.roll(gates, gates.shape[0] - n_g, axis=0))[MISC_GATES:MISC_GATES + n_g]
    for g in range(NSA_GROUP):
        c = slice(g * Q_TILE, (g + 1) * Q_TILE)
        o_g = (gk[3 * g:3 * g + 1] * o_cmp[:, c] + gk[3 * g + 1:3 * g + 2] * o_sel[:, c]
               + gk[3 * g + 2:3 * g + 3] * o_win[:, c])
        o_ref[:, g * HEAD_DIM:(g + 1) * HEAD_DIM] = o_g.T


def _nsa_prompt(z, kvb, vt, kcp, bias, cover, erel):
    gw = NSA_GROUP * HEAD_DIM

    def k_spec(j):
        return pl.BlockSpec((SEQ, HEAD_DIM), lambda h, t, j=j: (0, 2 * j + h))

    def vt_spec(j):
        return pl.BlockSpec((None, HEAD_DIM, SEQ), lambda h, t, j=j: (j, h, 0))
    return pl.pallas_call(
        _nsa_prompt_kernel,
        grid=(NSA_KV_HEADS, N_QT),
        in_specs=[pl.BlockSpec((Q_TILE, gw), lambda h, t: (t, h)),
                  pl.BlockSpec((Q_TILE, 128), lambda h, t: (t, Z_MISC // 128)),
                  k_spec(2), vt_spec(0), k_spec(4), vt_spec(1),
                  pl.BlockSpec((1, 1, KC_ROWS, HEAD_DIM), lambda h, t: (0, h, 0, 0)),
                  pl.BlockSpec((1, 1, KC_ROWS, HEAD_DIM), lambda h, t: (1, h, 0, 0)),
                  pl.BlockSpec((1,) + bias.shape[1:], lambda h, t: (h, 0, 0)),
                  pl.BlockSpec(cover.shape, lambda h, t: (0, 0)),
                  pl.BlockSpec(erel.shape, lambda h, t: (0, 0, 0))],
        out_specs=pl.BlockSpec((Q_TILE, gw), lambda h, t: (t, h)),
        out_shape=jax.ShapeDtypeStruct((SEQ, NSA_HEADS * HEAD_DIM), F32),
        compiler_params=_cparams(("arbitrary", "arbitrary")),
        name="nsa_prompt",
    )(z, z, kvb, vt, kvb, vt, kcp, kcp, bias, cover, erel)


S_ROWS = NSA_HEADS * DEC_SEQ
PAGE_ROWS = NSA_KV_HEADS * PAGE_SIZE
NEW_ROWS = NSA_KV_HEADS * DEC_SEQ


def _nsa_sample_kernel(pt_ref, *refs):
    del pt_ref
    n_pg = N_PAGES
    q_ref, gate_ref, new_ref, kwin_ref, vwin_ref = refs[0:5]
    pages = [refs[5 + k * n_pg:5 + (k + 1) * n_pg] for k in range(4)]
    (posk_ref, w1k_ref, w2k_ref, posv_ref, w1v_ref, w2v_ref,
     bias_ref, cover_ref, expand_ref) = refs[5 + 4 * n_pg:14 + 4 * n_pg]
    o_ref, kwin_o_ref, vwin_o_ref = refs[14 + 4 * n_pg:17 + 4 * n_pg]
    past_ref, newk_ref, newv_ref = refs[17 + 4 * n_pg:20 + 4 * n_pg]

    comp = []
    for k, (pos_ref, w1_ref, w2_ref) in enumerate(((posk_ref, w1k_ref, w2k_ref), (posv_ref, w1v_ref, w2v_ref))):
        for p in range(n_pg):
            past_ref[p * PAGE_ROWS:(p + 1) * PAGE_ROWS, :] = pages[k][p][...]

        def load_rows(s):
            return jnp.concatenate(
                [past_ref[pl.ds(NSA_KV_HEADS * s + h, S_CMP, stride=NSA_KV_HEADS * CMP_STRIDE), :]
                 for h in range(NSA_KV_HEADS)], axis=0)
        comp.append(_compress(load_rows, S_CMP_COLS, pos_ref, w1_ref, w2_ref).astype(BF16))

    keep = NSA_KV_HEADS * WINDOW - NEW_ROWS
    kwin_o_ref[0:keep, :] = kwin_ref[NEW_ROWS:NSA_KV_HEADS * WINDOW, :]
    kwin_o_ref[keep:keep + NEW_ROWS, :] = new_ref[0, 4]
    vwin_o_ref[0:keep, :] = vwin_ref[NEW_ROWS:NSA_KV_HEADS * WINDOW, :]
    vwin_o_ref[keep:keep + NEW_ROWS, :] = new_ref[0, 5]

    qs = (q_ref[0] * (HEAD_DIM ** -0.5)).astype(BF16)
    o0, o1 = S_CMP_COLS, S_CMP_COLS + S_SEL_COLS
    b_cmp = bias_ref[:, 0:o0]
    b_new = bias_ref[:, o1 - S_NEW_COLS:o1]

    def attend(scores, bias, values):
        s = jnp.concatenate(scores, axis=1) + bias
        e = jnp.exp(s - jnp.max(s, axis=-1, keepdims=True))
        acc = jnp.zeros((S_ROWS, HEAD_DIM), F32)
        c0 = 0
        for v in values:
            acc = acc + _dot(e[:, c0:c0 + v.shape[0]].astype(BF16), v)
            c0 += v.shape[0]
        return acc / jnp.sum(e, axis=-1, keepdims=True)

    p_cmp = _masked_softmax(_dot_nt(qs, comp[0]) + b_cmp, b_cmp > M_INIT)
    o_cmp = _dot(p_cmp.astype(BF16), comp[1])

    imp = _split_dot(p_cmp, cover_ref[...])
    imp = imp + pltpu.roll(imp, 8, axis=0) + pltpu.roll(imp, 16, axis=0) + pltpu.roll(imp, 24, axis=0)
    blk = lax.broadcasted_iota(jnp.int32, (S_ROWS, 128), 1)
    cur = PAST_LEN // SEL_BLOCK
    forced = (blk == 0) | (blk == cur) | (blk == cur - 1)
    score = jnp.where(blk <= cur, jnp.where(forced, FORCED_SCORE, imp), NEG)
    sel = _top_n_mask_by_rank(score, cur + 1).astype(BF16)
    mask_add = (_dot(sel, expand_ref[...]) - 1.0) * (-NEG)

    def new_tile(k_ref, v_ref, jk, jv):
        k_ref[...] = jnp.zeros_like(k_ref)
        v_ref[...] = jnp.zeros_like(v_ref)
        k_ref[0:NEW_ROWS, :] = new_ref[0, jk]
        v_ref[0:NEW_ROWS, :] = new_ref[0, jv]
        return k_ref[...].astype(BF16), v_ref[...].astype(BF16)

    nk, nv = new_tile(newk_ref, newv_ref, 2, 3)
    scores = [_dot_nt(qs, pages[2][p][...].astype(BF16)) for p in range(n_pg)] + [_dot_nt(qs, nk)]
    values = [pages[3][p][...].astype(BF16) for p in range(n_pg)] + [nv]
    o_sel = attend(scores, bias_ref[:, o0:o1] + mask_add, values)

    nk, nv = new_tile(newk_ref, newv_ref, 4, 5)
    scores = [_dot_nt(qs, kwin_ref[...].astype(BF16)), _dot_nt(qs, nk)]
    bias_w = jnp.concatenate([bias_ref[:, o1:o1 + S_WIN_COLS], b_new], axis=1)
    o_win = attend(scores, bias_w, [vwin_ref[...].astype(BF16), nv])

    g = jax.nn.sigmoid(gate_ref[0])
    o_ref[0] = g[:, 0:1] * o_cmp + g[:, 1:2] * o_sel + g[:, 2:3] * o_win


def _nsa_sample(page_table, q_s, gate_s, new_s, kwin, vwin, pools, cmp_w, bias, cover, expand):
    win_rows = NSA_KV_HEADS * WINDOW

    def full(a):
        return pl.BlockSpec(a.shape, lambda b, pt, n=a.ndim: (0,) * n)

    def per_b(a):
        return pl.BlockSpec((1,) + a.shape[1:], lambda b, pt, n=a.ndim: (b,) + (0,) * (n - 1))
    win_spec = pl.BlockSpec((win_rows, HEAD_DIM), lambda b, pt: (b, 0))
    page_specs = [pl.BlockSpec((PAGE_ROWS, HEAD_DIM), lambda b, pt, p=p: (pt[b, p], 0))
                  for _ in range(4) for p in range(N_PAGES)]
    page_args = [pool for pool in pools for _ in range(N_PAGES)]
    consts = list(cmp_w) + [bias, cover, expand]
    grid_spec = pltpu.PrefetchScalarGridSpec(
        num_scalar_prefetch=1,
        grid=(DEC_BATCH,),
        in_specs=[per_b(q_s), per_b(gate_s), per_b(new_s), win_spec, win_spec]
        + page_specs + [full(a) for a in consts],
        out_specs=[per_b(q_s), win_spec, win_spec],
        scratch_shapes=[pltpu.VMEM((NSA_KV_HEADS * PAST_LEN, HEAD_DIM), F32),
                        pltpu.VMEM((S_NEW_COLS, HEAD_DIM), F32),
                        pltpu.VMEM((S_NEW_COLS, HEAD_DIM), F32)],
    )
    return pl.pallas_call(
        _nsa_sample_kernel,
        grid_spec=grid_spec,
        out_shape=[jax.ShapeDtypeStruct(q_s.shape, F32),
                   jax.ShapeDtypeStruct(kwin.shape, F32),
                   jax.ShapeDtypeStruct(vwin.shape, F32)],
        compiler_params=_cparams(("arbitrary",)),
        name="nsa_sample",
    )(page_table, q_s, gate_s, new_s, kwin, vwin, *page_args, *consts)


def _log_decay(a_blk, wa_ref, ba_ref):
    x = _dot(a_blk.astype(BF16), wa_ref[...]) + ba_ref[...]
    return (jnp.minimum(x, 0.0) - jnp.log1p(jnp.exp(-jnp.abs(x)))) * (1.0 / GLA_TAU)


def _segment_cumsum(g, seg):
    pos = lax.broadcasted_iota(jnp.int32, g.shape, 0) % seg
    cum = g
    sh = 1
    while sh < seg:
        cum = cum + jnp.where(pos >= sh, pltpu.roll(cum, sh, axis=0), 0.0)
        sh *= 2
    return cum


def _gla_prompt_kernel(q_ref, k_ref, v_ref, r_ref, a_ref, wa_ref, ba_ref, gn_ref, o_ref, st_o_ref, st_ref,
                       *, n_blk, tb):
    tbi = pl.program_id(1)

    @pl.when(tbi == 0)
    def _():
        st_ref[...] = jnp.zeros_like(st_ref)

    c = GLA_CHUNK
    cum = _segment_cumsum(_log_decay(a_ref[...], wa_ref, ba_ref), c)
    q = q_ref[...] * (GLA_DK ** -0.5)
    k = k_ref[...]
    v = v_ref[...].astype(BF16)
    qe = (q * jnp.exp(cum)).astype(BF16)
    kd = (k * jnp.exp(-cum)).astype(BF16)
    tril = lax.broadcasted_iota(jnp.int32, (c, c), 0) >= lax.broadcasted_iota(jnp.int32, (c, c), 1)
    st = st_ref[...]
    outs = []
    for ci in range(tb // c):
        r = slice(ci * c, (ci + 1) * c)
        last = cum[ci * c + c - 1:ci * c + c, :]
        att = jnp.where(tril, _dot_nt(qe[r], kd[r]), 0.0)
        outs.append(_dot_nt(qe[r], st.astype(BF16)) + _dot(att.astype(BF16), v[r]))
        kl = (k[r] * jnp.exp(last - cum[r])).astype(BF16)
        st = jnp.exp(last) * st + _dot_tn(v[r], kl)
    st_ref[...] = st
    o = jnp.concatenate(outs, axis=0)
    o_ref[...] = _rms(o, gn_ref[...]) * _silu(r_ref[...])

    @pl.when(tbi == n_blk - 1)
    def _():
        st_o_ref[0] = st


def _gla_prompt(z, wa, ba, gn, tb=256):
    n_blk = SEQ // tb
    return pl.pallas_call(
        functools.partial(_gla_prompt_kernel, n_blk=n_blk, tb=tb),
        grid=(GLA_HEADS, n_blk),
        in_specs=[pl.BlockSpec((tb, GLA_DK), lambda h, i: (i, Z_QG // GLA_DK + h)),
                  pl.BlockSpec((tb, GLA_DK), lambda h, i: (i, Z_KG // GLA_DK + h)),
                  pl.BlockSpec((tb, GLA_DV), lambda h, i: (i, Z_VG // GLA_DV + h)),
                  pl.BlockSpec((tb, GLA_DV), lambda h, i: (i, Z_RG // GLA_DV + h)),
                  pl.BlockSpec((tb, 128), lambda h, i: (i, Z_MISC // 128)),
                  pl.BlockSpec((128, GLA_DK), lambda h, i: (0, h)),
                  pl.BlockSpec((1, GLA_DK), lambda h, i: (0, h)),
                  pl.BlockSpec((1, GLA_DV), lambda h, i: (0, 0))],
        out_specs=[pl.BlockSpec((tb, GLA_DV), lambda h, i: (i, h)),
                   pl.BlockSpec((1, GLA_DV, GLA_DK), lambda h, i: (h, 0, 0))],
        out_shape=[jax.ShapeDtypeStruct((SEQ, GLA_HEADS * GLA_DV), F32),
                   jax.ShapeDtypeStruct((GLA_HEADS, GLA_DV, GLA_DK), F32)],
        scratch_shapes=[pltpu.VMEM((GLA_DV, GLA_DK), F32)],
        compiler_params=_cparams(("parallel", "arbitrary")),
        name="gla_prompt",
    )(z, z, z, z, z, wa, ba, gn)


GS_B = 4


def _gla_sample_kernel(q_ref, k_ref, v_ref, r_ref, a_ref, wa_ref, ba_ref, gn_ref, s_ref, o_ref, s_o_ref):
    rows = GS_B * DEC_SEQ
    a = a_ref[...]
    ri = lax.broadcasted_iota(jnp.int32, (rows, rows), 0)
    ci = lax.broadcasted_iota(jnp.int32, (rows, rows), 1)
    same_causal = (ri // DEC_SEQ == ci // DEC_SEQ) & (ri >= ci)
    row_b = lax.broadcasted_iota(jnp.int32, (rows, 1), 0) // DEC_SEQ
    ones = jnp.ones((rows, 128), BF16)
    for h in range(GLA_HEADS):
        dk = slice(h * GLA_DK, (h + 1) * GLA_DK)
        dv = slice(h * GLA_DV, (h + 1) * GLA_DV)
        g = _log_decay(a, wa_ref.at[:, dk], ba_ref.at[:, dk])
        cum = _segment_cumsum(g, DEC_SEQ)
        q = q_ref[:, dk] * (GLA_DK ** -0.5)
        k = k_ref[:, dk]
        v = v_ref[:, dv].astype(BF16)
        qe = (q * jnp.exp(cum)).astype(BF16)
        kd = (k * jnp.exp(-cum)).astype(BF16)
        att = jnp.where(same_causal, _dot_nt(qe, kd), 0.0)
        o = _dot(att.astype(BF16), v)
        for b in range(GS_B):
            mine = row_b == b
            last = cum[b * DEC_SEQ + DEC_SEQ - 1:(b + 1) * DEC_SEQ, :]
            s = s_ref[b, h]
            o = o + jnp.where(mine, _dot(qe, s.astype(BF16)), 0.0)
            kl = jnp.where(mine, k * jnp.exp(last - cum), 0.0)
            hi = jnp.where(mine, g, 0.0).astype(BF16)
            lo = (jnp.where(mine, g, 0.0) - hi.astype(F32)).astype(BF16)
            last_col = (_dot_tn(hi, ones) + _dot_tn(lo, ones))[:, 0:1]
            s_o_ref[b, h] = jnp.exp(last_col) * s + _dot_tn(kl.astype(BF16), v)
        o_ref[:, dv] = _rms(o, gn_ref[...]) * _silu(r_ref[:, dv])


def _gla_sample(zs, state, wa, ba, gn):
    rows = GS_B * DEC_SEQ
    n = DEC_BATCH * DEC_SEQ
    hk, hv = GLA_HEADS * GLA_DK, GLA_HEADS * GLA_DV
    st_spec = pl.BlockSpec((GS_B, GLA_HEADS, GLA_DK, GLA_DV), lambda i: (i, 0, 0, 0))
    return pl.pallas_call(
        _gla_sample_kernel,
        grid=(DEC_BATCH // GS_B,),
        in_specs=[pl.BlockSpec((rows, hk), lambda i: (i, Z_QG // hk)),
                  pl.BlockSpec((rows, hk), lambda i: (i, Z_KG // hk)),
                  pl.BlockSpec((rows, hv), lambda i: (i, Z_VG // hv)),
                  pl.BlockSpec((rows, hv), lambda i: (i, Z_RG // hv)),
                  pl.BlockSpec((rows, 128), lambda i: (i, Z_MISC // 128)),
                  pl.BlockSpec((128, hk), lambda i: (0, 0)),
                  pl.BlockSpec((1, hk), lambda i: (0, 0)),
                  pl.BlockSpec((1, GLA_DV), lambda i: (0, 0)),
                  st_spec],
        out_specs=[pl.BlockSpec((rows, hv), lambda i: (i, 0)), st_spec],
        out_shape=[jax.ShapeDtypeStruct((n, hv), F32), jax.ShapeDtypeStruct(state.shape, F32)],
        compiler_params=_cparams(("parallel",)),
        name="gla_sample",
    )(zs, zs, zs, zs, zs, wa, ba, gn, state)


def _softmax_rows(s):
    m = jnp.max(s, axis=-1, keepdims=True)
    e = jnp.exp(s - m)
    return e / jnp.sum(e, axis=-1, keepdims=True)


def _mem_prompt_kernel(q_ref, k_ref, v_ref, o_ref):
    for h in range(MEM_HEADS):
        d = slice(h * MEM_HEAD_DIM, (h + 1) * MEM_HEAD_DIM)
        q = (q_ref[:, d] * (MEM_HEAD_DIM ** -0.5)).astype(BF16)
        p = _softmax_rows(_dot_nt(q, k_ref[:, d].astype(BF16)))
        o_ref[:, d] = _dot(p.astype(BF16), v_ref[:, d].astype(BF16))


def _mem_prompt(qm, memkv, tq=256):
    w = MEM_HEADS * MEM_HEAD_DIM
    return pl.pallas_call(
        _mem_prompt_kernel,
        grid=(SEQ // tq,),
        in_specs=[pl.BlockSpec((tq, w), lambda i: (i, 0)),
                  pl.BlockSpec((MEM_TOKENS, w), lambda i: (0, 0)),
                  pl.BlockSpec((MEM_TOKENS, w), lambda i: (0, 1))],
        out_specs=pl.BlockSpec((tq, w), lambda i: (i, 0)),
        out_shape=jax.ShapeDtypeStruct((SEQ, w), F32),
        compiler_params=_cparams(("parallel",)),
        name="mem_prompt",
    )(qm, memkv, memkv)


def _mem_sample_kernel(q_ref, k_ref, v_ref, o_ref):
    rows = MEM_HEADS * DEC_SEQ
    cols = MEM_HEADS * MEM_TOKENS
    q = (q_ref[0] * (MEM_HEAD_DIM ** -0.5)).astype(BF16)
    row_h = lax.broadcasted_iota(jnp.int32, (rows, cols), 0) // DEC_SEQ
    col_h = lax.broadcasted_iota(jnp.int32, (rows, cols), 1) % MEM_HEADS
    s = jnp.where(row_h == col_h, _dot_nt(q, k_ref[...].astype(BF16)), NEG)
    o_ref[0] = _dot(_softmax_rows(s).astype(BF16), v_ref[...].astype(BF16))


def _mem_sample(q_s, k_mem, v_mem):
    rows = MEM_HEADS * DEC_SEQ
    kv_spec = pl.BlockSpec((MEM_HEADS * MEM_TOKENS, MEM_HEAD_DIM), lambda b: (b, 0))
    return pl.pallas_call(
        _mem_sample_kernel,
        grid=(DEC_BATCH,),
        in_specs=[pl.BlockSpec((1, rows, MEM_HEAD_DIM), lambda b: (b, 0, 0)), kv_spec, kv_spec],
        out_specs=pl.BlockSpec((1, rows, MEM_HEAD_DIM), lambda b: (b, 0, 0)),
        out_shape=jax.ShapeDtypeStruct((DEC_BATCH, rows, MEM_HEAD_DIM), F32),
        compiler_params=_cparams(("parallel",)),
        name="mem_sample",
    )(q_s, k_mem, v_mem)


def _permute_w_in(w_in):
    qn, kv, gt, qg, kg, vg, rg, ag = jnp.split(
        w_in, np.cumsum([1024, 1536, 24, 512, 512, 1024, 1024, 16])[:-1].tolist(), axis=1)
    pad = jnp.zeros((w_in.shape[0], Z_W - Z_MISC - 40), w_in.dtype)
    return jnp.concatenate([qn, vg, rg, kv, qg, kg, gt, ag, pad], axis=1)


def kernel(x_prompt, x_sample, mem_prompt, cache_k_cmp, cache_v_cmp, cache_k_sel, cache_v_sel, cache_k_win,
           cache_v_win, state_gla, cache_k_mem, cache_v_mem, page_table, norm_ffn1, ffn1_w_gate, ffn1_w_up,
           ffn1_w_down, norm_mix, w_in, w_out, cmp_pos_k, cmp_w1_k, cmp_w2_k, cmp_pos_v, cmp_w1_v, cmp_w2_v,
           rel_bias, gla_w_a2, gla_b_a, gla_norm, norm_mem, norm_mem_src, w_mem_q, w_mem_k, w_mem_v, w_mem_o,
           norm_ffn2, ffn2_w_gate, ffn2_w_up, ffn2_w_down, norm_final):
    bf = lambda a: a.astype(BF16)
    row = lambda a: a.reshape(1, -1)
    nb, ns = DEC_BATCH, DEC_SEQ
    kvw = NSA_KV_HEADS * HEAD_DIM

    x = jnp.concatenate([x_prompt[0], x_sample.reshape(nb * ns, D_MODEL)], axis=0)
    ones = jnp.ones((1, D_MODEL), F32)
    h1 = _ffn(x, row(norm_ffn1[0]), bf(ffn1_w_gate[0]), bf(ffn1_w_up[0]), bf(ffn1_w_down[0]), ones, False)
    z = _norm_matmul(h1, row(norm_mix[0]), bf(_permute_w_in(w_in[0])), 512, Z_W // 5, "proj_in")

    kv_p = z[:SEQ, Z_KV:Z_QG]
    kv_s = z[SEQ:, Z_KV:Z_QG].reshape(nb, ns, 6 * kvw)
    rows_p = [kv_p[:, j * kvw:(j + 1) * kvw].reshape(1, 1, SEQ, NSA_KV_HEADS, HEAD_DIM) for j in range(6)]
    rows_s = [kv_s[:, :, j * kvw:(j + 1) * kvw].reshape(1, nb, ns, NSA_KV_HEADS, HEAD_DIM) for j in range(4)]

    tab_p = _bias_tables(rel_bias, _prompt_bucket_table(), "bias_prompt")
    far = rel_bias[NUM_BUCKETS - 1][:, None, None]
    near = tab_p[:, :3 * Q_TILE]
    tab_p = jnp.concatenate([jnp.where(near > M_INIT, near - far, NEG), tab_p[:, 3 * Q_TILE:]], axis=1)
    tab_p = tab_p.reshape(NSA_KV_HEADS, NSA_GROUP, -1, Q_TILE).transpose(0, 2, 1, 3)
    tab_p = tab_p.reshape(NSA_KV_HEADS, -1, NSA_GROUP * Q_TILE)
    tab_s = _bias_tables(rel_bias, _sample_bucket_table(), "bias_sample")
    tab_s = tab_s.reshape(NSA_KV_HEADS, NSA_GROUP, NSA_KV_HEADS, ns, -1)
    tab_s = jnp.stack([tab_s[h, :, h] for h in range(NSA_KV_HEADS)], axis=1).reshape(S_ROWS, -1)

    cmp_pos = jnp.stack([cmp_pos_k[0], cmp_pos_v[0]])
    cmp_w1 = bf(jnp.stack([cmp_w1_k[0], cmp_w1_v[0]]))
    cmp_w2 = bf(jnp.stack([cmp_w2_k[0], cmp_w2_v[0]]))
    kcp = _compress_prompt(z, cmp_pos, cmp_w1, cmp_w2)
    cover_p = jnp.asarray(_cover_np(SLAB, N_SEL, lambda u, j: u - 4 * j - SLAB_OFF + 4 * REL0).T, BF16)
    erel = (np.arange(N_SEL)[None, None, :] == REL0 - 2 * np.arange(N_QT)[:, None, None]
            + (np.arange(Q_TILE)[None, :, None] >= SEL_BLOCK))
    kvb = bf(kv_p)
    vt = jnp.stack([kvb[:, 3 * kvw:4 * kvw].T, kvb[:, 5 * kvw:6 * kvw].T])
    o_nsa_p = _nsa_prompt(z, kvb, vt, kcp, tab_p, cover_p, jnp.asarray(erel, BF16))

    zs = z[SEQ:]
    q_s = zs[:, Z_QN:Z_QN + NSA_HEADS * HEAD_DIM].reshape(nb, ns, NSA_KV_HEADS, NSA_GROUP, HEAD_DIM)
    q_s = q_s.transpose(0, 3, 2, 1, 4).reshape(nb, S_ROWS, HEAD_DIM)
    gate_s = zs[:, Z_MISC + MISC_GATES:Z_MISC + MISC_GATES + 3 * NSA_HEADS]
    gate_s = gate_s.reshape(nb, ns, NSA_KV_HEADS, NSA_GROUP, 3).transpose(0, 3, 2, 1, 4).reshape(nb, S_ROWS, 3)
    new_s = kv_s.reshape(nb, ns, 6, NSA_KV_HEADS, HEAD_DIM).transpose(0, 2, 1, 3, 4).reshape(nb, 6, NEW_ROWS, HEAD_DIM)
    as_rows = lambda c: c.reshape(-1, HEAD_DIM)
    pools = [as_rows(c) for c in (cache_k_cmp, cache_v_cmp, cache_k_sel, cache_v_sel)]
    cover_s = _cover_np(S_CMP, 128, lambda c, j: c - 4 * j)
    cover_s = jnp.asarray(np.concatenate([cover_s] * NSA_KV_HEADS, axis=0), BF16)
    expand_s = jnp.asarray(np.arange(128)[:, None]
                           == (np.arange(S_SEL_COLS)[None, :] // (NSA_KV_HEADS * SEL_BLOCK)), BF16)
    o_nsa_s, kwin_s, vwin_s = _nsa_sample(
        page_table, q_s, gate_s, new_s, as_rows(cache_k_win), as_rows(cache_v_win), pools,
        (cmp_pos_k[0], bf(cmp_w1_k[0]), bf(cmp_w2_k[0]), cmp_pos_v[0], bf(cmp_w1_v[0]), bf(cmp_w2_v[0])),
        tab_s, cover_s, expand_s)
    o_nsa_s = o_nsa_s.reshape(nb, NSA_GROUP, NSA_KV_HEADS, ns, HEAD_DIM).transpose(0, 3, 2, 1, 4)
    o_nsa = jnp.concatenate([o_nsa_p, o_nsa_s.reshape(nb * ns, NSA_HEADS * HEAD_DIM)], axis=0)

    wa = bf(jnp.zeros((128, GLA_HEADS * GLA_DK), F32).at[MISC_A:MISC_A + GLA_RANK].set(gla_w_a2[0]))
    ba, gn = row(gla_b_a[0]), row(gla_norm[0])
    o_gla_p, st_p = _gla_prompt(z, wa, ba, gn)
    o_gla_s, st_s = _gla_sample(zs, state_gla[0], wa, ba, gn)
    o_gla = jnp.concatenate([o_gla_p, o_gla_s], axis=0)

    half = NSA_HEADS * HEAD_DIM
    h2 = _matmul_res(h1, [o_nsa, o_gla], [bf(w_out[0][:half]), bf(w_out[0][half:])], 512, "proj_out")

    memkv = _norm_matmul(mem_prompt[0], row(norm_mem_src[0]),
                         bf(jnp.concatenate([w_mem_k[0], w_mem_v[0]], axis=1)), MEM_TOKENS, 512, "mem_kv")
    mw = MEM_HEADS * MEM_HEAD_DIM
    qm = _norm_matmul(h2, row(norm_mem[0]), bf(w_mem_q[0]), 512, mw, "mem_q")
    om_p = _mem_prompt(qm, memkv)
    qm_s = qm[SEQ:].reshape(nb, ns, MEM_HEADS, MEM_HEAD_DIM).transpose(0, 2, 1, 3)
    om_s = _mem_sample(qm_s.reshape(nb, MEM_HEADS * ns, MEM_HEAD_DIM),
                       as_rows(cache_k_mem), as_rows(cache_v_mem))
    om_s = om_s.reshape(nb, MEM_HEADS, ns, MEM_HEAD_DIM).transpose(0, 2, 1, 3).reshape(nb * ns, mw)
    h3 = _matmul_res(h2, [jnp.concatenate([om_p, om_s], axis=0)], [bf(w_mem_o[0])], 512, "mem_out")

    y = _ffn(h3, row(norm_ffn2[0]), bf(ffn2_w_gate[0]), bf(ffn2_w_up[0]), bf(ffn2_w_down[0]),
             row(norm_final), True)

    mem_shape = (1, 1, MEM_TOKENS, MEM_HEADS, MEM_HEAD_DIM)
    win_shape = (1, nb, WINDOW, NSA_KV_HEADS, HEAD_DIM)
    return (y[:SEQ].reshape(1, SEQ, D_MODEL), y[SEQ:].reshape(nb, ns, D_MODEL),
            rows_p[0], rows_p[1], rows_p[2], rows_p[3],
            rows_p[4][:, :, SEQ - WINDOW:], rows_p[5][:, :, SEQ - WINDOW:],
            st_p.transpose(0, 2, 1).reshape(1, 1, GLA_HEADS, GLA_DK, GLA_DV),
            memkv[:, :mw].reshape(mem_shape), memkv[:, mw:].reshape(mem_shape),
            rows_s[0], rows_s[1], rows_s[2], rows_s[3],
            kwin_s.reshape(win_shape), vwin_s.reshape(win_shape),
            st_s.reshape(1, nb, GLA_HEADS, GLA_DK, GLA_DV))
```

```python
import functools
import math

import numpy as np
import jax
import jax.numpy as jnp
from jax import lax
from jax.experimental import pallas as pl
from jax.experimental.pallas import tpu as pltpu

F32 = jnp.float32
BF16 = jnp.bfloat16

D_MODEL = 2048
SEQ = 8192
DEC_BATCH = 128
DEC_SEQ = 4
PAST_LEN = 2048
PAGE_SIZE = 128
N_PAGES = PAST_LEN // PAGE_SIZE
HEAD_DIM = 128
NSA_HEADS = 8
NSA_KV_HEADS = 2
NSA_GROUP = 4
CMP_BLOCK = 32
CMP_STRIDE = 16
SEL_BLOCK = 64
SEL_TOPN = 16
WINDOW = 512
FORCED_SCORE = 1.0e4
GLA_HEADS = 4
GLA_DV = 256
GLA_DK = 128
GLA_RANK = 16
GLA_TAU = 16.0
GLA_CHUNK = 32
MEM_TOKENS = 256
MEM_HEADS = 4
MEM_HEAD_DIM = 128
D_FF = 5632
NUM_BUCKETS = 32
MAX_DISTANCE = 128
RMS_EPS = 1e-6

N_TOK = SEQ + DEC_BATCH * DEC_SEQ
Z_QN, Z_VG, Z_RG, Z_KV, Z_QG, Z_KG, Z_MISC = 0, 1024, 2048, 3072, 4608, 5120, 5632
Z_W = 5760
MISC_GATES, MISC_A = 0, 24

NEG = -1e30
M_INIT = -1e29

Q_TILE = 128
N_QT = SEQ // Q_TILE
N_CMP_PAD = SEQ // CMP_STRIDE
SLAB = N_CMP_PAD + 128
SLAB_OFF = SLAB - 16
KC_ROWS = SLAB_OFF + N_CMP_PAD + 16
N_SEL = SEQ // SEL_BLOCK
REL0 = N_SEL - 2
SWEEP_TILES = 8
FLASH_SPLIT = 2

VMEM_LIMIT = 56 * 1024 * 1024


def _cparams(sem):
    return pltpu.CompilerParams(dimension_semantics=sem, vmem_limit_bytes=VMEM_LIMIT)


def _dot(a, b):
    return jnp.dot(a, b, preferred_element_type=F32)


def _dot_nt(a, b):
    return lax.dot_general(a, b, (((1,), (1,)), ((), ())), preferred_element_type=F32)


def _dot_tn(a, b):
    return lax.dot_general(a, b, (((0,), (0,)), ((), ())), preferred_element_type=F32)


def _rms(x, g):
    return x * lax.rsqrt(jnp.mean(x * x, axis=-1, keepdims=True) + RMS_EPS) * g


def _silu(x):
    return x * jax.nn.sigmoid(x)


def _ffn_kernel(*refs, n_ff, n_first, two_in, two_out):
    refs = list(refs)
    x_refs = [refs.pop(0) for _ in range(2 if two_in else 1)]
    g_ref, wg_ref, wu_ref, wd_ref, gf_ref = refs[:5]
    o_refs = refs[5:7] if two_out else refs[5:6]
    xn_ref, acc_ref = refs[-2:]
    i = pl.program_id(0)
    j = pl.program_id(1)

    def x_tile():
        return jnp.where(i < n_first, x_refs[0][...], x_refs[1][...]) if two_in else x_refs[0][...]

    @pl.when(j == 0)
    def _():
        xn_ref[...] = _rms(x_tile(), g_ref[...]).astype(BF16)
        acc_ref[...] = jnp.zeros_like(acc_ref)

    xn = xn_ref[...]
    hid = _silu(_dot(xn, wg_ref[...])) * _dot(xn, wu_ref[...])
    acc_ref[...] += _dot(hid.astype(BF16), wd_ref[...])

    @pl.when(j == n_ff - 1)
    def _():
        h = x_tile() + 0.5 * acc_ref[...]
        if two_out:
            h = _rms(h, gf_ref[...])

            @pl.when(i < n_first)
            def _():
                o_refs[0][...] = h

            @pl.when(i >= n_first)
            def _():
                o_refs[1][...] = h
        else:
            o_refs[0][...] = h


FFN_TM = 512
FFN_TF = 512


def _ffn(xs, g, wg, wu, wd, gf, split_out):
    tm, tf, d = FFN_TM, FFN_TF, D_MODEL
    n_ff = D_FF // tf
    n_first = SEQ // tm
    two_in = len(xs) == 2

    def first(i, j):
        return (jnp.minimum(i, n_first - 1), 0)

    def second(i, j):
        return (jnp.maximum(i - n_first, 0), 0)
    whole = pl.BlockSpec((tm, d), lambda i, j: (i, 0))
    pair = [pl.BlockSpec((tm, d), first), pl.BlockSpec((tm, d), second)]
    n_s = DEC_BATCH * DEC_SEQ
    return pl.pallas_call(
        functools.partial(_ffn_kernel, n_ff=n_ff, n_first=n_first, two_in=two_in, two_out=split_out),
        grid=(N_TOK // tm, n_ff),
        in_specs=(pair if two_in else [whole])
        + [pl.BlockSpec((1, d), lambda i, j: (0, 0)),
           pl.BlockSpec((d, tf), lambda i, j: (0, j)),
           pl.BlockSpec((d, tf), lambda i, j: (0, j)),
           pl.BlockSpec((tf, d), lambda i, j: (j, 0)),
           pl.BlockSpec((1, d), lambda i, j: (0, 0))],
        out_specs=pair if split_out else whole,
        out_shape=([jax.ShapeDtypeStruct((SEQ, d), F32), jax.ShapeDtypeStruct((n_s, d), F32)] if split_out
                   else jax.ShapeDtypeStruct((N_TOK, d), F32)),
        scratch_shapes=[pltpu.VMEM((tm, d), BF16), pltpu.VMEM((tm, d), F32)],
        compiler_params=_cparams(("arbitrary", "arbitrary")),
        name="ffn",
    )(*xs, g, wg, wu, wd, gf)


def _norm_matmul_kernel(x_ref, g_ref, w_ref, o_ref, xn_ref):
    @pl.when(pl.program_id(1) == 0)
    def _():
        xn_ref[...] = _rms(x_ref[...], g_ref[...]).astype(BF16)

    o_ref[...] = _dot(xn_ref[...], w_ref[...])


def _norm_matmul(x, g, w, tm, tn, name):
    n, d = x.shape
    dout = w.shape[1]
    return pl.pallas_call(
        _norm_matmul_kernel,
        grid=(n // tm, dout // tn),
        in_specs=[pl.BlockSpec((tm, d), lambda i, j: (i, 0)),
                  pl.BlockSpec((1, d), lambda i, j: (0, 0)),
                  pl.BlockSpec((d, tn), lambda i, j: (0, j))],
        out_specs=pl.BlockSpec((tm, tn), lambda i, j: (i, j)),
        out_shape=jax.ShapeDtypeStruct((n, dout), F32),
        scratch_shapes=[pltpu.VMEM((tm, d), BF16)],
        compiler_params=_cparams(("parallel", "arbitrary")),
        name=name,
    )(x, g, w)


def _matmul_res_kernel(*refs, n_lhs):
    res_ref = refs[0]
    o_ref = refs[1 + 2 * n_lhs]
    acc = res_ref[...]
    for i in range(n_lhs):
        acc = acc + _dot(refs[1 + i][...].astype(BF16), refs[1 + n_lhs + i][...])
    o_ref[...] = acc


def _matmul_res(res, lhs, ws, tm, name):
    n, d = res.shape
    n_lhs = len(lhs)
    return pl.pallas_call(
        functools.partial(_matmul_res_kernel, n_lhs=n_lhs),
        grid=(n // tm,),
        in_specs=([pl.BlockSpec((tm, d), lambda i: (i, 0))]
                  + [pl.BlockSpec((tm, a.shape[1]), lambda i: (i, 0)) for a in lhs]
                  + [pl.BlockSpec(w.shape, lambda i: (0, 0)) for w in ws]),
        out_specs=pl.BlockSpec((tm, d), lambda i: (i, 0)),
        out_shape=jax.ShapeDtypeStruct((n, d), F32),
        compiler_params=_cparams(("parallel",)),
        name=name,
    )(res, *lhs, *ws)


def _rel_bucket_np(dist):
    n = np.maximum(dist, 0)
    exact = NUM_BUCKETS // 2
    nf = np.maximum(n, 1).astype(np.float32)
    large = exact + (np.log(nf / np.float32(exact)) / np.float32(math.log(MAX_DISTANCE / exact))
                     * np.float32(NUM_BUCKETS - exact)).astype(np.int32)
    return np.where(n < exact, n, np.minimum(large, NUM_BUCKETS - 1)).astype(np.int32)


def _bucket_or_masked(dist, valid):
    return np.where(valid, _rel_bucket_np(dist), -1).astype(np.int32)


def _prompt_bucket_table():
    i = np.arange(Q_TILE)[:, None]
    j = np.arange(Q_TILE)[None, :]
    u = np.arange(SLAB)[None, :]
    dist_c = i - CMP_STRIDE * u + (CMP_STRIDE * SLAB_OFF - (CMP_BLOCK - 1))
    diag = _bucket_or_masked(i - j, i - j >= 0)
    prev = _bucket_or_masked(Q_TILE + i - j, np.ones((Q_TILE, Q_TILE), bool))
    first = _bucket_or_masked(WINDOW + i - j, j > i)
    cmp_ = _bucket_or_masked(dist_c, dist_c >= 0)
    return np.concatenate([diag, prev, first, cmp_], axis=1).T


S_CMP = PAST_LEN // CMP_STRIDE
S_CMP_COLS = NSA_KV_HEADS * S_CMP
S_NEW_COLS = 128
S_SEL_COLS = NSA_KV_HEADS * PAST_LEN + S_NEW_COLS
S_WIN_COLS = NSA_KV_HEADS * WINDOW


def _sample_bucket_table():
    i = np.arange(DEC_SEQ)[:, None]
    pos = PAST_LEN + i
    rows = []
    for h in range(NSA_KV_HEADS):
        col = np.arange(S_CMP_COLS)[None, :]
        c = col % S_CMP
        dist_c = pos - (c * CMP_STRIDE + CMP_BLOCK - 1)
        cmp_ = _bucket_or_masked(dist_c, (col // S_CMP == h) & (c < S_CMP - 1) & (dist_c >= 0))
        col = np.arange(S_SEL_COLS)[None, :]
        key = col // NSA_KV_HEADS
        sel = _bucket_or_masked(pos - key, (col % NSA_KV_HEADS == h) & (key <= pos))
        col = np.arange(S_WIN_COLS)[None, :]
        dist_w = pos - (PAST_LEN - WINDOW + col // NSA_KV_HEADS)
        win = _bucket_or_masked(dist_w, (col % NSA_KV_HEADS == h) & (dist_w < WINDOW))
        rows.append(np.concatenate([cmp_, sel, win], axis=1))
    return np.concatenate(rows, axis=0)


def _bias_table_kernel(tab_ref, idx_ref, o_ref):
    h = pl.program_id(0)
    idx = idx_ref[...]
    out = jnp.full(idx.shape, NEG, F32)
    for b in range(NUM_BUCKETS):
        out = jnp.where(idx == b, tab_ref[b, h], out)
    o_ref[0] = out


def _bias_tables(rel_bias, idx, name):
    r, c = idx.shape
    return pl.pallas_call(
        _bias_table_kernel,
        grid=(NSA_HEADS,),
        in_specs=[pl.BlockSpec(memory_space=pltpu.SMEM),
                  pl.BlockSpec((r, c), lambda h: (0, 0))],
        out_specs=pl.BlockSpec((1, r, c), lambda h: (h, 0, 0)),
        out_shape=jax.ShapeDtypeStruct((NSA_HEADS, r, c), F32),
        compiler_params=_cparams(("arbitrary",)),
        name=name,
    )(rel_bias, jnp.asarray(idx))


def _cover_np(n_cmp_cols, n_blk_cols, delta_of):
    u = np.arange(n_cmp_cols)[:, None]
    j = np.arange(n_blk_cols)[None, :]
    delta = delta_of(u, j)
    shared = np.minimum(CMP_STRIDE * delta + CMP_BLOCK, SEL_BLOCK) - np.maximum(CMP_STRIDE * delta, 0)
    return (np.maximum(shared, 0) / CMP_STRIDE).astype(np.float32)


def _compress(load_rows, n_chunk, pos_ref, w1_ref, w2_ref):
    first = jnp.zeros((n_chunk, HEAD_DIM), F32)
    second = jnp.zeros((n_chunk, HEAD_DIM), F32)
    for s in range(CMP_STRIDE):
        r = load_rows(s)
        first = first + _dot((r + pos_ref[s:s + 1, :]).astype(BF16), w1_ref[s])
        second = second + _dot((r + pos_ref[CMP_STRIDE + s:CMP_STRIDE + s + 1, :]).astype(BF16),
                               w1_ref[CMP_STRIDE + s])
    nxt = pltpu.roll(second, n_chunk - 1, axis=0)
    hid = _silu(first + nxt)
    return _dot(hid.astype(BF16), w2_ref[...])


def _flash_step(carry, s, v_t):
    outs = []
    w = s.shape[1] // FLASH_SPLIT
    for c in range(FLASH_SPLIT):
        m, l, acc = (x[:, c * w:(c + 1) * w] for x in carry)
        sc = s[:, c * w:(c + 1) * w]
        m_new = jnp.maximum(m, jnp.max(sc, axis=0, keepdims=True))
        alpha = jnp.exp(m - m_new)
        e = jnp.exp(sc - m_new)
        l = alpha * l + jnp.sum(e, axis=0, keepdims=True)
        acc = alpha * acc + _dot(v_t, e.astype(BF16))
        outs.append((m_new, l, acc))
    return tuple(jnp.concatenate([o[i] for o in outs], axis=1) for i in range(3))


def _flash_init(cols):
    return (jnp.full((1, cols), M_INIT, F32), jnp.zeros((1, cols), F32), jnp.zeros((HEAD_DIM, cols), F32))


def _masked_softmax(s, valid):
    s = jnp.where(valid, s, NEG)
    m = jnp.max(s, axis=-1, keepdims=True)
    e = jnp.where(valid, jnp.exp(s - m), 0.0)
    return e / jnp.maximum(jnp.sum(e, axis=-1, keepdims=True), 1e-30)


def _split_dot(x, w):
    hi = x.astype(BF16)
    lo = (x - hi.astype(F32)).astype(BF16)
    return _dot(hi, w) + _dot(lo, w)


def _top_n_mask(score, index, axis):
    sel = jnp.zeros(score.shape, F32)
    for _ in range(SEL_TOPN):
        mx = jnp.max(score, axis=axis, keepdims=True)
        first = jnp.min(jnp.where(score == mx, index, 1e9), axis=axis, keepdims=True)
        hit = index == first
        sel = jnp.where(hit, 1.0, sel)
        score = jnp.where(hit, -jnp.inf, score)
    return sel


def _top_n_mask_by_rank(score, n_cand):
    lane = lax.broadcasted_iota(jnp.int32, score.shape, 1)
    rank = jnp.zeros(score.shape, F32)
    for j in range(n_cand):
        col = score[:, j:j + 1]
        ahead = (col > score) | ((col == score) & (lane > j))
        rank = rank + ahead.astype(F32)
    return ((rank < SEL_TOPN) & (lane < n_cand)).astype(F32)


def _compress_prompt_kernel(rows_ref, pos_ref, w1_ref, w2_ref, o_ref):
    out = _compress(lambda s: rows_ref[pl.ds(s, N_CMP_PAD, stride=CMP_STRIDE), :], N_CMP_PAD,
                    pos_ref.at[0], w1_ref.at[0], w2_ref.at[0])
    real = lax.broadcasted_iota(jnp.int32, (N_CMP_PAD, HEAD_DIM), 0) < N_CMP_PAD - 1
    o_ref[0, 0, 0:SLAB_OFF, :] = jnp.zeros((SLAB_OFF, HEAD_DIM), F32)
    o_ref[0, 0, SLAB_OFF:SLAB_OFF + N_CMP_PAD, :] = jnp.where(real, out, 0.0)
    o_ref[0, 0, SLAB_OFF + N_CMP_PAD:KC_ROWS, :] = jnp.zeros((KC_ROWS - SLAB_OFF - N_CMP_PAD, HEAD_DIM), F32)


def _compress_prompt(z, pos, w1, w2):
    kv_blk = Z_KV // HEAD_DIM
    return pl.pallas_call(
        _compress_prompt_kernel,
        grid=(2, NSA_KV_HEADS),
        in_specs=[pl.BlockSpec((SEQ, HEAD_DIM), lambda i, h: (0, kv_blk + NSA_KV_HEADS * i + h)),
                  pl.BlockSpec((1, CMP_BLOCK, HEAD_DIM), lambda i, h: (i, 0, 0)),
                  pl.BlockSpec((1, CMP_BLOCK, HEAD_DIM, HEAD_DIM), lambda i, h: (i, 0, 0, 0)),
                  pl.BlockSpec((1, HEAD_DIM, HEAD_DIM), lambda i, h: (i, 0, 0))],
        out_specs=pl.BlockSpec((1, 1, KC_ROWS, HEAD_DIM), lambda i, h: (i, h, 0, 0)),
        out_shape=jax.ShapeDtypeStruct((2, NSA_KV_HEADS, KC_ROWS, HEAD_DIM), F32),
        compiler_params=_cparams(("parallel", "parallel")),
        name="compress_prompt",
    )(z, pos, w1, w2)


def _nsa_prompt_kernel(q_ref, misc_ref, ksel_ref, vsel_rows_ref, kwin_ref, vwin_rows_ref, kc_ref, vc_ref,
                       bias_ref, cover_ref, erel_ref, o_ref, vsel_ref, vwin_ref):
    kvh = pl.program_id(0)
    t = pl.program_id(1)
    cols = NSA_GROUP * Q_TILE

    @pl.when(t == 0)
    def _():
        def body(kt, _):
            rows = pl.ds(pl.multiple_of(kt * Q_TILE, Q_TILE), Q_TILE)
            vsel_ref[:, rows] = vsel_rows_ref[rows, :].astype(F32).T.astype(BF16)
            vwin_ref[:, rows] = vwin_rows_ref[rows, :].astype(F32).T.astype(BF16)
            return 0
        lax.fori_loop(0, N_QT, body, 0)

    q = q_ref[...] * (HEAD_DIM ** -0.5)
    qt = jnp.concatenate([q[:, g * HEAD_DIM:(g + 1) * HEAD_DIM].T for g in range(NSA_GROUP)], axis=1).astype(BF16)

    def bias_tile(k):
        return bias_ref[0, k * Q_TILE:(k + 1) * Q_TILE, :]
    b_diag, b_prev, b_first = bias_tile(0), bias_tile(1), bias_tile(2)

    def key_tile(ref, kt):
        return ref[pl.ds(pl.multiple_of(kt * Q_TILE, Q_TILE), Q_TILE), :]

    def value_tile(ref, kt):
        return ref[:, pl.ds(pl.multiple_of(kt * Q_TILE, Q_TILE), Q_TILE)]

    n_wt = WINDOW // Q_TILE + 1
    scores, values = [], []
    for w, b_tile in enumerate((b_first,) + (None,) * (n_wt - 3) + (b_prev, b_diag)):
        kt = t - (n_wt - 1) + w
        kc = jnp.maximum(kt, 0)
        s = _dot(key_tile(kwin_ref, kc), qt) + jnp.where(kt >= 0, 0.0, NEG)
        scores.append(s if b_tile is None else s + b_tile)
        values.append(value_tile(vwin_ref, kc))
    _, l, acc = _flash_step(_flash_init(cols), jnp.concatenate(scores, axis=0), jnp.concatenate(values, axis=1))
    o_win = acc / l

    start = pl.multiple_of(t * (Q_TILE // CMP_STRIDE), 8)
    kslab = kc_ref[0, 0, pl.ds(start, SLAB), :].astype(BF16)
    vslab = vc_ref[0, 0, pl.ds(start, SLAB), :].astype(BF16)
    b_cmp = bias_ref[0, 3 * Q_TILE:3 * Q_TILE + SLAB, :]
    u = lax.broadcasted_iota(jnp.int32, (SLAB, 1), 0)
    valid = (b_cmp > M_INIT) & (u >= SLAB_OFF - (Q_TILE // CMP_STRIDE) * t)
    s = jnp.where(valid, _dot(kslab, qt) + b_cmp, NEG)
    e = jnp.where(valid, jnp.exp(s - jnp.max(s, axis=0, keepdims=True)), 0.0)
    p_cmp = e / jnp.maximum(jnp.sum(e, axis=0, keepdims=True), 1e-30)
    o_cmp = _dot_tn(vslab, p_cmp.astype(BF16))

    p_sum = p_cmp[:, 0:Q_TILE]
    for g in range(1, NSA_GROUP):
        p_sum = p_sum + p_cmp[:, g * Q_TILE:(g + 1) * Q_TILE]
    hi = p_sum.astype(BF16)
    lo = (p_sum - hi.astype(F32)).astype(BF16)
    imp = _dot(cover_ref[...], hi) + _dot(cover_ref[...], lo)
    jr = lax.broadcasted_iota(jnp.int32, (N_SEL, Q_TILE), 0)
    qi = lax.broadcasted_iota(jnp.int32, (N_SEL, Q_TILE), 1)
    cur = REL0 + (qi >= SEL_BLOCK).astype(jnp.int32)
    first_blk = REL0 - 2 * t
    forced = (jr == first_blk) | (jr == cur) | (jr == cur - 1)
    in_range = (jr <= cur) & (jr >= first_blk)
    score = jnp.where(in_range, jnp.where(forced, FORCED_SCORE, imp), NEG)
    sel = _top_n_mask(score, jr.astype(F32), 0)

    unsel = ((sel - 1.0) * (-NEG)).astype(BF16)
    q_aug = jnp.concatenate([qt, jnp.concatenate([unsel] * NSA_GROUP, axis=1)], axis=0)

    def sweep_step(i, carry, masked, biases=()):
        scores, values = [], []
        for j in range(SWEEP_TILES):
            kt = t - SWEEP_TILES * i - (SWEEP_TILES - 1) + j
            kc = jnp.maximum(kt, 0) if masked else kt
            tile = jnp.concatenate([key_tile(ksel_ref, kc), erel_ref[t - kc]], axis=1)
            s = _dot(tile, q_aug)
            if masked:
                s = s + jnp.where(kt >= 0, 0.0, NEG)
            if j >= SWEEP_TILES - len(biases):
                s = s + biases[j - (SWEEP_TILES - len(biases))]
            scores.append(s)
            values.append(value_tile(vsel_ref, kc))
        return _flash_step(carry, jnp.concatenate(scores, axis=0), jnp.concatenate(values, axis=1))

    n_full = jnp.maximum(t - (SWEEP_TILES - 1), 0) // SWEEP_TILES
    carry = lax.fori_loop(1, n_full + 1, lambda i, c: sweep_step(i, c, False), _flash_init(cols))
    some_left = (t - SWEEP_TILES * (n_full + 1) >= 0).astype(jnp.int32)
    carry = lax.fori_loop(0, some_left, lambda _, c: sweep_step(n_full + 1, c, True), carry)
    _, l, acc = sweep_step(0, carry, True, (b_prev, b_diag))
    o_sel = acc / l

    gates = jax.nn.sigmoid(misc_ref[...]).T
    n_g = 3 * NSA_GROUP
    gk = jnp.where(kvh == 0, gates, pltpu.roll(gates, gates.shape[0] - n_g, axis=0))[MISC_GATES:MISC_GATES + n_g]
    for g in range(NSA_GROUP):
        c = slice(g * Q_TILE, (g + 1) * Q_TILE)
        o_g = (gk[3 * g:3 * g + 1] * o_cmp[:, c] + gk[3 * g + 1:3 * g + 2] * o_sel[:, c]
               + gk[3 * g + 2:3 * g + 3] * o_win[:, c])
        o_ref[:, g * HEAD_DIM:(g + 1) * HEAD_DIM] = o_g.T


def _nsa_prompt(z, kvb, kcp, bias, cover, erel):
    gw = NSA_GROUP * HEAD_DIM

    def k_spec(j):
        return pl.BlockSpec((SEQ, HEAD_DIM), lambda h, t, j=j: (0, 2 * j + h))
    return pl.pallas_call(
        _nsa_prompt_kernel,
        grid=(NSA_KV_HEADS, N_QT),
        in_specs=[pl.BlockSpec((Q_TILE, gw), lambda h, t: (t, h)),
                  pl.BlockSpec((Q_TILE, 128), lambda h, t: (t, Z_MISC // 128)),
                  k_spec(2), k_spec(3), k_spec(4), k_spec(5),
                  pl.BlockSpec((1, 1, KC_ROWS, HEAD_DIM), lambda h, t: (0, h, 0, 0)),
                  pl.BlockSpec((1, 1, KC_ROWS, HEAD_DIM), lambda h, t: (1, h, 0, 0)),
                  pl.BlockSpec((1,) + bias.shape[1:], lambda h, t: (h, 0, 0)),
                  pl.BlockSpec(cover.shape, lambda h, t: (0, 0)),
                  pl.BlockSpec(erel.shape, lambda h, t: (0, 0, 0))],
        out_specs=pl.BlockSpec((Q_TILE, gw), lambda h, t: (t, h)),
        out_shape=jax.ShapeDtypeStruct((SEQ, NSA_HEADS * HEAD_DIM), F32),
        scratch_shapes=[pltpu.VMEM((HEAD_DIM, SEQ), BF16), pltpu.VMEM((HEAD_DIM, SEQ), BF16)],
        compiler_params=_cparams(("arbitrary", "arbitrary")),
        name="nsa_prompt",
    )(z, z, kvb, kvb, kvb, kvb, kcp, kcp, bias, cover, erel)


S_ROWS = NSA_HEADS * DEC_SEQ
PAGE_ROWS = NSA_KV_HEADS * PAGE_SIZE
NEW_ROWS = NSA_KV_HEADS * DEC_SEQ
CHUNK_ROWS = NSA_KV_HEADS * CMP_STRIDE
CHUNK_PITCH = CHUNK_ROWS + 4


def _nsa_sample_kernel(pt_ref, *refs):
    del pt_ref
    n_pg = N_PAGES
    q_ref, gate_ref, new_ref, kwin_ref, vwin_ref = refs[0:5]
    pages = [refs[5 + k * n_pg:5 + (k + 1) * n_pg] for k in range(4)]
    (posk_ref, w1k_ref, w2k_ref, posv_ref, w1v_ref, w2v_ref,
     bias_ref, cover_ref, expand_ref) = refs[5 + 4 * n_pg:14 + 4 * n_pg]
    o_ref, kwin_o_ref, vwin_o_ref = refs[14 + 4 * n_pg:17 + 4 * n_pg]
    past_ref, newk_ref, newv_ref = refs[17 + 4 * n_pg:20 + 4 * n_pg]

    comp = []
    for k, (pos_ref, w1_ref, w2_ref) in enumerate(((posk_ref, w1k_ref, w2k_ref), (posv_ref, w1v_ref, w2v_ref))):
        for p in range(n_pg):
            for c in range(PAGE_ROWS // CHUNK_ROWS):
                dst = (p * (PAGE_ROWS // CHUNK_ROWS) + c) * CHUNK_PITCH
                past_ref[dst:dst + CHUNK_ROWS, :] = pages[k][p][c * CHUNK_ROWS:(c + 1) * CHUNK_ROWS, :]

        def load_rows(s):
            return jnp.concatenate(
                [past_ref[pl.ds(NSA_KV_HEADS * s + h, S_CMP, stride=CHUNK_PITCH), :]
                 for h in range(NSA_KV_HEADS)], axis=0)
        comp.append(_compress(load_rows, S_CMP_COLS, pos_ref, w1_ref, w2_ref).astype(BF16))

    keep = NSA_KV_HEADS * WINDOW - NEW_ROWS
    kwin_o_ref[0:keep, :] = kwin_ref[NEW_ROWS:NSA_KV_HEADS * WINDOW, :]
    kwin_o_ref[keep:keep + NEW_ROWS, :] = new_ref[0, 4]
    vwin_o_ref[0:keep, :] = vwin_ref[NEW_ROWS:NSA_KV_HEADS * WINDOW, :]
    vwin_o_ref[keep:keep + NEW_ROWS, :] = new_ref[0, 5]

    qs = (q_ref[0] * (HEAD_DIM ** -0.5)).astype(BF16)
    o0, o1 = S_CMP_COLS, S_CMP_COLS + S_SEL_COLS
    b_cmp = bias_ref[:, 0:o0]
    b_new = bias_ref[:, o1 - S_NEW_COLS:o1]

    def attend(scores, bias, values):
        s = jnp.concatenate(scores, axis=1) + bias
        e = jnp.exp(s - jnp.max(s, axis=-1, keepdims=True))
        acc = jnp.zeros((S_ROWS, HEAD_DIM), F32)
        c0 = 0
        for v in values:
            acc = acc + _dot(e[:, c0:c0 + v.shape[0]].astype(BF16), v)
            c0 += v.shape[0]
        return acc / jnp.sum(e, axis=-1, keepdims=True)

    p_cmp = _masked_softmax(_dot_nt(qs, comp[0]) + b_cmp, b_cmp > M_INIT)
    o_cmp = _dot(p_cmp.astype(BF16), comp[1])

    imp = _split_dot(p_cmp, cover_ref[...])
    imp = imp + pltpu.roll(imp, 8, axis=0) + pltpu.roll(imp, 16, axis=0) + pltpu.roll(imp, 24, axis=0)
    blk = lax.broadcasted_iota(jnp.int32, (S_ROWS, 128), 1)
    cur = PAST_LEN // SEL_BLOCK
    forced = (blk == 0) | (blk == cur) | (blk == cur - 1)
    score = jnp.where(blk <= cur, jnp.where(forced, FORCED_SCORE, imp), NEG)
    sel = _top_n_mask_by_rank(score, cur + 1).astype(BF16)
    mask_add = (_dot(sel, expand_ref[...]) - 1.0) * (-NEG)

    def new_tile(k_ref, v_ref, jk, jv):
        k_ref[...] = jnp.zeros_like(k_ref)
        v_ref[...] = jnp.zeros_like(v_ref)
        k_ref[0:NEW_ROWS, :] = new_ref[0, jk]
        v_ref[0:NEW_ROWS, :] = new_ref[0, jv]
        return k_ref[...].astype(BF16), v_ref[...].astype(BF16)

    nk, nv = new_tile(newk_ref, newv_ref, 2, 3)
    scores = [_dot_nt(qs, pages[2][p][...].astype(BF16)) for p in range(n_pg)] + [_dot_nt(qs, nk)]
    values = [pages[3][p][...].astype(BF16) for p in range(n_pg)] + [nv]
    o_sel = attend(scores, bias_ref[:, o0:o1] + mask_add, values)

    nk, nv = new_tile(newk_ref, newv_ref, 4, 5)
    scores = [_dot_nt(qs, kwin_ref[...].astype(BF16)), _dot_nt(qs, nk)]
    bias_w = jnp.concatenate([bias_ref[:, o1:o1 + S_WIN_COLS], b_new], axis=1)
    o_win = attend(scores, bias_w, [vwin_ref[...].astype(BF16), nv])

    g = jax.nn.sigmoid(gate_ref[0])
    o_ref[0] = g[:, 0:1] * o_cmp + g[:, 1:2] * o_sel + g[:, 2:3] * o_win


def _nsa_sample(page_table, q_s, gate_s, new_s, kwin, vwin, pools, cmp_w, bias, cover, expand):
    win_rows = NSA_KV_HEADS * WINDOW

    def full(a):
        return pl.BlockSpec(a.shape, lambda b, pt, n=a.ndim: (0,) * n)

    def per_b(a):
        return pl.BlockSpec((1,) + a.shape[1:], lambda b, pt, n=a.ndim: (b,) + (0,) * (n - 1))
    win_spec = pl.BlockSpec((win_rows, HEAD_DIM), lambda b, pt: (b, 0))
    page_specs = [pl.BlockSpec((PAGE_ROWS, HEAD_DIM), lambda b, pt, p=p: (pt[b, p], 0))
                  for _ in range(4) for p in range(N_PAGES)]
    page_args = [pool for pool in pools for _ in range(N_PAGES)]
    consts = list(cmp_w) + [bias, cover, expand]
    grid_spec = pltpu.PrefetchScalarGridSpec(
        num_scalar_prefetch=1,
        grid=(DEC_BATCH,),
        in_specs=[per_b(q_s), per_b(gate_s), per_b(new_s), win_spec, win_spec]
        + page_specs + [full(a) for a in consts],
        out_specs=[per_b(q_s), win_spec, win_spec],
        scratch_shapes=[pltpu.VMEM((S_CMP * CHUNK_PITCH, HEAD_DIM), F32),
                        pltpu.VMEM((S_NEW_COLS, HEAD_DIM), F32),
                        pltpu.VMEM((S_NEW_COLS, HEAD_DIM), F32)],
    )
    return pl.pallas_call(
        _nsa_sample_kernel,
        grid_spec=grid_spec,
        out_shape=[jax.ShapeDtypeStruct(q_s.shape, F32),
                   jax.ShapeDtypeStruct(kwin.shape, F32),
                   jax.ShapeDtypeStruct(vwin.shape, F32)],
        compiler_params=_cparams(("arbitrary",)),
        name="nsa_sample",
    )(page_table, q_s, gate_s, new_s, kwin, vwin, *page_args, *consts)


def _log_decay(a_blk, wa_ref, ba_ref):
    x = _dot(a_blk.astype(BF16), wa_ref[...]) + ba_ref[...]
    return (jnp.minimum(x, 0.0) - jnp.log1p(jnp.exp(-jnp.abs(x)))) * (1.0 / GLA_TAU)


def _segment_cumsum(g, seg):
    pos = lax.broadcasted_iota(jnp.int32, g.shape, 0) % seg
    cum = g
    sh = 1
    while sh < seg:
        cum = cum + jnp.where(pos >= sh, pltpu.roll(cum, sh, axis=0), 0.0)
        sh *= 2
    return cum


def _gla_prompt_kernel(q_ref, k_ref, v_ref, r_ref, a_ref, wa_ref, ba_ref, gn_ref, o_ref, st_o_ref, st_ref,
                       *, n_blk, tb):
    tbi = pl.program_id(1)

    @pl.when(tbi == 0)
    def _():
        st_ref[...] = jnp.zeros_like(st_ref)

    c = GLA_CHUNK
    cum = _segment_cumsum(_log_decay(a_ref[...], wa_ref, ba_ref), c)
    q = q_ref[...] * (GLA_DK ** -0.5)
    k = k_ref[...]
    v = v_ref[...].astype(BF16)
    qe = (q * jnp.exp(cum)).astype(BF16)
    kd = (k * jnp.exp(-cum)).astype(BF16)
    tril = lax.broadcasted_iota(jnp.int32, (c, c), 0) >= lax.broadcasted_iota(jnp.int32, (c, c), 1)
    st = st_ref[...]
    outs = []
    for ci in range(tb // c):
        r = slice(ci * c, (ci + 1) * c)
        last = cum[ci * c + c - 1:ci * c + c, :]
        att = jnp.where(tril, _dot_nt(qe[r], kd[r]), 0.0)
        outs.append(_dot_nt(qe[r], st.astype(BF16)) + _dot(att.astype(BF16), v[r]))
        kl = (k[r] * jnp.exp(last - cum[r])).astype(BF16)
        st = jnp.exp(last) * st + _dot_tn(v[r], kl)
    st_ref[...] = st
    o = jnp.concatenate(outs, axis=0)
    o_ref[...] = _rms(o, gn_ref[...]) * _silu(r_ref[...])

    @pl.when(tbi == n_blk - 1)
    def _():
        st_o_ref[0] = st


def _gla_prompt(z, wa, ba, gn, tb=256):
    n_blk = SEQ // tb
    return pl.pallas_call(
        functools.partial(_gla_prompt_kernel, n_blk=n_blk, tb=tb),
        grid=(GLA_HEADS, n_blk),
        in_specs=[pl.BlockSpec((tb, GLA_DK), lambda h, i: (i, Z_QG // GLA_DK + h)),
                  pl.BlockSpec((tb, GLA_DK), lambda h, i: (i, Z_KG // GLA_DK + h)),
                  pl.BlockSpec((tb, GLA_DV), lambda h, i: (i, Z_VG // GLA_DV + h)),
                  pl.BlockSpec((tb, GLA_DV), lambda h, i: (i, Z_RG // GLA_DV + h)),
                  pl.BlockSpec((tb, 128), lambda h, i: (i, Z_MISC // 128)),
                  pl.BlockSpec((128, GLA_DK), lambda h, i: (0, h)),
                  pl.BlockSpec((1, GLA_DK), lambda h, i: (0, h)),
                  pl.BlockSpec((1, GLA_DV), lambda h, i: (0, 0))],
        out_specs=[pl.BlockSpec((tb, GLA_DV), lambda h, i: (i, h)),
                   pl.BlockSpec((1, GLA_DV, GLA_DK), lambda h, i: (h, 0, 0))],
        out_shape=[jax.ShapeDtypeStruct((SEQ, GLA_HEADS * GLA_DV), F32),
                   jax.ShapeDtypeStruct((GLA_HEADS, GLA_DV, GLA_DK), F32)],
        scratch_shapes=[pltpu.VMEM((GLA_DV, GLA_DK), F32)],
        compiler_params=_cparams(("parallel", "arbitrary")),
        name="gla_prompt",
    )(z, z, z, z, z, wa, ba, gn)


GS_B = 4


def _gla_sample_kernel(q_ref, k_ref, v_ref, r_ref, a_ref, wa_ref, ba_ref, gn_ref, s_ref, o_ref, s_o_ref):
    rows = GS_B * DEC_SEQ
    a = a_ref[...]
    ri = lax.broadcasted_iota(jnp.int32, (rows, rows), 0)
    ci = lax.broadcasted_iota(jnp.int32, (rows, rows), 1)
    same_causal = (ri // DEC_SEQ == ci // DEC_SEQ) & (ri >= ci)
    row_b = lax.broadcasted_iota(jnp.int32, (rows, 1), 0) // DEC_SEQ
    ones = jnp.ones((rows, 128), BF16)
    for h in range(GLA_HEADS):
        dk = slice(h * GLA_DK, (h + 1) * GLA_DK)
        dv = slice(h * GLA_DV, (h + 1) * GLA_DV)
        g = _log_decay(a, wa_ref.at[:, dk], ba_ref.at[:, dk])
        cum = _segment_cumsum(g, DEC_SEQ)
        q = q_ref[:, dk] * (GLA_DK ** -0.5)
        k = k_ref[:, dk]
        v = v_ref[:, dv].astype(BF16)
        qe = (q * jnp.exp(cum)).astype(BF16)
        kd = (k * jnp.exp(-cum)).astype(BF16)
        att = jnp.where(same_causal, _dot_nt(qe, kd), 0.0)
        o = _dot(att.astype(BF16), v)
        for b in range(GS_B):
            mine = row_b == b
            last = cum[b * DEC_SEQ + DEC_SEQ - 1:(b + 1) * DEC_SEQ, :]
            s = s_ref[b, h]
            o = o + jnp.where(mine, _dot(qe, s.astype(BF16)), 0.0)
            kl = jnp.where(mine, k * jnp.exp(last - cum), 0.0)
            hi = jnp.where(mine, g, 0.0).astype(BF16)
            lo = (jnp.where(mine, g, 0.0) - hi.astype(F32)).astype(BF16)
            last_col = (_dot_tn(hi, ones) + _dot_tn(lo, ones))[:, 0:1]
            s_o_ref[b, h] = jnp.exp(last_col) * s + _dot_tn(kl.astype(BF16), v)
        o_ref[:, dv] = _rms(o, gn_ref[...]) * _silu(r_ref[:, dv])


def _gla_sample(zs, state, wa, ba, gn):
    rows = GS_B * DEC_SEQ
    n = DEC_BATCH * DEC_SEQ
    hk, hv = GLA_HEADS * GLA_DK, GLA_HEADS * GLA_DV
    st_spec = pl.BlockSpec((GS_B, GLA_HEADS, GLA_DK, GLA_DV), lambda i: (i, 0, 0, 0))
    return pl.pallas_call(
        _gla_sample_kernel,
        grid=(DEC_BATCH // GS_B,),
        in_specs=[pl.BlockSpec((rows, hk), lambda i: (i, Z_QG // hk)),
                  pl.BlockSpec((rows, hk), lambda i: (i, Z_KG // hk)),
                  pl.BlockSpec((rows, hv), lambda i: (i, Z_VG // hv)),
                  pl.BlockSpec((rows, hv), lambda i: (i, Z_RG // hv)),
                  pl.BlockSpec((rows, 128), lambda i: (i, Z_MISC // 128)),
                  pl.BlockSpec((128, hk), lambda i: (0, 0)),
                  pl.BlockSpec((1, hk), lambda i: (0, 0)),
                  pl.BlockSpec((1, GLA_DV), lambda i: (0, 0)),
                  st_spec],
        out_specs=[pl.BlockSpec((rows, hv), lambda i: (i, 0)), st_spec],
        out_shape=[jax.ShapeDtypeStruct((n, hv), F32), jax.ShapeDtypeStruct(state.shape, F32)],
        compiler_params=_cparams(("parallel",)),
        name="gla_sample",
    )(zs, zs, zs, zs, zs, wa, ba, gn, state)


def _softmax_rows(s):
    m = jnp.max(s, axis=-1, keepdims=True)
    e = jnp.exp(s - m)
    return e / jnp.sum(e, axis=-1, keepdims=True)


def _mem_prompt_kernel(q_ref, k_ref, v_ref, o_ref):
    for h in range(MEM_HEADS):
        d = slice(h * MEM_HEAD_DIM, (h + 1) * MEM_HEAD_DIM)
        q = (q_ref[:, d] * (MEM_HEAD_DIM ** -0.5)).astype(BF16)
        p = _softmax_rows(_dot_nt(q, k_ref[:, d].astype(BF16)))
        o_ref[:, d] = _dot(p.astype(BF16), v_ref[:, d].astype(BF16))


def _mem_prompt(qm, memkv, tq=256):
    w = MEM_HEADS * MEM_HEAD_DIM
    return pl.pallas_call(
        _mem_prompt_kernel,
        grid=(SEQ // tq,),
        in_specs=[pl.BlockSpec((tq, w), lambda i: (i, 0)),
                  pl.BlockSpec((MEM_TOKENS, w), lambda i: (0, 0)),
                  pl.BlockSpec((MEM_TOKENS, w), lambda i: (0, 1))],
        out_specs=pl.BlockSpec((tq, w), lambda i: (i, 0)),
        out_shape=jax.ShapeDtypeStruct((SEQ, w), F32),
        compiler_params=_cparams(("parallel",)),
        name="mem_prompt",
    )(qm, memkv, memkv)


def _mem_sample_kernel(q_ref, k_ref, v_ref, o_ref):
    rows = MEM_HEADS * DEC_SEQ
    cols = MEM_HEADS * MEM_TOKENS
    q = (q_ref[0] * (MEM_HEAD_DIM ** -0.5)).astype(BF16)
    row_h = lax.broadcasted_iota(jnp.int32, (rows, cols), 0) // DEC_SEQ
    col_h = lax.broadcasted_iota(jnp.int32, (rows, cols), 1) % MEM_HEADS
    s = jnp.where(row_h == col_h, _dot_nt(q, k_ref[...].astype(BF16)), NEG)
    o_ref[0] = _dot(_softmax_rows(s).astype(BF16), v_ref[...].astype(BF16))


def _mem_sample(q_s, k_mem, v_mem):
    rows = MEM_HEADS * DEC_SEQ
    kv_spec = pl.BlockSpec((MEM_HEADS * MEM_TOKENS, MEM_HEAD_DIM), lambda b: (b, 0))
    return pl.pallas_call(
        _mem_sample_kernel,
        grid=(DEC_BATCH,),
        in_specs=[pl.BlockSpec((1, rows, MEM_HEAD_DIM), lambda b: (b, 0, 0)), kv_spec, kv_spec],
        out_specs=pl.BlockSpec((1, rows, MEM_HEAD_DIM), lambda b: (b, 0, 0)),
        out_shape=jax.ShapeDtypeStruct((DEC_BATCH, rows, MEM_HEAD_DIM), F32),
        compiler_params=_cparams(("parallel",)),
        name="mem_sample",
    )(q_s, k_mem, v_mem)


def _permute_w_in(w_in):
    qn, kv, gt, qg, kg, vg, rg, ag = jnp.split(
        w_in, np.cumsum([1024, 1536, 24, 512, 512, 1024, 1024, 16])[:-1].tolist(), axis=1)
    pad = jnp.zeros((w_in.shape[0], Z_W - Z_MISC - 40), w_in.dtype)
    return jnp.concatenate([qn, vg, rg, kv, qg, kg, gt, ag, pad], axis=1)


def kernel(x_prompt, x_sample, mem_prompt, cache_k_cmp, cache_v_cmp, cache_k_sel, cache_v_sel, cache_k_win,
           cache_v_win, state_gla, cache_k_mem, cache_v_mem, page_table, norm_ffn1, ffn1_w_gate, ffn1_w_up,
           ffn1_w_down, norm_mix, w_in, w_out, cmp_pos_k, cmp_w1_k, cmp_w2_k, cmp_pos_v, cmp_w1_v, cmp_w2_v,
           rel_bias, gla_w_a2, gla_b_a, gla_norm, norm_mem, norm_mem_src, w_mem_q, w_mem_k, w_mem_v, w_mem_o,
           norm_ffn2, ffn2_w_gate, ffn2_w_up, ffn2_w_down, norm_final):
    bf = lambda a: a.astype(BF16)
    row = lambda a: a.reshape(1, -1)
    nb, ns = DEC_BATCH, DEC_SEQ
    kvw = NSA_KV_HEADS * HEAD_DIM

    h1 = _ffn([x_prompt[0], x_sample.reshape(nb * ns, D_MODEL)], row(norm_ffn1[0]), bf(ffn1_w_gate[0]),
              bf(ffn1_w_up[0]), bf(ffn1_w_down[0]), row(norm_final), False)
    z = _norm_matmul(h1, row(norm_mix[0]), bf(_permute_w_in(w_in[0])), 512, Z_W // 5, "proj_in")

    kv_p = z[:SEQ, Z_KV:Z_QG]
    kv_s = z[SEQ:, Z_KV:Z_QG].reshape(nb, ns, 6 * kvw)
    rows_p = [kv_p[:, j * kvw:(j + 1) * kvw].reshape(1, 1, SEQ, NSA_KV_HEADS, HEAD_DIM) for j in range(6)]
    rows_s = [kv_s[:, :, j * kvw:(j + 1) * kvw].reshape(1, nb, ns, NSA_KV_HEADS, HEAD_DIM) for j in range(4)]

    tab_p = _bias_tables(rel_bias, _prompt_bucket_table(), "bias_prompt")
    far = rel_bias[NUM_BUCKETS - 1][:, None, None]
    near = tab_p[:, :3 * Q_TILE]
    tab_p = jnp.concatenate([jnp.where(near > M_INIT, near - far, NEG), tab_p[:, 3 * Q_TILE:]], axis=1)
    tab_p = tab_p.reshape(NSA_KV_HEADS, NSA_GROUP, -1, Q_TILE).transpose(0, 2, 1, 3)
    tab_p = tab_p.reshape(NSA_KV_HEADS, -1, NSA_GROUP * Q_TILE)
    tab_s = _bias_tables(rel_bias, _sample_bucket_table(), "bias_sample")
    tab_s = tab_s.reshape(NSA_KV_HEADS, NSA_GROUP, NSA_KV_HEADS, ns, -1)
    tab_s = jnp.stack([tab_s[h, :, h] for h in range(NSA_KV_HEADS)], axis=1).reshape(S_ROWS, -1)

    cmp_pos = jnp.stack([cmp_pos_k[0], cmp_pos_v[0]])
    cmp_w1 = bf(jnp.stack([cmp_w1_k[0], cmp_w1_v[0]]))
    cmp_w2 = bf(jnp.stack([cmp_w2_k[0], cmp_w2_v[0]]))
    kcp = _compress_prompt(z, cmp_pos, cmp_w1, cmp_w2)
    cover_p = jnp.asarray(_cover_np(SLAB, N_SEL, lambda u, j: u - 4 * j - SLAB_OFF + 4 * REL0).T, BF16)
    erel = (np.arange(N_SEL)[None, None, :] == REL0 - 2 * np.arange(N_QT)[:, None, None]
            + (np.arange(Q_TILE)[None, :, None] >= SEL_BLOCK))
    o_nsa_p = _nsa_prompt(z, bf(kv_p), kcp, tab_p, cover_p, jnp.asarray(erel, BF16))

    zs = z[SEQ:]
    q_s = zs[:, Z_QN:Z_QN + NSA_HEADS * HEAD_DIM].reshape(nb, ns, NSA_KV_HEADS, NSA_GROUP, HEAD_DIM)
    q_s = q_s.transpose(0, 3, 2, 1, 4).reshape(nb, S_ROWS, HEAD_DIM)
    gate_s = zs[:, Z_MISC + MISC_GATES:Z_MISC + MISC_GATES + 3 * NSA_HEADS]
    gate_s = gate_s.reshape(nb, ns, NSA_KV_HEADS, NSA_GROUP, 3).transpose(0, 3, 2, 1, 4).reshape(nb, S_ROWS, 3)
    new_s = kv_s.reshape(nb, ns, 6, NSA_KV_HEADS, HEAD_DIM).transpose(0, 2, 1, 3, 4).reshape(nb, 6, NEW_ROWS, HEAD_DIM)
    as_rows = lambda c: c.reshape(-1, HEAD_DIM)
    pools = [as_rows(c) for c in (cache_k_cmp, cache_v_cmp, cache_k_sel, cache_v_sel)]
    cover_s = _cover_np(S_CMP, 128, lambda c, j: c - 4 * j)
    cover_s = jnp.asarray(np.concatenate([cover_s] * NSA_KV_HEADS, axis=0), BF16)
    expand_s = jnp.asarray(np.arange(128)[:, None]
                           == (np.arange(S_SEL_COLS)[None, :] // (NSA_KV_HEADS * SEL_BLOCK)), BF16)
    o_nsa_s, kwin_s, vwin_s = _nsa_sample(
        page_table, q_s, gate_s, new_s, as_rows(cache_k_win), as_rows(cache_v_win), pools,
        (cmp_pos_k[0], bf(cmp_w1_k[0]), bf(cmp_w2_k[0]), cmp_pos_v[0], bf(cmp_w1_v[0]), bf(cmp_w2_v[0])),
        tab_s, cover_s, expand_s)
    o_nsa_s = o_nsa_s.reshape(nb, NSA_GROUP, NSA_KV_HEADS, ns, HEAD_DIM).transpose(0, 3, 2, 1, 4)
    o_nsa = jnp.concatenate([o_nsa_p, o_nsa_s.reshape(nb * ns, NSA_HEADS * HEAD_DIM)], axis=0)

    wa = bf(jnp.zeros((128, GLA_HEADS * GLA_DK), F32).at[MISC_A:MISC_A + GLA_RANK].set(gla_w_a2[0]))
    ba, gn = row(gla_b_a[0]), row(gla_norm[0])
    o_gla_p, st_p = _gla_prompt(z, wa, ba, gn)
    o_gla_s, st_s = _gla_sample(zs, state_gla[0], wa, ba, gn)
    o_gla = jnp.concatenate([o_gla_p, o_gla_s], axis=0)

    half = NSA_HEADS * HEAD_DIM
    h2 = _matmul_res(h1, [o_nsa, o_gla], [bf(w_out[0][:half]), bf(w_out[0][half:])], 512, "proj_out")

    memkv = _norm_matmul(mem_prompt[0], row(norm_mem_src[0]),
                         bf(jnp.concatenate([w_mem_k[0], w_mem_v[0]], axis=1)), MEM_TOKENS, 512, "mem_kv")
    mw = MEM_HEADS * MEM_HEAD_DIM
    qm = _norm_matmul(h2, row(norm_mem[0]), bf(w_mem_q[0]), 512, mw, "mem_q")
    om_p = _mem_prompt(qm, memkv)
    qm_s = qm[SEQ:].reshape(nb, ns, MEM_HEADS, MEM_HEAD_DIM).transpose(0, 2, 1, 3)
    om_s = _mem_sample(qm_s.reshape(nb, MEM_HEADS * ns, MEM_HEAD_DIM),
                       as_rows(cache_k_mem), as_rows(cache_v_mem))
    om_s = om_s.reshape(nb, MEM_HEADS, ns, MEM_HEAD_DIM).transpose(0, 2, 1, 3).reshape(nb * ns, mw)
    h3 = _matmul_res(h2, [jnp.concatenate([om_p, om_s], axis=0)], [bf(w_mem_o[0])], 512, "mem_out")

    y_p, y_s = _ffn([h3], row(norm_ffn2[0]), bf(ffn2_w_gate[0]), bf(ffn2_w_up[0]), bf(ffn2_w_down[0]),
                    row(norm_final), True)

    mem_shape = (1, 1, MEM_TOKENS, MEM_HEADS, MEM_HEAD_DIM)
    win_shape = (1, nb, WINDOW, NSA_KV_HEADS, HEAD_DIM)
    return (y_p.reshape(1, SEQ, D_MODEL), y_s.reshape(nb, ns, D_MODEL),
            rows_p[0], rows_p[1], rows_p[2], rows_p[3],
            rows_p[4][:, :, SEQ - WINDOW:], rows_p[5][:, :, SEQ - WINDOW:],
            st_p.transpose(0, 2, 1).reshape(1, 1, GLA_HEADS, GLA_DK, GLA_DV),
            memkv[:, :mw].reshape(mem_shape), memkv[:, mw:].reshape(mem_shape),
            rows_s[0], rows_s[1], rows_s[2], rows_s[3],
            kwin_s.reshape(win_shape), vwin_s.reshape(win_shape),
            st_s.reshape(1, nb, GLA_HEADS, GLA_DK, GLA_DV))
```

```python
import functools
import math

import numpy as np
import jax
import jax.numpy as jnp
from jax import lax
from jax.experimental import pallas as pl
from jax.experimental.pallas import tpu as pltpu

F32 = jnp.float32
BF16 = jnp.bfloat16

D_MODEL = 2048
SEQ = 8192
DEC_BATCH = 128
DEC_SEQ = 4
PAST_LEN = 2048
PAGE_SIZE = 128
N_PAGES = PAST_LEN // PAGE_SIZE
HEAD_DIM = 128
NSA_HEADS = 8
NSA_KV_HEADS = 2
NSA_GROUP = 4
CMP_BLOCK = 32
CMP_STRIDE = 16
SEL_BLOCK = 64
SEL_TOPN = 16
WINDOW = 512
FORCED_SCORE = 1.0e4
GLA_HEADS = 4
GLA_DV = 256
GLA_DK = 128
GLA_RANK = 16
GLA_TAU = 16.0
GLA_CHUNK = 32
MEM_TOKENS = 256
MEM_HEADS = 4
MEM_HEAD_DIM = 128
D_FF = 5632
NUM_BUCKETS = 32
MAX_DISTANCE = 128
RMS_EPS = 1e-6

N_TOK = SEQ + DEC_BATCH * DEC_SEQ
Z_QN, Z_VG, Z_RG, Z_KV, Z_QG, Z_KG, Z_MISC = 0, 1024, 2048, 3072, 4608, 5120, 5632
Z_W = 5760
MISC_GATES, MISC_A = 0, 24

NEG = -1e30
M_INIT = -1e29

Q_TILE = 128
N_QT = SEQ // Q_TILE
N_CMP_PAD = SEQ // CMP_STRIDE
SLAB = N_CMP_PAD + 128
SLAB_OFF = SLAB - 16
KC_ROWS = SLAB_OFF + N_CMP_PAD + 16
N_SEL = SEQ // SEL_BLOCK
REL0 = N_SEL - 2
CMP_GROUP = 4
SWEEP_TILES = 8
FLASH_SPLIT = 2

VMEM_LIMIT = 56 * 1024 * 1024


def _cparams(sem):
    return pltpu.CompilerParams(dimension_semantics=sem, vmem_limit_bytes=VMEM_LIMIT)


def _dot(a, b):
    return jnp.dot(a, b, preferred_element_type=F32)


def _dot_nt(a, b):
    return lax.dot_general(a, b, (((1,), (1,)), ((), ())), preferred_element_type=F32)


def _dot_tn(a, b):
    return lax.dot_general(a, b, (((0,), (0,)), ((), ())), preferred_element_type=F32)


def _rms(x, g):
    return x * lax.rsqrt(jnp.mean(x * x, axis=-1, keepdims=True) + RMS_EPS) * g


def _silu(x):
    return x * jax.nn.sigmoid(x)


def _ffn_kernel(*refs, n_ff, n_first, two_in, two_out):
    refs = list(refs)
    x_refs = [refs.pop(0) for _ in range(2 if two_in else 1)]
    g_ref, wg_ref, wu_ref, wd_ref, gf_ref = refs[:5]
    o_refs = refs[5:7] if two_out else refs[5:6]
    xn_ref, acc_ref = refs[-2:]
    i = pl.program_id(0)
    j = pl.program_id(1)

    def x_tile():
        return jnp.where(i < n_first, x_refs[0][...], x_refs[1][...]) if two_in else x_refs[0][...]

    @pl.when(j == 0)
    def _():
        xn_ref[...] = _rms(x_tile(), g_ref[...]).astype(BF16)
        acc_ref[...] = jnp.zeros_like(acc_ref)

    xn = xn_ref[...]
    hid = _silu(_dot(xn, wg_ref[...])) * _dot(xn, wu_ref[...])
    acc_ref[...] += _dot(hid.astype(BF16), wd_ref[...])

    @pl.when(j == n_ff - 1)
    def _():
        h = x_tile() + 0.5 * acc_ref[...]
        if two_out:
            h = _rms(h, gf_ref[...])

            @pl.when(i < n_first)
            def _():
                o_refs[0][...] = h

            @pl.when(i >= n_first)
            def _():
                o_refs[1][...] = h
        else:
            o_refs[0][...] = h


FFN_TM = 512
FFN_TF = 512


def _ffn(xs, g, wg, wu, wd, gf, split_out):
    tm, tf, d = FFN_TM, FFN_TF, D_MODEL
    n_ff = D_FF // tf
    n_first = SEQ // tm
    two_in = len(xs) == 2

    def first(i, j):
        return (jnp.minimum(i, n_first - 1), 0)

    def second(i, j):
        return (jnp.maximum(i - n_first, 0), 0)
    whole = pl.BlockSpec((tm, d), lambda i, j: (i, 0))
    pair = [pl.BlockSpec((tm, d), first), pl.BlockSpec((tm, d), second)]
    n_s = DEC_BATCH * DEC_SEQ
    return pl.pallas_call(
        functools.partial(_ffn_kernel, n_ff=n_ff, n_first=n_first, two_in=two_in, two_out=split_out),
        grid=(N_TOK // tm, n_ff),
        in_specs=(pair if two_in else [whole])
        + [pl.BlockSpec((1, d), lambda i, j: (0, 0)),
           pl.BlockSpec((d, tf), lambda i, j: (0, j)),
           pl.BlockSpec((d, tf), lambda i, j: (0, j)),
           pl.BlockSpec((tf, d), lambda i, j: (j, 0)),
           pl.BlockSpec((1, d), lambda i, j: (0, 0))],
        out_specs=pair if split_out else whole,
        out_shape=([jax.ShapeDtypeStruct((SEQ, d), F32), jax.ShapeDtypeStruct((n_s, d), F32)] if split_out
                   else jax.ShapeDtypeStruct((N_TOK, d), F32)),
        scratch_shapes=[pltpu.VMEM((tm, d), BF16), pltpu.VMEM((tm, d), F32)],
        compiler_params=_cparams(("arbitrary", "arbitrary")),
        name="ffn",
    )(*xs, g, wg, wu, wd, gf)


def _norm_matmul_kernel(x_ref, g_ref, w_ref, o_ref, xn_ref):
    @pl.when(pl.program_id(1) == 0)
    def _():
        xn_ref[...] = _rms(x_ref[...], g_ref[...]).astype(BF16)

    o_ref[...] = _dot(xn_ref[...], w_ref[...])


def _norm_matmul(x, g, w, tm, tn, name):
    n, d = x.shape
    dout = w.shape[1]
    return pl.pallas_call(
        _norm_matmul_kernel,
        grid=(n // tm, dout // tn),
        in_specs=[pl.BlockSpec((tm, d), lambda i, j: (i, 0)),
                  pl.BlockSpec((1, d), lambda i, j: (0, 0)),
                  pl.BlockSpec((d, tn), lambda i, j: (0, j))],
        out_specs=pl.BlockSpec((tm, tn), lambda i, j: (i, j)),
        out_shape=jax.ShapeDtypeStruct((n, dout), F32),
        scratch_shapes=[pltpu.VMEM((tm, d), BF16)],
        compiler_params=_cparams(("parallel", "arbitrary")),
        name=name,
    )(x, g, w)


def _matmul_res_kernel(*refs, n_lhs, n_first):
    res_ref = refs[0]
    o_ref = refs[1 + 3 * n_lhs]
    first = pl.program_id(0) < n_first
    acc = res_ref[...]
    for k in range(n_lhs):
        lhs = jnp.where(first, refs[1 + 2 * k][...], refs[2 + 2 * k][...])
        acc = acc + _dot(lhs.astype(BF16), refs[1 + 2 * n_lhs + k][...])
    o_ref[...] = acc


def _matmul_res(res, lhs, ws, tm, name):
    n, d = res.shape
    n_first = SEQ // tm

    def pair_specs(width):
        return [pl.BlockSpec((tm, width), lambda i: (jnp.minimum(i, n_first - 1), 0)),
                pl.BlockSpec((tm, width), lambda i: (jnp.maximum(i - n_first, 0), 0))]
    return pl.pallas_call(
        functools.partial(_matmul_res_kernel, n_lhs=len(lhs), n_first=n_first),
        grid=(n // tm,),
        in_specs=([pl.BlockSpec((tm, d), lambda i: (i, 0))]
                  + [spec for a, _ in lhs for spec in pair_specs(a.shape[1])]
                  + [pl.BlockSpec(w.shape, lambda i: (0, 0)) for w in ws]),
        out_specs=pl.BlockSpec((tm, d), lambda i: (i, 0)),
        out_shape=jax.ShapeDtypeStruct((n, d), F32),
        compiler_params=_cparams(("arbitrary",)),
        name=name,
    )(res, *[a for pair in lhs for a in pair], *ws)


def _rel_bucket_np(dist):
    n = np.maximum(dist, 0)
    exact = NUM_BUCKETS // 2
    nf = np.maximum(n, 1).astype(np.float32)
    large = exact + (np.log(nf / np.float32(exact)) / np.float32(math.log(MAX_DISTANCE / exact))
                     * np.float32(NUM_BUCKETS - exact)).astype(np.int32)
    return np.where(n < exact, n, np.minimum(large, NUM_BUCKETS - 1)).astype(np.int32)


def _bucket_or_masked(dist, valid):
    return np.where(valid, _rel_bucket_np(dist), -1).astype(np.int32)


def _prompt_bucket_table():
    i = np.arange(Q_TILE)[:, None]
    j = np.arange(Q_TILE)[None, :]
    u = np.arange(SLAB)[None, :]
    dist_c = i - CMP_STRIDE * u + (CMP_STRIDE * SLAB_OFF - (CMP_BLOCK - 1))
    diag = _bucket_or_masked(i - j, i - j >= 0)
    prev = _bucket_or_masked(Q_TILE + i - j, np.ones((Q_TILE, Q_TILE), bool))
    first = _bucket_or_masked(WINDOW + i - j, j > i)
    cmp_ = _bucket_or_masked(dist_c, dist_c >= 0)
    return np.concatenate([diag, prev, first, cmp_], axis=1).T


S_CMP = PAST_LEN // CMP_STRIDE
S_CMP_COLS = NSA_KV_HEADS * S_CMP
S_NEW_COLS = 128
S_SEL_COLS = NSA_KV_HEADS * PAST_LEN + S_NEW_COLS
S_WIN_COLS = NSA_KV_HEADS * WINDOW


def _sample_bucket_table():
    i = np.arange(DEC_SEQ)[:, None]
    pos = PAST_LEN + i
    rows = []
    for h in range(NSA_KV_HEADS):
        col = np.arange(S_CMP_COLS)[None, :]
        c = col % S_CMP
        dist_c = pos - (c * CMP_STRIDE + CMP_BLOCK - 1)
        cmp_ = _bucket_or_masked(dist_c, (col // S_CMP == h) & (c < S_CMP - 1) & (dist_c >= 0))
        col = np.arange(S_SEL_COLS)[None, :]
        key = col // NSA_KV_HEADS
        sel = _bucket_or_masked(pos - key, (col % NSA_KV_HEADS == h) & (key <= pos))
        col = np.arange(S_WIN_COLS)[None, :]
        dist_w = pos - (PAST_LEN - WINDOW + col // NSA_KV_HEADS)
        win = _bucket_or_masked(dist_w, (col % NSA_KV_HEADS == h) & (dist_w < WINDOW))
        rows.append(np.concatenate([cmp_, sel, win], axis=1))
    return np.concatenate(rows, axis=0)


def _bias_table_kernel(tab_ref, idx_ref, o_ref):
    h = pl.program_id(0)
    idx = idx_ref[...]
    out = jnp.full(idx.shape, NEG, F32)
    for b in range(NUM_BUCKETS):
        out = jnp.where(idx == b, tab_ref[b, h], out)
    o_ref[0] = out


def _bias_tables(rel_bias, idx, name):
    r, c = idx.shape
    return pl.pallas_call(
        _bias_table_kernel,
        grid=(NSA_HEADS,),
        in_specs=[pl.BlockSpec(memory_space=pltpu.SMEM),
                  pl.BlockSpec((r, c), lambda h: (0, 0))],
        out_specs=pl.BlockSpec((1, r, c), lambda h: (h, 0, 0)),
        out_shape=jax.ShapeDtypeStruct((NSA_HEADS, r, c), F32),
        compiler_params=_cparams(("arbitrary",)),
        name=name,
    )(rel_bias, jnp.asarray(idx))


def _cover_np(n_cmp_cols, n_blk_cols, delta_of):
    u = np.arange(n_cmp_cols)[:, None]
    j = np.arange(n_blk_cols)[None, :]
    delta = delta_of(u, j)
    shared = np.minimum(CMP_STRIDE * delta + CMP_BLOCK, SEL_BLOCK) - np.maximum(CMP_STRIDE * delta, 0)
    return (np.maximum(shared, 0) / CMP_STRIDE).astype(np.float32)


def _compress(load_rows, n_chunk, pos_term_ref, w1_ref, w2_ref):
    parts = []
    for s0 in range(0, CMP_STRIDE, CMP_GROUP):
        lhs = jnp.concatenate([load_rows(s0 + k).astype(BF16) for k in range(CMP_GROUP)], axis=1)
        w = w1_ref[s0:s0 + CMP_GROUP].reshape(CMP_GROUP * HEAD_DIM, 2 * HEAD_DIM)
        parts.append(_dot(lhs, w))
    while len(parts) > 1:
        parts = [a + b for a, b in zip(parts[0::2], parts[1::2])]
    acc = parts[0] + pos_term_ref[0:1, :]
    nxt = pltpu.roll(acc[:, HEAD_DIM:], n_chunk - 1, axis=0)
    hid = _silu(acc[:, :HEAD_DIM] + nxt)
    return _dot(hid.astype(BF16), w2_ref[...])


def _pos_term_kernel(pos_ref, w1_ref, o_ref):
    halves = []
    for half in range(2):
        acc = jnp.zeros((8, HEAD_DIM), F32)
        for s in range(CMP_STRIDE):
            row = half * CMP_STRIDE + s
            p = jnp.broadcast_to(pos_ref[0, row:row + 1, :], (8, HEAD_DIM)).astype(BF16)
            acc = acc + _dot(p, w1_ref[0, s][:, half * HEAD_DIM:(half + 1) * HEAD_DIM])
        halves.append(acc)
    o_ref[0] = jnp.concatenate(halves, axis=1)


def _pos_term(pos, w1cat):
    return pl.pallas_call(
        _pos_term_kernel,
        grid=(2,),
        in_specs=[pl.BlockSpec((1, CMP_BLOCK, HEAD_DIM), lambda i: (i, 0, 0)),
                  pl.BlockSpec((1, CMP_STRIDE, HEAD_DIM, 2 * HEAD_DIM), lambda i: (i, 0, 0, 0))],
        out_specs=pl.BlockSpec((1, 8, 2 * HEAD_DIM), lambda i: (i, 0, 0)),
        out_shape=jax.ShapeDtypeStruct((2, 8, 2 * HEAD_DIM), F32),
        compiler_params=_cparams(("arbitrary",)),
        name="pos_term",
    )(pos, w1cat)


def _flash_step(carry, s, v_t):
    outs = []
    w = s.shape[1] // FLASH_SPLIT
    for c in range(FLASH_SPLIT):
        m, l, acc = (x[:, c * w:(c + 1) * w] for x in carry)
        sc = s[:, c * w:(c + 1) * w]
        m_new = jnp.maximum(m, jnp.max(sc, axis=0, keepdims=True))
        alpha = jnp.exp(m - m_new)
        e = jnp.exp(sc - m_new)
        l = alpha * l + jnp.sum(e, axis=0, keepdims=True)
        acc = alpha * acc + _dot(v_t, e.astype(BF16))
        outs.append((m_new, l, acc))
    return tuple(jnp.concatenate([o[i] for o in outs], axis=1) for i in range(3))


def _flash_init(cols):
    return (jnp.full((1, cols), M_INIT, F32), jnp.zeros((1, cols), F32), jnp.zeros((HEAD_DIM, cols), F32))


def _masked_softmax(s, valid):
    s = jnp.where(valid, s, NEG)
    m = jnp.max(s, axis=-1, keepdims=True)
    e = jnp.where(valid, jnp.exp(s - m), 0.0)
    return e / jnp.maximum(jnp.sum(e, axis=-1, keepdims=True), 1e-30)


def _split_dot(x, w):
    hi = x.astype(BF16)
    lo = (x - hi.astype(F32)).astype(BF16)
    return _dot(hi, w) + _dot(lo, w)


def _top_n_mask(score, index, axis):
    sel = jnp.zeros(score.shape, F32)
    for _ in range(SEL_TOPN):
        mx = jnp.max(score, axis=axis, keepdims=True)
        first = jnp.min(jnp.where(score == mx, index, 1e9), axis=axis, keepdims=True)
        hit = index == first
        sel = jnp.where(hit, 1.0, sel)
        score = jnp.where(hit, -jnp.inf, score)
    return sel


def _top_n_mask_by_rank(score, n_cand):
    lane = lax.broadcasted_iota(jnp.int32, score.shape, 1)
    rank = jnp.zeros(score.shape, F32)
    for j in range(n_cand):
        col = score[:, j:j + 1]
        ahead = (col > score) | ((col == score) & (lane > j))
        rank = rank + ahead.astype(F32)
    return ((rank < SEL_TOPN) & (lane < n_cand)).astype(F32)


def _compress_prompt_kernel(rows_ref, pos_ref, w1_ref, w2_ref, o_ref):
    out = _compress(lambda s: rows_ref[pl.ds(s, N_CMP_PAD, stride=CMP_STRIDE), :], N_CMP_PAD,
                    pos_ref.at[0], w1_ref.at[0], w2_ref.at[0])
    real = lax.broadcasted_iota(jnp.int32, (N_CMP_PAD, HEAD_DIM), 0) < N_CMP_PAD - 1
    o_ref[0, 0, 0:SLAB_OFF, :] = jnp.zeros((SLAB_OFF, HEAD_DIM), F32)
    o_ref[0, 0, SLAB_OFF:SLAB_OFF + N_CMP_PAD, :] = jnp.where(real, out, 0.0)
    o_ref[0, 0, SLAB_OFF + N_CMP_PAD:KC_ROWS, :] = jnp.zeros((KC_ROWS - SLAB_OFF - N_CMP_PAD, HEAD_DIM), F32)


def _compress_prompt(z, pos, w1, w2):
    kv_blk = Z_KV // HEAD_DIM
    return pl.pallas_call(
        _compress_prompt_kernel,
        grid=(2, NSA_KV_HEADS),
        in_specs=[pl.BlockSpec((SEQ, HEAD_DIM), lambda i, h: (0, kv_blk + NSA_KV_HEADS * i + h)),
                  pl.BlockSpec((1, 8, 2 * HEAD_DIM), lambda i, h: (i, 0, 0)),
                  pl.BlockSpec((1, CMP_STRIDE, HEAD_DIM, 2 * HEAD_DIM), lambda i, h: (i, 0, 0, 0)),
                  pl.BlockSpec((1, HEAD_DIM, HEAD_DIM), lambda i, h: (i, 0, 0))],
        out_specs=pl.BlockSpec((1, 1, KC_ROWS, HEAD_DIM), lambda i, h: (i, h, 0, 0)),
        out_shape=jax.ShapeDtypeStruct((2, NSA_KV_HEADS, KC_ROWS, HEAD_DIM), F32),
        compiler_params=_cparams(("parallel", "parallel")),
        name="compress_prompt",
    )(z, pos, w1, w2)


def _nsa_prompt_kernel(q_ref, misc_ref, ksel_ref, vsel_rows_ref, kwin_ref, vwin_rows_ref, kc_ref, vc_ref,
                       bias_ref, cover_ref, erel_ref, o_ref, vsel_ref, vwin_ref):
    kvh = pl.program_id(0)
    t = pl.program_id(1)
    cols = NSA_GROUP * Q_TILE

    @pl.when(t == 0)
    def _():
        def body(kt, _):
            rows = pl.ds(pl.multiple_of(kt * Q_TILE, Q_TILE), Q_TILE)
            vsel_ref[:, rows] = vsel_rows_ref[rows, :].astype(F32).T.astype(BF16)
            vwin_ref[:, rows] = vwin_rows_ref[rows, :].astype(F32).T.astype(BF16)
            return 0
        lax.fori_loop(0, N_QT, body, 0)

    q = q_ref[...] * (HEAD_DIM ** -0.5)
    qt = jnp.concatenate([q[:, g * HEAD_DIM:(g + 1) * HEAD_DIM].T for g in range(NSA_GROUP)], axis=1).astype(BF16)

    def bias_tile(k):
        return bias_ref[0, k * Q_TILE:(k + 1) * Q_TILE, :]
    b_diag, b_prev, b_first = bias_tile(0), bias_tile(1), bias_tile(2)

    def key_tile(ref, kt):
        return ref[pl.ds(pl.multiple_of(kt * Q_TILE, Q_TILE), Q_TILE), :]

    def value_tile(ref, kt):
        return ref[:, pl.ds(pl.multiple_of(kt * Q_TILE, Q_TILE), Q_TILE)]

    n_wt = WINDOW // Q_TILE + 1
    scores, values = [], []
    for w, b_tile in enumerate((b_first,) + (None,) * (n_wt - 3) + (b_prev, b_diag)):
        kt = t - (n_wt - 1) + w
        kc = jnp.maximum(kt, 0)
        s = _dot(key_tile(kwin_ref, kc), qt) + jnp.where(kt >= 0, 0.0, NEG)
        scores.append(s if b_tile is None else s + b_tile)
        values.append(value_tile(vwin_ref, kc))
    _, l, acc = _flash_step(_flash_init(cols), jnp.concatenate(scores, axis=0), jnp.concatenate(values, axis=1))
    o_win = acc / l

    start = pl.multiple_of(t * (Q_TILE // CMP_STRIDE), 8)
    kslab = kc_ref[0, 0, pl.ds(start, SLAB), :].astype(BF16)
    vslab = vc_ref[0, 0, pl.ds(start, SLAB), :].astype(BF16)
    b_cmp = bias_ref[0, 3 * Q_TILE:3 * Q_TILE + SLAB, :]
    u = lax.broadcasted_iota(jnp.int32, (SLAB, 1), 0)
    valid = (b_cmp > M_INIT) & (u >= SLAB_OFF - (Q_TILE // CMP_STRIDE) * t)
    s = jnp.where(valid, _dot(kslab, qt) + b_cmp, NEG)
    e = jnp.where(valid, jnp.exp(s - jnp.max(s, axis=0, keepdims=True)), 0.0)
    p_cmp = e / jnp.maximum(jnp.sum(e, axis=0, keepdims=True), 1e-30)
    o_cmp = _dot_tn(vslab, p_cmp.astype(BF16))

    p_sum = p_cmp[:, 0:Q_TILE]
    for g in range(1, NSA_GROUP):
        p_sum = p_sum + p_cmp[:, g * Q_TILE:(g + 1) * Q_TILE]
    hi = p_sum.astype(BF16)
    lo = (p_sum - hi.astype(F32)).astype(BF16)
    imp = _dot(cover_ref[...], hi) + _dot(cover_ref[...], lo)
    jr = lax.broadcasted_iota(jnp.int32, (N_SEL, Q_TILE), 0)
    qi = lax.broadcasted_iota(jnp.int32, (N_SEL, Q_TILE), 1)
    cur = REL0 + (qi >= SEL_BLOCK).astype(jnp.int32)
    first_blk = REL0 - 2 * t
    forced = (jr == first_blk) | (jr == cur) | (jr == cur - 1)
    in_range = (jr <= cur) & (jr >= first_blk)
    score = jnp.where(in_range, jnp.where(forced, FORCED_SCORE, imp), NEG)
    sel = _top_n_mask(score, jr.astype(F32), 0)

    unsel = ((sel - 1.0) * (-NEG)).astype(BF16)
    q_aug = jnp.concatenate([qt, jnp.concatenate([unsel] * NSA_GROUP, axis=1)], axis=0)

    def sweep_step(i, carry, masked, biases=()):
        scores, values = [], []
        for j in range(SWEEP_TILES):
            kt = t - SWEEP_TILES * i - (SWEEP_TILES - 1) + j
            kc = jnp.maximum(kt, 0) if masked else kt
            tile = jnp.concatenate([key_tile(ksel_ref, kc), erel_ref[t - kc]], axis=1)
            s = _dot(tile, q_aug)
            if masked:
                s = s + jnp.where(kt >= 0, 0.0, NEG)
            if j >= SWEEP_TILES - len(biases):
                s = s + biases[j - (SWEEP_TILES - len(biases))]
            scores.append(s)
            values.append(value_tile(vsel_ref, kc))
        return _flash_step(carry, jnp.concatenate(scores, axis=0), jnp.concatenate(values, axis=1))

    n_full = jnp.maximum(t - (SWEEP_TILES - 1), 0) // SWEEP_TILES
    carry = lax.fori_loop(1, n_full + 1, lambda i, c: sweep_step(i, c, False), _flash_init(cols))
    some_left = (t - SWEEP_TILES * (n_full + 1) >= 0).astype(jnp.int32)
    carry = lax.fori_loop(0, some_left, lambda _, c: sweep_step(n_full + 1, c, True), carry)
    _, l, acc = sweep_step(0, carry, True, (b_prev, b_diag))
    o_sel = acc / l

    gates = jax.nn.sigmoid(misc_ref[...]).T
    n_g = 3 * NSA_GROUP
    gk = jnp.where(kvh == 0, gates, pltpu.roll(gates, gates.shape[0] - n_g, axis=0))[MISC_GATES:MISC_GATES + n_g]
    for g in range(NSA_GROUP):
        c = slice(g * Q_TILE, (g + 1) * Q_TILE)
        o_g = (gk[3 * g:3 * g + 1] * o_cmp[:, c] + gk[3 * g + 1:3 * g + 2] * o_sel[:, c]
               + gk[3 * g + 2:3 * g + 3] * o_win[:, c])
        o_ref[:, g * HEAD_DIM:(g + 1) * HEAD_DIM] = o_g.T


def _nsa_prompt(z, kvb, kcp, bias, cover, erel):
    gw = NSA_GROUP * HEAD_DIM

    def k_spec(j):
        return pl.BlockSpec((SEQ, HEAD_DIM), lambda h, t, j=j: (0, 2 * j + h))
    return pl.pallas_call(
        _nsa_prompt_kernel,
        grid=(NSA_KV_HEADS, N_QT),
        in_specs=[pl.BlockSpec((Q_TILE, gw), lambda h, t: (t, h)),
                  pl.BlockSpec((Q_TILE, 128), lambda h, t: (t, Z_MISC // 128)),
                  k_spec(2), k_spec(3), k_spec(4), k_spec(5),
                  pl.BlockSpec((1, 1, KC_ROWS, HEAD_DIM), lambda h, t: (0, h, 0, 0)),
                  pl.BlockSpec((1, 1, KC_ROWS, HEAD_DIM), lambda h, t: (1, h, 0, 0)),
                  pl.BlockSpec((1,) + bias.shape[1:], lambda h, t: (h, 0, 0)),
                  pl.BlockSpec(cover.shape, lambda h, t: (0, 0)),
                  pl.BlockSpec(erel.shape, lambda h, t: (0, 0, 0))],
        out_specs=pl.BlockSpec((Q_TILE, gw), lambda h, t: (t, h)),
        out_shape=jax.ShapeDtypeStruct((SEQ, NSA_HEADS * HEAD_DIM), F32),
        scratch_shapes=[pltpu.VMEM((HEAD_DIM, SEQ), BF16), pltpu.VMEM((HEAD_DIM, SEQ), BF16)],
        compiler_params=_cparams(("arbitrary", "arbitrary")),
        name="nsa_prompt",
    )(z, z, kvb, kvb, kvb, kvb, kcp, kcp, bias, cover, erel)


S_ROWS = NSA_HEADS * DEC_SEQ
PAGE_ROWS = NSA_KV_HEADS * PAGE_SIZE
NEW_ROWS = NSA_KV_HEADS * DEC_SEQ
CHUNK_ROWS = NSA_KV_HEADS * CMP_STRIDE
CHUNK_PITCH = CHUNK_ROWS + 4


def _nsa_sample_kernel(pt_ref, *refs):
    del pt_ref
    n_pg = N_PAGES
    q_ref, gate_ref, new_ref, kwin_ref, vwin_ref = refs[0:5]
    pages = [refs[5 + k * n_pg:5 + (k + 1) * n_pg] for k in range(4)]
    (posk_ref, w1k_ref, w2k_ref, posv_ref, w1v_ref, w2v_ref,
     bias_ref, cover_ref, expand_ref) = refs[5 + 4 * n_pg:14 + 4 * n_pg]
    o_ref, kwin_o_ref, vwin_o_ref = refs[14 + 4 * n_pg:17 + 4 * n_pg]
    past_refs = refs[17 + 4 * n_pg:19 + 4 * n_pg]

    comp = []
    for k, (pos_ref, w1_ref, w2_ref) in enumerate(((posk_ref, w1k_ref, w2k_ref), (posv_ref, w1v_ref, w2v_ref))):
        past_ref = past_refs[k]
        for p in range(n_pg):
            for c in range(PAGE_ROWS // CHUNK_ROWS):
                dst = (p * (PAGE_ROWS // CHUNK_ROWS) + c) * CHUNK_PITCH
                past_ref[dst:dst + CHUNK_ROWS, :] = pages[k][p][c * CHUNK_ROWS:(c + 1) * CHUNK_ROWS, :]

        def load_rows(s, past_ref=past_ref):
            return jnp.concatenate(
                [past_ref[pl.ds(NSA_KV_HEADS * s + h, S_CMP, stride=CHUNK_PITCH), :]
                 for h in range(NSA_KV_HEADS)], axis=0)
        comp.append(_compress(load_rows, S_CMP_COLS, pos_ref, w1_ref, w2_ref).astype(BF16))

    keep = NSA_KV_HEADS * WINDOW - NEW_ROWS
    kwin_o_ref[0:keep, :] = kwin_ref[NEW_ROWS:NSA_KV_HEADS * WINDOW, :]
    kwin_o_ref[keep:keep + NEW_ROWS, :] = new_ref[0, 4]
    vwin_o_ref[0:keep, :] = vwin_ref[NEW_ROWS:NSA_KV_HEADS * WINDOW, :]
    vwin_o_ref[keep:keep + NEW_ROWS, :] = new_ref[0, 5]

    qs = (q_ref[0] * (HEAD_DIM ** -0.5)).astype(BF16)
    o0, o1 = S_CMP_COLS, S_CMP_COLS + S_SEL_COLS
    b_cmp = bias_ref[:, 0:o0]
    b_new = bias_ref[:, o1 - S_NEW_COLS:o1]

    def attend(scores, bias, values):
        s = jnp.concatenate(scores, axis=1) + bias
        e = jnp.exp(s - jnp.max(s, axis=-1, keepdims=True))
        acc = jnp.zeros((S_ROWS, HEAD_DIM), F32)
        c0 = 0
        for v in values:
            acc = acc + _dot(e[:, c0:c0 + v.shape[0]].astype(BF16), v)
            c0 += v.shape[0]
        return acc / jnp.sum(e, axis=-1, keepdims=True)

    p_cmp = _masked_softmax(_dot_nt(qs, comp[0]) + b_cmp, b_cmp > M_INIT)
    o_cmp = _dot(p_cmp.astype(BF16), comp[1])

    imp = _split_dot(p_cmp, cover_ref[...])
    imp = imp + pltpu.roll(imp, 8, axis=0) + pltpu.roll(imp, 16, axis=0) + pltpu.roll(imp, 24, axis=0)
    blk = lax.broadcasted_iota(jnp.int32, (S_ROWS, 128), 1)
    cur = PAST_LEN // SEL_BLOCK
    forced = (blk == 0) | (blk == cur) | (blk == cur - 1)
    score = jnp.where(blk <= cur, jnp.where(forced, FORCED_SCORE, imp), NEG)
    sel = _top_n_mask_by_rank(score, cur + 1).astype(BF16)
    mask_add = (_dot(sel, expand_ref[...]) - 1.0) * (-NEG)

    def new_tile(j):
        pad = jnp.zeros((S_NEW_COLS - NEW_ROWS, HEAD_DIM), F32)
        return jnp.concatenate([new_ref[0, j], pad], axis=0).astype(BF16)

    nk, nv = new_tile(2), new_tile(3)
    scores = [_dot_nt(qs, pages[2][p][...].astype(BF16)) for p in range(n_pg)] + [_dot_nt(qs, nk)]
    values = [pages[3][p][...].astype(BF16) for p in range(n_pg)] + [nv]
    o_sel = attend(scores, bias_ref[:, o0:o1] + mask_add, values)

    nk, nv = new_tile(4), new_tile(5)
    scores = [_dot_nt(qs, kwin_ref[...].astype(BF16)), _dot_nt(qs, nk)]
    bias_w = jnp.concatenate([bias_ref[:, o1:o1 + S_WIN_COLS], b_new], axis=1)
    o_win = attend(scores, bias_w, [vwin_ref[...].astype(BF16), nv])

    g = jax.nn.sigmoid(gate_ref[0])
    o_ref[0] = g[:, 0:1] * o_cmp + g[:, 1:2] * o_sel + g[:, 2:3] * o_win


def _nsa_sample(page_table, q_s, gate_s, new_s, kwin, vwin, pools, cmp_w, bias, cover, expand):
    win_rows = NSA_KV_HEADS * WINDOW

    def full(a):
        return pl.BlockSpec(a.shape, lambda b, pt, n=a.ndim: (0,) * n)

    def per_b(a):
        return pl.BlockSpec((1,) + a.shape[1:], lambda b, pt, n=a.ndim: (b,) + (0,) * (n - 1))
    win_spec = pl.BlockSpec((win_rows, HEAD_DIM), lambda b, pt: (b, 0))
    page_specs = [pl.BlockSpec((PAGE_ROWS, HEAD_DIM), lambda b, pt, p=p: (pt[b, p], 0))
                  for _ in range(4) for p in range(N_PAGES)]
    page_args = [pool for pool in pools for _ in range(N_PAGES)]
    consts = list(cmp_w) + [bias, cover, expand]
    grid_spec = pltpu.PrefetchScalarGridSpec(
        num_scalar_prefetch=1,
        grid=(DEC_BATCH,),
        in_specs=[per_b(q_s), per_b(gate_s), per_b(new_s), win_spec, win_spec]
        + page_specs + [full(a) for a in consts],
        out_specs=[per_b(q_s), win_spec, win_spec],
        scratch_shapes=[pltpu.VMEM((S_CMP * CHUNK_PITCH, HEAD_DIM), F32)] * 2,
    )
    return pl.pallas_call(
        _nsa_sample_kernel,
        grid_spec=grid_spec,
        out_shape=[jax.ShapeDtypeStruct(q_s.shape, F32),
                   jax.ShapeDtypeStruct(kwin.shape, F32),
                   jax.ShapeDtypeStruct(vwin.shape, F32)],
        compiler_params=_cparams(("arbitrary",)),
        name="nsa_sample",
    )(page_table, q_s, gate_s, new_s, kwin, vwin, *page_args, *consts)


def _log_decay(a_blk, wa_ref, ba_ref):
    x = _dot(a_blk.astype(BF16), wa_ref[...]) + ba_ref[...]
    return (jnp.minimum(x, 0.0) - jnp.log1p(jnp.exp(-jnp.abs(x)))) * (1.0 / GLA_TAU)


def _segment_cumsum(g, seg):
    pos = lax.broadcasted_iota(jnp.int32, g.shape, 0) % seg
    cum = g
    sh = 1
    while sh < seg:
        cum = cum + jnp.where(pos >= sh, pltpu.roll(cum, sh, axis=0), 0.0)
        sh *= 2
    return cum


def _gla_prompt_kernel(q_ref, k_ref, v_ref, r_ref, a_ref, wa_ref, ba_ref, gn_ref, o_ref, st_o_ref, st_ref,
                       *, n_blk, tb):
    tbi = pl.program_id(0)

    @pl.when(tbi == 0)
    def _():
        st_ref[...] = jnp.zeros_like(st_ref)

    c = GLA_CHUNK
    tril = lax.broadcasted_iota(jnp.int32, (c, c), 0) >= lax.broadcasted_iota(jnp.int32, (c, c), 1)
    cum = _segment_cumsum(_log_decay(a_ref[...], wa_ref, ba_ref), c)
    q = q_ref[...] * (GLA_DK ** -0.5)
    k = k_ref[...]
    v = v_ref[...].astype(BF16)
    qe = (q * jnp.exp(cum)).astype(BF16)
    kd = (k * jnp.exp(-cum)).astype(BF16)
    heads = [(slice(h * GLA_DK, (h + 1) * GLA_DK), slice(h * GLA_DV, (h + 1) * GLA_DV)) for h in range(GLA_HEADS)]
    sts = [st_ref[h] for h in range(GLA_HEADS)]
    outs = [[] for _ in range(GLA_HEADS)]
    for ci in range(tb // c):
        r = slice(ci * c, (ci + 1) * c)
        last = cum[ci * c + c - 1:ci * c + c, :]
        kl = (k[r] * jnp.exp(last - cum[r])).astype(BF16)
        decay = jnp.exp(last)
        for h, (dk, dv) in enumerate(heads):
            att = jnp.where(tril, _dot_nt(qe[r, dk], kd[r, dk]), 0.0)
            outs[h].append(_dot_nt(qe[r, dk], sts[h].astype(BF16)) + _dot(att.astype(BF16), v[r, dv]))
            sts[h] = decay[:, dk] * sts[h] + _dot_tn(v[r, dv], kl[:, dk])
    for h, (dk, dv) in enumerate(heads):
        st_ref[h] = sts[h]
        o = jnp.concatenate(outs[h], axis=0)
        o_ref[:, dv] = _rms(o, gn_ref[...]) * _silu(r_ref[:, dv])

    @pl.when(tbi == n_blk - 1)
    def _():
        st_o_ref[...] = st_ref[...]


def _gla_prompt(z, wa, ba, gn, tb=256):
    n_blk = SEQ // tb
    hk, hv = GLA_HEADS * GLA_DK, GLA_HEADS * GLA_DV
    st_shape = (GLA_HEADS, GLA_DV, GLA_DK)
    return pl.pallas_call(
        functools.partial(_gla_prompt_kernel, n_blk=n_blk, tb=tb),
        grid=(n_blk,),
        in_specs=[pl.BlockSpec((tb, hk), lambda i: (i, Z_QG // hk)),
                  pl.BlockSpec((tb, hk), lambda i: (i, Z_KG // hk)),
                  pl.BlockSpec((tb, hv), lambda i: (i, Z_VG // hv)),
                  pl.BlockSpec((tb, hv), lambda i: (i, Z_RG // hv)),
                  pl.BlockSpec((tb, 128), lambda i: (i, Z_MISC // 128)),
                  pl.BlockSpec((128, hk), lambda i: (0, 0)),
                  pl.BlockSpec((1, hk), lambda i: (0, 0)),
                  pl.BlockSpec((1, GLA_DV), lambda i: (0, 0))],
        out_specs=[pl.BlockSpec((tb, hv), lambda i: (i, 0)),
                   pl.BlockSpec(st_shape, lambda i: (0, 0, 0))],
        out_shape=[jax.ShapeDtypeStruct((SEQ, hv), F32), jax.ShapeDtypeStruct(st_shape, F32)],
        scratch_shapes=[pltpu.VMEM(st_shape, F32)],
        compiler_params=_cparams(("arbitrary",)),
        name="gla_prompt",
    )(z, z, z, z, z, wa, ba, gn)


GS_B = 4


def _gla_sample_kernel(q_ref, k_ref, v_ref, r_ref, a_ref, wa_ref, ba_ref, gn_ref, s_ref, o_ref, s_o_ref):
    rows = GS_B * DEC_SEQ
    a = a_ref[...]
    ri = lax.broadcasted_iota(jnp.int32, (rows, rows), 0)
    ci = lax.broadcasted_iota(jnp.int32, (rows, rows), 1)
    same_causal = (ri // DEC_SEQ == ci // DEC_SEQ) & (ri >= ci)
    row_b = lax.broadcasted_iota(jnp.int32, (rows, 1), 0) // DEC_SEQ
    ones = jnp.ones((rows, 128), BF16)
    for h in range(GLA_HEADS):
        dk = slice(h * GLA_DK, (h + 1) * GLA_DK)
        dv = slice(h * GLA_DV, (h + 1) * GLA_DV)
        g = _log_decay(a, wa_ref.at[:, dk], ba_ref.at[:, dk])
        cum = _segment_cumsum(g, DEC_SEQ)
        q = q_ref[:, dk] * (GLA_DK ** -0.5)
        k = k_ref[:, dk]
        v = v_ref[:, dv].astype(BF16)
        qe = (q * jnp.exp(cum)).astype(BF16)
        kd = (k * jnp.exp(-cum)).astype(BF16)
        att = jnp.where(same_causal, _dot_nt(qe, kd), 0.0)
        o = _dot(att.astype(BF16), v)
        for b in range(GS_B):
            mine = row_b == b
            last = cum[b * DEC_SEQ + DEC_SEQ - 1:(b + 1) * DEC_SEQ, :]
            s = s_ref[b, h]
            o = o + jnp.where(mine, _dot(qe, s.astype(BF16)), 0.0)
            kl = jnp.where(mine, k * jnp.exp(last - cum), 0.0)
            hi = jnp.where(mine, g, 0.0).astype(BF16)
            lo = (jnp.where(mine, g, 0.0) - hi.astype(F32)).astype(BF16)
            last_col = (_dot_tn(hi, ones) + _dot_tn(lo, ones))[:, 0:1]
            s_o_ref[b, h] = jnp.exp(last_col) * s + _dot_tn(kl.astype(BF16), v)
        o_ref[:, dv] = _rms(o, gn_ref[...]) * _silu(r_ref[:, dv])


def _gla_sample(zs, state, wa, ba, gn):
    rows = GS_B * DEC_SEQ
    n = DEC_BATCH * DEC_SEQ
    hk, hv = GLA_HEADS * GLA_DK, GLA_HEADS * GLA_DV
    st_spec = pl.BlockSpec((GS_B, GLA_HEADS, GLA_DK, GLA_DV), lambda i: (i, 0, 0, 0))
    return pl.pallas_call(
        _gla_sample_kernel,
        grid=(DEC_BATCH // GS_B,),
        in_specs=[pl.BlockSpec((rows, hk), lambda i: (i, Z_QG // hk)),
                  pl.BlockSpec((rows, hk), lambda i: (i, Z_KG // hk)),
                  pl.BlockSpec((rows, hv), lambda i: (i, Z_VG // hv)),
                  pl.BlockSpec((rows, hv), lambda i: (i, Z_RG // hv)),
                  pl.BlockSpec((rows, 128), lambda i: (i, Z_MISC // 128)),
                  pl.BlockSpec((128, hk), lambda i: (0, 0)),
                  pl.BlockSpec((1, hk), lambda i: (0, 0)),
                  pl.BlockSpec((1, GLA_DV), lambda i: (0, 0)),
                  st_spec],
        out_specs=[pl.BlockSpec((rows, hv), lambda i: (i, 0)), st_spec],
        out_shape=[jax.ShapeDtypeStruct((n, hv), F32), jax.ShapeDtypeStruct(state.shape, F32)],
        compiler_params=_cparams(("parallel",)),
        name="gla_sample",
    )(zs, zs, zs, zs, zs, wa, ba, gn, state)


def _softmax_rows(s):
    m = jnp.max(s, axis=-1, keepdims=True)
    e = jnp.exp(s - m)
    return e / jnp.sum(e, axis=-1, keepdims=True)


def _mem_prompt_kernel(q_ref, k_ref, v_ref, o_ref):
    for h in range(MEM_HEADS):
        d = slice(h * MEM_HEAD_DIM, (h + 1) * MEM_HEAD_DIM)
        q = (q_ref[:, d] * (MEM_HEAD_DIM ** -0.5)).astype(BF16)
        p = _softmax_rows(_dot_nt(q, k_ref[:, d].astype(BF16)))
        o_ref[:, d] = _dot(p.astype(BF16), v_ref[:, d].astype(BF16))


def _mem_prompt(qm, memkv, tq=256):
    w = MEM_HEADS * MEM_HEAD_DIM
    return pl.pallas_call(
        _mem_prompt_kernel,
        grid=(SEQ // tq,),
        in_specs=[pl.BlockSpec((tq, w), lambda i: (i, 0)),
                  pl.BlockSpec((MEM_TOKENS, w), lambda i: (0, 0)),
                  pl.BlockSpec((MEM_TOKENS, w), lambda i: (0, 1))],
        out_specs=pl.BlockSpec((tq, w), lambda i: (i, 0)),
        out_shape=jax.ShapeDtypeStruct((SEQ, w), F32),
        compiler_params=_cparams(("parallel",)),
        name="mem_prompt",
    )(qm, memkv, memkv)


def _mem_sample_kernel(q_ref, k_ref, v_ref, o_ref):
    rows = MEM_HEADS * DEC_SEQ
    cols = MEM_HEADS * MEM_TOKENS
    q = (q_ref[0] * (MEM_HEAD_DIM ** -0.5)).astype(BF16)
    row_h = lax.broadcasted_iota(jnp.int32, (rows, cols), 0) // DEC_SEQ
    col_h = lax.broadcasted_iota(jnp.int32, (rows, cols), 1) % MEM_HEADS
    s = jnp.where(row_h == col_h, _dot_nt(q, k_ref[...].astype(BF16)), NEG)
    o_ref[0] = _dot(_softmax_rows(s).astype(BF16), v_ref[...].astype(BF16))


def _mem_sample(q_s, k_mem, v_mem):
    rows = MEM_HEADS * DEC_SEQ
    kv_spec = pl.BlockSpec((MEM_HEADS * MEM_TOKENS, MEM_HEAD_DIM), lambda b: (b, 0))
    return pl.pallas_call(
        _mem_sample_kernel,
        grid=(DEC_BATCH,),
        in_specs=[pl.BlockSpec((1, rows, MEM_HEAD_DIM), lambda b: (b, 0, 0)), kv_spec, kv_spec],
        out_specs=pl.BlockSpec((1, rows, MEM_HEAD_DIM), lambda b: (b, 0, 0)),
        out_shape=jax.ShapeDtypeStruct((DEC_BATCH, rows, MEM_HEAD_DIM), F32),
        compiler_params=_cparams(("parallel",)),
        name="mem_sample",
    )(q_s, k_mem, v_mem)


def _permute_w_in(w_in):
    qn, kv, gt, qg, kg, vg, rg, ag = jnp.split(
        w_in, np.cumsum([1024, 1536, 24, 512, 512, 1024, 1024, 16])[:-1].tolist(), axis=1)
    pad = jnp.zeros((w_in.shape[0], Z_W - Z_MISC - 40), w_in.dtype)
    return jnp.concatenate([qn, vg, rg, kv, qg, kg, gt, ag, pad], axis=1)


def kernel(x_prompt, x_sample, mem_prompt, cache_k_cmp, cache_v_cmp, cache_k_sel, cache_v_sel, cache_k_win,
           cache_v_win, state_gla, cache_k_mem, cache_v_mem, page_table, norm_ffn1, ffn1_w_gate, ffn1_w_up,
           ffn1_w_down, norm_mix, w_in, w_out, cmp_pos_k, cmp_w1_k, cmp_w2_k, cmp_pos_v, cmp_w1_v, cmp_w2_v,
           rel_bias, gla_w_a2, gla_b_a, gla_norm, norm_mem, norm_mem_src, w_mem_q, w_mem_k, w_mem_v, w_mem_o,
           norm_ffn2, ffn2_w_gate, ffn2_w_up, ffn2_w_down, norm_final):
    bf = lambda a: a.astype(BF16)
    row = lambda a: a.reshape(1, -1)
    nb, ns = DEC_BATCH, DEC_SEQ
    kvw = NSA_KV_HEADS * HEAD_DIM

    h1 = _ffn([x_prompt[0], x_sample.reshape(nb * ns, D_MODEL)], row(norm_ffn1[0]), bf(ffn1_w_gate[0]),
              bf(ffn1_w_up[0]), bf(ffn1_w_down[0]), row(norm_final), False)
    z = _norm_matmul(h1, row(norm_mix[0]), bf(_permute_w_in(w_in[0])), 512, Z_W // 5, "proj_in")

    kv_p = z[:SEQ, Z_KV:Z_QG]
    kv_s = z[SEQ:, Z_KV:Z_QG].reshape(nb, ns, 6 * kvw)
    rows_p = [kv_p[:, j * kvw:(j + 1) * kvw].reshape(1, 1, SEQ, NSA_KV_HEADS, HEAD_DIM) for j in range(6)]
    rows_s = [kv_s[:, :, j * kvw:(j + 1) * kvw].reshape(1, nb, ns, NSA_KV_HEADS, HEAD_DIM) for j in range(4)]

    tab_p = _bias_tables(rel_bias, _prompt_bucket_table(), "bias_prompt")
    far = rel_bias[NUM_BUCKETS - 1][:, None, None]
    near = tab_p[:, :3 * Q_TILE]
    tab_p = jnp.concatenate([jnp.where(near > M_INIT, near - far, NEG), tab_p[:, 3 * Q_TILE:]], axis=1)
    tab_p = tab_p.reshape(NSA_KV_HEADS, NSA_GROUP, -1, Q_TILE).transpose(0, 2, 1, 3)
    tab_p = tab_p.reshape(NSA_KV_HEADS, -1, NSA_GROUP * Q_TILE)
    tab_s = _bias_tables(rel_bias, _sample_bucket_table(), "bias_sample")
    tab_s = tab_s.reshape(NSA_KV_HEADS, NSA_GROUP, NSA_KV_HEADS, ns, -1)
    tab_s = jnp.stack([tab_s[h, :, h] for h in range(NSA_KV_HEADS)], axis=1).reshape(S_ROWS, -1)

    cmp_w1 = bf(jnp.stack([cmp_w1_k[0], cmp_w1_v[0]]))
    cmp_w1 = jnp.concatenate([cmp_w1[:, :CMP_STRIDE], cmp_w1[:, CMP_STRIDE:]], axis=-1)
    cmp_w2 = bf(jnp.stack([cmp_w2_k[0], cmp_w2_v[0]]))
    cmp_pos = _pos_term(jnp.stack([cmp_pos_k[0], cmp_pos_v[0]]), cmp_w1)
    kcp = _compress_prompt(z, cmp_pos, cmp_w1, cmp_w2)
    cover_p = jnp.asarray(_cover_np(SLAB, N_SEL, lambda u, j: u - 4 * j - SLAB_OFF + 4 * REL0).T, BF16)
    erel = (np.arange(N_SEL)[None, None, :] == REL0 - 2 * np.arange(N_QT)[:, None, None]
            + (np.arange(Q_TILE)[None, :, None] >= SEL_BLOCK))
    o_nsa_p = _nsa_prompt(z, bf(kv_p), kcp, tab_p, cover_p, jnp.asarray(erel, BF16))

    zs = z[SEQ:]
    q_s = zs[:, Z_QN:Z_QN + NSA_HEADS * HEAD_DIM].reshape(nb, ns, NSA_KV_HEADS, NSA_GROUP, HEAD_DIM)
    q_s = q_s.transpose(0, 3, 2, 1, 4).reshape(nb, S_ROWS, HEAD_DIM)
    gate_s = zs[:, Z_MISC + MISC_GATES:Z_MISC + MISC_GATES + 3 * NSA_HEADS]
    gate_s = gate_s.reshape(nb, ns, NSA_KV_HEADS, NSA_GROUP, 3).transpose(0, 3, 2, 1, 4).reshape(nb, S_ROWS, 3)
    new_s = kv_s.reshape(nb, ns, 6, NSA_KV_HEADS, HEAD_DIM).transpose(0, 2, 1, 3, 4).reshape(nb, 6, NEW_ROWS, HEAD_DIM)
    as_rows = lambda c: c.reshape(-1, HEAD_DIM)
    pools = [as_rows(c) for c in (cache_k_cmp, cache_v_cmp, cache_k_sel, cache_v_sel)]
    cover_s = _cover_np(S_CMP, 128, lambda c, j: c - 4 * j)
    cover_s = jnp.asarray(np.concatenate([cover_s] * NSA_KV_HEADS, axis=0), BF16)
    expand_s = jnp.asarray(np.arange(128)[:, None]
                           == (np.arange(S_SEL_COLS)[None, :] // (NSA_KV_HEADS * SEL_BLOCK)), BF16)
    o_nsa_s, kwin_s, vwin_s = _nsa_sample(
        page_table, q_s, gate_s, new_s, as_rows(cache_k_win), as_rows(cache_v_win), pools,
        (cmp_pos[0], cmp_w1[0], cmp_w2[0], cmp_pos[1], cmp_w1[1], cmp_w2[1]),
        tab_s, cover_s, expand_s)
    o_nsa_s = o_nsa_s.reshape(nb, NSA_GROUP, NSA_KV_HEADS, ns, HEAD_DIM).transpose(0, 3, 2, 1, 4)
    o_nsa_s = o_nsa_s.reshape(nb * ns, NSA_HEADS * HEAD_DIM)

    wa = bf(jnp.zeros((128, GLA_HEADS * GLA_DK), F32).at[MISC_A:MISC_A + GLA_RANK].set(gla_w_a2[0]))
    ba, gn = row(gla_b_a[0]), row(gla_norm[0])
    o_gla_p, st_p = _gla_prompt(z, wa, ba, gn)
    o_gla_s, st_s = _gla_sample(zs, state_gla[0], wa, ba, gn)

    half = NSA_HEADS * HEAD_DIM
    h2 = _matmul_res(h1, [(o_nsa_p, o_nsa_s), (o_gla_p, o_gla_s)], [bf(w_out[0][:half]), bf(w_out[0][half:])],
                     512, "proj_out")

    memkv = _norm_matmul(mem_prompt[0], row(norm_mem_src[0]),
                         bf(jnp.concatenate([w_mem_k[0], w_mem_v[0]], axis=1)), MEM_TOKENS, 512, "mem_kv")
    mw = MEM_HEADS * MEM_HEAD_DIM
    qm = _norm_matmul(h2, row(norm_mem[0]), bf(w_mem_q[0]), 512, mw, "mem_q")
    om_p = _mem_prompt(qm, memkv)
    qm_s = qm[SEQ:].reshape(nb, ns, MEM_HEADS, MEM_HEAD_DIM).transpose(0, 2, 1, 3)
    om_s = _mem_sample(qm_s.reshape(nb, MEM_HEADS * ns, MEM_HEAD_DIM),
                       as_rows(cache_k_mem), as_rows(cache_v_mem))
    om_s = om_s.reshape(nb, MEM_HEADS, ns, MEM_HEAD_DIM).transpose(0, 2, 1, 3).reshape(nb * ns, mw)
    h3 = _matmul_res(h2, [(om_p, om_s)], [bf(w_mem_o[0])], 512, "mem_out")

    y_p, y_s = _ffn([h3], row(norm_ffn2[0]), bf(ffn2_w_gate[0]), bf(ffn2_w_up[0]), bf(ffn2_w_down[0]),
                    row(norm_final), True)

    mem_shape = (1, 1, MEM_TOKENS, MEM_HEADS, MEM_HEAD_DIM)
    win_shape = (1, nb, WINDOW, NSA_KV_HEADS, HEAD_DIM)
    return (y_p.reshape(1, SEQ, D_MODEL), y_s.reshape(nb, ns, D_MODEL),
            rows_p[0], rows_p[1], rows_p[2], rows_p[3],
            rows_p[4][:, :, SEQ - WINDOW:], rows_p[5][:, :, SEQ - WINDOW:],
            st_p.transpose(0, 2, 1).reshape(1, 1, GLA_HEADS, GLA_DK, GLA_DV),
            memkv[:, :mw].reshape(mem_shape), memkv[:, mw:].reshape(mem_shape),
            rows_s[0], rows_s[1], rows_s[2], rows_s[3],
            kwin_s.reshape(win_shape), vwin_s.reshape(win_shape),
            st_s.reshape(1, nb, GLA_HEADS, GLA_DK, GLA_DV))
```

```python
import functools
import math

import numpy as np
import jax
import jax.numpy as jnp
from jax import lax
from jax.experimental import pallas as pl
from jax.experimental.pallas import tpu as pltpu

F32 = jnp.float32
BF16 = jnp.bfloat16

D_MODEL = 2048
SEQ = 8192
DEC_BATCH = 128
DEC_SEQ = 4
PAST_LEN = 2048
PAGE_SIZE = 128
N_PAGES = PAST_LEN // PAGE_SIZE
HEAD_DIM = 128
NSA_HEADS = 8
NSA_KV_HEADS = 2
NSA_GROUP = 4
CMP_BLOCK = 32
CMP_STRIDE = 16
SEL_BLOCK = 64
SEL_TOPN = 16
WINDOW = 512
FORCED_SCORE = 1.0e4
GLA_HEADS = 4
GLA_DV = 256
GLA_DK = 128
GLA_RANK = 16
GLA_TAU = 16.0
GLA_CHUNK = 32
MEM_TOKENS = 256
MEM_HEADS = 4
MEM_HEAD_DIM = 128
D_FF = 5632
NUM_BUCKETS = 32
MAX_DISTANCE = 128
RMS_EPS = 1e-6

N_TOK = SEQ + DEC_BATCH * DEC_SEQ
Z_QN, Z_VG, Z_RG, Z_KV, Z_QG, Z_KG, Z_MISC = 0, 1024, 2048, 3072, 4608, 5120, 5632
Z_W = 5760
MISC_GATES, MISC_A = 0, 24

NEG = -1e30
M_INIT = -1e29

Q_TILE = 128
N_QT = SEQ // Q_TILE
N_CMP_PAD = SEQ // CMP_STRIDE
SLAB = N_CMP_PAD + 128
SLAB_OFF = SLAB - 16
KC_ROWS = SLAB_OFF + N_CMP_PAD + 16
N_SEL = SEQ // SEL_BLOCK
REL0 = N_SEL - 2
CMP_GROUP = 4
SWEEP_TILES = 8
FLASH_SPLIT = 2

VMEM_LIMIT = 56 * 1024 * 1024


def _cparams(sem):
    return pltpu.CompilerParams(dimension_semantics=sem, vmem_limit_bytes=VMEM_LIMIT)


def _dot(a, b):
    return jnp.dot(a, b, preferred_element_type=F32)


def _dot_nt(a, b):
    return lax.dot_general(a, b, (((1,), (1,)), ((), ())), preferred_element_type=F32)


def _dot_tn(a, b):
    return lax.dot_general(a, b, (((0,), (0,)), ((), ())), preferred_element_type=F32)


def _rms(x, g):
    return x * lax.rsqrt(jnp.mean(x * x, axis=-1, keepdims=True) + RMS_EPS) * g


def _silu(x):
    return x * jax.nn.sigmoid(x)


def _ffn_kernel(*refs, n_ff, n_first, two_in, two_out):
    refs = list(refs)
    x_refs = [refs.pop(0) for _ in range(2 if two_in else 1)]
    g_ref, wg_ref, wu_ref, wd_ref, gf_ref = refs[:5]
    o_refs = refs[5:7]
    xn_ref, acc_ref = refs[-2:]
    i = pl.program_id(0)
    j = pl.program_id(1)

    def x_tile():
        return jnp.where(i < n_first, x_refs[0][...], x_refs[1][...]) if two_in else x_refs[0][...]

    @pl.when(j == 0)
    def _():
        xn_ref[...] = _rms(x_tile(), g_ref[...]).astype(BF16)
        acc_ref[...] = jnp.zeros_like(acc_ref)

    xn = xn_ref[...]
    hid = _silu(_dot(xn, wg_ref[...])) * _dot(xn, wu_ref[...])
    acc_ref[...] += _dot(hid.astype(BF16), wd_ref[...])

    @pl.when(j == n_ff - 1)
    def _():
        h = x_tile() + 0.5 * acc_ref[...]
        if two_out:
            h = _rms(h, gf_ref[...])

            @pl.when(i < n_first)
            def _():
                o_refs[0][...] = h

            @pl.when(i >= n_first)
            def _():
                o_refs[1][...] = h
        else:
            o_refs[0][...] = h
            o_refs[1][...] = _rms(h, gf_ref[...]).astype(BF16)


FFN_TM = 512
FFN_TF = 512


def _ffn(xs, g, wg, wu, wd, gf, split_out):
    tm, tf, d = FFN_TM, FFN_TF, D_MODEL
    n_ff = D_FF // tf
    n_first = SEQ // tm
    two_in = len(xs) == 2

    def first(i, j):
        return (jnp.minimum(i, n_first - 1), 0)

    def second(i, j):
        return (jnp.maximum(i - n_first, 0), 0)
    whole = pl.BlockSpec((tm, d), lambda i, j: (i, 0))
    pair = [pl.BlockSpec((tm, d), first), pl.BlockSpec((tm, d), second)]
    n_s = DEC_BATCH * DEC_SEQ
    return pl.pallas_call(
        functools.partial(_ffn_kernel, n_ff=n_ff, n_first=n_first, two_in=two_in, two_out=split_out),
        grid=(N_TOK // tm, n_ff),
        in_specs=(pair if two_in else [whole])
        + [pl.BlockSpec((1, d), lambda i, j: (0, 0)),
           pl.BlockSpec((d, tf), lambda i, j: (0, j)),
           pl.BlockSpec((d, tf), lambda i, j: (0, j)),
           pl.BlockSpec((tf, d), lambda i, j: (j, 0)),
           pl.BlockSpec((1, d), lambda i, j: (0, 0))],
        out_specs=pair if split_out else [whole, whole],
        out_shape=([jax.ShapeDtypeStruct((SEQ, d), F32), jax.ShapeDtypeStruct((n_s, d), F32)] if split_out
                   else [jax.ShapeDtypeStruct((N_TOK, d), F32), jax.ShapeDtypeStruct((N_TOK, d), BF16)]),
        scratch_shapes=[pltpu.VMEM((tm, d), BF16), pltpu.VMEM((tm, d), F32)],
        compiler_params=_cparams(("arbitrary", "arbitrary")),
        name="ffn",
    )(*xs, g, wg, wu, wd, gf)


def _norm_matmul_kernel(x_ref, g_ref, w_ref, o_ref, xn_ref):
    @pl.when(pl.program_id(1) == 0)
    def _():
        xn_ref[...] = _rms(x_ref[...], g_ref[...]).astype(BF16)

    o_ref[...] = _dot(xn_ref[...], w_ref[...])


def _norm_matmul(x, g, w, tm, tn, name):
    n, d = x.shape
    dout = w.shape[1]
    return pl.pallas_call(
        _norm_matmul_kernel,
        grid=(n // tm, dout // tn),
        in_specs=[pl.BlockSpec((tm, d), lambda i, j: (i, 0)),
                  pl.BlockSpec((1, d), lambda i, j: (0, 0)),
                  pl.BlockSpec((d, tn), lambda i, j: (0, j))],
        out_specs=pl.BlockSpec((tm, tn), lambda i, j: (i, j)),
        out_shape=jax.ShapeDtypeStruct((n, dout), F32),
        scratch_shapes=[pltpu.VMEM((tm, d), BF16)],
        compiler_params=_cparams(("parallel", "arbitrary")),
        name=name,
    )(x, g, w)


def _matmul_kernel(x_ref, w_ref, o_ref):
    o_ref[...] = _dot(x_ref[...], w_ref[...])


def _matmul(x, w, tm, tn, name):
    n, d = x.shape
    dout = w.shape[1]
    return pl.pallas_call(
        _matmul_kernel,
        grid=(dout // tn, n // tm),
        in_specs=[pl.BlockSpec((tm, d), lambda j, i: (i, 0)),
                  pl.BlockSpec((d, tn), lambda j, i: (0, j))],
        out_specs=pl.BlockSpec((tm, tn), lambda j, i: (i, j)),
        out_shape=jax.ShapeDtypeStruct((n, dout), F32),
        compiler_params=_cparams(("arbitrary", "arbitrary")),
        name=name,
    )(x, w)


def _matmul_res_kernel(*refs, n_lhs, n_first):
    res_ref = refs[0]
    o_ref = refs[1 + 3 * n_lhs]
    first = pl.program_id(0) < n_first
    acc = res_ref[...]
    for k in range(n_lhs):
        lhs = jnp.where(first, refs[1 + 2 * k][...], refs[2 + 2 * k][...])
        acc = acc + _dot(lhs.astype(BF16), refs[1 + 2 * n_lhs + k][...])
    o_ref[...] = acc


def _matmul_res(res, lhs, ws, tm, name):
    n, d = res.shape
    n_first = SEQ // tm

    def pair_specs(width):
        return [pl.BlockSpec((tm, width), lambda i: (jnp.minimum(i, n_first - 1), 0)),
                pl.BlockSpec((tm, width), lambda i: (jnp.maximum(i - n_first, 0), 0))]
    return pl.pallas_call(
        functools.partial(_matmul_res_kernel, n_lhs=len(lhs), n_first=n_first),
        grid=(n // tm,),
        in_specs=([pl.BlockSpec((tm, d), lambda i: (i, 0))]
                  + [spec for a, _ in lhs for spec in pair_specs(a.shape[1])]
                  + [pl.BlockSpec(w.shape, lambda i: (0, 0)) for w in ws]),
        out_specs=pl.BlockSpec((tm, d), lambda i: (i, 0)),
        out_shape=jax.ShapeDtypeStruct((n, d), F32),
        compiler_params=_cparams(("arbitrary",)),
        name=name,
    )(res, *[a for pair in lhs for a in pair], *ws)


def _rel_bucket_np(dist):
    n = np.maximum(dist, 0)
    exact = NUM_BUCKETS // 2
    nf = np.maximum(n, 1).astype(np.float32)
    large = exact + (np.log(nf / np.float32(exact)) / np.float32(math.log(MAX_DISTANCE / exact))
                     * np.float32(NUM_BUCKETS - exact)).astype(np.int32)
    return np.where(n < exact, n, np.minimum(large, NUM_BUCKETS - 1)).astype(np.int32)


def _bucket_or_masked(dist, valid):
    return np.where(valid, _rel_bucket_np(dist), -1).astype(np.int32)


def _prompt_bucket_table():
    i = np.arange(Q_TILE)[:, None]
    j = np.arange(Q_TILE)[None, :]
    u = np.arange(SLAB)[None, :]
    dist_c = i - CMP_STRIDE * u + (CMP_STRIDE * SLAB_OFF - (CMP_BLOCK - 1))
    diag = _bucket_or_masked(i - j, i - j >= 0)
    prev = _bucket_or_masked(Q_TILE + i - j, np.ones((Q_TILE, Q_TILE), bool))
    first = _bucket_or_masked(WINDOW + i - j, j > i)
    cmp_ = _bucket_or_masked(dist_c, dist_c >= 0)
    return np.concatenate([diag, prev, first, cmp_], axis=1).T


S_CMP = PAST_LEN // CMP_STRIDE
S_CMP_COLS = NSA_KV_HEADS * S_CMP
S_NEW_COLS = 128
S_SEL_COLS = NSA_KV_HEADS * PAST_LEN + S_NEW_COLS
S_WIN_COLS = NSA_KV_HEADS * WINDOW


def _sample_bucket_table():
    i = np.arange(DEC_SEQ)[:, None]
    pos = PAST_LEN + i
    rows = []
    for h in range(NSA_KV_HEADS):
        col = np.arange(S_CMP_COLS)[None, :]
        c = col % S_CMP
        dist_c = pos - (c * CMP_STRIDE + CMP_BLOCK - 1)
        cmp_ = _bucket_or_masked(dist_c, (col // S_CMP == h) & (c < S_CMP - 1) & (dist_c >= 0))
        col = np.arange(S_SEL_COLS)[None, :]
        key = col // NSA_KV_HEADS
        sel = _bucket_or_masked(pos - key, (col % NSA_KV_HEADS == h) & (key <= pos))
        col = np.arange(S_WIN_COLS)[None, :]
        dist_w = pos - (PAST_LEN - WINDOW + col // NSA_KV_HEADS)
        win = _bucket_or_masked(dist_w, (col % NSA_KV_HEADS == h) & (dist_w < WINDOW))
        rows.append(np.concatenate([cmp_, sel, win], axis=1))
    return np.concatenate(rows, axis=0)


def _bias_table_kernel(tab_ref, idx_ref, o_ref):
    h = pl.program_id(0)
    idx = idx_ref[...]
    out = jnp.full(idx.shape, NEG, F32)
    for b in range(NUM_BUCKETS):
        out = jnp.where(idx == b, tab_ref[b, h], out)
    o_ref[0] = out


def _bias_tables(rel_bias, idx, name):
    r, c = idx.shape
    return pl.pallas_call(
        _bias_table_kernel,
        grid=(NSA_HEADS,),
        in_specs=[pl.BlockSpec(memory_space=pltpu.SMEM),
                  pl.BlockSpec((r, c), lambda h: (0, 0))],
        out_specs=pl.BlockSpec((1, r, c), lambda h: (h, 0, 0)),
        out_shape=jax.ShapeDtypeStruct((NSA_HEADS, r, c), F32),
        compiler_params=_cparams(("arbitrary",)),
        name=name,
    )(rel_bias, jnp.asarray(idx))


def _cover_np(n_cmp_cols, n_blk_cols, delta_of):
    u = np.arange(n_cmp_cols)[:, None]
    j = np.arange(n_blk_cols)[None, :]
    delta = delta_of(u, j)
    shared = np.minimum(CMP_STRIDE * delta + CMP_BLOCK, SEL_BLOCK) - np.maximum(CMP_STRIDE * delta, 0)
    return (np.maximum(shared, 0) / CMP_STRIDE).astype(np.float32)


def _compress(load_rows, n_chunk, pos_term_ref, w1_ref, w2_ref):
    parts = []
    for s0 in range(0, CMP_STRIDE, CMP_GROUP):
        lhs = jnp.concatenate([load_rows(s0 + k).astype(BF16) for k in range(CMP_GROUP)], axis=1)
        w = w1_ref[s0:s0 + CMP_GROUP].reshape(CMP_GROUP * HEAD_DIM, 2 * HEAD_DIM)
        parts.append(_dot(lhs, w))
    while len(parts) > 1:
        parts = [a + b for a, b in zip(parts[0::2], parts[1::2])]
    acc = parts[0] + pos_term_ref[0:1, :]
    nxt = pltpu.roll(acc[:, HEAD_DIM:], n_chunk - 1, axis=0)
    hid = _silu(acc[:, :HEAD_DIM] + nxt)
    return _dot(hid.astype(BF16), w2_ref[...])


def _pos_term_kernel(pos_ref, w1_ref, o_ref):
    halves = []
    for half in range(2):
        acc = jnp.zeros((8, HEAD_DIM), F32)
        for s in range(CMP_STRIDE):
            row = half * CMP_STRIDE + s
            p = jnp.broadcast_to(pos_ref[0, row:row + 1, :], (8, HEAD_DIM)).astype(BF16)
            acc = acc + _dot(p, w1_ref[0, s][:, half * HEAD_DIM:(half + 1) * HEAD_DIM])
        halves.append(acc)
    o_ref[0] = jnp.concatenate(halves, axis=1)


def _pos_term(pos, w1cat):
    return pl.pallas_call(
        _pos_term_kernel,
        grid=(2,),
        in_specs=[pl.BlockSpec((1, CMP_BLOCK, HEAD_DIM), lambda i: (i, 0, 0)),
                  pl.BlockSpec((1, CMP_STRIDE, HEAD_DIM, 2 * HEAD_DIM), lambda i: (i, 0, 0, 0))],
        out_specs=pl.BlockSpec((1, 8, 2 * HEAD_DIM), lambda i: (i, 0, 0)),
        out_shape=jax.ShapeDtypeStruct((2, 8, 2 * HEAD_DIM), F32),
        compiler_params=_cparams(("arbitrary",)),
        name="pos_term",
    )(pos, w1cat)


def _flash_step(carry, s, v_t):
    outs = []
    w = s.shape[1] // FLASH_SPLIT
    for c in range(FLASH_SPLIT):
        m, l, acc = (x[:, c * w:(c + 1) * w] for x in carry)
        sc = s[:, c * w:(c + 1) * w]
        m_new = jnp.maximum(m, jnp.max(sc, axis=0, keepdims=True))
        alpha = jnp.exp(m - m_new)
        e = jnp.exp(sc - m_new)
        l = alpha * l + jnp.sum(e, axis=0, keepdims=True)
        acc = alpha * acc + _dot(v_t, e.astype(BF16))
        outs.append((m_new, l, acc))
    return tuple(jnp.concatenate([o[i] for o in outs], axis=1) for i in range(3))


def _flash_init(cols):
    return (jnp.full((1, cols), M_INIT, F32), jnp.zeros((1, cols), F32), jnp.zeros((HEAD_DIM, cols), F32))


def _masked_softmax(s, valid):
    s = jnp.where(valid, s, NEG)
    m = jnp.max(s, axis=-1, keepdims=True)
    e = jnp.where(valid, jnp.exp(s - m), 0.0)
    return e / jnp.maximum(jnp.sum(e, axis=-1, keepdims=True), 1e-30)


def _split_dot(x, w):
    hi = x.astype(BF16)
    lo = (x - hi.astype(F32)).astype(BF16)
    return _dot(hi, w) + _dot(lo, w)


def _top_n_mask(score, index, axis):
    sel = jnp.zeros(score.shape, F32)
    for _ in range(SEL_TOPN):
        mx = jnp.max(score, axis=axis, keepdims=True)
        first = jnp.min(jnp.where(score == mx, index, 1e9), axis=axis, keepdims=True)
        hit = index == first
        sel = jnp.where(hit, 1.0, sel)
        score = jnp.where(hit, -jnp.inf, score)
    return sel


def _top_n_mask_by_rank(score, n_cand):
    lane = lax.broadcasted_iota(jnp.int32, score.shape, 1)
    rank = jnp.zeros(score.shape, F32)
    for j in range(n_cand):
        col = score[:, j:j + 1]
        ahead = (col > score) | ((col == score) & (lane > j))
        rank = rank + ahead.astype(F32)
    return ((rank < SEL_TOPN) & (lane < n_cand)).astype(F32)


def _compress_prompt_kernel(rows_ref, pos_ref, w1_ref, w2_ref, o_ref):
    out = _compress(lambda s: rows_ref[pl.ds(s, N_CMP_PAD, stride=CMP_STRIDE), :], N_CMP_PAD,
                    pos_ref.at[0], w1_ref.at[0], w2_ref.at[0])
    real = lax.broadcasted_iota(jnp.int32, (N_CMP_PAD, HEAD_DIM), 0) < N_CMP_PAD - 1
    o_ref[0, 0, 0:SLAB_OFF, :] = jnp.zeros((SLAB_OFF, HEAD_DIM), F32)
    o_ref[0, 0, SLAB_OFF:SLAB_OFF + N_CMP_PAD, :] = jnp.where(real, out, 0.0)
    o_ref[0, 0, SLAB_OFF + N_CMP_PAD:KC_ROWS, :] = jnp.zeros((KC_ROWS - SLAB_OFF - N_CMP_PAD, HEAD_DIM), F32)


def _compress_prompt(z, pos, w1, w2):
    kv_blk = Z_KV // HEAD_DIM
    return pl.pallas_call(
        _compress_prompt_kernel,
        grid=(2, NSA_KV_HEADS),
        in_specs=[pl.BlockSpec((SEQ, HEAD_DIM), lambda i, h: (0, kv_blk + NSA_KV_HEADS * i + h)),
                  pl.BlockSpec((1, 8, 2 * HEAD_DIM), lambda i, h: (i, 0, 0)),
                  pl.BlockSpec((1, CMP_STRIDE, HEAD_DIM, 2 * HEAD_DIM), lambda i, h: (i, 0, 0, 0)),
                  pl.BlockSpec((1, HEAD_DIM, HEAD_DIM), lambda i, h: (i, 0, 0))],
        out_specs=pl.BlockSpec((1, 1, KC_ROWS, HEAD_DIM), lambda i, h: (i, h, 0, 0)),
        out_shape=jax.ShapeDtypeStruct((2, NSA_KV_HEADS, KC_ROWS, HEAD_DIM), F32),
        compiler_params=_cparams(("parallel", "parallel")),
        name="compress_prompt",
    )(z, pos, w1, w2)


def _nsa_prompt_kernel(q_ref, misc_ref, ksel_ref, vsel_rows_ref, kwin_ref, vwin_rows_ref, kc_ref, vc_ref,
                       bias_ref, cover_ref, erel_ref, o_ref, vsel_ref, vwin_ref):
    kvh = pl.program_id(0)
    t = pl.program_id(1)
    cols = NSA_GROUP * Q_TILE

    @pl.when(t == 0)
    def _():
        def body(kt, _):
            rows = pl.ds(pl.multiple_of(kt * Q_TILE, Q_TILE), Q_TILE)
            vsel_ref[:, rows] = vsel_rows_ref[rows, :].astype(F32).T.astype(BF16)
            vwin_ref[:, rows] = vwin_rows_ref[rows, :].astype(F32).T.astype(BF16)
            return 0
        lax.fori_loop(0, N_QT, body, 0)

    q = q_ref[...] * (HEAD_DIM ** -0.5)
    qt = jnp.concatenate([q[:, g * HEAD_DIM:(g + 1) * HEAD_DIM].T for g in range(NSA_GROUP)], axis=1).astype(BF16)

    def bias_tile(k):
        return bias_ref[0, k * Q_TILE:(k + 1) * Q_TILE, :]
    b_diag, b_prev, b_first = bias_tile(0), bias_tile(1), bias_tile(2)

    def key_tile(ref, kt):
        return ref[pl.ds(pl.multiple_of(kt * Q_TILE, Q_TILE), Q_TILE), :]

    def value_tile(ref, kt):
        return ref[:, pl.ds(pl.multiple_of(kt * Q_TILE, Q_TILE), Q_TILE)]

    n_wt = WINDOW // Q_TILE + 1
    scores, values = [], []
    for w, b_tile in enumerate((b_first,) + (None,) * (n_wt - 3) + (b_prev, b_diag)):
        kt = t - (n_wt - 1) + w
        kc = jnp.maximum(kt, 0)
        s = _dot(key_tile(kwin_ref, kc), qt) + jnp.where(kt >= 0, 0.0, NEG)
        scores.append(s if b_tile is None else s + b_tile)
        values.append(value_tile(vwin_ref, kc))
    _, l, acc = _flash_step(_flash_init(cols), jnp.concatenate(scores, axis=0), jnp.concatenate(values, axis=1))
    o_win = acc / l

    start = pl.multiple_of(t * (Q_TILE // CMP_STRIDE), 8)
    kslab = kc_ref[0, 0, pl.ds(start, SLAB), :].astype(BF16)
    vslab = vc_ref[0, 0, pl.ds(start, SLAB), :].astype(BF16)
    b_cmp = bias_ref[0, 3 * Q_TILE:3 * Q_TILE + SLAB, :]
    u = lax.broadcasted_iota(jnp.int32, (SLAB, 1), 0)
    valid = (b_cmp > M_INIT) & (u >= SLAB_OFF - (Q_TILE // CMP_STRIDE) * t)
    s = jnp.where(valid, _dot(kslab, qt) + b_cmp, NEG)
    e = jnp.where(valid, jnp.exp(s - jnp.max(s, axis=0, keepdims=True)), 0.0)
    p_cmp = e / jnp.maximum(jnp.sum(e, axis=0, keepdims=True), 1e-30)
    o_cmp = _dot_tn(vslab, p_cmp.astype(BF16))

    p_sum = p_cmp[:, 0:Q_TILE]
    for g in range(1, NSA_GROUP):
        p_sum = p_sum + p_cmp[:, g * Q_TILE:(g + 1) * Q_TILE]
    hi = p_sum.astype(BF16)
    lo = (p_sum - hi.astype(F32)).astype(BF16)
    imp = _dot(cover_ref[...], hi) + _dot(cover_ref[...], lo)
    jr = lax.broadcasted_iota(jnp.int32, (N_SEL, Q_TILE), 0)
    qi = lax.broadcasted_iota(jnp.int32, (N_SEL, Q_TILE), 1)
    cur = REL0 + (qi >= SEL_BLOCK).astype(jnp.int32)
    first_blk = REL0 - 2 * t
    forced = (jr == first_blk) | (jr == cur) | (jr == cur - 1)
    in_range = (jr <= cur) & (jr >= first_blk)
    score = jnp.where(in_range, jnp.where(forced, FORCED_SCORE, imp), NEG)
    sel = _top_n_mask(score, jr.astype(F32), 0)

    unsel = ((sel - 1.0) * (-NEG)).astype(BF16)
    q_aug = jnp.concatenate([qt, jnp.concatenate([unsel] * NSA_GROUP, axis=1)], axis=0)

    def sweep_step(i, carry, masked, biases=()):
        scores, values = [], []
        for j in range(SWEEP_TILES):
            kt = t - SWEEP_TILES * i - (SWEEP_TILES - 1) + j
            kc = jnp.maximum(kt, 0) if masked else kt
            tile = jnp.concatenate([key_tile(ksel_ref, kc), erel_ref[t - kc]], axis=1)
            s = _dot(tile, q_aug)
            if masked:
                s = s + jnp.where(kt >= 0, 0.0, NEG)
            if j >= SWEEP_TILES - len(biases):
                s = s + biases[j - (SWEEP_TILES - len(biases))]
            scores.append(s)
            values.append(value_tile(vsel_ref, kc))
        return _flash_step(carry, jnp.concatenate(scores, axis=0), jnp.concatenate(values, axis=1))

    n_full = jnp.maximum(t - (SWEEP_TILES - 1), 0) // SWEEP_TILES
    carry = lax.fori_loop(1, n_full + 1, lambda i, c: sweep_step(i, c, False), _flash_init(cols))
    some_left = (t - SWEEP_TILES * (n_full + 1) >= 0).astype(jnp.int32)
    carry = lax.fori_loop(0, some_left, lambda _, c: sweep_step(n_full + 1, c, True), carry)
    _, l, acc = sweep_step(0, carry, True, (b_prev, b_diag))
    o_sel = acc / l

    gates = jax.nn.sigmoid(misc_ref[...]).T
    n_g = 3 * NSA_GROUP
    gk = jnp.where(kvh == 0, gates, pltpu.roll(gates, gates.shape[0] - n_g, axis=0))[MISC_GATES:MISC_GATES + n_g]
    for g in range(NSA_GROUP):
        c = slice(g * Q_TILE, (g + 1) * Q_TILE)
        o_g = (gk[3 * g:3 * g + 1] * o_cmp[:, c] + gk[3 * g + 1:3 * g + 2] * o_sel[:, c]
               + gk[3 * g + 2:3 * g + 3] * o_win[:, c])
        o_ref[:, g * HEAD_DIM:(g + 1) * HEAD_DIM] = o_g.T


def _nsa_prompt(z, kvb, kcp, bias, cover, erel):
    gw = NSA_GROUP * HEAD_DIM

    def k_spec(j):
        return pl.BlockSpec((SEQ, HEAD_DIM), lambda h, t, j=j: (0, 2 * j + h))
    return pl.pallas_call(
        _nsa_prompt_kernel,
        grid=(NSA_KV_HEADS, N_QT),
        in_specs=[pl.BlockSpec((Q_TILE, gw), lambda h, t: (t, h)),
                  pl.BlockSpec((Q_TILE, 128), lambda h, t: (t, Z_MISC // 128)),
                  k_spec(2), k_spec(3), k_spec(4), k_spec(5),
                  pl.BlockSpec((1, 1, KC_ROWS, HEAD_DIM), lambda h, t: (0, h, 0, 0)),
                  pl.BlockSpec((1, 1, KC_ROWS, HEAD_DIM), lambda h, t: (1, h, 0, 0)),
                  pl.BlockSpec((1,) + bias.shape[1:], lambda h, t: (h, 0, 0)),
                  pl.BlockSpec(cover.shape, lambda h, t: (0, 0)),
                  pl.BlockSpec(erel.shape, lambda h, t: (0, 0, 0))],
        out_specs=pl.BlockSpec((Q_TILE, gw), lambda h, t: (t, h)),
        out_shape=jax.ShapeDtypeStruct((SEQ, NSA_HEADS * HEAD_DIM), F32),
        scratch_shapes=[pltpu.VMEM((HEAD_DIM, SEQ), BF16), pltpu.VMEM((HEAD_DIM, SEQ), BF16)],
        compiler_params=_cparams(("arbitrary", "arbitrary")),
        name="nsa_prompt",
    )(z, z, kvb, kvb, kvb, kvb, kcp, kcp, bias, cover, erel)


S_ROWS = NSA_HEADS * DEC_SEQ
PAGE_ROWS = NSA_KV_HEADS * PAGE_SIZE
NEW_ROWS = NSA_KV_HEADS * DEC_SEQ
CHUNK_ROWS = NSA_KV_HEADS * CMP_STRIDE
CHUNK_PITCH = CHUNK_ROWS + 8


def _nsa_sample_kernel(pt_ref, q_ref, gate_ref, new_ref, kwin_ref, vwin_ref, kcmp_hbm, vcmp_hbm, ksel_hbm, vsel_hbm,
                       posk_ref, w1k_ref, w2k_ref, posv_ref, w1v_ref, w2v_ref, bias_ref, cover_ref, expand_ref,
                       o_ref, kwin_o_ref, vwin_o_ref, kcmp_buf, vcmp_buf, ksel_buf, vsel_buf, sem):
    n_pg = N_PAGES
    b = pl.program_id(0)
    slot = b % 2
    page_chunks = PAGE_ROWS // CHUNK_ROWS
    hbm = (kcmp_hbm, vcmp_hbm, ksel_hbm, vsel_hbm)
    bufs = (kcmp_buf, vcmp_buf, ksel_buf, vsel_buf)

    def page_copy(k, p, page, into):
        if k < 2:
            src = hbm[k].at[pl.ds(page * page_chunks, page_chunks)]
            dst = bufs[k].at[pl.ds(into * S_CMP + p * page_chunks, page_chunks), pl.ds(0, CHUNK_ROWS), :]
        else:
            src = hbm[k].at[pl.ds(page * PAGE_ROWS, PAGE_ROWS)]
            dst = bufs[k].at[pl.ds(into * (n_pg * PAGE_ROWS) + p * PAGE_ROWS, PAGE_ROWS), :]
        return pltpu.make_async_copy(src, dst, sem.at[into, k])

    def gather(seq, into):
        for p in range(n_pg):
            page = pt_ref[seq, p]
            for k in range(4):
                page_copy(k, p, page, into).start()

    @pl.when(b == 0)
    def _():
        gather(0, 0)

    @pl.when(b + 1 < DEC_BATCH)
    def _():
        gather(b + 1, 1 - slot)

    for p in range(n_pg):
        for k in range(4):
            page_copy(k, p, 0, slot).wait()

    def pages(k, p):
        return bufs[k][pl.ds(pl.multiple_of(slot * (n_pg * PAGE_ROWS), PAGE_ROWS) + p * PAGE_ROWS, PAGE_ROWS), :]

    comp = []
    for k, (pos_ref, w1_ref, w2_ref) in enumerate(((posk_ref, w1k_ref, w2k_ref), (posv_ref, w1v_ref, w2v_ref))):
        flat = bufs[k].reshape(2 * S_CMP * CHUNK_PITCH, HEAD_DIM)

        def load_rows(s, flat=flat):
            return jnp.concatenate(
                [flat[pl.ds(slot * (S_CMP * CHUNK_PITCH) + NSA_KV_HEADS * s + h, S_CMP, stride=CHUNK_PITCH), :]
                 for h in range(NSA_KV_HEADS)], axis=0)
        comp.append(_compress(load_rows, S_CMP_COLS, pos_ref, w1_ref, w2_ref).astype(BF16))

    keep = NSA_KV_HEADS * WINDOW - NEW_ROWS
    kwin_o_ref[0:keep, :] = kwin_ref[NEW_ROWS:NSA_KV_HEADS * WINDOW, :]
    kwin_o_ref[keep:keep + NEW_ROWS, :] = new_ref[0, 4]
    vwin_o_ref[0:keep, :] = vwin_ref[NEW_ROWS:NSA_KV_HEADS * WINDOW, :]
    vwin_o_ref[keep:keep + NEW_ROWS, :] = new_ref[0, 5]

    qs = (q_ref[0] * (HEAD_DIM ** -0.5)).astype(BF16)
    o0, o1 = S_CMP_COLS, S_CMP_COLS + S_SEL_COLS
    b_cmp = bias_ref[:, 0:o0]
    b_new = bias_ref[:, o1 - S_NEW_COLS:o1]

    def attend(scores, bias, values):
        s = jnp.concatenate(scores, axis=1) + bias
        e = jnp.exp(s - jnp.max(s, axis=-1, keepdims=True))
        acc = jnp.zeros((S_ROWS, HEAD_DIM), F32)
        c0 = 0
        for v in values:
            acc = acc + _dot(e[:, c0:c0 + v.shape[0]].astype(BF16), v)
            c0 += v.shape[0]
        return acc / jnp.sum(e, axis=-1, keepdims=True)

    p_cmp = _masked_softmax(_dot_nt(qs, comp[0]) + b_cmp, b_cmp > M_INIT)
    o_cmp = _dot(p_cmp.astype(BF16), comp[1])

    imp = _split_dot(p_cmp, cover_ref[...])
    imp = imp + pltpu.roll(imp, 8, axis=0) + pltpu.roll(imp, 16, axis=0) + pltpu.roll(imp, 24, axis=0)
    blk = lax.broadcasted_iota(jnp.int32, (S_ROWS, 128), 1)
    cur = PAST_LEN // SEL_BLOCK
    forced = (blk == 0) | (blk == cur) | (blk == cur - 1)
    score = jnp.where(blk <= cur, jnp.where(forced, FORCED_SCORE, imp), NEG)
    sel = _top_n_mask_by_rank(score, cur + 1).astype(BF16)
    mask_add = (_dot(sel, expand_ref[...]) - 1.0) * (-NEG)

    def new_tile(j):
        pad = jnp.zeros((S_NEW_COLS - NEW_ROWS, HEAD_DIM), F32)
        return jnp.concatenate([new_ref[0, j], pad], axis=0).astype(BF16)

    nk, nv = new_tile(2), new_tile(3)
    scores = [_dot_nt(qs, pages(2, p).astype(BF16)) for p in range(n_pg)] + [_dot_nt(qs, nk)]
    values = [pages(3, p).astype(BF16) for p in range(n_pg)] + [nv]
    o_sel = attend(scores, bias_ref[:, o0:o1] + mask_add, values)

    nk, nv = new_tile(4), new_tile(5)
    scores = [_dot_nt(qs, kwin_ref[...].astype(BF16)), _dot_nt(qs, nk)]
    bias_w = jnp.concatenate([bias_ref[:, o1:o1 + S_WIN_COLS], b_new], axis=1)
    o_win = attend(scores, bias_w, [vwin_ref[...].astype(BF16), nv])

    g = jax.nn.sigmoid(gate_ref[0])
    o_ref[0] = g[:, 0:1] * o_cmp + g[:, 1:2] * o_sel + g[:, 2:3] * o_win


def _nsa_sample(page_table, q_s, gate_s, new_s, kwin, vwin, pools, cmp_w, bias, cover, expand):
    win_rows = NSA_KV_HEADS * WINDOW

    def full(a):
        return pl.BlockSpec(a.shape, lambda b, pt, n=a.ndim: (0,) * n)

    def per_b(a):
        return pl.BlockSpec((1,) + a.shape[1:], lambda b, pt, n=a.ndim: (b,) + (0,) * (n - 1))
    win_spec = pl.BlockSpec((win_rows, HEAD_DIM), lambda b, pt: (b, 0))
    consts = list(cmp_w) + [bias, cover, expand]
    grid_spec = pltpu.PrefetchScalarGridSpec(
        num_scalar_prefetch=1,
        grid=(DEC_BATCH,),
        in_specs=[per_b(q_s), per_b(gate_s), per_b(new_s), win_spec, win_spec]
        + [pl.BlockSpec(memory_space=pl.ANY)] * 4 + [full(a) for a in consts],
        out_specs=[per_b(q_s), win_spec, win_spec],
        scratch_shapes=[pltpu.VMEM((2 * S_CMP, CHUNK_PITCH, HEAD_DIM), F32)] * 2
        + [pltpu.VMEM((2 * N_PAGES * PAGE_ROWS, HEAD_DIM), F32)] * 2
        + [pltpu.SemaphoreType.DMA((2, 4))],
    )
    return pl.pallas_call(
        _nsa_sample_kernel,
        grid_spec=grid_spec,
        out_shape=[jax.ShapeDtypeStruct(q_s.shape, F32),
                   jax.ShapeDtypeStruct(kwin.shape, F32),
                   jax.ShapeDtypeStruct(vwin.shape, F32)],
        compiler_params=_cparams(("arbitrary",)),
        name="nsa_sample",
    )(page_table, q_s, gate_s, new_s, kwin, vwin, *pools, *consts)


def _log_decay(a_blk, wa_ref, ba_ref):
    x = _dot(a_blk.astype(BF16), wa_ref[...]) + ba_ref[...]
    return (jnp.minimum(x, 0.0) - jnp.log1p(jnp.exp(-jnp.abs(x)))) * (1.0 / GLA_TAU)


def _segment_cumsum(g, seg):
    pos = lax.broadcasted_iota(jnp.int32, g.shape, 0) % seg
    cum = g
    sh = 1
    while sh < seg:
        cum = cum + jnp.where(pos >= sh, pltpu.roll(cum, sh, axis=0), 0.0)
        sh *= 2
    return cum


def _gla_prompt_kernel(q_ref, k_ref, v_ref, r_ref, a_ref, wa_ref, ba_ref, gn_ref, o_ref, st_o_ref, st_ref,
                       *, n_blk, tb):
    tbi = pl.program_id(0)

    @pl.when(tbi == 0)
    def _():
        st_ref[...] = jnp.zeros_like(st_ref)

    c = GLA_CHUNK
    tril = lax.broadcasted_iota(jnp.int32, (c, c), 0) >= lax.broadcasted_iota(jnp.int32, (c, c), 1)
    cum = _segment_cumsum(_log_decay(a_ref[...], wa_ref, ba_ref), c)
    q = q_ref[...] * (GLA_DK ** -0.5)
    k = k_ref[...]
    v = v_ref[...].astype(BF16)
    qe = (q * jnp.exp(cum)).astype(BF16)
    kd = (k * jnp.exp(-cum)).astype(BF16)
    heads = [(slice(h * GLA_DK, (h + 1) * GLA_DK), slice(h * GLA_DV, (h + 1) * GLA_DV)) for h in range(GLA_HEADS)]
    sts = [st_ref[h] for h in range(GLA_HEADS)]
    outs = [[] for _ in range(GLA_HEADS)]
    for ci in range(tb // c):
        r = slice(ci * c, (ci + 1) * c)
        last = cum[ci * c + c - 1:ci * c + c, :]
        kl = (k[r] * jnp.exp(last - cum[r])).astype(BF16)
        decay = jnp.exp(last)
        for h, (dk, dv) in enumerate(heads):
            att = jnp.where(tril, _dot_nt(qe[r, dk], kd[r, dk]), 0.0)
            outs[h].append(_dot_nt(qe[r, dk], sts[h].astype(BF16)) + _dot(att.astype(BF16), v[r, dv]))
            sts[h] = decay[:, dk] * sts[h] + _dot_tn(v[r, dv], kl[:, dk])
    for h, (dk, dv) in enumerate(heads):
        st_ref[h] = sts[h]
        o = jnp.concatenate(outs[h], axis=0)
        o_ref[:, dv] = _rms(o, gn_ref[...]) * _silu(r_ref[:, dv])

    @pl.when(tbi == n_blk - 1)
    def _():
        st_o_ref[...] = st_ref[...]


def _gla_prompt(z, wa, ba, gn, tb=256):
    n_blk = SEQ // tb
    hk, hv = GLA_HEADS * GLA_DK, GLA_HEADS * GLA_DV
    st_shape = (GLA_HEADS, GLA_DV, GLA_DK)
    return pl.pallas_call(
        functools.partial(_gla_prompt_kernel, n_blk=n_blk, tb=tb),
        grid=(n_blk,),
        in_specs=[pl.BlockSpec((tb, hk), lambda i: (i, Z_QG // hk)),
                  pl.BlockSpec((tb, hk), lambda i: (i, Z_KG // hk)),
                  pl.BlockSpec((tb, hv), lambda i: (i, Z_VG // hv)),
                  pl.BlockSpec((tb, hv), lambda i: (i, Z_RG // hv)),
                  pl.BlockSpec((tb, 128), lambda i: (i, Z_MISC // 128)),
                  pl.BlockSpec((128, hk), lambda i: (0, 0)),
                  pl.BlockSpec((1, hk), lambda i: (0, 0)),
                  pl.BlockSpec((1, GLA_DV), lambda i: (0, 0))],
        out_specs=[pl.BlockSpec((tb, hv), lambda i: (i, 0)),
                   pl.BlockSpec(st_shape, lambda i: (0, 0, 0))],
        out_shape=[jax.ShapeDtypeStruct((SEQ, hv), F32), jax.ShapeDtypeStruct(st_shape, F32)],
        scratch_shapes=[pltpu.VMEM(st_shape, F32)],
        compiler_params=_cparams(("arbitrary",)),
        name="gla_prompt",
    )(z, z, z, z, z, wa, ba, gn)


GS_B = 4


def _gla_sample_kernel(q_ref, k_ref, v_ref, r_ref, a_ref, wa_ref, ba_ref, gn_ref, s_ref, o_ref, s_o_ref):
    rows = GS_B * DEC_SEQ
    a = a_ref[...]
    ri = lax.broadcasted_iota(jnp.int32, (rows, rows), 0)
    ci = lax.broadcasted_iota(jnp.int32, (rows, rows), 1)
    same_causal = (ri // DEC_SEQ == ci // DEC_SEQ) & (ri >= ci)
    row_b = lax.broadcasted_iota(jnp.int32, (rows, 1), 0) // DEC_SEQ
    ones = jnp.ones((rows, 128), BF16)
    for h in range(GLA_HEADS):
        dk = slice(h * GLA_DK, (h + 1) * GLA_DK)
        dv = slice(h * GLA_DV, (h + 1) * GLA_DV)
        g = _log_decay(a, wa_ref.at[:, dk], ba_ref.at[:, dk])
        cum = _segment_cumsum(g, DEC_SEQ)
        q = q_ref[:, dk] * (GLA_DK ** -0.5)
        k = k_ref[:, dk]
        v = v_ref[:, dv].astype(BF16)
        qe = (q * jnp.exp(cum)).astype(BF16)
        kd = (k * jnp.exp(-cum)).astype(BF16)
        att = jnp.where(same_causal, _dot_nt(qe, kd), 0.0)
        o = _dot(att.astype(BF16), v)
        for b in range(GS_B):
            mine = row_b == b
            last = cum[b * DEC_SEQ + DEC_SEQ - 1:(b + 1) * DEC_SEQ, :]
            s = s_ref[b, h]
            o = o + jnp.where(mine, _dot(qe, s.astype(BF16)), 0.0)
            kl = jnp.where(mine, k * jnp.exp(last - cum), 0.0)
            hi = jnp.where(mine, g, 0.0).astype(BF16)
            lo = (jnp.where(mine, g, 0.0) - hi.astype(F32)).astype(BF16)
            last_col = (_dot_tn(hi, ones) + _dot_tn(lo, ones))[:, 0:1]
            s_o_ref[b, h] = jnp.exp(last_col) * s + _dot_tn(kl.astype(BF16), v)
        o_ref[:, dv] = _rms(o, gn_ref[...]) * _silu(r_ref[:, dv])


def _gla_sample(zs, state, wa, ba, gn):
    rows = GS_B * DEC_SEQ
    n = DEC_BATCH * DEC_SEQ
    hk, hv = GLA_HEADS * GLA_DK, GLA_HEADS * GLA_DV
    st_spec = pl.BlockSpec((GS_B, GLA_HEADS, GLA_DK, GLA_DV), lambda i: (i, 0, 0, 0))
    return pl.pallas_call(
        _gla_sample_kernel,
        grid=(DEC_BATCH // GS_B,),
        in_specs=[pl.BlockSpec((rows, hk), lambda i: (i, Z_QG // hk)),
                  pl.BlockSpec((rows, hk), lambda i: (i, Z_KG // hk)),
                  pl.BlockSpec((rows, hv), lambda i: (i, Z_VG // hv)),
                  pl.BlockSpec((rows, hv), lambda i: (i, Z_RG // hv)),
                  pl.BlockSpec((rows, 128), lambda i: (i, Z_MISC // 128)),
                  pl.BlockSpec((128, hk), lambda i: (0, 0)),
                  pl.BlockSpec((1, hk), lambda i: (0, 0)),
                  pl.BlockSpec((1, GLA_DV), lambda i: (0, 0)),
                  st_spec],
        out_specs=[pl.BlockSpec((rows, hv), lambda i: (i, 0)), st_spec],
        out_shape=[jax.ShapeDtypeStruct((n, hv), F32), jax.ShapeDtypeStruct(state.shape, F32)],
        compiler_params=_cparams(("parallel",)),
        name="gla_sample",
    )(zs, zs, zs, zs, zs, wa, ba, gn, state)


def _softmax_rows(s):
    m = jnp.max(s, axis=-1, keepdims=True)
    e = jnp.exp(s - m)
    return e / jnp.sum(e, axis=-1, keepdims=True)


def _mem_prompt_kernel(q_ref, k_ref, v_ref, o_ref):
    for h in range(MEM_HEADS):
        d = slice(h * MEM_HEAD_DIM, (h + 1) * MEM_HEAD_DIM)
        q = (q_ref[:, d] * (MEM_HEAD_DIM ** -0.5)).astype(BF16)
        p = _softmax_rows(_dot_nt(q, k_ref[:, d].astype(BF16)))
        o_ref[:, d] = _dot(p.astype(BF16), v_ref[:, d].astype(BF16))


def _mem_prompt(qm, memkv, tq=256):
    w = MEM_HEADS * MEM_HEAD_DIM
    return pl.pallas_call(
        _mem_prompt_kernel,
        grid=(SEQ // tq,),
        in_specs=[pl.BlockSpec((tq, w), lambda i: (i, 0)),
                  pl.BlockSpec((MEM_TOKENS, w), lambda i: (0, 0)),
                  pl.BlockSpec((MEM_TOKENS, w), lambda i: (0, 1))],
        out_specs=pl.BlockSpec((tq, w), lambda i: (i, 0)),
        out_shape=jax.ShapeDtypeStruct((SEQ, w), F32),
        compiler_params=_cparams(("parallel",)),
        name="mem_prompt",
    )(qm, memkv, memkv)


def _mem_sample_kernel(q_ref, k_ref, v_ref, o_ref):
    rows = MEM_HEADS * DEC_SEQ
    cols = MEM_HEADS * MEM_TOKENS
    row_h = lax.broadcasted_iota(jnp.int32, (rows, cols), 0) // DEC_SEQ
    col_h = lax.broadcasted_iota(jnp.int32, (rows, cols), 1) % MEM_HEADS
    for b in range(MS_B):
        kv_rows = slice(b * cols, (b + 1) * cols)
        q = (q_ref[b] * (MEM_HEAD_DIM ** -0.5)).astype(BF16)
        s = jnp.where(row_h == col_h, _dot_nt(q, k_ref[kv_rows, :].astype(BF16)), NEG)
        o_ref[b] = _dot(_softmax_rows(s).astype(BF16), v_ref[kv_rows, :].astype(BF16))


MS_B = 4


def _mem_sample(q_s, k_mem, v_mem):
    rows = MEM_HEADS * DEC_SEQ
    kv_spec = pl.BlockSpec((MS_B * MEM_HEADS * MEM_TOKENS, MEM_HEAD_DIM), lambda b: (b, 0))
    return pl.pallas_call(
        _mem_sample_kernel,
        grid=(DEC_BATCH // MS_B,),
        in_specs=[pl.BlockSpec((MS_B, rows, MEM_HEAD_DIM), lambda b: (b, 0, 0)), kv_spec, kv_spec],
        out_specs=pl.BlockSpec((MS_B, rows, MEM_HEAD_DIM), lambda b: (b, 0, 0)),
        out_shape=jax.ShapeDtypeStruct((DEC_BATCH, rows, MEM_HEAD_DIM), F32),
        compiler_params=_cparams(("parallel",)),
        name="mem_sample",
    )(q_s, k_mem, v_mem)


def _permute_w_in(w_in):
    qn, kv, gt, qg, kg, vg, rg, ag = jnp.split(
        w_in, np.cumsum([1024, 1536, 24, 512, 512, 1024, 1024, 16])[:-1].tolist(), axis=1)
    pad = jnp.zeros((w_in.shape[0], Z_W - Z_MISC - 40), w_in.dtype)
    return jnp.concatenate([qn, vg, rg, kv, qg, kg, gt, ag, pad], axis=1)


def kernel(x_prompt, x_sample, mem_prompt, cache_k_cmp, cache_v_cmp, cache_k_sel, cache_v_sel, cache_k_win,
           cache_v_win, state_gla, cache_k_mem, cache_v_mem, page_table, norm_ffn1, ffn1_w_gate, ffn1_w_up,
           ffn1_w_down, norm_mix, w_in, w_out, cmp_pos_k, cmp_w1_k, cmp_w2_k, cmp_pos_v, cmp_w1_v, cmp_w2_v,
           rel_bias, gla_w_a2, gla_b_a, gla_norm, norm_mem, norm_mem_src, w_mem_q, w_mem_k, w_mem_v, w_mem_o,
           norm_ffn2, ffn2_w_gate, ffn2_w_up, ffn2_w_down, norm_final):
    bf = lambda a: a.astype(BF16)
    row = lambda a: a.reshape(1, -1)
    nb, ns = DEC_BATCH, DEC_SEQ
    kvw = NSA_KV_HEADS * HEAD_DIM

    h1, h1n = _ffn([x_prompt[0], x_sample.reshape(nb * ns, D_MODEL)], row(norm_ffn1[0]), bf(ffn1_w_gate[0]),
                   bf(ffn1_w_up[0]), bf(ffn1_w_down[0]), row(norm_mix[0]), False)
    z = _matmul(h1n, bf(_permute_w_in(w_in[0])), 512, Z_W // 3, "proj_in")

    kv_p = z[:SEQ, Z_KV:Z_QG]
    kv_s = z[SEQ:, Z_KV:Z_QG].reshape(nb, ns, 6 * kvw)
    rows_p = [kv_p[:, j * kvw:(j + 1) * kvw].reshape(1, 1, SEQ, NSA_KV_HEADS, HEAD_DIM) for j in range(6)]
    rows_s = [kv_s[:, :, j * kvw:(j + 1) * kvw].reshape(1, nb, ns, NSA_KV_HEADS, HEAD_DIM) for j in range(4)]

    tab_p = _bias_tables(rel_bias, _prompt_bucket_table(), "bias_prompt")
    far = rel_bias[NUM_BUCKETS - 1][:, None, None]
    near = tab_p[:, :3 * Q_TILE]
    tab_p = jnp.concatenate([jnp.where(near > M_INIT, near - far, NEG), tab_p[:, 3 * Q_TILE:]], axis=1)
    tab_p = tab_p.reshape(NSA_KV_HEADS, NSA_GROUP, -1, Q_TILE).transpose(0, 2, 1, 3)
    tab_p = tab_p.reshape(NSA_KV_HEADS, -1, NSA_GROUP * Q_TILE)
    tab_s = _bias_tables(rel_bias, _sample_bucket_table(), "bias_sample")
    tab_s = tab_s.reshape(NSA_KV_HEADS, NSA_GROUP, NSA_KV_HEADS, ns, -1)
    tab_s = jnp.stack([tab_s[h, :, h] for h in range(NSA_KV_HEADS)], axis=1).reshape(S_ROWS, -1)

    cmp_w1 = bf(jnp.stack([cmp_w1_k[0], cmp_w1_v[0]]))
    cmp_w1 = jnp.concatenate([cmp_w1[:, :CMP_STRIDE], cmp_w1[:, CMP_STRIDE:]], axis=-1)
    cmp_w2 = bf(jnp.stack([cmp_w2_k[0], cmp_w2_v[0]]))
    cmp_pos = _pos_term(jnp.stack([cmp_pos_k[0], cmp_pos_v[0]]), cmp_w1)
    kcp = _compress_prompt(z, cmp_pos, cmp_w1, cmp_w2)
    cover_p = jnp.asarray(_cover_np(SLAB, N_SEL, lambda u, j: u - 4 * j - SLAB_OFF + 4 * REL0).T, BF16)
    erel = (np.arange(N_SEL)[None, None, :] == REL0 - 2 * np.arange(N_QT)[:, None, None]
            + (np.arange(Q_TILE)[None, :, None] >= SEL_BLOCK))
    o_nsa_p = _nsa_prompt(z, bf(kv_p), kcp, tab_p, cover_p, jnp.asarray(erel, BF16))

    zs = z[SEQ:]
    q_s = zs[:, Z_QN:Z_QN + NSA_HEADS * HEAD_DIM].reshape(nb, ns, NSA_KV_HEADS, NSA_GROUP, HEAD_DIM)
    q_s = q_s.transpose(0, 3, 2, 1, 4).reshape(nb, S_ROWS, HEAD_DIM)
    gate_s = zs[:, Z_MISC + MISC_GATES:Z_MISC + MISC_GATES + 3 * NSA_HEADS]
    gate_s = gate_s.reshape(nb, ns, NSA_KV_HEADS, NSA_GROUP, 3).transpose(0, 3, 2, 1, 4).reshape(nb, S_ROWS, 3)
    new_s = kv_s.reshape(nb, ns, 6, NSA_KV_HEADS, HEAD_DIM).transpose(0, 2, 1, 3, 4).reshape(nb, 6, NEW_ROWS, HEAD_DIM)
    as_rows = lambda c: c.reshape(-1, HEAD_DIM)
    pools = ([c.reshape(-1, CHUNK_ROWS, HEAD_DIM) for c in (cache_k_cmp, cache_v_cmp)]
             + [as_rows(c) for c in (cache_k_sel, cache_v_sel)])
    cover_s = _cover_np(S_CMP, 128, lambda c, j: c - 4 * j)
    cover_s = jnp.asarray(np.concatenate([cover_s] * NSA_KV_HEADS, axis=0), BF16)
    expand_s = jnp.asarray(np.arange(128)[:, None]
                           == (np.arange(S_SEL_COLS)[None, :] // (NSA_KV_HEADS * SEL_BLOCK)), BF16)
    o_nsa_s, kwin_s, vwin_s = _nsa_sample(
        page_table, q_s, gate_s, new_s, as_rows(cache_k_win), as_rows(cache_v_win), pools,
        (cmp_pos[0], cmp_w1[0], cmp_w2[0], cmp_pos[1], cmp_w1[1], cmp_w2[1]),
        tab_s, cover_s, expand_s)
    o_nsa_s = o_nsa_s.reshape(nb, NSA_GROUP, NSA_KV_HEADS, ns, HEAD_DIM).transpose(0, 3, 2, 1, 4)
    o_nsa_s = o_nsa_s.reshape(nb * ns, NSA_HEADS * HEAD_DIM)

    wa = bf(jnp.zeros((128, GLA_HEADS * GLA_DK), F32).at[MISC_A:MISC_A + GLA_RANK].set(gla_w_a2[0]))
    ba, gn = row(gla_b_a[0]), row(gla_norm[0])
    o_gla_p, st_p = _gla_prompt(z, wa, ba, gn)
    o_gla_s, st_s = _gla_sample(zs, state_gla[0], wa, ba, gn)

    half = NSA_HEADS * HEAD_DIM
    h2 = _matmul_res(h1, [(o_nsa_p, o_nsa_s), (o_gla_p, o_gla_s)], [bf(w_out[0][:half]), bf(w_out[0][half:])],
                     512, "proj_out")

    memkv = _norm_matmul(mem_prompt[0], row(norm_mem_src[0]),
                         bf(jnp.concatenate([w_mem_k[0], w_mem_v[0]], axis=1)), MEM_TOKENS, 512, "mem_kv")
    mw = MEM_HEADS * MEM_HEAD_DIM
    qm = _norm_matmul(h2, row(norm_mem[0]), bf(w_mem_q[0]), 512, mw, "mem_q")
    om_p = _mem_prompt(qm, memkv)
    qm_s = qm[SEQ:].reshape(nb, ns, MEM_HEADS, MEM_HEAD_DIM).transpose(0, 2, 1, 3)
    om_s = _mem_sample(qm_s.reshape(nb, MEM_HEADS * ns, MEM_HEAD_DIM),
                       as_rows(cache_k_mem), as_rows(cache_v_mem))
    om_s = om_s.reshape(nb, MEM_HEADS, ns, MEM_HEAD_DIM).transpose(0, 2, 1, 3).reshape(nb * ns, mw)
    h3 = _matmul_res(h2, [(om_p, om_s)], [bf(w_mem_o[0])], 512, "mem_out")

    y_p, y_s = _ffn([h3], row(norm_ffn2[0]), bf(ffn2_w_gate[0]), bf(ffn2_w_up[0]), bf(ffn2_w_down[0]),
                    row(norm_final), True)

    mem_shape = (1, 1, MEM_TOKENS, MEM_HEADS, MEM_HEAD_DIM)
    win_shape = (1, nb, WINDOW, NSA_KV_HEADS, HEAD_DIM)
    return (y_p.reshape(1, SEQ, D_MODEL), y_s.reshape(nb, ns, D_MODEL),
            rows_p[0], rows_p[1], rows_p[2], rows_p[3],
            rows_p[4][:, :, SEQ - WINDOW:], rows_p[5][:, :, SEQ - WINDOW:],
            st_p.transpose(0, 2, 1).reshape(1, 1, GLA_HEADS, GLA_DK, GLA_DV),
            memkv[:, :mw].reshape(mem_shape), memkv[:, mw:].reshape(mem_shape),
            rows_s[0], rows_s[1], rows_s[2], rows_s[3],
            kwin_s.reshape(win_shape), vwin_s.reshape(win_shape),
            st_s.reshape(1, nb, GLA_HEADS, GLA_DK, GLA_DV))
```

```python
import functools
import math

import numpy as np
import jax
import jax.numpy as jnp
from jax import lax
from jax.experimental import pallas as pl
from jax.experimental.pallas import tpu as pltpu

F32 = jnp.float32
BF16 = jnp.bfloat16

D_MODEL = 2048
SEQ = 8192
DEC_BATCH = 128
DEC_SEQ = 4
PAST_LEN = 2048
PAGE_SIZE = 128
N_PAGES = PAST_LEN // PAGE_SIZE
HEAD_DIM = 128
NSA_HEADS = 8
NSA_KV_HEADS = 2
NSA_GROUP = 4
CMP_BLOCK = 32
CMP_STRIDE = 16
SEL_BLOCK = 64
SEL_TOPN = 16
WINDOW = 512
FORCED_SCORE = 1.0e4
GLA_HEADS = 4
GLA_DV = 256
GLA_DK = 128
GLA_RANK = 16
GLA_TAU = 16.0
GLA_CHUNK = 32
MEM_TOKENS = 256
MEM_HEADS = 4
MEM_HEAD_DIM = 128
D_FF = 5632
NUM_BUCKETS = 32
MAX_DISTANCE = 128
RMS_EPS = 1e-6

N_TOK = SEQ + DEC_BATCH * DEC_SEQ
Z_QN, Z_VG, Z_RG, Z_KV, Z_QG, Z_KG, Z_MISC = 0, 1024, 2048, 3072, 4608, 5120, 5632
Z_W = 5760
MISC_GATES, MISC_A = 0, 24

NEG = -1e30
LOG2E = math.log2(math.e)
ONES_ROWS = 16
M_INIT = -1e29

Q_TILE = 128
N_QT = SEQ // Q_TILE
N_CMP_PAD = SEQ // CMP_STRIDE
SLAB = N_CMP_PAD + 128
SLAB_OFF = SLAB - 16
KC_ROWS = SLAB_OFF + N_CMP_PAD + 16
N_SEL = SEQ // SEL_BLOCK
REL0 = N_SEL - 2
CMP_GROUP = 4
SWEEP_TILES = 8
SWEEP_PARTS = 2
FLASH_SPLIT = 2

VMEM_LIMIT = 56 * 1024 * 1024


def _cparams(sem):
    return pltpu.CompilerParams(dimension_semantics=sem, vmem_limit_bytes=VMEM_LIMIT)


def _dot(a, b):
    return jnp.dot(a, b, preferred_element_type=F32)


def _dot_nt(a, b):
    return lax.dot_general(a, b, (((1,), (1,)), ((), ())), preferred_element_type=F32)


def _dot_tn(a, b):
    return lax.dot_general(a, b, (((0,), (0,)), ((), ())), preferred_element_type=F32)


def _rms(x, g):
    return x * lax.rsqrt(jnp.mean(x * x, axis=-1, keepdims=True) + RMS_EPS) * g


def _silu(x):
    return x * jax.nn.sigmoid(x)


def _ffn_kernel(*refs, n_ff, n_first, two_in, two_out):
    refs = list(refs)
    x_refs = [refs.pop(0) for _ in range(2 if two_in else 1)]
    g_ref, wg_ref, wu_ref, wd_ref, gf_ref = refs[:5]
    o_refs = refs[5:7]
    xn_ref, acc_ref = refs[-2:]
    i = pl.program_id(0)
    j = pl.program_id(1)

    def x_tile():
        return jnp.where(i < n_first, x_refs[0][...], x_refs[1][...]) if two_in else x_refs[0][...]

    @pl.when(j == 0)
    def _():
        xn_ref[...] = _rms(x_tile(), g_ref[...]).astype(BF16)
        acc_ref[...] = jnp.zeros_like(acc_ref)

    xn = xn_ref[...]
    hid = _silu(_dot(xn, wg_ref[...])) * _dot(xn, wu_ref[...])
    acc_ref[...] += _dot(hid.astype(BF16), wd_ref[...])

    @pl.when(j == n_ff - 1)
    def _():
        h = x_tile() + 0.5 * acc_ref[...]
        if two_out:
            h = _rms(h, gf_ref[...])

            @pl.when(i < n_first)
            def _():
                o_refs[0][...] = h

            @pl.when(i >= n_first)
            def _():
                o_refs[1][...] = h
        else:
            o_refs[0][...] = h
            o_refs[1][...] = _rms(h, gf_ref[...]).astype(BF16)


FFN_TM = 512
FFN_TF = 512


def _ffn(xs, g, wg, wu, wd, gf, split_out):
    tm, tf, d = FFN_TM, FFN_TF, D_MODEL
    n_ff = D_FF // tf
    n_first = SEQ // tm
    two_in = len(xs) == 2

    def first(i, j):
        return (jnp.minimum(i, n_first - 1), 0)

    def second(i, j):
        return (jnp.maximum(i - n_first, 0), 0)
    whole = pl.BlockSpec((tm, d), lambda i, j: (i, 0))
    pair = [pl.BlockSpec((tm, d), first), pl.BlockSpec((tm, d), second)]
    n_s = DEC_BATCH * DEC_SEQ
    return pl.pallas_call(
        functools.partial(_ffn_kernel, n_ff=n_ff, n_first=n_first, two_in=two_in, two_out=split_out),
        grid=(N_TOK // tm, n_ff),
        in_specs=(pair if two_in else [whole])
        + [pl.BlockSpec((1, d), lambda i, j: (0, 0)),
           pl.BlockSpec((d, tf), lambda i, j: (0, j)),
           pl.BlockSpec((d, tf), lambda i, j: (0, j)),
           pl.BlockSpec((tf, d), lambda i, j: (j, 0)),
           pl.BlockSpec((1, d), lambda i, j: (0, 0))],
        out_specs=pair if split_out else [whole, whole],
        out_shape=([jax.ShapeDtypeStruct((SEQ, d), F32), jax.ShapeDtypeStruct((n_s, d), F32)] if split_out
                   else [jax.ShapeDtypeStruct((N_TOK, d), F32), jax.ShapeDtypeStruct((N_TOK, d), BF16)]),
        scratch_shapes=[pltpu.VMEM((tm, d), BF16), pltpu.VMEM((tm, d), F32)],
        compiler_params=_cparams(("arbitrary", "arbitrary")),
        name="ffn",
    )(*xs, g, wg, wu, wd, gf)


def _norm_matmul_kernel(x_ref, g_ref, w_ref, o_ref, xn_ref):
    @pl.when(pl.program_id(1) == 0)
    def _():
        xn_ref[...] = _rms(x_ref[...], g_ref[...]).astype(BF16)

    o_ref[...] = _dot(xn_ref[...], w_ref[...])


def _norm_matmul(x, g, w, tm, tn, name):
    n, d = x.shape
    dout = w.shape[1]
    return pl.pallas_call(
        _norm_matmul_kernel,
        grid=(n // tm, dout // tn),
        in_specs=[pl.BlockSpec((tm, d), lambda i, j: (i, 0)),
                  pl.BlockSpec((1, d), lambda i, j: (0, 0)),
                  pl.BlockSpec((d, tn), lambda i, j: (0, j))],
        out_specs=pl.BlockSpec((tm, tn), lambda i, j: (i, j)),
        out_shape=jax.ShapeDtypeStruct((n, dout), F32),
        scratch_shapes=[pltpu.VMEM((tm, d), BF16)],
        compiler_params=_cparams(("parallel", "arbitrary")),
        name=name,
    )(x, g, w)


def _matmul_kernel(x_ref, w_ref, o_ref):
    o_ref[...] = _dot(x_ref[...], w_ref[...])


def _matmul(x, w, tm, tn, name):
    n, d = x.shape
    dout = w.shape[1]
    return pl.pallas_call(
        _matmul_kernel,
        grid=(dout // tn, n // tm),
        in_specs=[pl.BlockSpec((tm, d), lambda j, i: (i, 0)),
                  pl.BlockSpec((d, tn), lambda j, i: (0, j))],
        out_specs=pl.BlockSpec((tm, tn), lambda j, i: (i, j)),
        out_shape=jax.ShapeDtypeStruct((n, dout), F32),
        compiler_params=_cparams(("arbitrary", "arbitrary")),
        name=name,
    )(x, w)


def _matmul_res_kernel(*refs, n_lhs, n_first):
    res_ref = refs[0]
    o_ref = refs[1 + 3 * n_lhs]
    first = pl.program_id(0) < n_first
    acc = res_ref[...]
    for k in range(n_lhs):
        lhs = jnp.where(first, refs[1 + 2 * k][...], refs[2 + 2 * k][...])
        acc = acc + _dot(lhs.astype(BF16), refs[1 + 2 * n_lhs + k][...])
    o_ref[...] = acc


def _matmul_res(res, lhs, ws, tm, name):
    n, d = res.shape
    n_first = SEQ // tm

    def pair_specs(width):
        return [pl.BlockSpec((tm, width), lambda i: (jnp.minimum(i, n_first - 1), 0)),
                pl.BlockSpec((tm, width), lambda i: (jnp.maximum(i - n_first, 0), 0))]
    return pl.pallas_call(
        functools.partial(_matmul_res_kernel, n_lhs=len(lhs), n_first=n_first),
        grid=(n // tm,),
        in_specs=([pl.BlockSpec((tm, d), lambda i: (i, 0))]
                  + [spec for a, _ in lhs for spec in pair_specs(a.shape[1])]
                  + [pl.BlockSpec(w.shape, lambda i: (0, 0)) for w in ws]),
        out_specs=pl.BlockSpec((tm, d), lambda i: (i, 0)),
        out_shape=jax.ShapeDtypeStruct((n, d), F32),
        compiler_params=_cparams(("arbitrary",)),
        name=name,
    )(res, *[a for pair in lhs for a in pair], *ws)


def _rel_bucket_np(dist):
    n = np.maximum(dist, 0)
    exact = NUM_BUCKETS // 2
    nf = np.maximum(n, 1).astype(np.float32)
    large = exact + (np.log(nf / np.float32(exact)) / np.float32(math.log(MAX_DISTANCE / exact))
                     * np.float32(NUM_BUCKETS - exact)).astype(np.int32)
    return np.where(n < exact, n, np.minimum(large, NUM_BUCKETS - 1)).astype(np.int32)


def _bucket_or_masked(dist, valid):
    return np.where(valid, _rel_bucket_np(dist), -1).astype(np.int32)


def _prompt_bucket_table():
    i = np.arange(Q_TILE)[:, None]
    j = np.arange(Q_TILE)[None, :]
    u = np.arange(SLAB)[None, :]
    dist_c = i - CMP_STRIDE * u + (CMP_STRIDE * SLAB_OFF - (CMP_BLOCK - 1))
    diag = _bucket_or_masked(i - j, i - j >= 0)
    prev = _bucket_or_masked(Q_TILE + i - j, np.ones((Q_TILE, Q_TILE), bool))
    first = _bucket_or_masked(WINDOW + i - j, j > i)
    cmp_ = _bucket_or_masked(dist_c, dist_c >= 0)
    return np.concatenate([diag, prev, first, cmp_], axis=1).T


S_CMP = PAST_LEN // CMP_STRIDE
S_CMP_COLS = NSA_KV_HEADS * S_CMP
S_NEW_COLS = 128
S_SEL_COLS = NSA_KV_HEADS * PAST_LEN + S_NEW_COLS
S_WIN_COLS = NSA_KV_HEADS * WINDOW


def _sample_bucket_table():
    i = np.arange(DEC_SEQ)[:, None]
    pos = PAST_LEN + i
    rows = []
    for h in range(NSA_KV_HEADS):
        col = np.arange(S_CMP_COLS)[None, :]
        c = col % S_CMP
        dist_c = pos - (c * CMP_STRIDE + CMP_BLOCK - 1)
        cmp_ = _bucket_or_masked(dist_c, (col // S_CMP == h) & (c < S_CMP - 1) & (dist_c >= 0))
        col = np.arange(S_SEL_COLS)[None, :]
        key = col // NSA_KV_HEADS
        sel = _bucket_or_masked(pos - key, (col % NSA_KV_HEADS == h) & (key <= pos))
        col = np.arange(S_WIN_COLS)[None, :]
        dist_w = pos - (PAST_LEN - WINDOW + col // NSA_KV_HEADS)
        win = _bucket_or_masked(dist_w, (col % NSA_KV_HEADS == h) & (dist_w < WINDOW))
        rows.append(np.concatenate([cmp_, sel, win], axis=1))
    return np.concatenate(rows, axis=0)


def _bias_table_kernel(tab_ref, idx_ref, o_ref):
    h = pl.program_id(0)
    idx = idx_ref[...]
    out = jnp.full(idx.shape, NEG, F32)
    for b in range(NUM_BUCKETS):
        out = jnp.where(idx == b, tab_ref[b, h], out)
    o_ref[0] = out


def _bias_tables(rel_bias, idx, name):
    r, c = idx.shape
    return pl.pallas_call(
        _bias_table_kernel,
        grid=(NSA_HEADS,),
        in_specs=[pl.BlockSpec(memory_space=pltpu.SMEM),
                  pl.BlockSpec((r, c), lambda h: (0, 0))],
        out_specs=pl.BlockSpec((1, r, c), lambda h: (h, 0, 0)),
        out_shape=jax.ShapeDtypeStruct((NSA_HEADS, r, c), F32),
        compiler_params=_cparams(("arbitrary",)),
        name=name,
    )(rel_bias, jnp.asarray(idx))


def _cover_np(n_cmp_cols, n_blk_cols, delta_of):
    u = np.arange(n_cmp_cols)[:, None]
    j = np.arange(n_blk_cols)[None, :]
    delta = delta_of(u, j)
    shared = np.minimum(CMP_STRIDE * delta + CMP_BLOCK, SEL_BLOCK) - np.maximum(CMP_STRIDE * delta, 0)
    return (np.maximum(shared, 0) / CMP_STRIDE).astype(np.float32)


def _compress(load_rows, n_chunk, pos_term_ref, w1_ref, w2_ref):
    parts = []
    for s0 in range(0, CMP_STRIDE, CMP_GROUP):
        lhs = jnp.concatenate([load_rows(s0 + k).astype(BF16) for k in range(CMP_GROUP)], axis=1)
        w = w1_ref[s0:s0 + CMP_GROUP].reshape(CMP_GROUP * HEAD_DIM, 2 * HEAD_DIM)
        parts.append(_dot(lhs, w))
    while len(parts) > 1:
        parts = [a + b for a, b in zip(parts[0::2], parts[1::2])]
    acc = parts[0] + pos_term_ref[0:1, :]
    nxt = pltpu.roll(acc[:, HEAD_DIM:], n_chunk - 1, axis=0)
    hid = _silu(acc[:, :HEAD_DIM] + nxt)
    return _dot(hid.astype(BF16), w2_ref[...])


def _pos_term_kernel(pos_ref, w1_ref, o_ref):
    halves = []
    for half in range(2):
        acc = jnp.zeros((8, HEAD_DIM), F32)
        for s in range(CMP_STRIDE):
            row = half * CMP_STRIDE + s
            p = jnp.broadcast_to(pos_ref[0, row:row + 1, :], (8, HEAD_DIM)).astype(BF16)
            acc = acc + _dot(p, w1_ref[0, s][:, half * HEAD_DIM:(half + 1) * HEAD_DIM])
        halves.append(acc)
    o_ref[0] = jnp.concatenate(halves, axis=1)


def _pos_term(pos, w1cat):
    return pl.pallas_call(
        _pos_term_kernel,
        grid=(2,),
        in_specs=[pl.BlockSpec((1, CMP_BLOCK, HEAD_DIM), lambda i: (i, 0, 0)),
                  pl.BlockSpec((1, CMP_STRIDE, HEAD_DIM, 2 * HEAD_DIM), lambda i: (i, 0, 0, 0))],
        out_specs=pl.BlockSpec((1, 8, 2 * HEAD_DIM), lambda i: (i, 0, 0)),
        out_shape=jax.ShapeDtypeStruct((2, 8, 2 * HEAD_DIM), F32),
        compiler_params=_cparams(("arbitrary",)),
        name="pos_term",
    )(pos, w1cat)


def _flash_step(carry, s, v_t):
    outs = []
    w = s.shape[1] // FLASH_SPLIT
    v_ext = jnp.concatenate([v_t, jnp.ones((ONES_ROWS, v_t.shape[1]), BF16)], axis=0)
    for c in range(FLASH_SPLIT):
        m, acc = (x[:, c * w:(c + 1) * w] for x in carry)
        sc = s[:, c * w:(c + 1) * w]
        m_new = jnp.maximum(m, jnp.max(sc, axis=0, keepdims=True))
        acc = jnp.exp2(m - m_new) * acc + _dot(v_ext, jnp.exp2(sc - m_new).astype(BF16))
        outs.append((m_new, acc))
    return tuple(jnp.concatenate([o[i] for o in outs], axis=1) for i in range(2))


def _flash_init(cols):
    return (jnp.full((1, cols), M_INIT, F32), jnp.zeros((HEAD_DIM + ONES_ROWS, cols), F32))


def _flash_finish(carry):
    _, acc = carry
    return acc[:HEAD_DIM] / acc[HEAD_DIM:HEAD_DIM + 1]


def _masked_softmax(s, valid):
    s = jnp.where(valid, s, NEG)
    m = jnp.max(s, axis=-1, keepdims=True)
    e = jnp.where(valid, jnp.exp(s - m), 0.0)
    return e / jnp.maximum(jnp.sum(e, axis=-1, keepdims=True), 1e-30)


def _split_dot(x, w):
    hi = x.astype(BF16)
    lo = (x - hi.astype(F32)).astype(BF16)
    return _dot(hi, w) + _dot(lo, w)


def _top_n_mask(score, index, axis):
    sel = jnp.zeros(score.shape, F32)
    for _ in range(SEL_TOPN):
        mx = jnp.max(score, axis=axis, keepdims=True)
        first = jnp.min(jnp.where(score == mx, index, 1e9), axis=axis, keepdims=True)
        hit = index == first
        sel = jnp.where(hit, 1.0, sel)
        score = jnp.where(hit, -jnp.inf, score)
    return sel


def _top_n_mask_by_rank(score, n_cand):
    lane = lax.broadcasted_iota(jnp.int32, score.shape, 1)
    rank = jnp.zeros(score.shape, F32)
    for j in range(n_cand):
        col = score[:, j:j + 1]
        ahead = (col > score) | ((col == score) & (lane > j))
        rank = rank + ahead.astype(F32)
    return ((rank < SEL_TOPN) & (lane < n_cand)).astype(F32)


def _compress_prompt_kernel(rows_ref, pos_ref, w1_ref, w2_ref, o_ref):
    out = _compress(lambda s: rows_ref[pl.ds(s, N_CMP_PAD, stride=CMP_STRIDE), :], N_CMP_PAD,
                    pos_ref.at[0], w1_ref.at[0], w2_ref.at[0])
    real = lax.broadcasted_iota(jnp.int32, (N_CMP_PAD, HEAD_DIM), 0) < N_CMP_PAD - 1
    o_ref[0, 0, 0:SLAB_OFF, :] = jnp.zeros((SLAB_OFF, HEAD_DIM), F32)
    o_ref[0, 0, SLAB_OFF:SLAB_OFF + N_CMP_PAD, :] = jnp.where(real, out, 0.0)
    o_ref[0, 0, SLAB_OFF + N_CMP_PAD:KC_ROWS, :] = jnp.zeros((KC_ROWS - SLAB_OFF - N_CMP_PAD, HEAD_DIM), F32)


def _compress_prompt(z, pos, w1, w2):
    kv_blk = Z_KV // HEAD_DIM
    return pl.pallas_call(
        _compress_prompt_kernel,
        grid=(2, NSA_KV_HEADS),
        in_specs=[pl.BlockSpec((SEQ, HEAD_DIM), lambda i, h: (0, kv_blk + NSA_KV_HEADS * i + h)),
                  pl.BlockSpec((1, 8, 2 * HEAD_DIM), lambda i, h: (i, 0, 0)),
                  pl.BlockSpec((1, CMP_STRIDE, HEAD_DIM, 2 * HEAD_DIM), lambda i, h: (i, 0, 0, 0)),
                  pl.BlockSpec((1, HEAD_DIM, HEAD_DIM), lambda i, h: (i, 0, 0))],
        out_specs=pl.BlockSpec((1, 1, KC_ROWS, HEAD_DIM), lambda i, h: (i, h, 0, 0)),
        out_shape=jax.ShapeDtypeStruct((2, NSA_KV_HEADS, KC_ROWS, HEAD_DIM), F32),
        compiler_params=_cparams(("parallel", "parallel")),
        name="compress_prompt",
    )(z, pos, w1, w2)


def _nsa_prompt_kernel(q_ref, misc_ref, ksel_ref, vsel_rows_ref, kwin_ref, vwin_rows_ref, kc_ref, vc_ref,
                       bias_ref, cover_ref, erel_ref, o_ref, vsel_ref, vwin_ref):
    kvh = pl.program_id(0)
    t = pl.program_id(1)
    cols = NSA_GROUP * Q_TILE

    @pl.when(t == 0)
    def _():
        def body(kt, _):
            rows = pl.ds(pl.multiple_of(kt * Q_TILE, Q_TILE), Q_TILE)
            vsel_ref[:, rows] = vsel_rows_ref[rows, :].astype(F32).T.astype(BF16)
            vwin_ref[:, rows] = vwin_rows_ref[rows, :].astype(F32).T.astype(BF16)
            return 0
        lax.fori_loop(0, N_QT, body, 0)

    q = q_ref[...] * (HEAD_DIM ** -0.5 * LOG2E)
    qt =jnp.concatenate([q[:, g * HEAD_DIM:(g + 1) * HEAD_DIM].T for g in range(NSA_GROUP)], axis=1).astype(BF16)

    def bias_tile(k):
        return bias_ref[0, k * Q_TILE:(k + 1) * Q_TILE, :]
    b_diag, b_prev, b_first = bias_tile(0), bias_tile(1), bias_tile(2)

    def key_tile(ref, kt):
        return ref[pl.ds(pl.multiple_of(kt * Q_TILE, Q_TILE), Q_TILE), :]

    def value_tile(ref, kt):
        return ref[:, pl.ds(pl.multiple_of(kt * Q_TILE, Q_TILE), Q_TILE)]

    n_wt = WINDOW // Q_TILE + 1
    scores, values = [], []
    for w, b_tile in enumerate((b_first,) + (None,) * (n_wt - 3) + (b_prev, b_diag)):
        kt = t - (n_wt - 1) + w
        kc = jnp.maximum(kt, 0)
        s = _dot(key_tile(kwin_ref, kc), qt) + jnp.where(kt >= 0, 0.0, NEG)
        scores.append(s if b_tile is None else s + b_tile)
        values.append(value_tile(vwin_ref, kc))
    o_win = _flash_finish(_flash_step(_flash_init(cols), jnp.concatenate(scores, axis=0),
                                      jnp.concatenate(values, axis=1)))

    start = pl.multiple_of(t * (Q_TILE // CMP_STRIDE), 8)
    kslab = kc_ref[0, 0, pl.ds(start, SLAB), :].astype(BF16)
    vslab = vc_ref[0, 0, pl.ds(start, SLAB), :].astype(BF16)
    b_cmp = bias_ref[0, 3 * Q_TILE:3 * Q_TILE + SLAB, :]
    u = lax.broadcasted_iota(jnp.int32, (SLAB, 1), 0)
    valid = (b_cmp > M_INIT) & (u >= SLAB_OFF - (Q_TILE // CMP_STRIDE) * t)
    s = jnp.where(valid, _dot(kslab, qt) + b_cmp, NEG)
    e = jnp.where(valid, jnp.exp2(s - jnp.max(s, axis=0, keepdims=True)), 0.0)
    p_cmp = e / jnp.maximum(jnp.sum(e, axis=0, keepdims=True), 1e-30)
    o_cmp = _dot_tn(vslab, p_cmp.astype(BF16))

    p_sum = p_cmp[:, 0:Q_TILE]
    for g in range(1, NSA_GROUP):
        p_sum = p_sum + p_cmp[:, g * Q_TILE:(g + 1) * Q_TILE]
    hi = p_sum.astype(BF16)
    lo = (p_sum - hi.astype(F32)).astype(BF16)
    imp = _dot(cover_ref[...], hi) + _dot(cover_ref[...], lo)
    jr = lax.broadcasted_iota(jnp.int32, (N_SEL, Q_TILE), 0)
    qi = lax.broadcasted_iota(jnp.int32, (N_SEL, Q_TILE), 1)
    cur = REL0 + (qi >= SEL_BLOCK).astype(jnp.int32)
    first_blk = REL0 - 2 * t
    forced = (jr == first_blk) | (jr == cur) | (jr == cur - 1)
    in_range = (jr <= cur) & (jr >= first_blk)
    score = jnp.where(in_range, jnp.where(forced, FORCED_SCORE, imp), NEG)
    sel = _top_n_mask(score, jr.astype(F32), 0)

    unsel = ((sel - 1.0) * (-NEG)).astype(BF16)
    q_aug = jnp.concatenate([qt, jnp.concatenate([unsel] * NSA_GROUP, axis=1)], axis=0)

    def sweep_step(i, carry, masked, biases=()):
        scores, values = [], []
        for j in range(SWEEP_TILES):
            kt = t - SWEEP_TILES * i - (SWEEP_TILES - 1) + j
            kc = jnp.maximum(kt, 0) if masked else kt
            tile = jnp.concatenate([key_tile(ksel_ref, kc), erel_ref[t - kc]], axis=1)
            s = _dot(tile, q_aug)
            if masked:
                s = s + jnp.where(kt >= 0, 0.0, NEG)
            if j >= SWEEP_TILES - len(biases):
                s = s + biases[j - (SWEEP_TILES - len(biases))]
            scores.append(s)
            values.append(value_tile(vsel_ref, kc))
        per = SWEEP_TILES // SWEEP_PARTS
        for k in range(SWEEP_PARTS):
            carry = _flash_step(carry, jnp.concatenate(scores[k * per:(k + 1) * per], axis=0),
                                jnp.concatenate(values[k * per:(k + 1) * per], axis=1))
        return carry

    n_full = jnp.maximum(t - (SWEEP_TILES - 1), 0) // SWEEP_TILES
    carry = lax.fori_loop(1, n_full + 1, lambda i, c: sweep_step(i, c, False), _flash_init(cols))
    some_left = (t - SWEEP_TILES * (n_full + 1) >= 0).astype(jnp.int32)
    carry = lax.fori_loop(0, some_left, lambda _, c: sweep_step(n_full + 1, c, True), carry)
    o_sel = _flash_finish(sweep_step(0, carry, True, (b_prev, b_diag)))

    gates = jax.nn.sigmoid(misc_ref[...]).T
    n_g = 3 * NSA_GROUP
    gk = jnp.where(kvh == 0, gates, pltpu.roll(gates, gates.shape[0] - n_g, axis=0))[MISC_GATES:MISC_GATES + n_g]
    for g in range(NSA_GROUP):
        c = slice(g * Q_TILE, (g + 1) * Q_TILE)
        o_g = (gk[3 * g:3 * g + 1] * o_cmp[:, c] + gk[3 * g + 1:3 * g + 2] * o_sel[:, c]
               + gk[3 * g + 2:3 * g + 3] * o_win[:, c])
        o_ref[:, g * HEAD_DIM:(g + 1) * HEAD_DIM] = o_g.T


def _nsa_prompt(z, kvb, kcp, bias, cover, erel):
    gw = NSA_GROUP * HEAD_DIM

    def k_spec(j):
        return pl.BlockSpec((SEQ, HEAD_DIM), lambda h, t, j=j: (0, 2 * j + h))
    return pl.pallas_call(
        _nsa_prompt_kernel,
        grid=(NSA_KV_HEADS, N_QT),
        in_specs=[pl.BlockSpec((Q_TILE, gw), lambda h, t: (t, h)),
                  pl.BlockSpec((Q_TILE, 128), lambda h, t: (t, Z_MISC // 128)),
                  k_spec(2), k_spec(3), k_spec(4), k_spec(5),
                  pl.BlockSpec((1, 1, KC_ROWS, HEAD_DIM), lambda h, t: (0, h, 0, 0)),
                  pl.BlockSpec((1, 1, KC_ROWS, HEAD_DIM), lambda h, t: (1, h, 0, 0)),
                  pl.BlockSpec((1,) + bias.shape[1:], lambda h, t: (h, 0, 0)),
                  pl.BlockSpec(cover.shape, lambda h, t: (0, 0)),
                  pl.BlockSpec(erel.shape, lambda h, t: (0, 0, 0))],
        out_specs=pl.BlockSpec((Q_TILE, gw), lambda h, t: (t, h)),
        out_shape=jax.ShapeDtypeStruct((SEQ, NSA_HEADS * HEAD_DIM), F32),
        scratch_shapes=[pltpu.VMEM((HEAD_DIM, SEQ), BF16), pltpu.VMEM((HEAD_DIM, SEQ), BF16)],
        compiler_params=_cparams(("arbitrary", "arbitrary")),
        name="nsa_prompt",
    )(z, z, kvb, kvb, kvb, kvb, kcp, kcp, bias, cover, erel)


S_ROWS = NSA_HEADS * DEC_SEQ
PAGE_ROWS = NSA_KV_HEADS * PAGE_SIZE
NEW_ROWS = NSA_KV_HEADS * DEC_SEQ
CHUNK_ROWS = NSA_KV_HEADS * CMP_STRIDE
CHUNK_PITCH = CHUNK_ROWS + 8


def _nsa_sample_kernel(pt_ref, q_ref, gate_ref, new_ref, kwin_ref, vwin_ref, kcmp_hbm, vcmp_hbm, ksel_hbm, vsel_hbm,
                       posk_ref, w1k_ref, w2k_ref, posv_ref, w1v_ref, w2v_ref, bias_ref, cover_ref, expand_ref,
                       o_ref, kwin_o_ref, vwin_o_ref, kcmp_buf, vcmp_buf, ksel_buf, vsel_buf, sem):
    n_pg = N_PAGES
    b = pl.program_id(0)
    slot = b % 2
    page_chunks = PAGE_ROWS // CHUNK_ROWS
    hbm = (kcmp_hbm, vcmp_hbm, ksel_hbm, vsel_hbm)
    bufs = (kcmp_buf, vcmp_buf, ksel_buf, vsel_buf)

    def page_copy(k, p, page, into):
        if k < 2:
            src = hbm[k].at[pl.ds(page * page_chunks, page_chunks)]
            dst = bufs[k].at[pl.ds(into * S_CMP + p * page_chunks, page_chunks), pl.ds(0, CHUNK_ROWS), :]
        else:
            src = hbm[k].at[pl.ds(page * PAGE_ROWS, PAGE_ROWS)]
            dst = bufs[k].at[pl.ds(into * (n_pg * PAGE_ROWS) + p * PAGE_ROWS, PAGE_ROWS), :]
        return pltpu.make_async_copy(src, dst, sem.at[into, k])

    def gather(seq, into):
        for p in range(n_pg):
            page = pt_ref[seq, p]
            for k in range(4):
                page_copy(k, p, page, into).start()

    @pl.when(b == 0)
    def _():
        gather(0, 0)

    @pl.when(b + 1 < DEC_BATCH)
    def _():
        gather(b + 1, 1 - slot)

    for p in range(n_pg):
        for k in range(4):
            page_copy(k, p, 0, slot).wait()

    def pages(k, p):
        return bufs[k][pl.ds(pl.multiple_of(slot * (n_pg * PAGE_ROWS), PAGE_ROWS) + p * PAGE_ROWS, PAGE_ROWS), :]

    comp = []
    for k, (pos_ref, w1_ref, w2_ref) in enumerate(((posk_ref, w1k_ref, w2k_ref), (posv_ref, w1v_ref, w2v_ref))):
        flat = bufs[k].reshape(2 * S_CMP * CHUNK_PITCH, HEAD_DIM)

        def load_rows(s, flat=flat):
            return jnp.concatenate(
                [flat[pl.ds(slot * (S_CMP * CHUNK_PITCH) + NSA_KV_HEADS * s + h, S_CMP, stride=CHUNK_PITCH), :]
                 for h in range(NSA_KV_HEADS)], axis=0)
        comp.append(_compress(load_rows, S_CMP_COLS, pos_ref, w1_ref, w2_ref).astype(BF16))

    keep = NSA_KV_HEADS * WINDOW - NEW_ROWS
    kwin_o_ref[0:keep, :] = kwin_ref[NEW_ROWS:NSA_KV_HEADS * WINDOW, :]
    kwin_o_ref[keep:keep + NEW_ROWS, :] = new_ref[0, 4]
    vwin_o_ref[0:keep, :] = vwin_ref[NEW_ROWS:NSA_KV_HEADS * WINDOW, :]
    vwin_o_ref[keep:keep + NEW_ROWS, :] = new_ref[0, 5]

    qs = (q_ref[0] * (HEAD_DIM ** -0.5)).astype(BF16)
    o0, o1 = S_CMP_COLS, S_CMP_COLS + S_SEL_COLS
    b_cmp = bias_ref[:, 0:o0]
    b_new = bias_ref[:, o1 - S_NEW_COLS:o1]

    def attend(scores, bias, values):
        s = jnp.concatenate(scores, axis=1) + bias
        e = jnp.exp(s - jnp.max(s, axis=-1, keepdims=True))
        acc = jnp.zeros((S_ROWS, HEAD_DIM), F32)
        c0 = 0
        for v in values:
            acc = acc + _dot(e[:, c0:c0 + v.shape[0]].astype(BF16), v)
            c0 += v.shape[0]
        return acc / jnp.sum(e, axis=-1, keepdims=True)

    p_cmp = _masked_softmax(_dot_nt(qs, comp[0]) + b_cmp, b_cmp > M_INIT)
    o_cmp = _dot(p_cmp.astype(BF16), comp[1])

    imp = _split_dot(p_cmp, cover_ref[...])
    imp = imp + pltpu.roll(imp, 8, axis=0) + pltpu.roll(imp, 16, axis=0) + pltpu.roll(imp, 24, axis=0)
    blk = lax.broadcasted_iota(jnp.int32, (S_ROWS, 128), 1)
    cur = PAST_LEN // SEL_BLOCK
    forced = (blk == 0) | (blk == cur) | (blk == cur - 1)
    score = jnp.where(blk <= cur, jnp.where(forced, FORCED_SCORE, imp), NEG)
    sel = _top_n_mask_by_rank(score, cur + 1).astype(BF16)
    mask_add = (_dot(sel, expand_ref[...]) - 1.0) * (-NEG)

    def new_tile(j):
        pad = jnp.zeros((S_NEW_COLS - NEW_ROWS, HEAD_DIM), F32)
        return jnp.concatenate([new_ref[0, j], pad], axis=0).astype(BF16)

    nk, nv = new_tile(2), new_tile(3)
    scores = [_dot_nt(qs, pages(2, p).astype(BF16)) for p in range(n_pg)] + [_dot_nt(qs, nk)]
    values = [pages(3, p).astype(BF16) for p in range(n_pg)] + [nv]
    o_sel = attend(scores, bias_ref[:, o0:o1] + mask_add, values)

    nk, nv = new_tile(4), new_tile(5)
    scores = [_dot_nt(qs, kwin_ref[...].astype(BF16)), _dot_nt(qs, nk)]
    bias_w = jnp.concatenate([bias_ref[:, o1:o1 + S_WIN_COLS], b_new], axis=1)
    o_win = attend(scores, bias_w, [vwin_ref[...].astype(BF16), nv])

    g = jax.nn.sigmoid(gate_ref[0])
    o_ref[0] = g[:, 0:1] * o_cmp + g[:, 1:2] * o_sel + g[:, 2:3] * o_win


def _nsa_sample(page_table, q_s, gate_s, new_s, kwin, vwin, pools, cmp_w, bias, cover, expand):
    win_rows = NSA_KV_HEADS * WINDOW

    def full(a):
        return pl.BlockSpec(a.shape, lambda b, pt, n=a.ndim: (0,) * n)

    def per_b(a):
        return pl.BlockSpec((1,) + a.shape[1:], lambda b, pt, n=a.ndim: (b,) + (0,) * (n - 1))
    win_spec = pl.BlockSpec((win_rows, HEAD_DIM), lambda b, pt: (b, 0))
    consts = list(cmp_w) + [bias, cover, expand]
    grid_spec = pltpu.PrefetchScalarGridSpec(
        num_scalar_prefetch=1,
        grid=(DEC_BATCH,),
        in_specs=[per_b(q_s), per_b(gate_s), per_b(new_s), win_spec, win_spec]
        + [pl.BlockSpec(memory_space=pl.ANY)] * 4 + [full(a) for a in consts],
        out_specs=[per_b(q_s), win_spec, win_spec],
        scratch_shapes=[pltpu.VMEM((2 * S_CMP, CHUNK_PITCH, HEAD_DIM), F32)] * 2
        + [pltpu.VMEM((2 * N_PAGES * PAGE_ROWS, HEAD_DIM), F32)] * 2
        + [pltpu.SemaphoreType.DMA((2, 4))],
    )
    return pl.pallas_call(
        _nsa_sample_kernel,
        grid_spec=grid_spec,
        out_shape=[jax.ShapeDtypeStruct(q_s.shape, F32),
                   jax.ShapeDtypeStruct(kwin.shape, F32),
                   jax.ShapeDtypeStruct(vwin.shape, F32)],
        compiler_params=_cparams(("arbitrary",)),
        name="nsa_sample",
    )(page_table, q_s, gate_s, new_s, kwin, vwin, *pools, *consts)


def _log_decay(a_blk, wa_ref, ba_ref):
    x = _dot(a_blk.astype(BF16), wa_ref[...]) + ba_ref[...]
    return (jnp.minimum(x, 0.0) - jnp.log1p(jnp.exp(-jnp.abs(x)))) * (1.0 / GLA_TAU)


def _segment_cumsum(g, seg):
    pos = lax.broadcasted_iota(jnp.int32, g.shape, 0) % seg
    cum = g
    sh = 1
    while sh < seg:
        cum = cum + jnp.where(pos >= sh, pltpu.roll(cum, sh, axis=0), 0.0)
        sh *= 2
    return cum


def _gla_prompt_kernel(q_ref, k_ref, v_ref, r_ref, a_ref, wa_ref, ba_ref, gn_ref, o_ref, st_o_ref, st_ref,
                       *, n_blk, tb):
    tbi = pl.program_id(0)

    @pl.when(tbi == 0)
    def _():
        st_ref[...] = jnp.zeros_like(st_ref)

    c = GLA_CHUNK
    tril = lax.broadcasted_iota(jnp.int32, (c, c), 0) >= lax.broadcasted_iota(jnp.int32, (c, c), 1)
    cum = _segment_cumsum(_log_decay(a_ref[...], wa_ref, ba_ref), c)
    q = q_ref[...] * (GLA_DK ** -0.5)
    k = k_ref[...]
    v = v_ref[...].astype(BF16)
    qe = (q * jnp.exp(cum)).astype(BF16)
    kd = (k * jnp.exp(-cum)).astype(BF16)
    heads = [(slice(h * GLA_DK, (h + 1) * GLA_DK), slice(h * GLA_DV, (h + 1) * GLA_DV)) for h in range(GLA_HEADS)]
    sts = [st_ref[h] for h in range(GLA_HEADS)]
    outs = [[] for _ in range(GLA_HEADS)]
    for ci in range(tb // c):
        r = slice(ci * c, (ci + 1) * c)
        last = cum[ci * c + c - 1:ci * c + c, :]
        kl = (k[r] * jnp.exp(last - cum[r])).astype(BF16)
        decay = jnp.exp(last)
        for h, (dk, dv) in enumerate(heads):
            att = jnp.where(tril, _dot_nt(qe[r, dk], kd[r, dk]), 0.0)
            outs[h].append(_dot_nt(qe[r, dk], sts[h].astype(BF16)) + _dot(att.astype(BF16), v[r, dv]))
            sts[h] = decay[:, dk] * sts[h] + _dot_tn(v[r, dv], kl[:, dk])
    for h, (dk, dv) in enumerate(heads):
        st_ref[h] = sts[h]
        o = jnp.concatenate(outs[h], axis=0)
        o_ref[:, dv] = _rms(o, gn_ref[...]) * _silu(r_ref[:, dv])

    @pl.when(tbi == n_blk - 1)
    def _():
        st_o_ref[...] = st_ref[...]


def _gla_prompt(z, wa, ba, gn, tb=256):
    n_blk = SEQ // tb
    hk, hv = GLA_HEADS * GLA_DK, GLA_HEADS * GLA_DV
    st_shape = (GLA_HEADS, GLA_DV, GLA_DK)
    return pl.pallas_call(
        functools.partial(_gla_prompt_kernel, n_blk=n_blk, tb=tb),
        grid=(n_blk,),
        in_specs=[pl.BlockSpec((tb, hk), lambda i: (i, Z_QG // hk)),
                  pl.BlockSpec((tb, hk), lambda i: (i, Z_KG // hk)),
                  pl.BlockSpec((tb, hv), lambda i: (i, Z_VG // hv)),
                  pl.BlockSpec((tb, hv), lambda i: (i, Z_RG // hv)),
                  pl.BlockSpec((tb, 128), lambda i: (i, Z_MISC // 128)),
                  pl.BlockSpec((128, hk), lambda i: (0, 0)),
                  pl.BlockSpec((1, hk), lambda i: (0, 0)),
                  pl.BlockSpec((1, GLA_DV), lambda i: (0, 0))],
        out_specs=[pl.BlockSpec((tb, hv), lambda i: (i, 0)),
                   pl.BlockSpec(st_shape, lambda i: (0, 0, 0))],
        out_shape=[jax.ShapeDtypeStruct((SEQ, hv), F32), jax.ShapeDtypeStruct(st_shape, F32)],
        scratch_shapes=[pltpu.VMEM(st_shape, F32)],
        compiler_params=_cparams(("arbitrary",)),
        name="gla_prompt",
    )(z, z, z, z, z, wa, ba, gn)


GS_B = 4


def _gla_sample_kernel(q_ref, k_ref, v_ref, r_ref, a_ref, wa_ref, ba_ref, gn_ref, s_ref, o_ref, s_o_ref):
    rows = GS_B * DEC_SEQ
    a = a_ref[...]
    ri = lax.broadcasted_iota(jnp.int32, (rows, rows), 0)
    ci = lax.broadcasted_iota(jnp.int32, (rows, rows), 1)
    same_causal = (ri // DEC_SEQ == ci // DEC_SEQ) & (ri >= ci)
    row_b = lax.broadcasted_iota(jnp.int32, (rows, 1), 0) // DEC_SEQ
    ones = jnp.ones((rows, 128), BF16)
    for h in range(GLA_HEADS):
        dk = slice(h * GLA_DK, (h + 1) * GLA_DK)
        dv = slice(h * GLA_DV, (h + 1) * GLA_DV)
        g = _log_decay(a, wa_ref.at[:, dk], ba_ref.at[:, dk])
        cum = _segment_cumsum(g, DEC_SEQ)
        q = q_ref[:, dk] * (GLA_DK ** -0.5)
        k = k_ref[:, dk]
        v = v_ref[:, dv].astype(BF16)
        qe = (q * jnp.exp(cum)).astype(BF16)
        kd = (k * jnp.exp(-cum)).astype(BF16)
        att = jnp.where(same_causal, _dot_nt(qe, kd), 0.0)
        o = _dot(att.astype(BF16), v)
        for b in range(GS_B):
            mine = row_b == b
            last = cum[b * DEC_SEQ + DEC_SEQ - 1:(b + 1) * DEC_SEQ, :]
            s = s_ref[b, h]
            o = o + jnp.where(mine, _dot(qe, s.astype(BF16)), 0.0)
            kl = jnp.where(mine, k * jnp.exp(last - cum), 0.0)
            hi = jnp.where(mine, g, 0.0).astype(BF16)
            lo = (jnp.where(mine, g, 0.0) - hi.astype(F32)).astype(BF16)
            last_col = (_dot_tn(hi, ones) + _dot_tn(lo, ones))[:, 0:1]
            s_o_ref[b, h] = jnp.exp(last_col) * s + _dot_tn(kl.astype(BF16), v)
        o_ref[:, dv] = _rms(o, gn_ref[...]) * _silu(r_ref[:, dv])


def _gla_sample(zs, state, wa, ba, gn):
    rows = GS_B * DEC_SEQ
    n = DEC_BATCH * DEC_SEQ
    hk, hv = GLA_HEADS * GLA_DK, GLA_HEADS * GLA_DV
    st_spec = pl.BlockSpec((GS_B, GLA_HEADS, GLA_DK, GLA_DV), lambda i: (i, 0, 0, 0))
    return pl.pallas_call(
        _gla_sample_kernel,
        grid=(DEC_BATCH // GS_B,),
        in_specs=[pl.BlockSpec((rows, hk), lambda i: (i, Z_QG // hk)),
                  pl.BlockSpec((rows, hk), lambda i: (i, Z_KG // hk)),
                  pl.BlockSpec((rows, hv), lambda i: (i, Z_VG // hv)),
                  pl.BlockSpec((rows, hv), lambda i: (i, Z_RG // hv)),
                  pl.BlockSpec((rows, 128), lambda i: (i, Z_MISC // 128)),
                  pl.BlockSpec((128, hk), lambda i: (0, 0)),
                  pl.BlockSpec((1, hk), lambda i: (0, 0)),
                  pl.BlockSpec((1, GLA_DV), lambda i: (0, 0)),
                  st_spec],
        out_specs=[pl.BlockSpec((rows, hv), lambda i: (i, 0)), st_spec],
        out_shape=[jax.ShapeDtypeStruct((n, hv), F32), jax.ShapeDtypeStruct(state.shape, F32)],
        compiler_params=_cparams(("parallel",)),
        name="gla_sample",
    )(zs, zs, zs, zs, zs, wa, ba, gn, state)


def _softmax_rows(s):
    m = jnp.max(s, axis=-1, keepdims=True)
    e = jnp.exp(s - m)
    return e / jnp.sum(e, axis=-1, keepdims=True)


def _mem_prompt_kernel(q_ref, k_ref, v_ref, o_ref):
    for h in range(MEM_HEADS):
        d = slice(h * MEM_HEAD_DIM, (h + 1) * MEM_HEAD_DIM)
        q = (q_ref[:, d] * (MEM_HEAD_DIM ** -0.5)).astype(BF16)
        p = _softmax_rows(_dot_nt(q, k_ref[:, d].astype(BF16)))
        o_ref[:, d] = _dot(p.astype(BF16), v_ref[:, d].astype(BF16))


def _mem_prompt(qm, memkv, tq=256):
    w = MEM_HEADS * MEM_HEAD_DIM
    return pl.pallas_call(
        _mem_prompt_kernel,
        grid=(SEQ // tq,),
        in_specs=[pl.BlockSpec((tq, w), lambda i: (i, 0)),
                  pl.BlockSpec((MEM_TOKENS, w), lambda i: (0, 0)),
                  pl.BlockSpec((MEM_TOKENS, w), lambda i: (0, 1))],
        out_specs=pl.BlockSpec((tq, w), lambda i: (i, 0)),
        out_shape=jax.ShapeDtypeStruct((SEQ, w), F32),
        compiler_params=_cparams(("parallel",)),
        name="mem_prompt",
    )(qm, memkv, memkv)


def _mem_sample_kernel(q_ref, k_ref, v_ref, o_ref):
    rows = MEM_HEADS * DEC_SEQ
    cols = MEM_HEADS * MEM_TOKENS
    row_h = lax.broadcasted_iota(jnp.int32, (rows, cols), 0) // DEC_SEQ
    col_h = lax.broadcasted_iota(jnp.int32, (rows, cols), 1) % MEM_HEADS
    for b in range(MS_B):
        kv_rows = slice(b * cols, (b + 1) * cols)
        q = (q_ref[b] * (MEM_HEAD_DIM ** -0.5)).astype(BF16)
        s = jnp.where(row_h == col_h, _dot_nt(q, k_ref[kv_rows, :].astype(BF16)), NEG)
        o_ref[b] = _dot(_softmax_rows(s).astype(BF16), v_ref[kv_rows, :].astype(BF16))


MS_B = 4


def _mem_sample(q_s, k_mem, v_mem):
    rows = MEM_HEADS * DEC_SEQ
    kv_spec = pl.BlockSpec((MS_B * MEM_HEADS * MEM_TOKENS, MEM_HEAD_DIM), lambda b: (b, 0))
    return pl.pallas_call(
        _mem_sample_kernel,
        grid=(DEC_BATCH // MS_B,),
        in_specs=[pl.BlockSpec((MS_B, rows, MEM_HEAD_DIM), lambda b: (b, 0, 0)), kv_spec, kv_spec],
        out_specs=pl.BlockSpec((MS_B, rows, MEM_HEAD_DIM), lambda b: (b, 0, 0)),
        out_shape=jax.ShapeDtypeStruct((DEC_BATCH, rows, MEM_HEAD_DIM), F32),
        compiler_params=_cparams(("parallel",)),
        name="mem_sample",
    )(q_s, k_mem, v_mem)


W_IN_SEGMENTS = ((0, 1024, Z_QN), (1024, 1536, Z_KV), (2560, 24, Z_MISC + MISC_GATES), (2584, 512, Z_QG),
                 (3096, 512, Z_KG), (3608, 1024, Z_VG), (4632, 1024, Z_RG), (5656, 16, Z_MISC + MISC_A))
W_IN_COLS = 5672


def _permute_w_in_kernel(w_ref, o_ref):
    o_ref[:, Z_MISC:Z_W] = jnp.zeros((o_ref.shape[0], Z_W - Z_MISC), BF16)
    for src, width, dst in W_IN_SEGMENTS:
        o_ref[:, dst:dst + width] = w_ref[:, src:src + width].astype(BF16)


def _permute_w_in(w_in, tr=256):
    d = w_in.shape[0]
    return pl.pallas_call(
        _permute_w_in_kernel,
        grid=(d // tr,),
        in_specs=[pl.BlockSpec((tr, W_IN_COLS), lambda i: (i, 0))],
        out_specs=pl.BlockSpec((tr, Z_W), lambda i: (i, 0)),
        out_shape=jax.ShapeDtypeStruct((d, Z_W), BF16),
        compiler_params=_cparams(("parallel",)),
        name="permute_w_in",
    )(w_in)


def kernel(x_prompt, x_sample, mem_prompt, cache_k_cmp, cache_v_cmp, cache_k_sel, cache_v_sel, cache_k_win,
           cache_v_win, state_gla, cache_k_mem, cache_v_mem, page_table, norm_ffn1, ffn1_w_gate, ffn1_w_up,
           ffn1_w_down, norm_mix, w_in, w_out, cmp_pos_k, cmp_w1_k, cmp_w2_k, cmp_pos_v, cmp_w1_v, cmp_w2_v,
           rel_bias, gla_w_a2, gla_b_a, gla_norm, norm_mem, norm_mem_src, w_mem_q, w_mem_k, w_mem_v, w_mem_o,
           norm_ffn2, ffn2_w_gate, ffn2_w_up, ffn2_w_down, norm_final):
    bf = lambda a: a.astype(BF16)
    row = lambda a: a.reshape(1, -1)
    nb, ns = DEC_BATCH, DEC_SEQ
    kvw = NSA_KV_HEADS * HEAD_DIM

    h1, h1n = _ffn([x_prompt[0], x_sample.reshape(nb * ns, D_MODEL)], row(norm_ffn1[0]), bf(ffn1_w_gate[0]),
                   bf(ffn1_w_up[0]), bf(ffn1_w_down[0]), row(norm_mix[0]), False)
    z = _matmul(h1n, _permute_w_in(w_in[0]), 512, Z_W // 3, "proj_in")

    kv_p = z[:SEQ, Z_KV:Z_QG]
    kv_s = z[SEQ:, Z_KV:Z_QG].reshape(nb, ns, 6 * kvw)
    rows_p = [kv_p[:, j * kvw:(j + 1) * kvw].reshape(1, 1, SEQ, NSA_KV_HEADS, HEAD_DIM) for j in range(6)]
    rows_s = [kv_s[:, :, j * kvw:(j + 1) * kvw].reshape(1, nb, ns, NSA_KV_HEADS, HEAD_DIM) for j in range(4)]

    tab_p = _bias_tables(rel_bias, _prompt_bucket_table(), "bias_prompt")
    far = rel_bias[NUM_BUCKETS - 1][:, None, None]
    near = tab_p[:, :3 * Q_TILE]
    tab_p = jnp.concatenate([jnp.where(near > M_INIT, near - far, NEG), tab_p[:, 3 * Q_TILE:]], axis=1)
    tab_p = jnp.where(tab_p > M_INIT, tab_p * LOG2E, NEG)
    tab_p = tab_p.reshape(NSA_KV_HEADS, NSA_GROUP, -1, Q_TILE).transpose(0, 2, 1, 3)
    tab_p = tab_p.reshape(NSA_KV_HEADS, -1, NSA_GROUP * Q_TILE)
    tab_s = _bias_tables(rel_bias, _sample_bucket_table(), "bias_sample")
    tab_s = tab_s.reshape(NSA_KV_HEADS, NSA_GROUP, NSA_KV_HEADS, ns, -1)
    tab_s = jnp.stack([tab_s[h, :, h] for h in range(NSA_KV_HEADS)], axis=1).reshape(S_ROWS, -1)

    cmp_w1 = bf(jnp.stack([cmp_w1_k[0], cmp_w1_v[0]]))
    cmp_w1 = jnp.concatenate([cmp_w1[:, :CMP_STRIDE], cmp_w1[:, CMP_STRIDE:]], axis=-1)
    cmp_w2 = bf(jnp.stack([cmp_w2_k[0], cmp_w2_v[0]]))
    cmp_pos = _pos_term(jnp.stack([cmp_pos_k[0], cmp_pos_v[0]]), cmp_w1)
    kcp = _compress_prompt(z, cmp_pos, cmp_w1, cmp_w2)
    cover_p = jnp.asarray(_cover_np(SLAB, N_SEL, lambda u, j: u - 4 * j - SLAB_OFF + 4 * REL0).T, BF16)
    erel = (np.arange(N_SEL)[None, None, :] == REL0 - 2 * np.arange(N_QT)[:, None, None]
            + (np.arange(Q_TILE)[None, :, None] >= SEL_BLOCK))
    o_nsa_p = _nsa_prompt(z, bf(kv_p), kcp, tab_p, cover_p, jnp.asarray(erel, BF16))

    zs = z[SEQ:]
    q_s = zs[:, Z_QN:Z_QN + NSA_HEADS * HEAD_DIM].reshape(nb, ns, NSA_KV_HEADS, NSA_GROUP, HEAD_DIM)
    q_s = q_s.transpose(0, 3, 2, 1, 4).reshape(nb, S_ROWS, HEAD_DIM)
    gate_s = zs[:, Z_MISC + MISC_GATES:Z_MISC + MISC_GATES + 3 * NSA_HEADS]
    gate_s = gate_s.reshape(nb, ns, NSA_KV_HEADS, NSA_GROUP, 3).transpose(0, 3, 2, 1, 4).reshape(nb, S_ROWS, 3)
    new_s = kv_s.reshape(nb, ns, 6, NSA_KV_HEADS, HEAD_DIM).transpose(0, 2, 1, 3, 4).reshape(nb, 6, NEW_ROWS, HEAD_DIM)
    as_rows = lambda c: c.reshape(-1, HEAD_DIM)
    pools = ([c.reshape(-1, CHUNK_ROWS, HEAD_DIM) for c in (cache_k_cmp, cache_v_cmp)]
             + [as_rows(c) for c in (cache_k_sel, cache_v_sel)])
    cover_s = _cover_np(S_CMP, 128, lambda c, j: c - 4 * j)
    cover_s = jnp.asarray(np.concatenate([cover_s] * NSA_KV_HEADS, axis=0), BF16)
    expand_s = jnp.asarray(np.arange(128)[:, None]
                           == (np.arange(S_SEL_COLS)[None, :] // (NSA_KV_HEADS * SEL_BLOCK)), BF16)
    o_nsa_s, kwin_s, vwin_s = _nsa_sample(
        page_table, q_s, gate_s, new_s, as_rows(cache_k_win), as_rows(cache_v_win), pools,
        (cmp_pos[0], cmp_w1[0], cmp_w2[0], cmp_pos[1], cmp_w1[1], cmp_w2[1]),
        tab_s, cover_s, expand_s)
    o_nsa_s = o_nsa_s.reshape(nb, NSA_GROUP, NSA_KV_HEADS, ns, HEAD_DIM).transpose(0, 3, 2, 1, 4)
    o_nsa_s = o_nsa_s.reshape(nb * ns, NSA_HEADS * HEAD_DIM)

    wa = bf(jnp.zeros((128, GLA_HEADS * GLA_DK), F32).at[MISC_A:MISC_A + GLA_RANK].set(gla_w_a2[0]))
    ba, gn = row(gla_b_a[0]), row(gla_norm[0])
    o_gla_p, st_p = _gla_prompt(z, wa, ba, gn)
    o_gla_s, st_s = _gla_sample(zs, state_gla[0], wa, ba, gn)

    half = NSA_HEADS * HEAD_DIM
    h2 = _matmul_res(h1, [(o_nsa_p, o_nsa_s), (o_gla_p, o_gla_s)], [bf(w_out[0][:half]), bf(w_out[0][half:])],
                     512, "proj_out")

    memkv = _norm_matmul(mem_prompt[0], row(norm_mem_src[0]),
                         bf(jnp.concatenate([w_mem_k[0], w_mem_v[0]], axis=1)), MEM_TOKENS, 512, "mem_kv")
    mw = MEM_HEADS * MEM_HEAD_DIM
    qm = _norm_matmul(h2, row(norm_mem[0]), bf(w_mem_q[0]), 512, mw, "mem_q")
    om_p = _mem_prompt(qm, memkv)
    qm_s = qm[SEQ:].reshape(nb, ns, MEM_HEADS, MEM_HEAD_DIM).transpose(0, 2, 1, 3)
    om_s = _mem_sample(qm_s.reshape(nb, MEM_HEADS * ns, MEM_HEAD_DIM),
                       as_rows(cache_k_mem), as_rows(cache_v_mem))
    om_s = om_s.reshape(nb, MEM_HEADS, ns, MEM_HEAD_DIM).transpose(0, 2, 1, 3).reshape(nb * ns, mw)
    h3 = _matmul_res(h2, [(om_p, om_s)], [bf(w_mem_o[0])], 512, "mem_out")

    y_p, y_s = _ffn([h3], row(norm_ffn2[0]), bf(ffn2_w_gate[0]), bf(ffn2_w_up[0]), bf(ffn2_w_down[0]),
                    row(norm_final), True)

    mem_shape = (1, 1, MEM_TOKENS, MEM_HEADS, MEM_HEAD_DIM)
    win_shape = (1, nb, WINDOW, NSA_KV_HEADS, HEAD_DIM)
    return (y_p.reshape(1, SEQ, D_MODEL), y_s.reshape(nb, ns, D_MODEL),
            rows_p[0], rows_p[1], rows_p[2], rows_p[3],
            rows_p[4][:, :, SEQ - WINDOW:], rows_p[5][:, :, SEQ - WINDOW:],
            st_p.transpose(0, 2, 1).reshape(1, 1, GLA_HEADS, GLA_DK, GLA_DV),
            memkv[:, :mw].reshape(mem_shape), memkv[:, mw:].reshape(mem_shape),
            rows_s[0], rows_s[1], rows_s[2], rows_s[3],
            kwin_s.reshape(win_shape), vwin_s.reshape(win_shape),
            st_s.reshape(1, nb, GLA_HEADS, GLA_DK, GLA_DV))
```

```python
import functools
import math

import numpy as np
import jax
import jax.numpy as jnp
from jax import lax
from jax.experimental import pallas as pl
from jax.experimental.pallas import tpu as pltpu

F32 = jnp.float32
BF16 = jnp.bfloat16

D_MODEL = 2048
SEQ = 8192
DEC_BATCH = 128
DEC_SEQ = 4
PAST_LEN = 2048
PAGE_SIZE = 128
N_PAGES = PAST_LEN // PAGE_SIZE
HEAD_DIM = 128
NSA_HEADS = 8
NSA_KV_HEADS = 2
NSA_GROUP = 4
CMP_BLOCK = 32
CMP_STRIDE = 16
SEL_BLOCK = 64
SEL_TOPN = 16
WINDOW = 512
FORCED_SCORE = 1.0e4
GLA_HEADS = 4
GLA_DV = 256
GLA_DK = 128
GLA_RANK = 16
GLA_TAU = 16.0
GLA_CHUNK = 32
MEM_TOKENS = 256
MEM_HEADS = 4
MEM_HEAD_DIM = 128
D_FF = 5632
NUM_BUCKETS = 32
MAX_DISTANCE = 128
RMS_EPS = 1e-6

N_TOK = SEQ + DEC_BATCH * DEC_SEQ
Z_QN, Z_VG, Z_RG, Z_KV, Z_QG, Z_KG, Z_MISC = 0, 1024, 2048, 3072, 4608, 5120, 5632
Z_W = 5760
MISC_GATES, MISC_A = 0, 24

NEG = -1e30
LOG2E = math.log2(math.e)
ONES_ROWS = 16
M_INIT = -1e29

Q_TILE = 128
N_QT = SEQ // Q_TILE
N_CMP_PAD = SEQ // CMP_STRIDE
SLAB = N_CMP_PAD + 128
SLAB_OFF = SLAB - 16
KC_ROWS = SLAB_OFF + N_CMP_PAD + 16
N_SEL = SEQ // SEL_BLOCK
REL0 = N_SEL - 2
CMP_GROUP = 4
SWEEP_TILES = 8
SWEEP_PARTS = 4
WIN_PART_TILES = 2
FLASH_SPLIT = 1

VMEM_LIMIT = 56 * 1024 * 1024


def _cparams(sem):
    return pltpu.CompilerParams(dimension_semantics=sem, vmem_limit_bytes=VMEM_LIMIT)


def _dot(a, b):
    return jnp.dot(a, b, preferred_element_type=F32)


def _dot_nt(a, b):
    return lax.dot_general(a, b, (((1,), (1,)), ((), ())), preferred_element_type=F32)


def _dot_tn(a, b):
    return lax.dot_general(a, b, (((0,), (0,)), ((), ())), preferred_element_type=F32)


def _rms(x, g):
    return x * lax.rsqrt(jnp.mean(x * x, axis=-1, keepdims=True) + RMS_EPS) * g


def _silu(x):
    return x * jax.nn.sigmoid(x)


def _ffn_kernel(*refs, n_ff, n_first, two_in, two_out):
    refs = list(refs)
    x_refs = [refs.pop(0) for _ in range(2 if two_in else 1)]
    g_ref, wg_ref, wu_ref, wd_ref, gf_ref = refs[:5]
    o_refs = refs[5:7]
    xn_ref, acc_ref = refs[-2:]
    i = pl.program_id(0)
    j = pl.program_id(1)

    def x_tile():
        return jnp.where(i < n_first, x_refs[0][...], x_refs[1][...]) if two_in else x_refs[0][...]

    @pl.when(j == 0)
    def _():
        xn_ref[...] = _rms(x_tile(), g_ref[...]).astype(BF16)
        acc_ref[...] = jnp.zeros_like(acc_ref)

    xn = xn_ref[...]
    hid = _silu(_dot(xn, wg_ref[...])) * _dot(xn, wu_ref[...])
    acc_ref[...] += _dot(hid.astype(BF16), wd_ref[...])

    @pl.when(j == n_ff - 1)
    def _():
        h = x_tile() + 0.5 * acc_ref[...]
        if two_out:
            h = _rms(h, gf_ref[...])

            @pl.when(i < n_first)
            def _():
                o_refs[0][...] = h

            @pl.when(i >= n_first)
            def _():
                o_refs[1][...] = h
        else:
            o_refs[0][...] = h
            o_refs[1][...] = _rms(h, gf_ref[...]).astype(BF16)


FFN_TM = 512
FFN_TF = 512


def _ffn(xs, g, wg, wu, wd, gf, split_out):
    tm, tf, d = FFN_TM, FFN_TF, D_MODEL
    n_ff = D_FF // tf
    n_first = SEQ // tm
    two_in = len(xs) == 2

    def first(i, j):
        return (jnp.minimum(i, n_first - 1), 0)

    def second(i, j):
        return (jnp.maximum(i - n_first, 0), 0)
    whole = pl.BlockSpec((tm, d), lambda i, j: (i, 0))
    pair = [pl.BlockSpec((tm, d), first), pl.BlockSpec((tm, d), second)]
    n_s = DEC_BATCH * DEC_SEQ
    return pl.pallas_call(
        functools.partial(_ffn_kernel, n_ff=n_ff, n_first=n_first, two_in=two_in, two_out=split_out),
        grid=(N_TOK // tm, n_ff),
        in_specs=(pair if two_in else [whole])
        + [pl.BlockSpec((1, d), lambda i, j: (0, 0)),
           pl.BlockSpec((d, tf), lambda i, j: (0, j)),
           pl.BlockSpec((d, tf), lambda i, j: (0, j)),
           pl.BlockSpec((tf, d), lambda i, j: (j, 0)),
           pl.BlockSpec((1, d), lambda i, j: (0, 0))],
        out_specs=pair if split_out else [whole, whole],
        out_shape=([jax.ShapeDtypeStruct((SEQ, d), F32), jax.ShapeDtypeStruct((n_s, d), F32)] if split_out
                   else [jax.ShapeDtypeStruct((N_TOK, d), F32), jax.ShapeDtypeStruct((N_TOK, d), BF16)]),
        scratch_shapes=[pltpu.VMEM((tm, d), BF16), pltpu.VMEM((tm, d), F32)],
        compiler_params=_cparams(("arbitrary", "arbitrary")),
        name="ffn",
    )(*xs, g, wg, wu, wd, gf)


def _norm_matmul_kernel(x_ref, g_ref, w_ref, o_ref, xn_ref):
    @pl.when(pl.program_id(1) == 0)
    def _():
        xn_ref[...] = _rms(x_ref[...], g_ref[...]).astype(BF16)

    o_ref[...] = _dot(xn_ref[...], w_ref[...])


def _norm_matmul(x, g, w, tm, tn, name):
    n, d = x.shape
    dout = w.shape[1]
    return pl.pallas_call(
        _norm_matmul_kernel,
        grid=(n // tm, dout // tn),
        in_specs=[pl.BlockSpec((tm, d), lambda i, j: (i, 0)),
                  pl.BlockSpec((1, d), lambda i, j: (0, 0)),
                  pl.BlockSpec((d, tn), lambda i, j: (0, j))],
        out_specs=pl.BlockSpec((tm, tn), lambda i, j: (i, j)),
        out_shape=jax.ShapeDtypeStruct((n, dout), F32),
        scratch_shapes=[pltpu.VMEM((tm, d), BF16)],
        compiler_params=_cparams(("parallel", "arbitrary")),
        name=name,
    )(x, g, w)


def _matmul_kernel(x_ref, w_ref, o_ref):
    o_ref[...] = _dot(x_ref[...], w_ref[...])


def _matmul(x, w, tm, tn, name):
    n, d = x.shape
    dout = w.shape[1]
    return pl.pallas_call(
        _matmul_kernel,
        grid=(dout // tn, n // tm),
        in_specs=[pl.BlockSpec((tm, d), lambda j, i: (i, 0)),
                  pl.BlockSpec((d, tn), lambda j, i: (0, j))],
        out_specs=pl.BlockSpec((tm, tn), lambda j, i: (i, j)),
        out_shape=jax.ShapeDtypeStruct((n, dout), F32),
        compiler_params=_cparams(("arbitrary", "arbitrary")),
        name=name,
    )(x, w)


def _matmul_res_kernel(*refs, n_lhs, n_first):
    res_ref = refs[0]
    o_ref = refs[1 + 3 * n_lhs]
    first = pl.program_id(0) < n_first
    acc = res_ref[...]
    for k in range(n_lhs):
        lhs = jnp.where(first, refs[1 + 2 * k][...], refs[2 + 2 * k][...])
        acc = acc + _dot(lhs.astype(BF16), refs[1 + 2 * n_lhs + k][...])
    o_ref[...] = acc


def _matmul_res(res, lhs, ws, tm, name):
    n, d = res.shape
    n_first = SEQ // tm

    def pair_specs(width):
        return [pl.BlockSpec((tm, width), lambda i: (jnp.minimum(i, n_first - 1), 0)),
                pl.BlockSpec((tm, width), lambda i: (jnp.maximum(i - n_first, 0), 0))]
    return pl.pallas_call(
        functools.partial(_matmul_res_kernel, n_lhs=len(lhs), n_first=n_first),
        grid=(n // tm,),
        in_specs=([pl.BlockSpec((tm, d), lambda i: (i, 0))]
                  + [spec for a, _ in lhs for spec in pair_specs(a.shape[1])]
                  + [pl.BlockSpec(w.shape, lambda i: (0, 0)) for w in ws]),
        out_specs=pl.BlockSpec((tm, d), lambda i: (i, 0)),
        out_shape=jax.ShapeDtypeStruct((n, d), F32),
        compiler_params=_cparams(("arbitrary",)),
        name=name,
    )(res, *[a for pair in lhs for a in pair], *ws)


def _rel_bucket_np(dist):
    n = np.maximum(dist, 0)
    exact = NUM_BUCKETS // 2
    nf = np.maximum(n, 1).astype(np.float32)
    large = exact + (np.log(nf / np.float32(exact)) / np.float32(math.log(MAX_DISTANCE / exact))
                     * np.float32(NUM_BUCKETS - exact)).astype(np.int32)
    return np.where(n < exact, n, np.minimum(large, NUM_BUCKETS - 1)).astype(np.int32)


def _bucket_or_masked(dist, valid):
    return np.where(valid, _rel_bucket_np(dist), -1).astype(np.int32)


def _prompt_bucket_table():
    i = np.arange(Q_TILE)[:, None]
    j = np.arange(Q_TILE)[None, :]
    u = np.arange(SLAB)[None, :]
    dist_c = i - CMP_STRIDE * u + (CMP_STRIDE * SLAB_OFF - (CMP_BLOCK - 1))
    diag = _bucket_or_masked(i - j, i - j >= 0)
    prev = _bucket_or_masked(Q_TILE + i - j, np.ones((Q_TILE, Q_TILE), bool))
    first = _bucket_or_masked(WINDOW + i - j, j > i)
    cmp_ = _bucket_or_masked(dist_c, dist_c >= 0)
    return np.concatenate([diag, prev, first, cmp_], axis=1).T


S_CMP = PAST_LEN // CMP_STRIDE
S_CMP_COLS = NSA_KV_HEADS * S_CMP
S_NEW_COLS = 128
S_SEL_COLS = NSA_KV_HEADS * PAST_LEN + S_NEW_COLS
S_WIN_COLS = NSA_KV_HEADS * WINDOW


def _sample_bucket_table():
    i = np.arange(DEC_SEQ)[:, None]
    pos = PAST_LEN + i
    rows = []
    for h in range(NSA_KV_HEADS):
        col = np.arange(S_CMP_COLS)[None, :]
        c = col % S_CMP
        dist_c = pos - (c * CMP_STRIDE + CMP_BLOCK - 1)
        cmp_ = _bucket_or_masked(dist_c, (col // S_CMP == h) & (c < S_CMP - 1) & (dist_c >= 0))
        col = np.arange(S_SEL_COLS)[None, :]
        key = col // NSA_KV_HEADS
        sel = _bucket_or_masked(pos - key, (col % NSA_KV_HEADS == h) & (key <= pos))
        col = np.arange(S_WIN_COLS)[None, :]
        dist_w = pos - (PAST_LEN - WINDOW + col // NSA_KV_HEADS)
        win = _bucket_or_masked(dist_w, (col % NSA_KV_HEADS == h) & (dist_w < WINDOW))
        rows.append(np.concatenate([cmp_, sel, win], axis=1))
    return np.concatenate(rows, axis=0)


def _bias_table_kernel(tab_ref, idx_ref, o_ref):
    h = pl.program_id(0)
    idx = idx_ref[...]
    out = jnp.full(idx.shape, NEG, F32)
    for b in range(NUM_BUCKETS):
        out = jnp.where(idx == b, tab_ref[b, h], out)
    o_ref[0] = out


def _bias_tables(rel_bias, idx, name):
    r, c = idx.shape
    return pl.pallas_call(
        _bias_table_kernel,
        grid=(NSA_HEADS,),
        in_specs=[pl.BlockSpec(memory_space=pltpu.SMEM),
                  pl.BlockSpec((r, c), lambda h: (0, 0))],
        out_specs=pl.BlockSpec((1, r, c), lambda h: (h, 0, 0)),
        out_shape=jax.ShapeDtypeStruct((NSA_HEADS, r, c), F32),
        compiler_params=_cparams(("arbitrary",)),
        name=name,
    )(rel_bias, jnp.asarray(idx))


def _cover_np(n_cmp_cols, n_blk_cols, delta_of):
    u = np.arange(n_cmp_cols)[:, None]
    j = np.arange(n_blk_cols)[None, :]
    delta = delta_of(u, j)
    shared = np.minimum(CMP_STRIDE * delta + CMP_BLOCK, SEL_BLOCK) - np.maximum(CMP_STRIDE * delta, 0)
    return (np.maximum(shared, 0) / CMP_STRIDE).astype(np.float32)


def _compress(load_rows, n_chunk, pos_term_ref, w1_ref, w2_ref):
    parts = []
    for s0 in range(0, CMP_STRIDE, CMP_GROUP):
        lhs = jnp.concatenate([load_rows(s0 + k).astype(BF16) for k in range(CMP_GROUP)], axis=1)
        w = w1_ref[s0:s0 + CMP_GROUP].reshape(CMP_GROUP * HEAD_DIM, 2 * HEAD_DIM)
        parts.append(_dot(lhs, w))
    while len(parts) > 1:
        parts = [a + b for a, b in zip(parts[0::2], parts[1::2])]
    acc = parts[0] + pos_term_ref[0:1, :]
    nxt = pltpu.roll(acc[:, HEAD_DIM:], n_chunk - 1, axis=0)
    hid = _silu(acc[:, :HEAD_DIM] + nxt)
    return _dot(hid.astype(BF16), w2_ref[...])


def _pos_term_kernel(pos_ref, w1_ref, o_ref):
    halves = []
    for half in range(2):
        acc = jnp.zeros((8, HEAD_DIM), F32)
        for s in range(CMP_STRIDE):
            row = half * CMP_STRIDE + s
            p = jnp.broadcast_to(pos_ref[0, row:row + 1, :], (8, HEAD_DIM)).astype(BF16)
            acc = acc + _dot(p, w1_ref[0, s][:, half * HEAD_DIM:(half + 1) * HEAD_DIM])
        halves.append(acc)
    o_ref[0] = jnp.concatenate(halves, axis=1)


def _pos_term(pos, w1cat):
    return pl.pallas_call(
        _pos_term_kernel,
        grid=(2,),
        in_specs=[pl.BlockSpec((1, CMP_BLOCK, HEAD_DIM), lambda i: (i, 0, 0)),
                  pl.BlockSpec((1, CMP_STRIDE, HEAD_DIM, 2 * HEAD_DIM), lambda i: (i, 0, 0, 0))],
        out_specs=pl.BlockSpec((1, 8, 2 * HEAD_DIM), lambda i: (i, 0, 0)),
        out_shape=jax.ShapeDtypeStruct((2, 8, 2 * HEAD_DIM), F32),
        compiler_params=_cparams(("arbitrary",)),
        name="pos_term",
    )(pos, w1cat)


def _flash_step(carry, s, v_t):
    outs = []
    w = s.shape[1] // FLASH_SPLIT
    v_ext = jnp.concatenate([v_t, jnp.ones((ONES_ROWS, v_t.shape[1]), BF16)], axis=0)
    for c in range(FLASH_SPLIT):
        m, acc = (x[:, c * w:(c + 1) * w] for x in carry)
        sc = s[:, c * w:(c + 1) * w]
        m_new = jnp.maximum(m, jnp.max(sc, axis=0, keepdims=True))
        acc = jnp.exp2(m - m_new) * acc + _dot(v_ext, jnp.exp2(sc - m_new).astype(BF16))
        outs.append((m_new, acc))
    return tuple(jnp.concatenate([o[i] for o in outs], axis=1) for i in range(2))


def _flash_init(cols):
    return (jnp.full((1, cols), M_INIT, F32), jnp.zeros((HEAD_DIM + ONES_ROWS, cols), F32))


def _flash_finish(carry):
    _, acc = carry
    return acc[:HEAD_DIM] / acc[HEAD_DIM:HEAD_DIM + 1]


def _masked_softmax(s, valid):
    s = jnp.where(valid, s, NEG)
    m = jnp.max(s, axis=-1, keepdims=True)
    e = jnp.where(valid, jnp.exp(s - m), 0.0)
    return e / jnp.maximum(jnp.sum(e, axis=-1, keepdims=True), 1e-30)


def _split_dot(x, w):
    hi = x.astype(BF16)
    lo = (x - hi.astype(F32)).astype(BF16)
    return _dot(hi, w) + _dot(lo, w)


def _top_n_mask(score, index, axis):
    sel = jnp.zeros(score.shape, F32)
    for _ in range(SEL_TOPN):
        mx = jnp.max(score, axis=axis, keepdims=True)
        first = jnp.min(jnp.where(score == mx, index, 1e9), axis=axis, keepdims=True)
        hit = index == first
        sel = jnp.where(hit, 1.0, sel)
        score = jnp.where(hit, -jnp.inf, score)
    return sel


def _top_n_mask_by_rank(score, n_cand):
    lane = lax.broadcasted_iota(jnp.int32, score.shape, 1)
    rank = jnp.zeros(score.shape, F32)
    for j in range(n_cand):
        col = score[:, j:j + 1]
        ahead = (col > score) | ((col == score) & (lane > j))
        rank = rank + ahead.astype(F32)
    return ((rank < SEL_TOPN) & (lane < n_cand)).astype(F32)


def _compress_prompt_kernel(rows_ref, pos_ref, w1_ref, w2_ref, o_ref):
    out = _compress(lambda s: rows_ref[pl.ds(s, N_CMP_PAD, stride=CMP_STRIDE), :], N_CMP_PAD,
                    pos_ref.at[0], w1_ref.at[0], w2_ref.at[0])
    real = lax.broadcasted_iota(jnp.int32, (N_CMP_PAD, HEAD_DIM), 0) < N_CMP_PAD - 1
    o_ref[0, 0, 0:SLAB_OFF, :] = jnp.zeros((SLAB_OFF, HEAD_DIM), F32)
    o_ref[0, 0, SLAB_OFF:SLAB_OFF + N_CMP_PAD, :] = jnp.where(real, out, 0.0)
    o_ref[0, 0, SLAB_OFF + N_CMP_PAD:KC_ROWS, :] = jnp.zeros((KC_ROWS - SLAB_OFF - N_CMP_PAD, HEAD_DIM), F32)


def _compress_prompt(z, pos, w1, w2):
    kv_blk = Z_KV // HEAD_DIM
    return pl.pallas_call(
        _compress_prompt_kernel,
        grid=(2, NSA_KV_HEADS),
        in_specs=[pl.BlockSpec((SEQ, HEAD_DIM), lambda i, h: (0, kv_blk + NSA_KV_HEADS * i + h)),
                  pl.BlockSpec((1, 8, 2 * HEAD_DIM), lambda i, h: (i, 0, 0)),
                  pl.BlockSpec((1, CMP_STRIDE, HEAD_DIM, 2 * HEAD_DIM), lambda i, h: (i, 0, 0, 0)),
                  pl.BlockSpec((1, HEAD_DIM, HEAD_DIM), lambda i, h: (i, 0, 0))],
        out_specs=pl.BlockSpec((1, 1, KC_ROWS, HEAD_DIM), lambda i, h: (i, h, 0, 0)),
        out_shape=jax.ShapeDtypeStruct((2, NSA_KV_HEADS, KC_ROWS, HEAD_DIM), F32),
        compiler_params=_cparams(("parallel", "parallel")),
        name="compress_prompt",
    )(z, pos, w1, w2)


def _nsa_prompt_kernel(q_ref, misc_ref, ksel_ref, vsel_rows_ref, kwin_ref, vwin_rows_ref, kc_ref, vc_ref,
                       bias_ref, cover_ref, erel_ref, o_ref, vsel_ref, vwin_ref):
    kvh = pl.program_id(0)
    t = pl.program_id(1)
    cols = NSA_GROUP * Q_TILE

    @pl.when(t == 0)
    def _():
        def body(kt, _):
            rows = pl.ds(pl.multiple_of(kt * Q_TILE, Q_TILE), Q_TILE)
            vsel_ref[:, rows] = vsel_rows_ref[rows, :].astype(F32).T.astype(BF16)
            vwin_ref[:, rows] = vwin_rows_ref[rows, :].astype(F32).T.astype(BF16)
            return 0
        lax.fori_loop(0, N_QT, body, 0)

    q = q_ref[...] * (HEAD_DIM ** -0.5 * LOG2E)
    qt =jnp.concatenate([q[:, g * HEAD_DIM:(g + 1) * HEAD_DIM].T for g in range(NSA_GROUP)], axis=1).astype(BF16)

    def bias_tile(k):
        return bias_ref[0, k * Q_TILE:(k + 1) * Q_TILE, :]
    b_diag, b_prev, b_first = bias_tile(0), bias_tile(1), bias_tile(2)

    def key_tile(ref, kt):
        return ref[pl.ds(pl.multiple_of(kt * Q_TILE, Q_TILE), Q_TILE), :]

    def value_tile(ref, kt):
        return ref[:, pl.ds(pl.multiple_of(kt * Q_TILE, Q_TILE), Q_TILE)]

    n_wt = WINDOW // Q_TILE + 1
    scores, values = [], []
    for w, b_tile in enumerate((b_first,) + (None,) * (n_wt - 3) + (b_prev, b_diag)):
        kt = t - (n_wt - 1) + w
        kc = jnp.maximum(kt, 0)
        s = _dot(key_tile(kwin_ref, kc), qt) + jnp.where(kt >= 0, 0.0, NEG)
        scores.append(s if b_tile is None else s + b_tile)
        values.append(value_tile(vwin_ref, kc))
    carry = _flash_init(cols)
    for lo in range(0, n_wt, WIN_PART_TILES):
        carry = _flash_step(carry, jnp.concatenate(scores[lo:lo + WIN_PART_TILES], axis=0),
                            jnp.concatenate(values[lo:lo + WIN_PART_TILES], axis=1))
    o_win = _flash_finish(carry)

    start = pl.multiple_of(t * (Q_TILE // CMP_STRIDE), 8)
    kslab = kc_ref[0, 0, pl.ds(start, SLAB), :].astype(BF16)
    vslab = vc_ref[0, 0, pl.ds(start, SLAB), :].astype(BF16)
    b_cmp = bias_ref[0, 3 * Q_TILE:3 * Q_TILE + SLAB, :]
    u = lax.broadcasted_iota(jnp.int32, (SLAB, 1), 0)
    valid = (b_cmp > M_INIT) & (u >= SLAB_OFF - (Q_TILE // CMP_STRIDE) * t)
    s = jnp.where(valid, _dot(kslab, qt) + b_cmp, NEG)
    e = jnp.where(valid, jnp.exp2(s - jnp.max(s, axis=0, keepdims=True)), 0.0)
    p_cmp = e / jnp.maximum(jnp.sum(e, axis=0, keepdims=True), 1e-30)
    o_cmp = _dot_tn(vslab, p_cmp.astype(BF16))

    p_sum = p_cmp[:, 0:Q_TILE]
    for g in range(1, NSA_GROUP):
        p_sum = p_sum + p_cmp[:, g * Q_TILE:(g + 1) * Q_TILE]
    hi = p_sum.astype(BF16)
    lo = (p_sum - hi.astype(F32)).astype(BF16)
    imp = _dot(cover_ref[...], hi) + _dot(cover_ref[...], lo)
    jr = lax.broadcasted_iota(jnp.int32, (N_SEL, Q_TILE), 0)
    qi = lax.broadcasted_iota(jnp.int32, (N_SEL, Q_TILE), 1)
    cur = REL0 + (qi >= SEL_BLOCK).astype(jnp.int32)
    first_blk = REL0 - 2 * t
    forced = (jr == first_blk) | (jr == cur) | (jr == cur - 1)
    in_range = (jr <= cur) & (jr >= first_blk)
    score = jnp.where(in_range, jnp.where(forced, FORCED_SCORE, imp), NEG)
    sel = _top_n_mask(score, jr.astype(F32), 0)

    unsel = ((sel - 1.0) * (-NEG)).astype(BF16)
    q_aug = jnp.concatenate([qt, jnp.concatenate([unsel] * NSA_GROUP, axis=1)], axis=0)

    def sweep_step(i, carry, masked, biases=()):
        scores, values = [], []
        for j in range(SWEEP_TILES):
            kt = t - SWEEP_TILES * i - (SWEEP_TILES - 1) + j
            kc = jnp.maximum(kt, 0) if masked else kt
            tile = jnp.concatenate([key_tile(ksel_ref, kc), erel_ref[t - kc]], axis=1)
            s = _dot(tile, q_aug)
            if masked:
                s = s + jnp.where(kt >= 0, 0.0, NEG)
            if j >= SWEEP_TILES - len(biases):
                s = s + biases[j - (SWEEP_TILES - len(biases))]
            scores.append(s)
            values.append(value_tile(vsel_ref, kc))
        per = SWEEP_TILES // SWEEP_PARTS
        for k in range(SWEEP_PARTS):
            carry = _flash_step(carry, jnp.concatenate(scores[k * per:(k + 1) * per], axis=0),
                                jnp.concatenate(values[k * per:(k + 1) * per], axis=1))
        return carry

    n_full = jnp.maximum(t - (SWEEP_TILES - 1), 0) // SWEEP_TILES
    carry = lax.fori_loop(1, n_full + 1, lambda i, c: sweep_step(i, c, False), _flash_init(cols))
    some_left = (t - SWEEP_TILES * (n_full + 1) >= 0).astype(jnp.int32)
    carry = lax.fori_loop(0, some_left, lambda _, c: sweep_step(n_full + 1, c, True), carry)
    o_sel = _flash_finish(sweep_step(0, carry, True, (b_prev, b_diag)))

    gates = jax.nn.sigmoid(misc_ref[...]).T
    n_g = 3 * NSA_GROUP
    gk = jnp.where(kvh == 0, gates, pltpu.roll(gates, gates.shape[0] - n_g, axis=0))[MISC_GATES:MISC_GATES + n_g]
    for g in range(NSA_GROUP):
        c = slice(g * Q_TILE, (g + 1) * Q_TILE)
        o_g = (gk[3 * g:3 * g + 1] * o_cmp[:, c] + gk[3 * g + 1:3 * g + 2] * o_sel[:, c]
               + gk[3 * g + 2:3 * g + 3] * o_win[:, c])
        o_ref[:, g * HEAD_DIM:(g + 1) * HEAD_DIM] = o_g.T


def _nsa_prompt(z, kvb, kcp, bias, cover, erel):
    gw = NSA_GROUP * HEAD_DIM

    def k_spec(j):
        return pl.BlockSpec((SEQ, HEAD_DIM), lambda h, t, j=j: (0, 2 * j + h))
    return pl.pallas_call(
        _nsa_prompt_kernel,
        grid=(NSA_KV_HEADS, N_QT),
        in_specs=[pl.BlockSpec((Q_TILE, gw), lambda h, t: (t, h)),
                  pl.BlockSpec((Q_TILE, 128), lambda h, t: (t, Z_MISC // 128)),
                  k_spec(2), k_spec(3), k_spec(4), k_spec(5),
                  pl.BlockSpec((1, 1, KC_ROWS, HEAD_DIM), lambda h, t: (0, h, 0, 0)),
                  pl.BlockSpec((1, 1, KC_ROWS, HEAD_DIM), lambda h, t: (1, h, 0, 0)),
                  pl.BlockSpec((1,) + bias.shape[1:], lambda h, t: (h, 0, 0)),
                  pl.BlockSpec(cover.shape, lambda h, t: (0, 0)),
                  pl.BlockSpec(erel.shape, lambda h, t: (0, 0, 0))],
        out_specs=pl.BlockSpec((Q_TILE, gw), lambda h, t: (t, h)),
        out_shape=jax.ShapeDtypeStruct((SEQ, NSA_HEADS * HEAD_DIM), F32),
        scratch_shapes=[pltpu.VMEM((HEAD_DIM, SEQ), BF16), pltpu.VMEM((HEAD_DIM, SEQ), BF16)],
        compiler_params=_cparams(("arbitrary", "arbitrary")),
        name="nsa_prompt",
    )(z, z, kvb, kvb, kvb, kvb, kcp, kcp, bias, cover, erel)


S_ROWS = NSA_HEADS * DEC_SEQ
PAGE_ROWS = NSA_KV_HEADS * PAGE_SIZE
NEW_ROWS = NSA_KV_HEADS * DEC_SEQ
CHUNK_ROWS = NSA_KV_HEADS * CMP_STRIDE
CHUNK_PITCH = CHUNK_ROWS + 8


def _nsa_sample_kernel(pt_ref, q_ref, gate_ref, new_ref, kwin_ref, vwin_ref, kcmp_hbm, vcmp_hbm, ksel_hbm, vsel_hbm,
                       posk_ref, w1k_ref, w2k_ref, posv_ref, w1v_ref, w2v_ref, bias_ref, cover_ref, expand_ref,
                       o_ref, kwin_o_ref, vwin_o_ref, kcmp_buf, vcmp_buf, ksel_buf, vsel_buf, sem):
    n_pg = N_PAGES
    b = pl.program_id(0)
    slot = b % 2
    page_chunks = PAGE_ROWS // CHUNK_ROWS
    hbm = (kcmp_hbm, vcmp_hbm, ksel_hbm, vsel_hbm)
    bufs = (kcmp_buf, vcmp_buf, ksel_buf, vsel_buf)

    def page_copy(k, p, page, into):
        if k < 2:
            src = hbm[k].at[pl.ds(page * page_chunks, page_chunks)]
            dst = bufs[k].at[pl.ds(into * S_CMP + p * page_chunks, page_chunks), pl.ds(0, CHUNK_ROWS), :]
        else:
            src = hbm[k].at[pl.ds(page * PAGE_ROWS, PAGE_ROWS)]
            dst = bufs[k].at[pl.ds(into * (n_pg * PAGE_ROWS) + p * PAGE_ROWS, PAGE_ROWS), :]
        return pltpu.make_async_copy(src, dst, sem.at[into, k])

    def gather(seq, into):
        for p in range(n_pg):
            page = pt_ref[seq, p]
            for k in range(4):
                page_copy(k, p, page, into).start()

    @pl.when(b == 0)
    def _():
        gather(0, 0)

    @pl.when(b + 1 < DEC_BATCH)
    def _():
        gather(b + 1, 1 - slot)

    for p in range(n_pg):
        for k in range(4):
            page_copy(k, p, 0, slot).wait()

    def pages(k, p):
        return bufs[k][pl.ds(pl.multiple_of(slot * (n_pg * PAGE_ROWS), PAGE_ROWS) + p * PAGE_ROWS, PAGE_ROWS), :]

    comp = []
    for k, (pos_ref, w1_ref, w2_ref) in enumerate(((posk_ref, w1k_ref, w2k_ref), (posv_ref, w1v_ref, w2v_ref))):
        flat = bufs[k].reshape(2 * S_CMP * CHUNK_PITCH, HEAD_DIM)

        def load_rows(s, flat=flat):
            return jnp.concatenate(
                [flat[pl.ds(slot * (S_CMP * CHUNK_PITCH) + NSA_KV_HEADS * s + h, S_CMP, stride=CHUNK_PITCH), :]
                 for h in range(NSA_KV_HEADS)], axis=0)
        comp.append(_compress(load_rows, S_CMP_COLS, pos_ref, w1_ref, w2_ref).astype(BF16))

    keep = NSA_KV_HEADS * WINDOW - NEW_ROWS
    kwin_o_ref[0:keep, :] = kwin_ref[NEW_ROWS:NSA_KV_HEADS * WINDOW, :]
    kwin_o_ref[keep:keep + NEW_ROWS, :] = new_ref[0, 4]
    vwin_o_ref[0:keep, :] = vwin_ref[NEW_ROWS:NSA_KV_HEADS * WINDOW, :]
    vwin_o_ref[keep:keep + NEW_ROWS, :] = new_ref[0, 5]

    qs = (q_ref[0] * (HEAD_DIM ** -0.5)).astype(BF16)
    o0, o1 = S_CMP_COLS, S_CMP_COLS + S_SEL_COLS
    b_cmp = bias_ref[:, 0:o0]
    b_new = bias_ref[:, o1 - S_NEW_COLS:o1]

    def attend(scores, bias, values):
        s = jnp.concatenate(scores, axis=1) + bias
        e = jnp.exp(s - jnp.max(s, axis=-1, keepdims=True))
        acc = jnp.zeros((S_ROWS, HEAD_DIM), F32)
        c0 = 0
        for v in values:
            acc = acc + _dot(e[:, c0:c0 + v.shape[0]].astype(BF16), v)
            c0 += v.shape[0]
        return acc / jnp.sum(e, axis=-1, keepdims=True)

    p_cmp = _masked_softmax(_dot_nt(qs, comp[0]) + b_cmp, b_cmp > M_INIT)
    o_cmp = _dot(p_cmp.astype(BF16), comp[1])

    imp = _split_dot(p_cmp, cover_ref[...])
    imp = imp + pltpu.roll(imp, 8, axis=0) + pltpu.roll(imp, 16, axis=0) + pltpu.roll(imp, 24, axis=0)
    blk = lax.broadcasted_iota(jnp.int32, (S_ROWS, 128), 1)
    cur = PAST_LEN // SEL_BLOCK
    forced = (blk == 0) | (blk == cur) | (blk == cur - 1)
    score = jnp.where(blk <= cur, jnp.where(forced, FORCED_SCORE, imp), NEG)
    sel = _top_n_mask_by_rank(score, cur + 1).astype(BF16)
    mask_add = (_dot(sel, expand_ref[...]) - 1.0) * (-NEG)

    def new_tile(j):
        pad = jnp.zeros((S_NEW_COLS - NEW_ROWS, HEAD_DIM), F32)
        return jnp.concatenate([new_ref[0, j], pad], axis=0).astype(BF16)

    nk, nv = new_tile(2), new_tile(3)
    scores = [_dot_nt(qs, pages(2, p).astype(BF16)) for p in range(n_pg)] + [_dot_nt(qs, nk)]
    values = [pages(3, p).astype(BF16) for p in range(n_pg)] + [nv]
    o_sel = attend(scores, bias_ref[:, o0:o1] + mask_add, values)

    nk, nv = new_tile(4), new_tile(5)
    scores = [_dot_nt(qs, kwin_ref[...].astype(BF16)), _dot_nt(qs, nk)]
    bias_w = jnp.concatenate([bias_ref[:, o1:o1 + S_WIN_COLS], b_new], axis=1)
    o_win = attend(scores, bias_w, [vwin_ref[...].astype(BF16), nv])

    g = jax.nn.sigmoid(gate_ref[0])
    o_ref[0] = g[:, 0:1] * o_cmp + g[:, 1:2] * o_sel + g[:, 2:3] * o_win


def _nsa_sample(page_table, q_s, gate_s, new_s, kwin, vwin, pools, cmp_w, bias, cover, expand):
    win_rows = NSA_KV_HEADS * WINDOW

    def full(a):
        return pl.BlockSpec(a.shape, lambda b, pt, n=a.ndim: (0,) * n)

    def per_b(a):
        return pl.BlockSpec((1,) + a.shape[1:], lambda b, pt, n=a.ndim: (b,) + (0,) * (n - 1))
    win_spec = pl.BlockSpec((win_rows, HEAD_DIM), lambda b, pt: (b, 0))
    consts = list(cmp_w) + [bias, cover, expand]
    grid_spec = pltpu.PrefetchScalarGridSpec(
        num_scalar_prefetch=1,
        grid=(DEC_BATCH,),
        in_specs=[per_b(q_s), per_b(gate_s), per_b(new_s), win_spec, win_spec]
        + [pl.BlockSpec(memory_space=pl.ANY)] * 4 + [full(a) for a in consts],
        out_specs=[per_b(q_s), win_spec, win_spec],
        scratch_shapes=[pltpu.VMEM((2 * S_CMP, CHUNK_PITCH, HEAD_DIM), F32)] * 2
        + [pltpu.VMEM((2 * N_PAGES * PAGE_ROWS, HEAD_DIM), F32)] * 2
        + [pltpu.SemaphoreType.DMA((2, 4))],
    )
    return pl.pallas_call(
        _nsa_sample_kernel,
        grid_spec=grid_spec,
        out_shape=[jax.ShapeDtypeStruct(q_s.shape, F32),
                   jax.ShapeDtypeStruct(kwin.shape, F32),
                   jax.ShapeDtypeStruct(vwin.shape, F32)],
        compiler_params=_cparams(("arbitrary",)),
        name="nsa_sample",
    )(page_table, q_s, gate_s, new_s, kwin, vwin, *pools, *consts)


def _log_decay(a_blk, wa_ref, ba_ref):
    x = _dot(a_blk.astype(BF16), wa_ref[...]) + ba_ref[...]
    return (jnp.minimum(x, 0.0) - jnp.log1p(jnp.exp(-jnp.abs(x)))) * (1.0 / GLA_TAU)


def _segment_cumsum(g, seg):
    pos = lax.broadcasted_iota(jnp.int32, g.shape, 0) % seg
    cum = g
    sh = 1
    while sh < seg:
        cum = cum + jnp.where(pos >= sh, pltpu.roll(cum, sh, axis=0), 0.0)
        sh *= 2
    return cum


def _gla_prompt_kernel(q_ref, k_ref, v_ref, r_ref, a_ref, wa_ref, ba_ref, gn_ref, o_ref, st_o_ref, st_ref,
                       *, n_blk, tb):
    tbi = pl.program_id(0)

    @pl.when(tbi == 0)
    def _():
        st_ref[...] = jnp.zeros_like(st_ref)

    c = GLA_CHUNK
    tril = lax.broadcasted_iota(jnp.int32, (c, c), 0) >= lax.broadcasted_iota(jnp.int32, (c, c), 1)
    cum = _segment_cumsum(_log_decay(a_ref[...], wa_ref, ba_ref), c)
    q = q_ref[...] * (GLA_DK ** -0.5)
    k = k_ref[...]
    v = v_ref[...].astype(BF16)
    qe = (q * jnp.exp(cum)).astype(BF16)
    kd = (k * jnp.exp(-cum)).astype(BF16)
    heads = [(slice(h * GLA_DK, (h + 1) * GLA_DK), slice(h * GLA_DV, (h + 1) * GLA_DV)) for h in range(GLA_HEADS)]
    sts = [st_ref[h] for h in range(GLA_HEADS)]
    outs = [[] for _ in range(GLA_HEADS)]
    for ci in range(tb // c):
        r = slice(ci * c, (ci + 1) * c)
        last = cum[ci * c + c - 1:ci * c + c, :]
        kl = (k[r] * jnp.exp(last - cum[r])).astype(BF16)
        decay = jnp.exp(last)
        for h, (dk, dv) in enumerate(heads):
            att = jnp.where(tril, _dot_nt(qe[r, dk], kd[r, dk]), 0.0)
            outs[h].append(_dot_nt(qe[r, dk], sts[h].astype(BF16)) + _dot(att.astype(BF16), v[r, dv]))
            sts[h] = decay[:, dk] * sts[h] + _dot_tn(v[r, dv], kl[:, dk])
    for h, (dk, dv) in enumerate(heads):
        st_ref[h] = sts[h]
        o = jnp.concatenate(outs[h], axis=0)
        o_ref[:, dv] = _rms(o, gn_ref[...]) * _silu(r_ref[:, dv])

    @pl.when(tbi == n_blk - 1)
    def _():
        st_o_ref[...] = st_ref[...]


def _gla_prompt(z, wa, ba, gn, tb=256):
    n_blk = SEQ // tb
    hk, hv = GLA_HEADS * GLA_DK, GLA_HEADS * GLA_DV
    st_shape = (GLA_HEADS, GLA_DV, GLA_DK)
    return pl.pallas_call(
        functools.partial(_gla_prompt_kernel, n_blk=n_blk, tb=tb),
        grid=(n_blk,),
        in_specs=[pl.BlockSpec((tb, hk), lambda i: (i, Z_QG // hk)),
                  pl.BlockSpec((tb, hk), lambda i: (i, Z_KG // hk)),
                  pl.BlockSpec((tb, hv), lambda i: (i, Z_VG // hv)),
                  pl.BlockSpec((tb, hv), lambda i: (i, Z_RG // hv)),
                  pl.BlockSpec((tb, 128), lambda i: (i, Z_MISC // 128)),
                  pl.BlockSpec((128, hk), lambda i: (0, 0)),
                  pl.BlockSpec((1, hk), lambda i: (0, 0)),
                  pl.BlockSpec((1, GLA_DV), lambda i: (0, 0))],
        out_specs=[pl.BlockSpec((tb, hv), lambda i: (i, 0)),
                   pl.BlockSpec(st_shape, lambda i: (0, 0, 0))],
        out_shape=[jax.ShapeDtypeStruct((SEQ, hv), F32), jax.ShapeDtypeStruct(st_shape, F32)],
        scratch_shapes=[pltpu.VMEM(st_shape, F32)],
        compiler_params=_cparams(("arbitrary",)),
        name="gla_prompt",
    )(z, z, z, z, z, wa, ba, gn)


GS_B = 4


def _gla_sample_kernel(q_ref, k_ref, v_ref, r_ref, a_ref, wa_ref, ba_ref, gn_ref, s_ref, o_ref, s_o_ref):
    rows = GS_B * DEC_SEQ
    a = a_ref[...]
    ri = lax.broadcasted_iota(jnp.int32, (rows, rows), 0)
    ci = lax.broadcasted_iota(jnp.int32, (rows, rows), 1)
    same_causal = (ri // DEC_SEQ == ci // DEC_SEQ) & (ri >= ci)
    row_b = lax.broadcasted_iota(jnp.int32, (rows, 1), 0) // DEC_SEQ
    ones = jnp.ones((rows, 128), BF16)
    for h in range(GLA_HEADS):
        dk = slice(h * GLA_DK, (h + 1) * GLA_DK)
        dv = slice(h * GLA_DV, (h + 1) * GLA_DV)
        g = _log_decay(a, wa_ref.at[:, dk], ba_ref.at[:, dk])
        cum = _segment_cumsum(g, DEC_SEQ)
        q = q_ref[:, dk] * (GLA_DK ** -0.5)
        k = k_ref[:, dk]
        v = v_ref[:, dv].astype(BF16)
        qe = (q * jnp.exp(cum)).astype(BF16)
        kd = (k * jnp.exp(-cum)).astype(BF16)
        att = jnp.where(same_causal, _dot_nt(qe, kd), 0.0)
        o = _dot(att.astype(BF16), v)
        for b in range(GS_B):
            mine = row_b == b
            last = cum[b * DEC_SEQ + DEC_SEQ - 1:(b + 1) * DEC_SEQ, :]
            s = s_ref[b, h]
            o = o + jnp.where(mine, _dot(qe, s.astype(BF16)), 0.0)
            kl = jnp.where(mine, k * jnp.exp(last - cum), 0.0)
            hi = jnp.where(mine, g, 0.0).astype(BF16)
            lo = (jnp.where(mine, g, 0.0) - hi.astype(F32)).astype(BF16)
            last_col = (_dot_tn(hi, ones) + _dot_tn(lo, ones))[:, 0:1]
            s_o_ref[b, h] = jnp.exp(last_col) * s + _dot_tn(kl.astype(BF16), v)
        o_ref[:, dv] = _rms(o, gn_ref[...]) * _silu(r_ref[:, dv])


def _gla_sample(zs, state, wa, ba, gn):
    rows = GS_B * DEC_SEQ
    n = DEC_BATCH * DEC_SEQ
    hk, hv = GLA_HEADS * GLA_DK, GLA_HEADS * GLA_DV
    st_spec = pl.BlockSpec((GS_B, GLA_HEADS, GLA_DK, GLA_DV), lambda i: (i, 0, 0, 0))
    return pl.pallas_call(
        _gla_sample_kernel,
        grid=(DEC_BATCH // GS_B,),
        in_specs=[pl.BlockSpec((rows, hk), lambda i: (i, Z_QG // hk)),
                  pl.BlockSpec((rows, hk), lambda i: (i, Z_KG // hk)),
                  pl.BlockSpec((rows, hv), lambda i: (i, Z_VG // hv)),
                  pl.BlockSpec((rows, hv), lambda i: (i, Z_RG // hv)),
                  pl.BlockSpec((rows, 128), lambda i: (i, Z_MISC // 128)),
                  pl.BlockSpec((128, hk), lambda i: (0, 0)),
                  pl.BlockSpec((1, hk), lambda i: (0, 0)),
                  pl.BlockSpec((1, GLA_DV), lambda i: (0, 0)),
                  st_spec],
        out_specs=[pl.BlockSpec((rows, hv), lambda i: (i, 0)), st_spec],
        out_shape=[jax.ShapeDtypeStruct((n, hv), F32), jax.ShapeDtypeStruct(state.shape, F32)],
        compiler_params=_cparams(("parallel",)),
        name="gla_sample",
    )(zs, zs, zs, zs, zs, wa, ba, gn, state)


def _softmax_rows(s):
    m = jnp.max(s, axis=-1, keepdims=True)
    e = jnp.exp(s - m)
    return e / jnp.sum(e, axis=-1, keepdims=True)


def _mem_prompt_kernel(q_ref, k_ref, v_ref, o_ref):
    for h in range(MEM_HEADS):
        d = slice(h * MEM_HEAD_DIM, (h + 1) * MEM_HEAD_DIM)
        q = (q_ref[:, d] * (MEM_HEAD_DIM ** -0.5)).astype(BF16)
        p = _softmax_rows(_dot_nt(q, k_ref[:, d].astype(BF16)))
        o_ref[:, d] = _dot(p.astype(BF16), v_ref[:, d].astype(BF16))


def _mem_prompt(qm, memkv, tq=256):
    w = MEM_HEADS * MEM_HEAD_DIM
    return pl.pallas_call(
        _mem_prompt_kernel,
        grid=(SEQ // tq,),
        in_specs=[pl.BlockSpec((tq, w), lambda i: (i, 0)),
                  pl.BlockSpec((MEM_TOKENS, w), lambda i: (0, 0)),
                  pl.BlockSpec((MEM_TOKENS, w), lambda i: (0, 1))],
        out_specs=pl.BlockSpec((tq, w), lambda i: (i, 0)),
        out_shape=jax.ShapeDtypeStruct((SEQ, w), F32),
        compiler_params=_cparams(("parallel",)),
        name="mem_prompt",
    )(qm, memkv, memkv)


def _mem_sample_kernel(q_ref, k_ref, v_ref, o_ref):
    rows = MEM_HEADS * DEC_SEQ
    cols = MEM_HEADS * MEM_TOKENS
    row_h = lax.broadcasted_iota(jnp.int32, (rows, cols), 0) // DEC_SEQ
    col_h = lax.broadcasted_iota(jnp.int32, (rows, cols), 1) % MEM_HEADS
    for b in range(MS_B):
        kv_rows = slice(b * cols, (b + 1) * cols)
        q = (q_ref[b] * (MEM_HEAD_DIM ** -0.5)).astype(BF16)
        s = jnp.where(row_h == col_h, _dot_nt(q, k_ref[kv_rows, :].astype(BF16)), NEG)
        o_ref[b] = _dot(_softmax_rows(s).astype(BF16), v_ref[kv_rows, :].astype(BF16))


MS_B = 4


def _mem_sample(q_s, k_mem, v_mem):
    rows = MEM_HEADS * DEC_SEQ
    kv_spec = pl.BlockSpec((MS_B * MEM_HEADS * MEM_TOKENS, MEM_HEAD_DIM), lambda b: (b, 0))
    return pl.pallas_call(
        _mem_sample_kernel,
        grid=(DEC_BATCH // MS_B,),
        in_specs=[pl.BlockSpec((MS_B, rows, MEM_HEAD_DIM), lambda b: (b, 0, 0)), kv_spec, kv_spec],
        out_specs=pl.BlockSpec((MS_B, rows, MEM_HEAD_DIM), lambda b: (b, 0, 0)),
        out_shape=jax.ShapeDtypeStruct((DEC_BATCH, rows, MEM_HEAD_DIM), F32),
        compiler_params=_cparams(("parallel",)),
        name="mem_sample",
    )(q_s, k_mem, v_mem)


W_IN_SEGMENTS = ((0, 1024, Z_QN), (1024, 1536, Z_KV), (2560, 24, Z_MISC + MISC_GATES), (2584, 512, Z_QG),
                 (3096, 512, Z_KG), (3608, 1024, Z_VG), (4632, 1024, Z_RG), (5656, 16, Z_MISC + MISC_A))
W_IN_COLS = 5672


def _permute_w_in_kernel(w_ref, o_ref):
    o_ref[:, Z_MISC:Z_W] = jnp.zeros((o_ref.shape[0], Z_W - Z_MISC), BF16)
    for src, width, dst in W_IN_SEGMENTS:
        o_ref[:, dst:dst + width] = w_ref[:, src:src + width].astype(BF16)


def _permute_w_in(w_in, tr=256):
    d = w_in.shape[0]
    return pl.pallas_call(
        _permute_w_in_kernel,
        grid=(d // tr,),
        in_specs=[pl.BlockSpec((tr, W_IN_COLS), lambda i: (i, 0))],
        out_specs=pl.BlockSpec((tr, Z_W), lambda i: (i, 0)),
        out_shape=jax.ShapeDtypeStruct((d, Z_W), BF16),
        compiler_params=_cparams(("parallel",)),
        name="permute_w_in",
    )(w_in)


def kernel(x_prompt, x_sample, mem_prompt, cache_k_cmp, cache_v_cmp, cache_k_sel, cache_v_sel, cache_k_win,
           cache_v_win, state_gla, cache_k_mem, cache_v_mem, page_table, norm_ffn1, ffn1_w_gate, ffn1_w_up,
           ffn1_w_down, norm_mix, w_in, w_out, cmp_pos_k, cmp_w1_k, cmp_w2_k, cmp_pos_v, cmp_w1_v, cmp_w2_v,
           rel_bias, gla_w_a2, gla_b_a, gla_norm, norm_mem, norm_mem_src, w_mem_q, w_mem_k, w_mem_v, w_mem_o,
           norm_ffn2, ffn2_w_gate, ffn2_w_up, ffn2_w_down, norm_final):
    bf = lambda a: a.astype(BF16)
    row = lambda a: a.reshape(1, -1)
    nb, ns = DEC_BATCH, DEC_SEQ
    kvw = NSA_KV_HEADS * HEAD_DIM

    h1, h1n = _ffn([x_prompt[0], x_sample.reshape(nb * ns, D_MODEL)], row(norm_ffn1[0]), bf(ffn1_w_gate[0]),
                   bf(ffn1_w_up[0]), bf(ffn1_w_down[0]), row(norm_mix[0]), False)
    z = _matmul(h1n, _permute_w_in(w_in[0]), 512, Z_W // 3, "proj_in")

    kv_p = z[:SEQ, Z_KV:Z_QG]
    kv_s = z[SEQ:, Z_KV:Z_QG].reshape(nb, ns, 6 * kvw)
    rows_p = [kv_p[:, j * kvw:(j + 1) * kvw].reshape(1, 1, SEQ, NSA_KV_HEADS, HEAD_DIM) for j in range(6)]
    rows_s = [kv_s[:, :, j * kvw:(j + 1) * kvw].reshape(1, nb, ns, NSA_KV_HEADS, HEAD_DIM) for j in range(4)]

    tab_p = _bias_tables(rel_bias, _prompt_bucket_table(), "bias_prompt")
    far = rel_bias[NUM_BUCKETS - 1][:, None, None]
    near = tab_p[:, :3 * Q_TILE]
    tab_p = jnp.concatenate([jnp.where(near > M_INIT, near - far, NEG), tab_p[:, 3 * Q_TILE:]], axis=1)
    tab_p = jnp.where(tab_p > M_INIT, tab_p * LOG2E, NEG)
    tab_p = tab_p.reshape(NSA_KV_HEADS, NSA_GROUP, -1, Q_TILE).transpose(0, 2, 1, 3)
    tab_p = tab_p.reshape(NSA_KV_HEADS, -1, NSA_GROUP * Q_TILE)
    tab_s = _bias_tables(rel_bias, _sample_bucket_table(), "bias_sample")
    tab_s = tab_s.reshape(NSA_KV_HEADS, NSA_GROUP, NSA_KV_HEADS, ns, -1)
    tab_s = jnp.stack([tab_s[h, :, h] for h in range(NSA_KV_HEADS)], axis=1).reshape(S_ROWS, -1)

    cmp_w1 = bf(jnp.stack([cmp_w1_k[0], cmp_w1_v[0]]))
    cmp_w1 = jnp.concatenate([cmp_w1[:, :CMP_STRIDE], cmp_w1[:, CMP_STRIDE:]], axis=-1)
    cmp_w2 = bf(jnp.stack([cmp_w2_k[0], cmp_w2_v[0]]))
    cmp_pos = _pos_term(jnp.stack([cmp_pos_k[0], cmp_pos_v[0]]), cmp_w1)
    kcp = _compress_prompt(z, cmp_pos, cmp_w1, cmp_w2)
    cover_p = jnp.asarray(_cover_np(SLAB, N_SEL, lambda u, j: u - 4 * j - SLAB_OFF + 4 * REL0).T, BF16)
    erel = (np.arange(N_SEL)[None, None, :] == REL0 - 2 * np.arange(N_QT)[:, None, None]
            + (np.arange(Q_TILE)[None, :, None] >= SEL_BLOCK))
    o_nsa_p = _nsa_prompt(z, bf(kv_p), kcp, tab_p, cover_p, jnp.asarray(erel, BF16))

    zs = z[SEQ:]
    q_s = zs[:, Z_QN:Z_QN + NSA_HEADS * HEAD_DIM].reshape(nb, ns, NSA_KV_HEADS, NSA_GROUP, HEAD_DIM)
    q_s = q_s.transpose(0, 3, 2, 1, 4).reshape(nb, S_ROWS, HEAD_DIM)
    gate_s = zs[:, Z_MISC + MISC_GATES:Z_MISC + MISC_GATES + 3 * NSA_HEADS]
    gate_s = gate_s.reshape(nb, ns, NSA_KV_HEADS, NSA_GROUP, 3).transpose(0, 3, 2, 1, 4).reshape(nb, S_ROWS, 3)
    new_s = kv_s.reshape(nb, ns, 6, NSA_KV_HEADS, HEAD_DIM).transpose(0, 2, 1, 3, 4).reshape(nb, 6, NEW_ROWS, HEAD_DIM)
    as_rows = lambda c: c.reshape(-1, HEAD_DIM)
    pools = ([c.reshape(-1, CHUNK_ROWS, HEAD_DIM) for c in (cache_k_cmp, cache_v_cmp)]
             + [as_rows(c) for c in (cache_k_sel, cache_v_sel)])
    cover_s = _cover_np(S_CMP, 128, lambda c, j: c - 4 * j)
    cover_s = jnp.asarray(np.concatenate([cover_s] * NSA_KV_HEADS, axis=0), BF16)
    expand_s = jnp.asarray(np.arange(128)[:, None]
                           == (np.arange(S_SEL_COLS)[None, :] // (NSA_KV_HEADS * SEL_BLOCK)), BF16)
    o_nsa_s, kwin_s, vwin_s = _nsa_sample(
        page_table, q_s, gate_s, new_s, as_rows(cache_k_win), as_rows(cache_v_win), pools,
        (cmp_pos[0], cmp_w1[0], cmp_w2[0], cmp_pos[1], cmp_w1[1], cmp_w2[1]),
        tab_s, cover_s, expand_s)
    o_nsa_s = o_nsa_s.reshape(nb, NSA_GROUP, NSA_KV_HEADS, ns, HEAD_DIM).transpose(0, 3, 2, 1, 4)
    o_nsa_s = o_nsa_s.reshape(nb * ns, NSA_HEADS * HEAD_DIM)

    wa = bf(jnp.zeros((128, GLA_HEADS * GLA_DK), F32).at[MISC_A:MISC_A + GLA_RANK].set(gla_w_a2[0]))
    ba, gn = row(gla_b_a[0]), row(gla_norm[0])
    o_gla_p, st_p = _gla_prompt(z, wa, ba, gn)
    o_gla_s, st_s = _gla_sample(zs, state_gla[0], wa, ba, gn)

    half = NSA_HEADS * HEAD_DIM
    h2 = _matmul_res(h1, [(o_nsa_p, o_nsa_s), (o_gla_p, o_gla_s)], [bf(w_out[0][:half]), bf(w_out[0][half:])],
                     512, "proj_out")

    memkv = _norm_matmul(mem_prompt[0], row(norm_mem_src[0]),
                         bf(jnp.concatenate([w_mem_k[0], w_mem_v[0]], axis=1)), MEM_TOKENS, 512, "mem_kv")
    mw = MEM_HEADS * MEM_HEAD_DIM
    qm = _norm_matmul(h2, row(norm_mem[0]), bf(w_mem_q[0]), 512, mw, "mem_q")
    om_p = _mem_prompt(qm, memkv)
    qm_s = qm[SEQ:].reshape(nb, ns, MEM_HEADS, MEM_HEAD_DIM).transpose(0, 2, 1, 3)
    om_s = _mem_sample(qm_s.reshape(nb, MEM_HEADS * ns, MEM_HEAD_DIM),
                       as_rows(cache_k_mem), as_rows(cache_v_mem))
    om_s = om_s.reshape(nb, MEM_HEADS, ns, MEM_HEAD_DIM).transpose(0, 2, 1, 3).reshape(nb * ns, mw)
    h3 = _matmul_res(h2, [(om_p, om_s)], [bf(w_mem_o[0])], 512, "mem_out")

    y_p, y_s = _ffn([h3], row(norm_ffn2[0]), bf(ffn2_w_gate[0]), bf(ffn2_w_up[0]), bf(ffn2_w_down[0]),
                    row(norm_final), True)

    mem_shape = (1, 1, MEM_TOKENS, MEM_HEADS, MEM_HEAD_DIM)
    win_shape = (1, nb, WINDOW, NSA_KV_HEADS, HEAD_DIM)
    return (y_p.reshape(1, SEQ, D_MODEL), y_s.reshape(nb, ns, D_MODEL),
            rows_p[0], rows_p[1], rows_p[2], rows_p[3],
            rows_p[4][:, :, SEQ - WINDOW:], rows_p[5][:, :, SEQ - WINDOW:],
            st_p.transpose(0, 2, 1).reshape(1, 1, GLA_HEADS, GLA_DK, GLA_DV),
            memkv[:, :mw].reshape(mem_shape), memkv[:, mw:].reshape(mem_shape),
            rows_s[0], rows_s[1], rows_s[2], rows_s[3],
            kwin_s.reshape(win_shape), vwin_s.reshape(win_shape),
            st_s.reshape(1, nb, GLA_HEADS, GLA_DK, GLA_DV))
```

```python
import functools
import math

import numpy as np
import jax
import jax.numpy as jnp
from jax import lax
from jax.experimental import pallas as pl
from jax.experimental.pallas import tpu as pltpu

F32 = jnp.float32
BF16 = jnp.bfloat16

D_MODEL = 2048
SEQ = 8192
DEC_BATCH = 128
DEC_SEQ = 4
PAST_LEN = 2048
PAGE_SIZE = 128
N_PAGES = PAST_LEN // PAGE_SIZE
HEAD_DIM = 128
NSA_HEADS = 8
NSA_KV_HEADS = 2
NSA_GROUP = 4
CMP_BLOCK = 32
CMP_STRIDE = 16
SEL_BLOCK = 64
SEL_TOPN = 16
WINDOW = 512
FORCED_SCORE = 1.0e4
GLA_HEADS = 4
GLA_DV = 256
GLA_DK = 128
GLA_RANK = 16
GLA_TAU = 16.0
GLA_CHUNK = 32
MEM_TOKENS = 256
MEM_HEADS = 4
MEM_HEAD_DIM = 128
D_FF = 5632
NUM_BUCKETS = 32
MAX_DISTANCE = 128
RMS_EPS = 1e-6

N_TOK = SEQ + DEC_BATCH * DEC_SEQ
Z_KV, Z_QG, Z_VG, Z_RG, Z_QN, Z_KG, Z_MISC = 0, 1536, 2048, 3072, 4096, 5120, 5632
Z_W = 5760
MISC_GATES, MISC_A = 0, 24

NEG = -1e30
LOG2E = math.log2(math.e)
ONES_ROWS = 16
M_INIT = -1e29

Q_TILE = 128
N_QT = SEQ // Q_TILE
N_CMP_PAD = SEQ // CMP_STRIDE
SLAB = N_CMP_PAD + 128
SLAB_OFF = SLAB - 16
KC_ROWS = SLAB_OFF + N_CMP_PAD + 16
N_SEL = SEQ // SEL_BLOCK
REL0 = N_SEL - 2
CMP_GROUP = 4
SWEEP_TILES = 8
SWEEP_PARTS = 4
WIN_PART_TILES = 2
FLASH_SPLIT = 1

VMEM_LIMIT = 56 * 1024 * 1024


def _cparams(sem):
    return pltpu.CompilerParams(dimension_semantics=sem, vmem_limit_bytes=VMEM_LIMIT)


def _dot(a, b):
    return jnp.dot(a, b, preferred_element_type=F32)


def _dot_nt(a, b):
    return lax.dot_general(a, b, (((1,), (1,)), ((), ())), preferred_element_type=F32)


def _dot_tn(a, b):
    return lax.dot_general(a, b, (((0,), (0,)), ((), ())), preferred_element_type=F32)


def _rms(x, g):
    return x * lax.rsqrt(jnp.mean(x * x, axis=-1, keepdims=True) + RMS_EPS) * g


def _silu(x):
    return x * jax.nn.sigmoid(x)


def _ffn_kernel(*refs, n_ff, n_first, two_in, two_out):
    refs = list(refs)
    x_refs = [refs.pop(0) for _ in range(2 if two_in else 1)]
    g_ref, wg_ref, wu_ref, wd_ref, gf_ref = refs[:5]
    o_refs = refs[5:7]
    xn_ref, acc_ref = refs[-2:]
    i = pl.program_id(0)
    j = pl.program_id(1)

    def x_tile():
        return jnp.where(i < n_first, x_refs[0][...], x_refs[1][...]) if two_in else x_refs[0][...]

    @pl.when(j == 0)
    def _():
        xn_ref[...] = _rms(x_tile(), g_ref[...]).astype(BF16)
        acc_ref[...] = jnp.zeros_like(acc_ref)

    xn = xn_ref[...]
    hid = _silu(_dot(xn, wg_ref[...])) * _dot(xn, wu_ref[...])
    acc_ref[...] += _dot(hid.astype(BF16), wd_ref[...])

    @pl.when(j == n_ff - 1)
    def _():
        h = x_tile() + 0.5 * acc_ref[...]
        if two_out:
            h = _rms(h, gf_ref[...])

            @pl.when(i < n_first)
            def _():
                o_refs[0][...] = h

            @pl.when(i >= n_first)
            def _():
                o_refs[1][...] = h
        else:
            o_refs[0][...] = h
            o_refs[1][...] = _rms(h, gf_ref[...]).astype(BF16)


FFN_TM = 512
FFN_TF = 512


def _ffn(xs, g, wg, wu, wd, gf, split_out):
    tm, tf, d = FFN_TM, FFN_TF, D_MODEL
    n_ff = D_FF // tf
    n_first = SEQ // tm
    two_in = len(xs) == 2

    def first(i, j):
        return (jnp.minimum(i, n_first - 1), 0)

    def second(i, j):
        return (jnp.maximum(i - n_first, 0), 0)
    whole = pl.BlockSpec((tm, d), lambda i, j: (i, 0))
    pair = [pl.BlockSpec((tm, d), first), pl.BlockSpec((tm, d), second)]
    n_s = DEC_BATCH * DEC_SEQ
    return pl.pallas_call(
        functools.partial(_ffn_kernel, n_ff=n_ff, n_first=n_first, two_in=two_in, two_out=split_out),
        grid=(N_TOK // tm, n_ff),
        in_specs=(pair if two_in else [whole])
        + [pl.BlockSpec((1, d), lambda i, j: (0, 0)),
           pl.BlockSpec((d, tf), lambda i, j: (0, j)),
           pl.BlockSpec((d, tf), lambda i, j: (0, j)),
           pl.BlockSpec((tf, d), lambda i, j: (j, 0)),
           pl.BlockSpec((1, d), lambda i, j: (0, 0))],
        out_specs=pair if split_out else [whole, whole],
        out_shape=([jax.ShapeDtypeStruct((SEQ, d), F32), jax.ShapeDtypeStruct((n_s, d), F32)] if split_out
                   else [jax.ShapeDtypeStruct((N_TOK, d), F32), jax.ShapeDtypeStruct((N_TOK, d), BF16)]),
        scratch_shapes=[pltpu.VMEM((tm, d), BF16), pltpu.VMEM((tm, d), F32)],
        compiler_params=_cparams(("arbitrary", "arbitrary")),
        name="ffn",
    )(*xs, g, wg, wu, wd, gf)


def _norm_matmul_kernel(x_ref, g_ref, w_ref, o_ref, xn_ref):
    @pl.when(pl.program_id(1) == 0)
    def _():
        xn_ref[...] = _rms(x_ref[...], g_ref[...]).astype(BF16)

    o_ref[...] = _dot(xn_ref[...], w_ref[...])


def _norm_matmul(x, g, w, tm, tn, name):
    n, d = x.shape
    dout = w.shape[1]
    return pl.pallas_call(
        _norm_matmul_kernel,
        grid=(n // tm, dout // tn),
        in_specs=[pl.BlockSpec((tm, d), lambda i, j: (i, 0)),
                  pl.BlockSpec((1, d), lambda i, j: (0, 0)),
                  pl.BlockSpec((d, tn), lambda i, j: (0, j))],
        out_specs=pl.BlockSpec((tm, tn), lambda i, j: (i, j)),
        out_shape=jax.ShapeDtypeStruct((n, dout), F32),
        scratch_shapes=[pltpu.VMEM((tm, d), BF16)],
        compiler_params=_cparams(("parallel", "arbitrary")),
        name=name,
    )(x, g, w)


N_KV_ARRAYS = 6
KV_W = NSA_KV_HEADS * HEAD_DIM


def _proj_in_kernel(x_ref, w_ref, o_ref, kvb_ref, rows_s_ref, *rows_p_refs, n_first):
    j, i = pl.program_id(0), pl.program_id(1)
    res = _dot(x_ref[...], w_ref[...])
    o_ref[...] = res
    tm = res.shape[0]

    @pl.when(j == 0)
    def _():
        kvb_ref[...] = res[:, Z_KV:Z_KV + N_KV_ARRAYS * KV_W].astype(BF16)

        def head_rows(a, h):
            c0 = Z_KV + a * KV_W + h * HEAD_DIM
            return res[:, c0:c0 + HEAD_DIM]

        @pl.when(i < n_first)
        def _():
            for a in range(N_KV_ARRAYS):
                for h in range(NSA_KV_HEADS):
                    rows_p_refs[a][pl.ds(h, tm, stride=NSA_KV_HEADS), :] = head_rows(a, h)

        @pl.when(i >= n_first)
        def _():
            for a in range(N_KV_ARRAYS):
                for h in range(NSA_KV_HEADS):
                    rows_s_ref.at[a][pl.ds(h, tm, stride=NSA_KV_HEADS), :] = head_rows(a, h)


def _proj_in(x, w, tm=512, tn=1920):
    n, d = x.shape
    n_i, n_first = n // tm, SEQ // tm
    n_s = n - SEQ
    assert Z_KV == 0 and N_KV_ARRAYS * KV_W <= tn and n_s == tm

    def once(i_of):
        return lambda j, i: jnp.where(j == 0, i_of(i), i_of(n_i - 1))
    p_idx = once(lambda i: jnp.minimum(i, n_first - 1))
    return pl.pallas_call(
        functools.partial(_proj_in_kernel, n_first=n_first),
        grid=(Z_W // tn, n_i),
        in_specs=[pl.BlockSpec((tm, d), lambda j, i: (i, 0)),
                  pl.BlockSpec((d, tn), lambda j, i: (0, j))],
        out_specs=[pl.BlockSpec((tm, tn), lambda j, i: (i, j)),
                   pl.BlockSpec((tm, N_KV_ARRAYS * KV_W), lambda j, i: (once(lambda i: i)(j, i), 0)),
                   pl.BlockSpec((N_KV_ARRAYS, NSA_KV_HEADS * n_s, HEAD_DIM), lambda j, i: (0, 0, 0))]
        + [pl.BlockSpec((NSA_KV_HEADS * tm, HEAD_DIM), lambda j, i: (p_idx(j, i), 0))] * N_KV_ARRAYS,
        out_shape=[jax.ShapeDtypeStruct((n, Z_W), F32),
                   jax.ShapeDtypeStruct((n, N_KV_ARRAYS * KV_W), BF16),
                   jax.ShapeDtypeStruct((N_KV_ARRAYS, NSA_KV_HEADS * n_s, HEAD_DIM), F32)]
        + [jax.ShapeDtypeStruct((NSA_KV_HEADS * SEQ, HEAD_DIM), F32)] * N_KV_ARRAYS,
        compiler_params=_cparams(("arbitrary", "arbitrary")),
        name="proj_in",
    )(x, w)


def _matmul_res_kernel(*refs, n_lhs, n_first):
    res_ref = refs[0]
    o_ref = refs[1 + 3 * n_lhs]
    first = pl.program_id(0) < n_first
    acc = res_ref[...]
    for k in range(n_lhs):
        lhs = jnp.where(first, refs[1 + 2 * k][...], refs[2 + 2 * k][...])
        acc = acc + _dot(lhs.astype(BF16), refs[1 + 2 * n_lhs + k][...])
    o_ref[...] = acc


def _matmul_res(res, lhs, ws, tm, name):
    n, d = res.shape
    n_first = SEQ // tm

    def pair_specs(width):
        return [pl.BlockSpec((tm, width), lambda i: (jnp.minimum(i, n_first - 1), 0)),
                pl.BlockSpec((tm, width), lambda i: (jnp.maximum(i - n_first, 0), 0))]
    return pl.pallas_call(
        functools.partial(_matmul_res_kernel, n_lhs=len(lhs), n_first=n_first),
        grid=(n // tm,),
        in_specs=([pl.BlockSpec((tm, d), lambda i: (i, 0))]
                  + [spec for a, _ in lhs for spec in pair_specs(a.shape[1])]
                  + [pl.BlockSpec(w.shape, lambda i: (0, 0)) for w in ws]),
        out_specs=pl.BlockSpec((tm, d), lambda i: (i, 0)),
        out_shape=jax.ShapeDtypeStruct((n, d), F32),
        compiler_params=_cparams(("arbitrary",)),
        name=name,
    )(res, *[a for pair in lhs for a in pair], *ws)


def _rel_bucket_np(dist):
    n = np.maximum(dist, 0)
    exact = NUM_BUCKETS // 2
    nf = np.maximum(n, 1).astype(np.float32)
    large = exact + (np.log(nf / np.float32(exact)) / np.float32(math.log(MAX_DISTANCE / exact))
                     * np.float32(NUM_BUCKETS - exact)).astype(np.int32)
    return np.where(n < exact, n, np.minimum(large, NUM_BUCKETS - 1)).astype(np.int32)


def _bucket_or_masked(dist, valid):
    return np.where(valid, _rel_bucket_np(dist), -1).astype(np.int32)


def _prompt_bucket_table():
    i = np.arange(Q_TILE)[:, None]
    j = np.arange(Q_TILE)[None, :]
    u = np.arange(SLAB)[None, :]
    dist_c = i - CMP_STRIDE * u + (CMP_STRIDE * SLAB_OFF - (CMP_BLOCK - 1))
    diag = _bucket_or_masked(i - j, i - j >= 0)
    prev = _bucket_or_masked(Q_TILE + i - j, np.ones((Q_TILE, Q_TILE), bool))
    first = _bucket_or_masked(WINDOW + i - j, j > i)
    cmp_ = _bucket_or_masked(dist_c, dist_c >= 0)
    return np.concatenate([diag, prev, first, cmp_], axis=1).T


S_CMP = PAST_LEN // CMP_STRIDE
S_CMP_COLS = NSA_KV_HEADS * S_CMP
S_NEW_COLS = 128
S_SEL_COLS = NSA_KV_HEADS * PAST_LEN + S_NEW_COLS
S_WIN_COLS = NSA_KV_HEADS * WINDOW


def _sample_bucket_table():
    i = np.arange(DEC_SEQ)[:, None]
    pos = PAST_LEN + i
    rows = []
    for h in range(NSA_KV_HEADS):
        col = np.arange(S_CMP_COLS)[None, :]
        c = col % S_CMP
        dist_c = pos - (c * CMP_STRIDE + CMP_BLOCK - 1)
        cmp_ = _bucket_or_masked(dist_c, (col // S_CMP == h) & (c < S_CMP - 1) & (dist_c >= 0))
        col = np.arange(S_SEL_COLS)[None, :]
        key = col // NSA_KV_HEADS
        sel = _bucket_or_masked(pos - key, (col % NSA_KV_HEADS == h) & (key <= pos))
        col = np.arange(S_WIN_COLS)[None, :]
        dist_w = pos - (PAST_LEN - WINDOW + col // NSA_KV_HEADS)
        win = _bucket_or_masked(dist_w, (col % NSA_KV_HEADS == h) & (dist_w < WINDOW))
        rows.append(np.concatenate([cmp_, sel, win], axis=1))
    return np.concatenate(rows, axis=0)


def _bias_table_kernel(tab_ref, idx_ref, o_ref):
    h = pl.program_id(0)
    idx = idx_ref[...]
    out = jnp.full(idx.shape, NEG, F32)
    for b in range(NUM_BUCKETS):
        out = jnp.where(idx == b, tab_ref[b, h], out)
    o_ref[0] = out


def _bias_tables(rel_bias, idx, name):
    r, c = idx.shape
    return pl.pallas_call(
        _bias_table_kernel,
        grid=(NSA_HEADS,),
        in_specs=[pl.BlockSpec(memory_space=pltpu.SMEM),
                  pl.BlockSpec((r, c), lambda h: (0, 0))],
        out_specs=pl.BlockSpec((1, r, c), lambda h: (h, 0, 0)),
        out_shape=jax.ShapeDtypeStruct((NSA_HEADS, r, c), F32),
        compiler_params=_cparams(("arbitrary",)),
        name=name,
    )(rel_bias, jnp.asarray(idx))


def _cover_np(n_cmp_cols, n_blk_cols, delta_of):
    u = np.arange(n_cmp_cols)[:, None]
    j = np.arange(n_blk_cols)[None, :]
    delta = delta_of(u, j)
    shared = np.minimum(CMP_STRIDE * delta + CMP_BLOCK, SEL_BLOCK) - np.maximum(CMP_STRIDE * delta, 0)
    return (np.maximum(shared, 0) / CMP_STRIDE).astype(np.float32)


def _compress(load_rows, n_chunk, pos_term_ref, w1_ref, w2_ref):
    parts = []
    for s0 in range(0, CMP_STRIDE, CMP_GROUP):
        lhs = jnp.concatenate([load_rows(s0 + k).astype(BF16) for k in range(CMP_GROUP)], axis=1)
        w = w1_ref[s0:s0 + CMP_GROUP].reshape(CMP_GROUP * HEAD_DIM, 2 * HEAD_DIM)
        parts.append(_dot(lhs, w))
    while len(parts) > 1:
        parts = [a + b for a, b in zip(parts[0::2], parts[1::2])]
    acc = parts[0] + pos_term_ref[0:1, :]
    nxt = pltpu.roll(acc[:, HEAD_DIM:], n_chunk - 1, axis=0)
    hid = _silu(acc[:, :HEAD_DIM] + nxt)
    return _dot(hid.astype(BF16), w2_ref[...])


def _pos_term_kernel(pos_ref, w1_ref, o_ref):
    halves = []
    for half in range(2):
        acc = jnp.zeros((8, HEAD_DIM), F32)
        for s in range(CMP_STRIDE):
            row = half * CMP_STRIDE + s
            p = jnp.broadcast_to(pos_ref[0, row:row + 1, :], (8, HEAD_DIM)).astype(BF16)
            acc = acc + _dot(p, w1_ref[0, s][:, half * HEAD_DIM:(half + 1) * HEAD_DIM])
        halves.append(acc)
    o_ref[0] = jnp.concatenate(halves, axis=1)


def _pos_term(pos, w1cat):
    return pl.pallas_call(
        _pos_term_kernel,
        grid=(2,),
        in_specs=[pl.BlockSpec((1, CMP_BLOCK, HEAD_DIM), lambda i: (i, 0, 0)),
                  pl.BlockSpec((1, CMP_STRIDE, HEAD_DIM, 2 * HEAD_DIM), lambda i: (i, 0, 0, 0))],
        out_specs=pl.BlockSpec((1, 8, 2 * HEAD_DIM), lambda i: (i, 0, 0)),
        out_shape=jax.ShapeDtypeStruct((2, 8, 2 * HEAD_DIM), F32),
        compiler_params=_cparams(("arbitrary",)),
        name="pos_term",
    )(pos, w1cat)


def _flash_step(carry, s, v_t):
    outs = []
    w = s.shape[1] // FLASH_SPLIT
    v_ext = jnp.concatenate([v_t, jnp.ones((ONES_ROWS, v_t.shape[1]), BF16)], axis=0)
    for c in range(FLASH_SPLIT):
        m, acc = (x[:, c * w:(c + 1) * w] for x in carry)
        sc = s[:, c * w:(c + 1) * w]
        m_new = jnp.maximum(m, jnp.max(sc, axis=0, keepdims=True))
        acc = jnp.exp2(m - m_new) * acc + _dot(v_ext, jnp.exp2(sc - m_new).astype(BF16))
        outs.append((m_new, acc))
    return tuple(jnp.concatenate([o[i] for o in outs], axis=1) for i in range(2))


def _flash_init(cols):
    return (jnp.full((1, cols), M_INIT, F32), jnp.zeros((HEAD_DIM + ONES_ROWS, cols), F32))


def _flash_finish(carry):
    _, acc = carry
    return acc[:HEAD_DIM] / acc[HEAD_DIM:HEAD_DIM + 1]


def _masked_softmax(s, valid):
    s = jnp.where(valid, s, NEG)
    m = jnp.max(s, axis=-1, keepdims=True)
    e = jnp.where(valid, jnp.exp(s - m), 0.0)
    return e / jnp.maximum(jnp.sum(e, axis=-1, keepdims=True), 1e-30)


def _split_dot(x, w):
    hi = x.astype(BF16)
    lo = (x - hi.astype(F32)).astype(BF16)
    return _dot(hi, w) + _dot(lo, w)


def _top_n_mask(score, index, axis):
    sel = jnp.zeros(score.shape, F32)
    for _ in range(SEL_TOPN):
        mx = jnp.max(score, axis=axis, keepdims=True)
        first = jnp.min(jnp.where(score == mx, index, 1e9), axis=axis, keepdims=True)
        hit = index == first
        sel = jnp.where(hit, 1.0, sel)
        score = jnp.where(hit, -jnp.inf, score)
    return sel


def _top_n_mask_by_rank(score, n_cand):
    lane = lax.broadcasted_iota(jnp.int32, score.shape, 1)
    rank = jnp.zeros(score.shape, F32)
    for j in range(n_cand):
        col = score[:, j:j + 1]
        ahead = (col > score) | ((col == score) & (lane > j))
        rank = rank + ahead.astype(F32)
    return ((rank < SEL_TOPN) & (lane < n_cand)).astype(F32)


def _compress_prompt_kernel(rows_ref, pos_ref, w1_ref, w2_ref, o_ref):
    out = _compress(lambda s: rows_ref[pl.ds(s, N_CMP_PAD, stride=CMP_STRIDE), :], N_CMP_PAD,
                    pos_ref.at[0], w1_ref.at[0], w2_ref.at[0])
    real = lax.broadcasted_iota(jnp.int32, (N_CMP_PAD, HEAD_DIM), 0) < N_CMP_PAD - 1
    o_ref[0, 0, 0:SLAB_OFF, :] = jnp.zeros((SLAB_OFF, HEAD_DIM), F32)
    o_ref[0, 0, SLAB_OFF:SLAB_OFF + N_CMP_PAD, :] = jnp.where(real, out, 0.0)
    o_ref[0, 0, SLAB_OFF + N_CMP_PAD:KC_ROWS, :] = jnp.zeros((KC_ROWS - SLAB_OFF - N_CMP_PAD, HEAD_DIM), F32)


def _compress_prompt(z, pos, w1, w2):
    kv_blk = Z_KV // HEAD_DIM
    return pl.pallas_call(
        _compress_prompt_kernel,
        grid=(2, NSA_KV_HEADS),
        in_specs=[pl.BlockSpec((SEQ, HEAD_DIM), lambda i, h: (0, kv_blk + NSA_KV_HEADS * i + h)),
                  pl.BlockSpec((1, 8, 2 * HEAD_DIM), lambda i, h: (i, 0, 0)),
                  pl.BlockSpec((1, CMP_STRIDE, HEAD_DIM, 2 * HEAD_DIM), lambda i, h: (i, 0, 0, 0)),
                  pl.BlockSpec((1, HEAD_DIM, HEAD_DIM), lambda i, h: (i, 0, 0))],
        out_specs=pl.BlockSpec((1, 1, KC_ROWS, HEAD_DIM), lambda i, h: (i, h, 0, 0)),
        out_shape=jax.ShapeDtypeStruct((2, NSA_KV_HEADS, KC_ROWS, HEAD_DIM), F32),
        compiler_params=_cparams(("parallel", "parallel")),
        name="compress_prompt",
    )(z, pos, w1, w2)


def _nsa_prompt_kernel(q_ref, misc_ref, ksel_ref, vsel_rows_ref, kwin_ref, vwin_rows_ref, kc_ref, vc_ref,
                       bias_ref, cover_ref, erel_ref, o_ref, vsel_ref, vwin_ref):
    kvh = pl.program_id(0)
    t = pl.program_id(1)
    cols = NSA_GROUP * Q_TILE

    @pl.when(t == 0)
    def _():
        def body(kt, _):
            rows = pl.ds(pl.multiple_of(kt * Q_TILE, Q_TILE), Q_TILE)
            vsel_ref[:, rows] = vsel_rows_ref[rows, :].astype(F32).T.astype(BF16)
            vwin_ref[:, rows] = vwin_rows_ref[rows, :].astype(F32).T.astype(BF16)
            return 0
        lax.fori_loop(0, N_QT, body, 0)

    q = q_ref[...] * (HEAD_DIM ** -0.5 * LOG2E)
    qt =jnp.concatenate([q[:, g * HEAD_DIM:(g + 1) * HEAD_DIM].T for g in range(NSA_GROUP)], axis=1).astype(BF16)

    def bias_tile(k):
        return bias_ref[0, k * Q_TILE:(k + 1) * Q_TILE, :]
    b_diag, b_prev, b_first = bias_tile(0), bias_tile(1), bias_tile(2)

    def key_tile(ref, kt):
        return ref[pl.ds(pl.multiple_of(kt * Q_TILE, Q_TILE), Q_TILE), :]

    def value_tile(ref, kt):
        return ref[:, pl.ds(pl.multiple_of(kt * Q_TILE, Q_TILE), Q_TILE)]

    n_wt = WINDOW // Q_TILE + 1
    scores, values = [], []
    for w, b_tile in enumerate((b_first,) + (None,) * (n_wt - 3) + (b_prev, b_diag)):
        kt = t - (n_wt - 1) + w
        kc = jnp.maximum(kt, 0)
        s = _dot(key_tile(kwin_ref, kc), qt) + jnp.where(kt >= 0, 0.0, NEG)
        scores.append(s if b_tile is None else s + b_tile)
        values.append(value_tile(vwin_ref, kc))
    carry = _flash_init(cols)
    for lo in range(0, n_wt, WIN_PART_TILES):
        carry = _flash_step(carry, jnp.concatenate(scores[lo:lo + WIN_PART_TILES], axis=0),
                            jnp.concatenate(values[lo:lo + WIN_PART_TILES], axis=1))
    o_win = _flash_finish(carry)

    start = pl.multiple_of(t * (Q_TILE // CMP_STRIDE), 8)
    kslab = kc_ref[0, 0, pl.ds(start, SLAB), :].astype(BF16)
    vslab = vc_ref[0, 0, pl.ds(start, SLAB), :].astype(BF16)
    b_cmp = bias_ref[0, 3 * Q_TILE:3 * Q_TILE + SLAB, :]
    u = lax.broadcasted_iota(jnp.int32, (SLAB, 1), 0)
    valid = (b_cmp > M_INIT) & (u >= SLAB_OFF - (Q_TILE // CMP_STRIDE) * t)
    s = jnp.where(valid, _dot(kslab, qt) + b_cmp, NEG)
    e = jnp.where(valid, jnp.exp2(s - jnp.max(s, axis=0, keepdims=True)), 0.0)
    p_cmp = e / jnp.maximum(jnp.sum(e, axis=0, keepdims=True), 1e-30)
    o_cmp = _dot_tn(vslab, p_cmp.astype(BF16))

    p_sum = p_cmp[:, 0:Q_TILE]
    for g in range(1, NSA_GROUP):
        p_sum = p_sum + p_cmp[:, g * Q_TILE:(g + 1) * Q_TILE]
    hi = p_sum.astype(BF16)
    lo = (p_sum - hi.astype(F32)).astype(BF16)
    imp = _dot(cover_ref[...], hi) + _dot(cover_ref[...], lo)
    jr = lax.broadcasted_iota(jnp.int32, (N_SEL, Q_TILE), 0)
    qi = lax.broadcasted_iota(jnp.int32, (N_SEL, Q_TILE), 1)
    cur = REL0 + (qi >= SEL_BLOCK).astype(jnp.int32)
    first_blk = REL0 - 2 * t
    forced = (jr == first_blk) | (jr == cur) | (jr == cur - 1)
    in_range = (jr <= cur) & (jr >= first_blk)
    score = jnp.where(in_range, jnp.where(forced, FORCED_SCORE, imp), NEG)
    sel = _top_n_mask(score, jr.astype(F32), 0)

    unsel = ((sel - 1.0) * (-NEG)).astype(BF16)
    q_aug = jnp.concatenate([qt, jnp.concatenate([unsel] * NSA_GROUP, axis=1)], axis=0)

    def sweep_step(i, carry, masked, biases=()):
        scores, values = [], []
        for j in range(SWEEP_TILES):
            kt = t - SWEEP_TILES * i - (SWEEP_TILES - 1) + j
            kc = jnp.maximum(kt, 0) if masked else kt
            tile = jnp.concatenate([key_tile(ksel_ref, kc), erel_ref[t - kc]], axis=1)
            s = _dot(tile, q_aug)
            if masked:
                s = s + jnp.where(kt >= 0, 0.0, NEG)
            if j >= SWEEP_TILES - len(biases):
                s = s + biases[j - (SWEEP_TILES - len(biases))]
            scores.append(s)
            values.append(value_tile(vsel_ref, kc))
        per = SWEEP_TILES // SWEEP_PARTS
        for k in range(SWEEP_PARTS):
            carry = _flash_step(carry, jnp.concatenate(scores[k * per:(k + 1) * per], axis=0),
                                jnp.concatenate(values[k * per:(k + 1) * per], axis=1))
        return carry

    n_full = jnp.maximum(t - (SWEEP_TILES - 1), 0) // SWEEP_TILES
    carry = lax.fori_loop(1, n_full + 1, lambda i, c: sweep_step(i, c, False), _flash_init(cols))
    some_left = (t - SWEEP_TILES * (n_full + 1) >= 0).astype(jnp.int32)
    carry = lax.fori_loop(0, some_left, lambda _, c: sweep_step(n_full + 1, c, True), carry)
    o_sel = _flash_finish(sweep_step(0, carry, True, (b_prev, b_diag)))

    gates = jax.nn.sigmoid(misc_ref[...]).T
    n_g = 3 * NSA_GROUP
    gk = jnp.where(kvh == 0, gates, pltpu.roll(gates, gates.shape[0] - n_g, axis=0))[MISC_GATES:MISC_GATES + n_g]
    for g in range(NSA_GROUP):
        c = slice(g * Q_TILE, (g + 1) * Q_TILE)
        o_g = (gk[3 * g:3 * g + 1] * o_cmp[:, c] + gk[3 * g + 1:3 * g + 2] * o_sel[:, c]
               + gk[3 * g + 2:3 * g + 3] * o_win[:, c])
        o_ref[:, g * HEAD_DIM:(g + 1) * HEAD_DIM] = o_g.T


def _nsa_prompt(z, kvb, kcp, bias, cover, erel):
    gw = NSA_GROUP * HEAD_DIM

    def k_spec(j):
        return pl.BlockSpec((SEQ, HEAD_DIM), lambda h, t, j=j: (0, 2 * j + h))
    return pl.pallas_call(
        _nsa_prompt_kernel,
        grid=(NSA_KV_HEADS, N_QT),
        in_specs=[pl.BlockSpec((Q_TILE, gw), lambda h, t: (t, Z_QN // gw + h)),
                  pl.BlockSpec((Q_TILE, 128), lambda h, t: (t, Z_MISC // 128)),
                  k_spec(2), k_spec(3), k_spec(4), k_spec(5),
                  pl.BlockSpec((1, 1, KC_ROWS, HEAD_DIM), lambda h, t: (0, h, 0, 0)),
                  pl.BlockSpec((1, 1, KC_ROWS, HEAD_DIM), lambda h, t: (1, h, 0, 0)),
                  pl.BlockSpec((1,) + bias.shape[1:], lambda h, t: (h, 0, 0)),
                  pl.BlockSpec(cover.shape, lambda h, t: (0, 0)),
                  pl.BlockSpec(erel.shape, lambda h, t: (0, 0, 0))],
        out_specs=pl.BlockSpec((Q_TILE, gw), lambda h, t: (t, h)),
        out_shape=jax.ShapeDtypeStruct((SEQ, NSA_HEADS * HEAD_DIM), F32),
        scratch_shapes=[pltpu.VMEM((HEAD_DIM, SEQ), BF16), pltpu.VMEM((HEAD_DIM, SEQ), BF16)],
        compiler_params=_cparams(("arbitrary", "arbitrary")),
        name="nsa_prompt",
    )(z, z, kvb, kvb, kvb, kvb, kcp, kcp, bias, cover, erel)


S_ROWS = NSA_HEADS * DEC_SEQ
PAGE_ROWS = NSA_KV_HEADS * PAGE_SIZE
NEW_ROWS = NSA_KV_HEADS * DEC_SEQ
CHUNK_ROWS = NSA_KV_HEADS * CMP_STRIDE
CHUNK_PITCH = CHUNK_ROWS + 8


def _nsa_sample_kernel(pt_ref, q_ref, gate_ref, new_ref, kwin_ref, vwin_ref, kcmp_hbm, vcmp_hbm, ksel_hbm, vsel_hbm,
                       posk_ref, w1k_ref, w2k_ref, posv_ref, w1v_ref, w2v_ref, bias_ref, cover_ref, expand_ref,
                       o_ref, kwin_o_ref, vwin_o_ref, kcmp_buf, vcmp_buf, ksel_buf, vsel_buf, sem):
    n_pg = N_PAGES
    b = pl.program_id(0)
    slot = b % 2
    page_chunks = PAGE_ROWS // CHUNK_ROWS
    hbm = (kcmp_hbm, vcmp_hbm, ksel_hbm, vsel_hbm)
    bufs = (kcmp_buf, vcmp_buf, ksel_buf, vsel_buf)

    def page_copy(k, p, page, into):
        if k < 2:
            src = hbm[k].at[pl.ds(page * page_chunks, page_chunks)]
            dst = bufs[k].at[pl.ds(into * S_CMP + p * page_chunks, page_chunks), pl.ds(0, CHUNK_ROWS), :]
        else:
            src = hbm[k].at[pl.ds(page * PAGE_ROWS, PAGE_ROWS)]
            dst = bufs[k].at[pl.ds(into * (n_pg * PAGE_ROWS) + p * PAGE_ROWS, PAGE_ROWS), :]
        return pltpu.make_async_copy(src, dst, sem.at[into, k])

    def gather(seq, into):
        for p in range(n_pg):
            page = pt_ref[seq, p]
            for k in range(4):
                page_copy(k, p, page, into).start()

    @pl.when(b == 0)
    def _():
        gather(0, 0)

    @pl.when(b + 1 < DEC_BATCH)
    def _():
        gather(b + 1, 1 - slot)

    for p in range(n_pg):
        for k in range(4):
            page_copy(k, p, 0, slot).wait()

    def pages(k, p):
        return bufs[k][pl.ds(pl.multiple_of(slot * (n_pg * PAGE_ROWS), PAGE_ROWS) + p * PAGE_ROWS, PAGE_ROWS), :]

    comp = []
    for k, (pos_ref, w1_ref, w2_ref) in enumerate(((posk_ref, w1k_ref, w2k_ref), (posv_ref, w1v_ref, w2v_ref))):
        flat = bufs[k].reshape(2 * S_CMP * CHUNK_PITCH, HEAD_DIM)

        def load_rows(s, flat=flat):
            return jnp.concatenate(
                [flat[pl.ds(slot * (S_CMP * CHUNK_PITCH) + NSA_KV_HEADS * s + h, S_CMP, stride=CHUNK_PITCH), :]
                 for h in range(NSA_KV_HEADS)], axis=0)
        comp.append(_compress(load_rows, S_CMP_COLS, pos_ref, w1_ref, w2_ref).astype(BF16))

    keep = NSA_KV_HEADS * WINDOW - NEW_ROWS
    kwin_o_ref[0:keep, :] = kwin_ref[NEW_ROWS:NSA_KV_HEADS * WINDOW, :]
    kwin_o_ref[keep:keep + NEW_ROWS, :] = new_ref[4]
    vwin_o_ref[0:keep, :] = vwin_ref[NEW_ROWS:NSA_KV_HEADS * WINDOW, :]
    vwin_o_ref[keep:keep + NEW_ROWS, :] = new_ref[5]

    qs = (q_ref[0] * (HEAD_DIM ** -0.5)).astype(BF16)
    o0, o1 = S_CMP_COLS, S_CMP_COLS + S_SEL_COLS
    b_cmp = bias_ref[:, 0:o0]
    b_new = bias_ref[:, o1 - S_NEW_COLS:o1]

    def attend(scores, bias, values):
        s = jnp.concatenate(scores, axis=1) + bias
        e = jnp.exp(s - jnp.max(s, axis=-1, keepdims=True))
        acc = jnp.zeros((S_ROWS, HEAD_DIM), F32)
        c0 = 0
        for v in values:
            acc = acc + _dot(e[:, c0:c0 + v.shape[0]].astype(BF16), v)
            c0 += v.shape[0]
        return acc / jnp.sum(e, axis=-1, keepdims=True)

    p_cmp = _masked_softmax(_dot_nt(qs, comp[0]) + b_cmp, b_cmp > M_INIT)
    o_cmp = _dot(p_cmp.astype(BF16), comp[1])

    imp = _split_dot(p_cmp, cover_ref[...])
    imp = imp + pltpu.roll(imp, 8, axis=0) + pltpu.roll(imp, 16, axis=0) + pltpu.roll(imp, 24, axis=0)
    blk = lax.broadcasted_iota(jnp.int32, (S_ROWS, 128), 1)
    cur = PAST_LEN // SEL_BLOCK
    forced = (blk == 0) | (blk == cur) | (blk == cur - 1)
    score = jnp.where(blk <= cur, jnp.where(forced, FORCED_SCORE, imp), NEG)
    sel = _top_n_mask_by_rank(score, cur + 1).astype(BF16)
    mask_add = (_dot(sel, expand_ref[...]) - 1.0) * (-NEG)

    def new_tile(j):
        pad = jnp.zeros((S_NEW_COLS - NEW_ROWS, HEAD_DIM), F32)
        return jnp.concatenate([new_ref[j], pad], axis=0).astype(BF16)

    nk, nv = new_tile(2), new_tile(3)
    scores = [_dot_nt(qs, pages(2, p).astype(BF16)) for p in range(n_pg)] + [_dot_nt(qs, nk)]
    values = [pages(3, p).astype(BF16) for p in range(n_pg)] + [nv]
    o_sel = attend(scores, bias_ref[:, o0:o1] + mask_add, values)

    nk, nv = new_tile(4), new_tile(5)
    scores = [_dot_nt(qs, kwin_ref[...].astype(BF16)), _dot_nt(qs, nk)]
    bias_w = jnp.concatenate([bias_ref[:, o1:o1 + S_WIN_COLS], b_new], axis=1)
    o_win = attend(scores, bias_w, [vwin_ref[...].astype(BF16), nv])

    g = jax.nn.sigmoid(gate_ref[0])
    o_ref[0] = g[:, 0:1] * o_cmp + g[:, 1:2] * o_sel + g[:, 2:3] * o_win


def _nsa_sample(page_table, q_s, gate_s, new_s, kwin, vwin, pools, cmp_w, bias, cover, expand):
    win_rows = NSA_KV_HEADS * WINDOW

    def full(a):
        return pl.BlockSpec(a.shape, lambda b, pt, n=a.ndim: (0,) * n)

    def per_b(a):
        return pl.BlockSpec((1,) + a.shape[1:], lambda b, pt, n=a.ndim: (b,) + (0,) * (n - 1))
    win_spec = pl.BlockSpec((win_rows, HEAD_DIM), lambda b, pt: (b, 0))
    consts = list(cmp_w) + [bias, cover, expand]
    grid_spec = pltpu.PrefetchScalarGridSpec(
        num_scalar_prefetch=1,
        grid=(DEC_BATCH,),
        in_specs=[per_b(q_s), per_b(gate_s),
                  pl.BlockSpec((N_KV_ARRAYS, NEW_ROWS, HEAD_DIM), lambda b, pt: (0, b, 0)), win_spec, win_spec]
        + [pl.BlockSpec(memory_space=pl.ANY)] * 4 + [full(a) for a in consts],
        out_specs=[per_b(q_s), win_spec, win_spec],
        scratch_shapes=[pltpu.VMEM((2 * S_CMP, CHUNK_PITCH, HEAD_DIM), F32)] * 2
        + [pltpu.VMEM((2 * N_PAGES * PAGE_ROWS, HEAD_DIM), F32)] * 2
        + [pltpu.SemaphoreType.DMA((2, 4))],
    )
    return pl.pallas_call(
        _nsa_sample_kernel,
        grid_spec=grid_spec,
        out_shape=[jax.ShapeDtypeStruct(q_s.shape, F32),
                   jax.ShapeDtypeStruct(kwin.shape, F32),
                   jax.ShapeDtypeStruct(vwin.shape, F32)],
        compiler_params=_cparams(("arbitrary",)),
        name="nsa_sample",
    )(page_table, q_s, gate_s, new_s, kwin, vwin, *pools, *consts)


def _log_decay(a_blk, wa_ref, ba_ref):
    x = _dot(a_blk.astype(BF16), wa_ref[...]) + ba_ref[...]
    return (jnp.minimum(x, 0.0) - jnp.log1p(jnp.exp(-jnp.abs(x)))) * (1.0 / GLA_TAU)


def _segment_cumsum(g, seg):
    pos = lax.broadcasted_iota(jnp.int32, g.shape, 0) % seg
    cum = g
    sh = 1
    while sh < seg:
        cum = cum + jnp.where(pos >= sh, pltpu.roll(cum, sh, axis=0), 0.0)
        sh *= 2
    return cum


def _gla_prompt_kernel(q_ref, k_ref, v_ref, r_ref, a_ref, wa_ref, ba_ref, gn_ref, o_ref, st_o_ref, st_ref,
                       *, n_blk, tb):
    tbi = pl.program_id(0)

    @pl.when(tbi == 0)
    def _():
        st_ref[...] = jnp.zeros_like(st_ref)

    c = GLA_CHUNK
    tril = lax.broadcasted_iota(jnp.int32, (c, c), 0) >= lax.broadcasted_iota(jnp.int32, (c, c), 1)
    cum = _segment_cumsum(_log_decay(a_ref[...], wa_ref, ba_ref), c)
    q = q_ref[...] * (GLA_DK ** -0.5)
    k = k_ref[...]
    v = v_ref[...].astype(BF16)
    qe = (q * jnp.exp(cum)).astype(BF16)
    kd = (k * jnp.exp(-cum)).astype(BF16)
    heads = [(slice(h * GLA_DK, (h + 1) * GLA_DK), slice(h * GLA_DV, (h + 1) * GLA_DV)) for h in range(GLA_HEADS)]
    sts = [st_ref[h] for h in range(GLA_HEADS)]
    outs = [[] for _ in range(GLA_HEADS)]
    for ci in range(tb // c):
        r = slice(ci * c, (ci + 1) * c)
        last = cum[ci * c + c - 1:ci * c + c, :]
        kl = (k[r] * jnp.exp(last - cum[r])).astype(BF16)
        decay = jnp.exp(last)
        for h, (dk, dv) in enumerate(heads):
            att = jnp.where(tril, _dot_nt(qe[r, dk], kd[r, dk]), 0.0)
            outs[h].append(_dot_nt(qe[r, dk], sts[h].astype(BF16)) + _dot(att.astype(BF16), v[r, dv]))
            sts[h] = decay[:, dk] * sts[h] + _dot_tn(v[r, dv], kl[:, dk])
    for h, (dk, dv) in enumerate(heads):
        st_ref[h] = sts[h]
        o = jnp.concatenate(outs[h], axis=0)
        o_ref[:, dv] = _rms(o, gn_ref[...]) * _silu(r_ref[:, dv])

    @pl.when(tbi == n_blk - 1)
    def _():
        st_o_ref[...] = st_ref[...]


def _gla_prompt(z, wa, ba, gn, tb=256):
    n_blk = SEQ // tb
    hk, hv = GLA_HEADS * GLA_DK, GLA_HEADS * GLA_DV
    st_shape = (GLA_HEADS, GLA_DV, GLA_DK)
    return pl.pallas_call(
        functools.partial(_gla_prompt_kernel, n_blk=n_blk, tb=tb),
        grid=(n_blk,),
        in_specs=[pl.BlockSpec((tb, hk), lambda i: (i, Z_QG // hk)),
                  pl.BlockSpec((tb, hk), lambda i: (i, Z_KG // hk)),
                  pl.BlockSpec((tb, hv), lambda i: (i, Z_VG // hv)),
                  pl.BlockSpec((tb, hv), lambda i: (i, Z_RG // hv)),
                  pl.BlockSpec((tb, 128), lambda i: (i, Z_MISC // 128)),
                  pl.BlockSpec((128, hk), lambda i: (0, 0)),
                  pl.BlockSpec((1, hk), lambda i: (0, 0)),
                  pl.BlockSpec((1, GLA_DV), lambda i: (0, 0))],
        out_specs=[pl.BlockSpec((tb, hv), lambda i: (i, 0)),
                   pl.BlockSpec(st_shape, lambda i: (0, 0, 0))],
        out_shape=[jax.ShapeDtypeStruct((SEQ, hv), F32), jax.ShapeDtypeStruct(st_shape, F32)],
        scratch_shapes=[pltpu.VMEM(st_shape, F32)],
        compiler_params=_cparams(("arbitrary",)),
        name="gla_prompt",
    )(z, z, z, z, z, wa, ba, gn)


GS_B = 4


def _gla_sample_kernel(q_ref, k_ref, v_ref, r_ref, a_ref, wa_ref, ba_ref, gn_ref, s_ref, o_ref, s_o_ref):
    rows = GS_B * DEC_SEQ
    a = a_ref[...]
    ri = lax.broadcasted_iota(jnp.int32, (rows, rows), 0)
    ci = lax.broadcasted_iota(jnp.int32, (rows, rows), 1)
    same_causal = (ri // DEC_SEQ == ci // DEC_SEQ) & (ri >= ci)
    row_b = lax.broadcasted_iota(jnp.int32, (rows, 1), 0) // DEC_SEQ
    ones = jnp.ones((rows, 128), BF16)
    for h in range(GLA_HEADS):
        dk = slice(h * GLA_DK, (h + 1) * GLA_DK)
        dv = slice(h * GLA_DV, (h + 1) * GLA_DV)
        g = _log_decay(a, wa_ref.at[:, dk], ba_ref.at[:, dk])
        cum = _segment_cumsum(g, DEC_SEQ)
        q = q_ref[:, dk] * (GLA_DK ** -0.5)
        k = k_ref[:, dk]
        v = v_ref[:, dv].astype(BF16)
        qe = (q * jnp.exp(cum)).astype(BF16)
        kd = (k * jnp.exp(-cum)).astype(BF16)
        att = jnp.where(same_causal, _dot_nt(qe, kd), 0.0)
        o = _dot(att.astype(BF16), v)
        for b in range(GS_B):
            mine = row_b == b
            last = cum[b * DEC_SEQ + DEC_SEQ - 1:(b + 1) * DEC_SEQ, :]
            s = s_ref[b, h]
            o = o + jnp.where(mine, _dot(qe, s.astype(BF16)), 0.0)
            kl = jnp.where(mine, k * jnp.exp(last - cum), 0.0)
            hi = jnp.where(mine, g, 0.0).astype(BF16)
            lo = (jnp.where(mine, g, 0.0) - hi.astype(F32)).astype(BF16)
            last_col = (_dot_tn(hi, ones) + _dot_tn(lo, ones))[:, 0:1]
            s_o_ref[b, h] = jnp.exp(last_col) * s + _dot_tn(kl.astype(BF16), v)
        o_ref[:, dv] = _rms(o, gn_ref[...]) * _silu(r_ref[:, dv])


def _gla_sample(zs, state, wa, ba, gn):
    rows = GS_B * DEC_SEQ
    n = DEC_BATCH * DEC_SEQ
    hk, hv = GLA_HEADS * GLA_DK, GLA_HEADS * GLA_DV
    st_spec = pl.BlockSpec((GS_B, GLA_HEADS, GLA_DK, GLA_DV), lambda i: (i, 0, 0, 0))
    return pl.pallas_call(
        _gla_sample_kernel,
        grid=(DEC_BATCH // GS_B,),
        in_specs=[pl.BlockSpec((rows, hk), lambda i: (i, Z_QG // hk)),
                  pl.BlockSpec((rows, hk), lambda i: (i, Z_KG // hk)),
                  pl.BlockSpec((rows, hv), lambda i: (i, Z_VG // hv)),
                  pl.BlockSpec((rows, hv), lambda i: (i, Z_RG // hv)),
                  pl.BlockSpec((rows, 128), lambda i: (i, Z_MISC // 128)),
                  pl.BlockSpec((128, hk), lambda i: (0, 0)),
                  pl.BlockSpec((1, hk), lambda i: (0, 0)),
                  pl.BlockSpec((1, GLA_DV), lambda i: (0, 0)),
                  st_spec],
        out_specs=[pl.BlockSpec((rows, hv), lambda i: (i, 0)), st_spec],
        out_shape=[jax.ShapeDtypeStruct((n, hv), F32), jax.ShapeDtypeStruct(state.shape, F32)],
        compiler_params=_cparams(("parallel",)),
        name="gla_sample",
    )(zs, zs, zs, zs, zs, wa, ba, gn, state)


def _softmax_rows(s):
    m = jnp.max(s, axis=-1, keepdims=True)
    e = jnp.exp(s - m)
    return e / jnp.sum(e, axis=-1, keepdims=True)


def _mem_prompt_kernel(q_ref, k_ref, v_ref, o_ref):
    for h in range(MEM_HEADS):
        d = slice(h * MEM_HEAD_DIM, (h + 1) * MEM_HEAD_DIM)
        q = (q_ref[:, d] * (MEM_HEAD_DIM ** -0.5)).astype(BF16)
        p = _softmax_rows(_dot_nt(q, k_ref[:, d].astype(BF16)))
        o_ref[:, d] = _dot(p.astype(BF16), v_ref[:, d].astype(BF16))


def _mem_prompt(qm, memkv, tq=256):
    w = MEM_HEADS * MEM_HEAD_DIM
    return pl.pallas_call(
        _mem_prompt_kernel,
        grid=(SEQ // tq,),
        in_specs=[pl.BlockSpec((tq, w), lambda i: (i, 0)),
                  pl.BlockSpec((MEM_TOKENS, w), lambda i: (0, 0)),
                  pl.BlockSpec((MEM_TOKENS, w), lambda i: (0, 1))],
        out_specs=pl.BlockSpec((tq, w), lambda i: (i, 0)),
        out_shape=jax.ShapeDtypeStruct((SEQ, w), F32),
        compiler_params=_cparams(("parallel",)),
        name="mem_prompt",
    )(qm, memkv, memkv)


def _mem_sample_kernel(q_ref, k_ref, v_ref, o_ref):
    rows = MEM_HEADS * DEC_SEQ
    cols = MEM_HEADS * MEM_TOKENS
    row_h = lax.broadcasted_iota(jnp.int32, (rows, cols), 0) // DEC_SEQ
    col_h = lax.broadcasted_iota(jnp.int32, (rows, cols), 1) % MEM_HEADS
    for b in range(MS_B):
        kv_rows = slice(b * cols, (b + 1) * cols)
        q = (q_ref[b] * (MEM_HEAD_DIM ** -0.5)).astype(BF16)
        s = jnp.where(row_h == col_h, _dot_nt(q, k_ref[kv_rows, :].astype(BF16)), NEG)
        o_ref[b] = _dot(_softmax_rows(s).astype(BF16), v_ref[kv_rows, :].astype(BF16))


MS_B = 4


def _mem_sample(q_s, k_mem, v_mem):
    rows = MEM_HEADS * DEC_SEQ
    kv_spec = pl.BlockSpec((MS_B * MEM_HEADS * MEM_TOKENS, MEM_HEAD_DIM), lambda b: (b, 0))
    return pl.pallas_call(
        _mem_sample_kernel,
        grid=(DEC_BATCH // MS_B,),
        in_specs=[pl.BlockSpec((MS_B, rows, MEM_HEAD_DIM), lambda b: (b, 0, 0)), kv_spec, kv_spec],
        out_specs=pl.BlockSpec((MS_B, rows, MEM_HEAD_DIM), lambda b: (b, 0, 0)),
        out_shape=jax.ShapeDtypeStruct((DEC_BATCH, rows, MEM_HEAD_DIM), F32),
        compiler_params=_cparams(("parallel",)),
        name="mem_sample",
    )(q_s, k_mem, v_mem)


W_IN_SEGMENTS = ((0, 1024, Z_QN), (1024, 1536, Z_KV), (2560, 24, Z_MISC + MISC_GATES), (2584, 512, Z_QG),
                 (3096, 512, Z_KG), (3608, 1024, Z_VG), (4632, 1024, Z_RG), (5656, 16, Z_MISC + MISC_A))
W_IN_COLS = 5672


def _permute_w_in_kernel(w_ref, o_ref):
    o_ref[:, Z_MISC:Z_W] = jnp.zeros((o_ref.shape[0], Z_W - Z_MISC), BF16)
    for src, width, dst in W_IN_SEGMENTS:
        o_ref[:, dst:dst + width] = w_ref[:, src:src + width].astype(BF16)


def _permute_w_in(w_in, tr=256):
    d = w_in.shape[0]
    return pl.pallas_call(
        _permute_w_in_kernel,
        grid=(d // tr,),
        in_specs=[pl.BlockSpec((tr, W_IN_COLS), lambda i: (i, 0))],
        out_specs=pl.BlockSpec((tr, Z_W), lambda i: (i, 0)),
        out_shape=jax.ShapeDtypeStruct((d, Z_W), BF16),
        compiler_params=_cparams(("parallel",)),
        name="permute_w_in",
    )(w_in)


def kernel(x_prompt, x_sample, mem_prompt, cache_k_cmp, cache_v_cmp, cache_k_sel, cache_v_sel, cache_k_win,
           cache_v_win, state_gla, cache_k_mem, cache_v_mem, page_table, norm_ffn1, ffn1_w_gate, ffn1_w_up,
           ffn1_w_down, norm_mix, w_in, w_out, cmp_pos_k, cmp_w1_k, cmp_w2_k, cmp_pos_v, cmp_w1_v, cmp_w2_v,
           rel_bias, gla_w_a2, gla_b_a, gla_norm, norm_mem, norm_mem_src, w_mem_q, w_mem_k, w_mem_v, w_mem_o,
           norm_ffn2, ffn2_w_gate, ffn2_w_up, ffn2_w_down, norm_final):
    bf = lambda a: a.astype(BF16)
    row = lambda a: a.reshape(1, -1)
    nb, ns = DEC_BATCH, DEC_SEQ
    kvw = NSA_KV_HEADS * HEAD_DIM

    h1, h1n = _ffn([x_prompt[0], x_sample.reshape(nb * ns, D_MODEL)], row(norm_ffn1[0]), bf(ffn1_w_gate[0]),
                   bf(ffn1_w_up[0]), bf(ffn1_w_down[0]), row(norm_mix[0]), False)
    z, kvb, new_s, *new_p = _proj_in(h1n, _permute_w_in(w_in[0]))
    rows_p = [a.reshape(1, 1, SEQ, NSA_KV_HEADS, HEAD_DIM) for a in new_p]
    rows_s = [new_s[j].reshape(1, nb, ns, NSA_KV_HEADS, HEAD_DIM) for j in range(4)]

    tab_p = _bias_tables(rel_bias, _prompt_bucket_table(), "bias_prompt")
    far = rel_bias[NUM_BUCKETS - 1][:, None, None]
    near = tab_p[:, :3 * Q_TILE]
    tab_p = jnp.concatenate([jnp.where(near > M_INIT, near - far, NEG), tab_p[:, 3 * Q_TILE:]], axis=1)
    tab_p = jnp.where(tab_p > M_INIT, tab_p * LOG2E, NEG)
    tab_p = tab_p.reshape(NSA_KV_HEADS, NSA_GROUP, -1, Q_TILE).transpose(0, 2, 1, 3)
    tab_p = tab_p.reshape(NSA_KV_HEADS, -1, NSA_GROUP * Q_TILE)
    tab_s = _bias_tables(rel_bias, _sample_bucket_table(), "bias_sample")
    tab_s = tab_s.reshape(NSA_KV_HEADS, NSA_GROUP, NSA_KV_HEADS, ns, -1)
    tab_s = jnp.stack([tab_s[h, :, h] for h in range(NSA_KV_HEADS)], axis=1).reshape(S_ROWS, -1)

    cmp_w1 = bf(jnp.stack([cmp_w1_k[0], cmp_w1_v[0]]))
    cmp_w1 = jnp.concatenate([cmp_w1[:, :CMP_STRIDE], cmp_w1[:, CMP_STRIDE:]], axis=-1)
    cmp_w2 = bf(jnp.stack([cmp_w2_k[0], cmp_w2_v[0]]))
    cmp_pos = _pos_term(jnp.stack([cmp_pos_k[0], cmp_pos_v[0]]), cmp_w1)
    kcp = _compress_prompt(z, cmp_pos, cmp_w1, cmp_w2)
    cover_p = jnp.asarray(_cover_np(SLAB, N_SEL, lambda u, j: u - 4 * j - SLAB_OFF + 4 * REL0).T, BF16)
    erel = (np.arange(N_SEL)[None, None, :] == REL0 - 2 * np.arange(N_QT)[:, None, None]
            + (np.arange(Q_TILE)[None, :, None] >= SEL_BLOCK))
    o_nsa_p = _nsa_prompt(z, kvb, kcp, tab_p, cover_p, jnp.asarray(erel, BF16))

    zs = z[SEQ:]
    q_s = zs[:, Z_QN:Z_QN + NSA_HEADS * HEAD_DIM].reshape(nb, ns, NSA_KV_HEADS, NSA_GROUP, HEAD_DIM)
    q_s = q_s.transpose(0, 3, 2, 1, 4).reshape(nb, S_ROWS, HEAD_DIM)
    gate_s = zs[:, Z_MISC + MISC_GATES:Z_MISC + MISC_GATES + 3 * NSA_HEADS]
    gate_s = gate_s.reshape(nb, ns, NSA_KV_HEADS, NSA_GROUP, 3).transpose(0, 3, 2, 1, 4).reshape(nb, S_ROWS, 3)
    as_rows = lambda c: c.reshape(-1, HEAD_DIM)
    pools = ([c.reshape(-1, CHUNK_ROWS, HEAD_DIM) for c in (cache_k_cmp, cache_v_cmp)]
             + [as_rows(c) for c in (cache_k_sel, cache_v_sel)])
    cover_s = _cover_np(S_CMP, 128, lambda c, j: c - 4 * j)
    cover_s = jnp.asarray(np.concatenate([cover_s] * NSA_KV_HEADS, axis=0), BF16)
    expand_s = jnp.asarray(np.arange(128)[:, None]
                           == (np.arange(S_SEL_COLS)[None, :] // (NSA_KV_HEADS * SEL_BLOCK)), BF16)
    o_nsa_s, kwin_s, vwin_s = _nsa_sample(
        page_table, q_s, gate_s, new_s, as_rows(cache_k_win), as_rows(cache_v_win), pools,
        (cmp_pos[0], cmp_w1[0], cmp_w2[0], cmp_pos[1], cmp_w1[1], cmp_w2[1]),
        tab_s, cover_s, expand_s)
    o_nsa_s = o_nsa_s.reshape(nb, NSA_GROUP, NSA_KV_HEADS, ns, HEAD_DIM).transpose(0, 3, 2, 1, 4)
    o_nsa_s = o_nsa_s.reshape(nb * ns, NSA_HEADS * HEAD_DIM)

    wa = bf(jnp.zeros((128, GLA_HEADS * GLA_DK), F32).at[MISC_A:MISC_A + GLA_RANK].set(gla_w_a2[0]))
    ba, gn = row(gla_b_a[0]), row(gla_norm[0])
    o_gla_p, st_p = _gla_prompt(z, wa, ba, gn)
    o_gla_s, st_s = _gla_sample(zs, state_gla[0], wa, ba, gn)

    half = NSA_HEADS * HEAD_DIM
    h2 = _matmul_res(h1, [(o_nsa_p, o_nsa_s), (o_gla_p, o_gla_s)], [bf(w_out[0][:half]), bf(w_out[0][half:])],
                     512, "proj_out")

    memkv = _norm_matmul(mem_prompt[0], row(norm_mem_src[0]),
                         bf(jnp.concatenate([w_mem_k[0], w_mem_v[0]], axis=1)), MEM_TOKENS, 512, "mem_kv")
    mw = MEM_HEADS * MEM_HEAD_DIM
    qm = _norm_matmul(h2, row(norm_mem[0]), bf(w_mem_q[0]), 512, mw, "mem_q")
    om_p = _mem_prompt(qm, memkv)
    qm_s = qm[SEQ:].reshape(nb, ns, MEM_HEADS, MEM_HEAD_DIM).transpose(0, 2, 1, 3)
    om_s = _mem_sample(qm_s.reshape(nb, MEM_HEADS * ns, MEM_HEAD_DIM),
                       as_rows(cache_k_mem), as_rows(cache_v_mem))
    om_s = om_s.reshape(nb, MEM_HEADS, ns, MEM_HEAD_DIM).transpose(0, 2, 1, 3).reshape(nb * ns, mw)
    h3 = _matmul_res(h2, [(om_p, om_s)], [bf(w_mem_o[0])], 512, "mem_out")

    y_p, y_s = _ffn([h3], row(norm_ffn2[0]), bf(ffn2_w_gate[0]), bf(ffn2_w_up[0]), bf(ffn2_w_down[0]),
                    row(norm_final), True)

    mem_shape = (1, 1, MEM_TOKENS, MEM_HEADS, MEM_HEAD_DIM)
    win_shape = (1, nb, WINDOW, NSA_KV_HEADS, HEAD_DIM)
    return (y_p.reshape(1, SEQ, D_MODEL), y_s.reshape(nb, ns, D_MODEL),
            rows_p[0], rows_p[1], rows_p[2], rows_p[3],
            rows_p[4][:, :, SEQ - WINDOW:], rows_p[5][:, :, SEQ - WINDOW:],
            st_p.transpose(0, 2, 1).reshape(1, 1, GLA_HEADS, GLA_DK, GLA_DV),
            memkv[:, :mw].reshape(mem_shape), memkv[:, mw:].reshape(mem_shape),
            rows_s[0], rows_s[1], rows_s[2], rows_s[3],
            kwin_s.reshape(win_shape), vwin_s.reshape(win_shape),
            st_s.reshape(1, nb, GLA_HEADS, GLA_DK, GLA_DV))
```

```python
import functools
import math

import numpy as np
import jax
import jax.numpy as jnp
from jax import lax
from jax.experimental import pallas as pl
from jax.experimental.pallas import tpu as pltpu

F32 = jnp.float32
BF16 = jnp.bfloat16

D_MODEL = 2048
SEQ = 8192
DEC_BATCH = 128
DEC_SEQ = 4
PAST_LEN = 2048
PAGE_SIZE = 128
N_PAGES = PAST_LEN // PAGE_SIZE
HEAD_DIM = 128
NSA_HEADS = 8
NSA_KV_HEADS = 2
NSA_GROUP = 4
CMP_BLOCK = 32
CMP_STRIDE = 16
SEL_BLOCK = 64
SEL_TOPN = 16
WINDOW = 512
FORCED_SCORE = 1.0e4
GLA_HEADS = 4
GLA_DV = 256
GLA_DK = 128
GLA_RANK = 16
GLA_TAU = 16.0
GLA_CHUNK = 32
MEM_TOKENS = 256
MEM_HEADS = 4
MEM_HEAD_DIM = 128
D_FF = 5632
NUM_BUCKETS = 32
MAX_DISTANCE = 128
RMS_EPS = 1e-6

N_TOK = SEQ + DEC_BATCH * DEC_SEQ
Z_KV, Z_QG, Z_VG, Z_RG, Z_QN, Z_KG, Z_MISC = 0, 1536, 2048, 3072, 4096, 5120, 5632
Z_W = 5760
MISC_GATES, MISC_A = 0, 24

NEG = -1e30
LOG2E = math.log2(math.e)
ONES_ROWS = 16
M_INIT = -1e29

Q_TILE = 128
N_QT = SEQ // Q_TILE
N_CMP_PAD = SEQ // CMP_STRIDE
SLAB = N_CMP_PAD + 128
SLAB_OFF = SLAB - 16
KC_ROWS = SLAB_OFF + N_CMP_PAD + 16
N_SEL = SEQ // SEL_BLOCK
REL0 = N_SEL - 2
CMP_GROUP = 4
SWEEP_TILES = 8
SWEEP_PARTS = 4
WIN_PART_TILES = 2
FLASH_SPLIT = 1

VMEM_LIMIT = 56 * 1024 * 1024


def _cparams(sem):
    return pltpu.CompilerParams(dimension_semantics=sem, vmem_limit_bytes=VMEM_LIMIT)


def _dot(a, b):
    return jnp.dot(a, b, preferred_element_type=F32)


def _dot_nt(a, b):
    return lax.dot_general(a, b, (((1,), (1,)), ((), ())), preferred_element_type=F32)


def _dot_tn(a, b):
    return lax.dot_general(a, b, (((0,), (0,)), ((), ())), preferred_element_type=F32)


def _rms(x, g):
    return x * lax.rsqrt(jnp.mean(x * x, axis=-1, keepdims=True) + RMS_EPS) * g


def _silu(x):
    return x * jax.nn.sigmoid(x)


def _ffn_kernel(*refs, n_ff, n_first, two_in, two_out):
    refs = list(refs)
    x_refs = [refs.pop(0) for _ in range(2 if two_in else 1)]
    g_ref, wg_ref, wu_ref, wd_ref, gf_ref = refs[:5]
    o_refs = refs[5:7]
    xn_ref, acc_ref = refs[-2:]
    i = pl.program_id(0)
    j = pl.program_id(1)

    def x_tile():
        return jnp.where(i < n_first, x_refs[0][...], x_refs[1][...]) if two_in else x_refs[0][...]

    @pl.when(j == 0)
    def _():
        xn_ref[...] = _rms(x_tile(), g_ref[...]).astype(BF16)
        acc_ref[...] = jnp.zeros_like(acc_ref)

    xn = xn_ref[...]
    hid = _silu(_dot(xn, wg_ref[...])) * _dot(xn, wu_ref[...])
    acc_ref[...] += _dot(hid.astype(BF16), wd_ref[...])

    @pl.when(j == n_ff - 1)
    def _():
        h = x_tile() + 0.5 * acc_ref[...]
        if two_out:
            h = _rms(h, gf_ref[...])

            @pl.when(i < n_first)
            def _():
                o_refs[0][...] = h

            @pl.when(i >= n_first)
            def _():
                o_refs[1][...] = h
        else:
            o_refs[0][...] = h
            o_refs[1][...] = _rms(h, gf_ref[...]).astype(BF16)


FFN_TM = 512
FFN_TF = 512


def _ffn(xs, g, wg, wu, wd, gf, split_out):
    tm, tf, d = FFN_TM, FFN_TF, D_MODEL
    n_ff = D_FF // tf
    n_first = SEQ // tm
    two_in = len(xs) == 2

    def first(i, j):
        return (jnp.minimum(i, n_first - 1), 0)

    def second(i, j):
        return (jnp.maximum(i - n_first, 0), 0)
    whole = pl.BlockSpec((tm, d), lambda i, j: (i, 0))
    pair = [pl.BlockSpec((tm, d), first), pl.BlockSpec((tm, d), second)]
    n_s = DEC_BATCH * DEC_SEQ
    return pl.pallas_call(
        functools.partial(_ffn_kernel, n_ff=n_ff, n_first=n_first, two_in=two_in, two_out=split_out),
        grid=(N_TOK // tm, n_ff),
        in_specs=(pair if two_in else [whole])
        + [pl.BlockSpec((1, d), lambda i, j: (0, 0)),
           pl.BlockSpec((d, tf), lambda i, j: (0, j)),
           pl.BlockSpec((d, tf), lambda i, j: (0, j)),
           pl.BlockSpec((tf, d), lambda i, j: (j, 0)),
           pl.BlockSpec((1, d), lambda i, j: (0, 0))],
        out_specs=pair if split_out else [whole, whole],
        out_shape=([jax.ShapeDtypeStruct((SEQ, d), F32), jax.ShapeDtypeStruct((n_s, d), F32)] if split_out
                   else [jax.ShapeDtypeStruct((N_TOK, d), F32), jax.ShapeDtypeStruct((N_TOK, d), BF16)]),
        scratch_shapes=[pltpu.VMEM((tm, d), BF16), pltpu.VMEM((tm, d), F32)],
        compiler_params=_cparams(("arbitrary", "arbitrary")),
        name="ffn",
    )(*xs, g, wg, wu, wd, gf)


def _norm_matmul_kernel(x_ref, g_ref, w_ref, o_ref, xn_ref):
    @pl.when(pl.program_id(1) == 0)
    def _():
        xn_ref[...] = _rms(x_ref[...], g_ref[...]).astype(BF16)

    o_ref[...] = _dot(xn_ref[...], w_ref[...])


def _norm_matmul(x, g, w, tm, tn, name):
    n, d = x.shape
    dout = w.shape[1]
    return pl.pallas_call(
        _norm_matmul_kernel,
        grid=(n // tm, dout // tn),
        in_specs=[pl.BlockSpec((tm, d), lambda i, j: (i, 0)),
                  pl.BlockSpec((1, d), lambda i, j: (0, 0)),
                  pl.BlockSpec((d, tn), lambda i, j: (0, j))],
        out_specs=pl.BlockSpec((tm, tn), lambda i, j: (i, j)),
        out_shape=jax.ShapeDtypeStruct((n, dout), F32),
        scratch_shapes=[pltpu.VMEM((tm, d), BF16)],
        compiler_params=_cparams(("parallel", "arbitrary")),
        name=name,
    )(x, g, w)


N_KV_ARRAYS = 6
KV_W = NSA_KV_HEADS * HEAD_DIM


def _proj_in_kernel(x_ref, w_ref, o_ref, kvb_ref, rows_s_ref, *rows_p_refs, n_first):
    j, i = pl.program_id(0), pl.program_id(1)
    res = _dot(x_ref[...], w_ref[...])
    o_ref[...] = res
    tm = res.shape[0]

    @pl.when(j == 0)
    def _():
        kvb_ref[...] = res[:, Z_KV:Z_KV + N_KV_ARRAYS * KV_W].astype(BF16)

        def head_rows(a, h):
            c0 = Z_KV + a * KV_W + h * HEAD_DIM
            return res[:, c0:c0 + HEAD_DIM]

        @pl.when(i < n_first)
        def _():
            for a in range(N_KV_ARRAYS):
                for h in range(NSA_KV_HEADS):
                    rows_p_refs[a][pl.ds(h, tm, stride=NSA_KV_HEADS), :] = head_rows(a, h)

        @pl.when(i >= n_first)
        def _():
            for a in range(N_KV_ARRAYS):
                for h in range(NSA_KV_HEADS):
                    rows_s_ref.at[a][pl.ds(h, tm, stride=NSA_KV_HEADS), :] = head_rows(a, h)


def _proj_in(x, w, tm=512, tn=1920):
    n, d = x.shape
    n_i, n_first = n // tm, SEQ // tm
    n_s = n - SEQ
    assert Z_KV == 0 and N_KV_ARRAYS * KV_W <= tn and n_s == tm

    def once(i_of):
        return lambda j, i: jnp.where(j == 0, i_of(i), i_of(n_i - 1))
    p_idx = once(lambda i: jnp.minimum(i, n_first - 1))
    return pl.pallas_call(
        functools.partial(_proj_in_kernel, n_first=n_first),
        grid=(Z_W // tn, n_i),
        in_specs=[pl.BlockSpec((tm, d), lambda j, i: (i, 0)),
                  pl.BlockSpec((d, tn), lambda j, i: (0, j))],
        out_specs=[pl.BlockSpec((tm, tn), lambda j, i: (i, j)),
                   pl.BlockSpec((tm, N_KV_ARRAYS * KV_W), lambda j, i: (once(lambda i: i)(j, i), 0)),
                   pl.BlockSpec((N_KV_ARRAYS, NSA_KV_HEADS * n_s, HEAD_DIM), lambda j, i: (0, 0, 0))]
        + [pl.BlockSpec((NSA_KV_HEADS * tm, HEAD_DIM), lambda j, i: (p_idx(j, i), 0))] * N_KV_ARRAYS,
        out_shape=[jax.ShapeDtypeStruct((n, Z_W), F32),
                   jax.ShapeDtypeStruct((n, N_KV_ARRAYS * KV_W), BF16),
                   jax.ShapeDtypeStruct((N_KV_ARRAYS, NSA_KV_HEADS * n_s, HEAD_DIM), F32)]
        + [jax.ShapeDtypeStruct((NSA_KV_HEADS * SEQ, HEAD_DIM), F32)] * N_KV_ARRAYS,
        compiler_params=_cparams(("arbitrary", "arbitrary")),
        name="proj_in",
    )(x, w)


def _matmul_res_kernel(*refs, n_lhs, n_first):
    res_ref = refs[0]
    o_ref = refs[1 + 3 * n_lhs]
    first = pl.program_id(0) < n_first
    acc = res_ref[...]
    for k in range(n_lhs):
        lhs = jnp.where(first, refs[1 + 2 * k][...], refs[2 + 2 * k][...])
        acc = acc + _dot(lhs.astype(BF16), refs[1 + 2 * n_lhs + k][...])
    o_ref[...] = acc


def _matmul_res(res, lhs, ws, tm, name):
    n, d = res.shape
    n_first = SEQ // tm

    def pair_specs(width):
        return [pl.BlockSpec((tm, width), lambda i: (jnp.minimum(i, n_first - 1), 0)),
                pl.BlockSpec((tm, width), lambda i: (jnp.maximum(i - n_first, 0), 0))]
    return pl.pallas_call(
        functools.partial(_matmul_res_kernel, n_lhs=len(lhs), n_first=n_first),
        grid=(n // tm,),
        in_specs=([pl.BlockSpec((tm, d), lambda i: (i, 0))]
                  + [spec for a, _ in lhs for spec in pair_specs(a.shape[1])]
                  + [pl.BlockSpec(w.shape, lambda i: (0, 0)) for w in ws]),
        out_specs=pl.BlockSpec((tm, d), lambda i: (i, 0)),
        out_shape=jax.ShapeDtypeStruct((n, d), F32),
        compiler_params=_cparams(("arbitrary",)),
        name=name,
    )(res, *[a for pair in lhs for a in pair], *ws)


def _rel_bucket_np(dist):
    n = np.maximum(dist, 0)
    exact = NUM_BUCKETS // 2
    nf = np.maximum(n, 1).astype(np.float32)
    large = exact + (np.log(nf / np.float32(exact)) / np.float32(math.log(MAX_DISTANCE / exact))
                     * np.float32(NUM_BUCKETS - exact)).astype(np.int32)
    return np.where(n < exact, n, np.minimum(large, NUM_BUCKETS - 1)).astype(np.int32)


def _bucket_or_masked(dist, valid):
    return np.where(valid, _rel_bucket_np(dist), -1).astype(np.int32)


def _prompt_bucket_table():
    i = np.arange(Q_TILE)[:, None]
    j = np.arange(Q_TILE)[None, :]
    u = np.arange(SLAB)[None, :]
    dist_c = i - CMP_STRIDE * u + (CMP_STRIDE * SLAB_OFF - (CMP_BLOCK - 1))
    diag = _bucket_or_masked(i - j, i - j >= 0)
    prev = _bucket_or_masked(Q_TILE + i - j, np.ones((Q_TILE, Q_TILE), bool))
    first = _bucket_or_masked(WINDOW + i - j, j > i)
    cmp_ = _bucket_or_masked(dist_c, dist_c >= 0)
    return np.concatenate([diag, prev, first, cmp_], axis=1).T


S_CMP = PAST_LEN // CMP_STRIDE
S_CMP_COLS = NSA_KV_HEADS * S_CMP
S_NEW_COLS = 128
S_SEL_COLS = NSA_KV_HEADS * PAST_LEN + S_NEW_COLS
S_WIN_COLS = NSA_KV_HEADS * WINDOW


def _sample_bucket_table():
    i = np.arange(DEC_SEQ)[:, None]
    pos = PAST_LEN + i
    rows = []
    for h in range(NSA_KV_HEADS):
        col = np.arange(S_CMP_COLS)[None, :]
        c = col % S_CMP
        dist_c = pos - (c * CMP_STRIDE + CMP_BLOCK - 1)
        cmp_ = _bucket_or_masked(dist_c, (col // S_CMP == h) & (c < S_CMP - 1) & (dist_c >= 0))
        col = np.arange(S_SEL_COLS)[None, :]
        key = col // NSA_KV_HEADS
        sel = _bucket_or_masked(pos - key, (col % NSA_KV_HEADS == h) & (key <= pos))
        col = np.arange(S_WIN_COLS)[None, :]
        dist_w = pos - (PAST_LEN - WINDOW + col // NSA_KV_HEADS)
        win = _bucket_or_masked(dist_w, (col % NSA_KV_HEADS == h) & (dist_w < WINDOW))
        rows.append(np.concatenate([cmp_, sel, win], axis=1))
    return np.concatenate(rows, axis=0)


def _bias_table_kernel(tab_ref, idx_ref, o_ref):
    h = pl.program_id(0)
    idx = idx_ref[...]
    out = jnp.full(idx.shape, NEG, F32)
    for b in range(NUM_BUCKETS):
        out = jnp.where(idx == b, tab_ref[b, h], out)
    o_ref[0] = out


def _bias_tables(rel_bias, idx, name):
    r, c = idx.shape
    return pl.pallas_call(
        _bias_table_kernel,
        grid=(NSA_HEADS,),
        in_specs=[pl.BlockSpec(memory_space=pltpu.SMEM),
                  pl.BlockSpec((r, c), lambda h: (0, 0))],
        out_specs=pl.BlockSpec((1, r, c), lambda h: (h, 0, 0)),
        out_shape=jax.ShapeDtypeStruct((NSA_HEADS, r, c), F32),
        compiler_params=_cparams(("arbitrary",)),
        name=name,
    )(rel_bias, jnp.asarray(idx))


def _cover_np(n_cmp_cols, n_blk_cols, delta_of):
    u = np.arange(n_cmp_cols)[:, None]
    j = np.arange(n_blk_cols)[None, :]
    delta = delta_of(u, j)
    shared = np.minimum(CMP_STRIDE * delta + CMP_BLOCK, SEL_BLOCK) - np.maximum(CMP_STRIDE * delta, 0)
    return (np.maximum(shared, 0) / CMP_STRIDE).astype(np.float32)


def _compress(load_rows, n_chunk, pos_term_ref, w1_ref, w2_ref):
    parts = []
    for s0 in range(0, CMP_STRIDE, CMP_GROUP):
        lhs = jnp.concatenate([load_rows(s0 + k).astype(BF16) for k in range(CMP_GROUP)], axis=1)
        w = w1_ref[s0:s0 + CMP_GROUP].reshape(CMP_GROUP * HEAD_DIM, 2 * HEAD_DIM)
        parts.append(_dot(lhs, w))
    while len(parts) > 1:
        parts = [a + b for a, b in zip(parts[0::2], parts[1::2])]
    acc = parts[0] + pos_term_ref[0:1, :]
    nxt = pltpu.roll(acc[:, HEAD_DIM:], n_chunk - 1, axis=0)
    hid = _silu(acc[:, :HEAD_DIM] + nxt)
    return _dot(hid.astype(BF16), w2_ref[...])


def _pos_term_kernel(pos_ref, w1_ref, o_ref):
    halves = []
    for half in range(2):
        acc = jnp.zeros((8, HEAD_DIM), F32)
        for s in range(CMP_STRIDE):
            row = half * CMP_STRIDE + s
            p = jnp.broadcast_to(pos_ref[0, row:row + 1, :], (8, HEAD_DIM)).astype(BF16)
            acc = acc + _dot(p, w1_ref[0, s][:, half * HEAD_DIM:(half + 1) * HEAD_DIM])
        halves.append(acc)
    o_ref[0] = jnp.concatenate(halves, axis=1)


def _pos_term(pos, w1cat):
    return pl.pallas_call(
        _pos_term_kernel,
        grid=(2,),
        in_specs=[pl.BlockSpec((1, CMP_BLOCK, HEAD_DIM), lambda i: (i, 0, 0)),
                  pl.BlockSpec((1, CMP_STRIDE, HEAD_DIM, 2 * HEAD_DIM), lambda i: (i, 0, 0, 0))],
        out_specs=pl.BlockSpec((1, 8, 2 * HEAD_DIM), lambda i: (i, 0, 0)),
        out_shape=jax.ShapeDtypeStruct((2, 8, 2 * HEAD_DIM), F32),
        compiler_params=_cparams(("arbitrary",)),
        name="pos_term",
    )(pos, w1cat)


def _flash_step(carry, s, v_t):
    outs = []
    w = s.shape[1] // FLASH_SPLIT
    v_ext = jnp.concatenate([v_t, jnp.ones((ONES_ROWS, v_t.shape[1]), BF16)], axis=0)
    for c in range(FLASH_SPLIT):
        m, acc = (x[:, c * w:(c + 1) * w] for x in carry)
        sc = s[:, c * w:(c + 1) * w]
        m_new = jnp.maximum(m, jnp.max(sc, axis=0, keepdims=True))
        acc = jnp.exp2(m - m_new) * acc + _dot(v_ext, jnp.exp2(sc - m_new).astype(BF16))
        outs.append((m_new, acc))
    return tuple(jnp.concatenate([o[i] for o in outs], axis=1) for i in range(2))


def _flash_init(cols):
    return (jnp.full((1, cols), M_INIT, F32), jnp.zeros((HEAD_DIM + ONES_ROWS, cols), F32))


def _flash_finish(carry):
    _, acc = carry
    return acc[:HEAD_DIM] / acc[HEAD_DIM:HEAD_DIM + 1]


def _masked_softmax(s, valid):
    s = jnp.where(valid, s, NEG)
    m = jnp.max(s, axis=-1, keepdims=True)
    e = jnp.where(valid, jnp.exp(s - m), 0.0)
    return e / jnp.maximum(jnp.sum(e, axis=-1, keepdims=True), 1e-30)


def _split_dot(x, w):
    hi = x.astype(BF16)
    lo = (x - hi.astype(F32)).astype(BF16)
    return _dot(hi, w) + _dot(lo, w)


def _top_n_mask(score, index, axis, interleave=()):
    sel = jnp.zeros(score.shape, F32)
    every = SEL_TOPN // (len(interleave) + 1)
    for r in range(SEL_TOPN):
        mx = jnp.max(score, axis=axis, keepdims=True)
        first = jnp.min(jnp.where(score == mx, index, 1e9), axis=axis, keepdims=True)
        hit = index == first
        sel = jnp.where(hit, 1.0, sel)
        score = jnp.where(hit, -jnp.inf, score)
        if (r + 1) % every == 0 and (r + 1) // every <= len(interleave):
            interleave[(r + 1) // every - 1]()
    return sel


def _top_n_mask_by_rank(score, n_cand):
    lane = lax.broadcasted_iota(jnp.int32, score.shape, 1)
    rank = jnp.zeros(score.shape, F32)
    for j in range(n_cand):
        col = score[:, j:j + 1]
        ahead = (col > score) | ((col == score) & (lane > j))
        rank = rank + ahead.astype(F32)
    return ((rank < SEL_TOPN) & (lane < n_cand)).astype(F32)


def _compress_prompt_kernel(rows_ref, pos_ref, w1_ref, w2_ref, o_ref):
    out = _compress(lambda s: rows_ref[pl.ds(s, N_CMP_PAD, stride=CMP_STRIDE), :], N_CMP_PAD,
                    pos_ref.at[0], w1_ref.at[0], w2_ref.at[0])
    real = lax.broadcasted_iota(jnp.int32, (N_CMP_PAD, HEAD_DIM), 0) < N_CMP_PAD - 1
    o_ref[0, 0, 0:SLAB_OFF, :] = jnp.zeros((SLAB_OFF, HEAD_DIM), F32)
    o_ref[0, 0, SLAB_OFF:SLAB_OFF + N_CMP_PAD, :] = jnp.where(real, out, 0.0)
    o_ref[0, 0, SLAB_OFF + N_CMP_PAD:KC_ROWS, :] = jnp.zeros((KC_ROWS - SLAB_OFF - N_CMP_PAD, HEAD_DIM), F32)


def _compress_prompt(z, pos, w1, w2):
    kv_blk = Z_KV // HEAD_DIM
    return pl.pallas_call(
        _compress_prompt_kernel,
        grid=(2, NSA_KV_HEADS),
        in_specs=[pl.BlockSpec((SEQ, HEAD_DIM), lambda i, h: (0, kv_blk + NSA_KV_HEADS * i + h)),
                  pl.BlockSpec((1, 8, 2 * HEAD_DIM), lambda i, h: (i, 0, 0)),
                  pl.BlockSpec((1, CMP_STRIDE, HEAD_DIM, 2 * HEAD_DIM), lambda i, h: (i, 0, 0, 0)),
                  pl.BlockSpec((1, HEAD_DIM, HEAD_DIM), lambda i, h: (i, 0, 0))],
        out_specs=pl.BlockSpec((1, 1, KC_ROWS, HEAD_DIM), lambda i, h: (i, h, 0, 0)),
        out_shape=jax.ShapeDtypeStruct((2, NSA_KV_HEADS, KC_ROWS, HEAD_DIM), F32),
        compiler_params=_cparams(("parallel", "parallel")),
        name="compress_prompt",
    )(z, pos, w1, w2)


def _nsa_prompt_kernel(q_ref, misc_ref, ksel_ref, vsel_rows_ref, kwin_ref, vwin_rows_ref, kc_ref, vc_ref,
                       bias_ref, cover_ref, erel_ref, o_ref, vsel_ref, vwin_ref):
    kvh = pl.program_id(0)
    t = pl.program_id(1)
    cols = NSA_GROUP * Q_TILE

    @pl.when(t == 0)
    def _():
        def body(kt, _):
            rows = pl.ds(pl.multiple_of(kt * Q_TILE, Q_TILE), Q_TILE)
            vsel_ref[:, rows] = vsel_rows_ref[rows, :].astype(F32).T.astype(BF16)
            vwin_ref[:, rows] = vwin_rows_ref[rows, :].astype(F32).T.astype(BF16)
            return 0
        lax.fori_loop(0, N_QT, body, 0)

    q = q_ref[...] * (HEAD_DIM ** -0.5 * LOG2E)
    qt =jnp.concatenate([q[:, g * HEAD_DIM:(g + 1) * HEAD_DIM].T for g in range(NSA_GROUP)], axis=1).astype(BF16)

    def bias_tile(k):
        return bias_ref[0, k * Q_TILE:(k + 1) * Q_TILE, :]
    b_diag, b_prev, b_first = bias_tile(0), bias_tile(1), bias_tile(2)

    def key_tile(ref, kt):
        return ref[pl.ds(pl.multiple_of(kt * Q_TILE, Q_TILE), Q_TILE), :]

    def value_tile(ref, kt):
        return ref[:, pl.ds(pl.multiple_of(kt * Q_TILE, Q_TILE), Q_TILE)]

    n_wt = WINDOW // Q_TILE + 1
    scores, values = [], []
    for w, b_tile in enumerate((b_first,) + (None,) * (n_wt - 3) + (b_prev, b_diag)):
        kt = t - (n_wt - 1) + w
        kc = jnp.maximum(kt, 0)
        s = _dot(key_tile(kwin_ref, kc), qt) + jnp.where(kt >= 0, 0.0, NEG)
        scores.append(s if b_tile is None else s + b_tile)
        values.append(value_tile(vwin_ref, kc))
    win_carry = [_flash_init(cols)]

    def win_part(lo):
        def run():
            win_carry[0] = _flash_step(win_carry[0], jnp.concatenate(scores[lo:lo + WIN_PART_TILES], axis=0),
                                       jnp.concatenate(values[lo:lo + WIN_PART_TILES], axis=1))
        return run
    win_parts = [win_part(lo) for lo in range(0, n_wt, WIN_PART_TILES)]

    start = pl.multiple_of(t * (Q_TILE // CMP_STRIDE), 8)
    kslab = kc_ref[0, 0, pl.ds(start, SLAB), :].astype(BF16)
    vslab = vc_ref[0, 0, pl.ds(start, SLAB), :].astype(BF16)
    b_cmp = bias_ref[0, 3 * Q_TILE:3 * Q_TILE + SLAB, :]
    u = lax.broadcasted_iota(jnp.int32, (SLAB, 1), 0)
    s = _dot(kslab, qt) + b_cmp + jnp.where(u >= SLAB_OFF - (Q_TILE // CMP_STRIDE) * t, 0.0, NEG)
    e = jnp.exp2(s - jnp.maximum(jnp.max(s, axis=0, keepdims=True), M_INIT))
    p_cmp = e / jnp.maximum(jnp.sum(e, axis=0, keepdims=True), 1e-30)
    o_cmp = _dot_tn(vslab, p_cmp.astype(BF16))

    p_sum = p_cmp[:, 0:Q_TILE]
    for g in range(1, NSA_GROUP):
        p_sum = p_sum + p_cmp[:, g * Q_TILE:(g + 1) * Q_TILE]
    hi = p_sum.astype(BF16)
    lo = (p_sum - hi.astype(F32)).astype(BF16)
    imp = _dot(cover_ref[...], hi) + _dot(cover_ref[...], lo)
    jr = lax.broadcasted_iota(jnp.int32, (N_SEL, Q_TILE), 0)
    qi = lax.broadcasted_iota(jnp.int32, (N_SEL, Q_TILE), 1)
    cur = REL0 + (qi >= SEL_BLOCK).astype(jnp.int32)
    first_blk = REL0 - 2 * t
    forced = (jr == first_blk) | (jr == cur) | (jr == cur - 1)
    in_range = (jr <= cur) & (jr >= first_blk)
    score = jnp.where(in_range, jnp.where(forced, FORCED_SCORE, imp), NEG)
    sel = _top_n_mask(score, jr.astype(F32), 0, win_parts)
    o_win = _flash_finish(win_carry[0])

    unsel = ((sel - 1.0) * (-NEG)).astype(BF16)
    q_aug = jnp.concatenate([qt, jnp.concatenate([unsel] * NSA_GROUP, axis=1)], axis=0)

    def sweep_step(i, carry, masked, biases=()):
        scores, values = [], []
        for j in range(SWEEP_TILES):
            kt = t - SWEEP_TILES * i - (SWEEP_TILES - 1) + j
            kc = jnp.maximum(kt, 0) if masked else kt
            tile = jnp.concatenate([key_tile(ksel_ref, kc), erel_ref[t - kc]], axis=1)
            s = _dot(tile, q_aug)
            if masked:
                s = s + jnp.where(kt >= 0, 0.0, NEG)
            if j >= SWEEP_TILES - len(biases):
                s = s + biases[j - (SWEEP_TILES - len(biases))]
            scores.append(s)
            values.append(value_tile(vsel_ref, kc))
        per = SWEEP_TILES // SWEEP_PARTS
        for k in range(SWEEP_PARTS):
            carry = _flash_step(carry, jnp.concatenate(scores[k * per:(k + 1) * per], axis=0),
                                jnp.concatenate(values[k * per:(k + 1) * per], axis=1))
        return carry

    n_full = jnp.maximum(t - (SWEEP_TILES - 1), 0) // SWEEP_TILES
    carry = lax.fori_loop(1, n_full + 1, lambda i, c: sweep_step(i, c, False), _flash_init(cols))
    some_left = (t - SWEEP_TILES * (n_full + 1) >= 0).astype(jnp.int32)
    carry = lax.fori_loop(0, some_left, lambda _, c: sweep_step(n_full + 1, c, True), carry)
    o_sel = _flash_finish(sweep_step(0, carry, True, (b_prev, b_diag)))

    gates = jax.nn.sigmoid(misc_ref[...]).T
    n_g = 3 * NSA_GROUP
    gk = jnp.where(kvh == 0, gates, pltpu.roll(gates, gates.shape[0] - n_g, axis=0))[MISC_GATES:MISC_GATES + n_g]
    for g in range(NSA_GROUP):
        c = slice(g * Q_TILE, (g + 1) * Q_TILE)
        o_g = (gk[3 * g:3 * g + 1] * o_cmp[:, c] + gk[3 * g + 1:3 * g + 2] * o_sel[:, c]
               + gk[3 * g + 2:3 * g + 3] * o_win[:, c])
        o_ref[:, g * HEAD_DIM:(g + 1) * HEAD_DIM] = o_g.T


def _nsa_prompt(z, kvb, kcp, bias, cover, erel):
    gw = NSA_GROUP * HEAD_DIM

    def k_spec(j):
        return pl.BlockSpec((SEQ, HEAD_DIM), lambda h, t, j=j: (0, 2 * j + h))
    return pl.pallas_call(
        _nsa_prompt_kernel,
        grid=(NSA_KV_HEADS, N_QT),
        in_specs=[pl.BlockSpec((Q_TILE, gw), lambda h, t: (t, Z_QN // gw + h)),
                  pl.BlockSpec((Q_TILE, 128), lambda h, t: (t, Z_MISC // 128)),
                  k_spec(2), k_spec(3), k_spec(4), k_spec(5),
                  pl.BlockSpec((1, 1, KC_ROWS, HEAD_DIM), lambda h, t: (0, h, 0, 0)),
                  pl.BlockSpec((1, 1, KC_ROWS, HEAD_DIM), lambda h, t: (1, h, 0, 0)),
                  pl.BlockSpec((1,) + bias.shape[1:], lambda h, t: (h, 0, 0)),
                  pl.BlockSpec(cover.shape, lambda h, t: (0, 0)),
                  pl.BlockSpec(erel.shape, lambda h, t: (0, 0, 0))],
        out_specs=pl.BlockSpec((Q_TILE, gw), lambda h, t: (t, h)),
        out_shape=jax.ShapeDtypeStruct((SEQ, NSA_HEADS * HEAD_DIM), F32),
        scratch_shapes=[pltpu.VMEM((HEAD_DIM, SEQ), BF16), pltpu.VMEM((HEAD_DIM, SEQ), BF16)],
        compiler_params=_cparams(("arbitrary", "arbitrary")),
        name="nsa_prompt",
    )(z, z, kvb, kvb, kvb, kvb, kcp, kcp, bias, cover, erel)


S_ROWS = NSA_HEADS * DEC_SEQ
PAGE_ROWS = NSA_KV_HEADS * PAGE_SIZE
NEW_ROWS = NSA_KV_HEADS * DEC_SEQ
CHUNK_ROWS = NSA_KV_HEADS * CMP_STRIDE
CHUNK_PITCH = CHUNK_ROWS + 8


def _nsa_sample_kernel(pt_ref, q_ref, gate_ref, new_ref, kwin_ref, vwin_ref, kcmp_hbm, vcmp_hbm, ksel_hbm, vsel_hbm,
                       posk_ref, w1k_ref, w2k_ref, posv_ref, w1v_ref, w2v_ref, bias_ref, cover_ref, expand_ref,
                       o_ref, kwin_o_ref, vwin_o_ref, kcmp_buf, vcmp_buf, ksel_buf, vsel_buf, sem):
    n_pg = N_PAGES
    b = pl.program_id(0)
    slot = b % 2
    page_chunks = PAGE_ROWS // CHUNK_ROWS
    hbm = (kcmp_hbm, vcmp_hbm, ksel_hbm, vsel_hbm)
    bufs = (kcmp_buf, vcmp_buf, ksel_buf, vsel_buf)

    def page_copy(k, p, page, into):
        if k < 2:
            src = hbm[k].at[pl.ds(page * page_chunks, page_chunks)]
            dst = bufs[k].at[pl.ds(into * S_CMP + p * page_chunks, page_chunks), pl.ds(0, CHUNK_ROWS), :]
        else:
            src = hbm[k].at[pl.ds(page * PAGE_ROWS, PAGE_ROWS)]
            dst = bufs[k].at[pl.ds(into * (n_pg * PAGE_ROWS) + p * PAGE_ROWS, PAGE_ROWS), :]
        return pltpu.make_async_copy(src, dst, sem.at[into, k])

    def gather(seq, into):
        for p in range(n_pg):
            page = pt_ref[seq, p]
            for k in range(4):
                page_copy(k, p, page, into).start()

    @pl.when(b == 0)
    def _():
        gather(0, 0)

    @pl.when(b + 1 < DEC_BATCH)
    def _():
        gather(b + 1, 1 - slot)

    for p in range(n_pg):
        for k in range(4):
            page_copy(k, p, 0, slot).wait()

    def pages(k, p):
        return bufs[k][pl.ds(pl.multiple_of(slot * (n_pg * PAGE_ROWS), PAGE_ROWS) + p * PAGE_ROWS, PAGE_ROWS), :]

    comp = []
    for k, (pos_ref, w1_ref, w2_ref) in enumerate(((posk_ref, w1k_ref, w2k_ref), (posv_ref, w1v_ref, w2v_ref))):
        flat = bufs[k].reshape(2 * S_CMP * CHUNK_PITCH, HEAD_DIM)

        def load_rows(s, flat=flat):
            return jnp.concatenate(
                [flat[pl.ds(slot * (S_CMP * CHUNK_PITCH) + NSA_KV_HEADS * s + h, S_CMP, stride=CHUNK_PITCH), :]
                 for h in range(NSA_KV_HEADS)], axis=0)
        comp.append(_compress(load_rows, S_CMP_COLS, pos_ref, w1_ref, w2_ref).astype(BF16))

    keep = NSA_KV_HEADS * WINDOW - NEW_ROWS
    kwin_o_ref[0:keep, :] = kwin_ref[NEW_ROWS:NSA_KV_HEADS * WINDOW, :]
    kwin_o_ref[keep:keep + NEW_ROWS, :] = new_ref[4]
    vwin_o_ref[0:keep, :] = vwin_ref[NEW_ROWS:NSA_KV_HEADS * WINDOW, :]
    vwin_o_ref[keep:keep + NEW_ROWS, :] = new_ref[5]

    qs = (q_ref[0] * (HEAD_DIM ** -0.5)).astype(BF16)
    o0, o1 = S_CMP_COLS, S_CMP_COLS + S_SEL_COLS
    b_cmp = bias_ref[:, 0:o0]
    b_new = bias_ref[:, o1 - S_NEW_COLS:o1]

    def attend(scores, bias, values):
        s = jnp.concatenate(scores, axis=1) + bias
        e = jnp.exp(s - jnp.max(s, axis=-1, keepdims=True))
        acc = jnp.zeros((S_ROWS, HEAD_DIM), F32)
        c0 = 0
        for v in values:
            acc = acc + _dot(e[:, c0:c0 + v.shape[0]].astype(BF16), v)
            c0 += v.shape[0]
        return acc / jnp.sum(e, axis=-1, keepdims=True)

    p_cmp = _masked_softmax(_dot_nt(qs, comp[0]) + b_cmp, b_cmp > M_INIT)
    o_cmp = _dot(p_cmp.astype(BF16), comp[1])

    imp = _split_dot(p_cmp, cover_ref[...])
    imp = imp + pltpu.roll(imp, 8, axis=0) + pltpu.roll(imp, 16, axis=0) + pltpu.roll(imp, 24, axis=0)
    blk = lax.broadcasted_iota(jnp.int32, (S_ROWS, 128), 1)
    cur = PAST_LEN // SEL_BLOCK
    forced = (blk == 0) | (blk == cur) | (blk == cur - 1)
    score = jnp.where(blk <= cur, jnp.where(forced, FORCED_SCORE, imp), NEG)
    sel = _top_n_mask_by_rank(score, cur + 1).astype(BF16)
    mask_add = (_dot(sel, expand_ref[...]) - 1.0) * (-NEG)

    def new_tile(j):
        pad = jnp.zeros((S_NEW_COLS - NEW_ROWS, HEAD_DIM), F32)
        return jnp.concatenate([new_ref[j], pad], axis=0).astype(BF16)

    nk, nv = new_tile(2), new_tile(3)
    scores = [_dot_nt(qs, pages(2, p).astype(BF16)) for p in range(n_pg)] + [_dot_nt(qs, nk)]
    values = [pages(3, p).astype(BF16) for p in range(n_pg)] + [nv]
    o_sel = attend(scores, bias_ref[:, o0:o1] + mask_add, values)

    nk, nv = new_tile(4), new_tile(5)
    scores = [_dot_nt(qs, kwin_ref[...].astype(BF16)), _dot_nt(qs, nk)]
    bias_w = jnp.concatenate([bias_ref[:, o1:o1 + S_WIN_COLS], b_new], axis=1)
    o_win = attend(scores, bias_w, [vwin_ref[...].astype(BF16), nv])

    g = jax.nn.sigmoid(gate_ref[0])
    o_ref[0] = g[:, 0:1] * o_cmp + g[:, 1:2] * o_sel + g[:, 2:3] * o_win


def _nsa_sample(page_table, q_s, gate_s, new_s, kwin, vwin, pools, cmp_w, bias, cover, expand):
    win_rows = NSA_KV_HEADS * WINDOW

    def full(a):
        return pl.BlockSpec(a.shape, lambda b, pt, n=a.ndim: (0,) * n)

    def per_b(a):
        return pl.BlockSpec((1,) + a.shape[1:], lambda b, pt, n=a.ndim: (b,) + (0,) * (n - 1))
    win_spec = pl.BlockSpec((win_rows, HEAD_DIM), lambda b, pt: (b, 0))
    consts = list(cmp_w) + [bias, cover, expand]
    grid_spec = pltpu.PrefetchScalarGridSpec(
        num_scalar_prefetch=1,
        grid=(DEC_BATCH,),
        in_specs=[per_b(q_s), per_b(gate_s),
                  pl.BlockSpec((N_KV_ARRAYS, NEW_ROWS, HEAD_DIM), lambda b, pt: (0, b, 0)), win_spec, win_spec]
        + [pl.BlockSpec(memory_space=pl.ANY)] * 4 + [full(a) for a in consts],
        out_specs=[per_b(q_s), win_spec, win_spec],
        scratch_shapes=[pltpu.VMEM((2 * S_CMP, CHUNK_PITCH, HEAD_DIM), F32)] * 2
        + [pltpu.VMEM((2 * N_PAGES * PAGE_ROWS, HEAD_DIM), F32)] * 2
        + [pltpu.SemaphoreType.DMA((2, 4))],
    )
    return pl.pallas_call(
        _nsa_sample_kernel,
        grid_spec=grid_spec,
        out_shape=[jax.ShapeDtypeStruct(q_s.shape, F32),
                   jax.ShapeDtypeStruct(kwin.shape, F32),
                   jax.ShapeDtypeStruct(vwin.shape, F32)],
        compiler_params=_cparams(("arbitrary",)),
        name="nsa_sample",
    )(page_table, q_s, gate_s, new_s, kwin, vwin, *pools, *consts)


def _log_decay(a_blk, wa_ref, ba_ref):
    x = _dot(a_blk.astype(BF16), wa_ref[...]) + ba_ref[...]
    return (jnp.minimum(x, 0.0) - jnp.log1p(jnp.exp(-jnp.abs(x)))) * (1.0 / GLA_TAU)


def _segment_cumsum(g, seg):
    pos = lax.broadcasted_iota(jnp.int32, g.shape, 0) % seg
    cum = g
    sh = 1
    while sh < seg:
        cum = cum + jnp.where(pos >= sh, pltpu.roll(cum, sh, axis=0), 0.0)
        sh *= 2
    return cum


def _gla_prompt_kernel(q_ref, k_ref, v_ref, r_ref, a_ref, wa_ref, ba_ref, gn_ref, o_ref, st_o_ref, st_ref,
                       *, n_blk, tb):
    tbi = pl.program_id(0)

    @pl.when(tbi == 0)
    def _():
        st_ref[...] = jnp.zeros_like(st_ref)

    c = GLA_CHUNK
    tril = lax.broadcasted_iota(jnp.int32, (c, c), 0) >= lax.broadcasted_iota(jnp.int32, (c, c), 1)
    cum = _segment_cumsum(_log_decay(a_ref[...], wa_ref, ba_ref), c)
    q = q_ref[...] * (GLA_DK ** -0.5)
    k = k_ref[...]
    v = v_ref[...].astype(BF16)
    qe = (q * jnp.exp(cum)).astype(BF16)
    kd = (k * jnp.exp(-cum)).astype(BF16)
    heads = [(slice(h * GLA_DK, (h + 1) * GLA_DK), slice(h * GLA_DV, (h + 1) * GLA_DV)) for h in range(GLA_HEADS)]
    sts = [st_ref[h] for h in range(GLA_HEADS)]
    outs = [[] for _ in range(GLA_HEADS)]
    for ci in range(tb // c):
        r = slice(ci * c, (ci + 1) * c)
        last = cum[ci * c + c - 1:ci * c + c, :]
        kl = (k[r] * jnp.exp(last - cum[r])).astype(BF16)
        decay = jnp.exp(last)
        for h, (dk, dv) in enumerate(heads):
            att = jnp.where(tril, _dot_nt(qe[r, dk], kd[r, dk]), 0.0)
            outs[h].append(_dot_nt(qe[r, dk], sts[h].astype(BF16)) + _dot(att.astype(BF16), v[r, dv]))
            sts[h] = decay[:, dk] * sts[h] + _dot_tn(v[r, dv], kl[:, dk])
    for h, (dk, dv) in enumerate(heads):
        st_ref[h] = sts[h]
        o = jnp.concatenate(outs[h], axis=0)
        o_ref[:, dv] = _rms(o, gn_ref[...]) * _silu(r_ref[:, dv])

    @pl.when(tbi == n_blk - 1)
    def _():
        st_o_ref[...] = st_ref[...]


def _gla_prompt(z, wa, ba, gn, tb=256):
    n_blk = SEQ // tb
    hk, hv = GLA_HEADS * GLA_DK, GLA_HEADS * GLA_DV
    st_shape = (GLA_HEADS, GLA_DV, GLA_DK)
    return pl.pallas_call(
        functools.partial(_gla_prompt_kernel, n_blk=n_blk, tb=tb),
        grid=(n_blk,),
        in_specs=[pl.BlockSpec((tb, hk), lambda i: (i, Z_QG // hk)),
                  pl.BlockSpec((tb, hk), lambda i: (i, Z_KG // hk)),
                  pl.BlockSpec((tb, hv), lambda i: (i, Z_VG // hv)),
                  pl.BlockSpec((tb, hv), lambda i: (i, Z_RG // hv)),
                  pl.BlockSpec((tb, 128), lambda i: (i, Z_MISC // 128)),
                  pl.BlockSpec((128, hk), lambda i: (0, 0)),
                  pl.BlockSpec((1, hk), lambda i: (0, 0)),
                  pl.BlockSpec((1, GLA_DV), lambda i: (0, 0))],
        out_specs=[pl.BlockSpec((tb, hv), lambda i: (i, 0)),
                   pl.BlockSpec(st_shape, lambda i: (0, 0, 0))],
        out_shape=[jax.ShapeDtypeStruct((SEQ, hv), F32), jax.ShapeDtypeStruct(st_shape, F32)],
        scratch_shapes=[pltpu.VMEM(st_shape, F32)],
        compiler_params=_cparams(("arbitrary",)),
        name="gla_prompt",
    )(z, z, z, z, z, wa, ba, gn)


GS_B = 4


def _gla_sample_kernel(q_ref, k_ref, v_ref, r_ref, a_ref, wa_ref, ba_ref, gn_ref, s_ref, o_ref, s_o_ref):
    rows = GS_B * DEC_SEQ
    a = a_ref[...]
    ri = lax.broadcasted_iota(jnp.int32, (rows, rows), 0)
    ci = lax.broadcasted_iota(jnp.int32, (rows, rows), 1)
    same_causal = (ri // DEC_SEQ == ci // DEC_SEQ) & (ri >= ci)
    row_b = lax.broadcasted_iota(jnp.int32, (rows, 1), 0) // DEC_SEQ
    ones = jnp.ones((rows, 128), BF16)
    for h in range(GLA_HEADS):
        dk = slice(h * GLA_DK, (h + 1) * GLA_DK)
        dv = slice(h * GLA_DV, (h + 1) * GLA_DV)
        g = _log_decay(a, wa_ref.at[:, dk], ba_ref.at[:, dk])
        cum = _segment_cumsum(g, DEC_SEQ)
        q = q_ref[:, dk] * (GLA_DK ** -0.5)
        k = k_ref[:, dk]
        v = v_ref[:, dv].astype(BF16)
        qe = (q * jnp.exp(cum)).astype(BF16)
        kd = (k * jnp.exp(-cum)).astype(BF16)
        att = jnp.where(same_causal, _dot_nt(qe, kd), 0.0)
        o = _dot(att.astype(BF16), v)
        for b in range(GS_B):
            mine = row_b == b
            last = cum[b * DEC_SEQ + DEC_SEQ - 1:(b + 1) * DEC_SEQ, :]
            s = s_ref[b, h]
            o = o + jnp.where(mine, _dot(qe, s.astype(BF16)), 0.0)
            kl = jnp.where(mine, k * jnp.exp(last - cum), 0.0)
            hi = jnp.where(mine, g, 0.0).astype(BF16)
            lo = (jnp.where(mine, g, 0.0) - hi.astype(F32)).astype(BF16)
            last_col = (_dot_tn(hi, ones) + _dot_tn(lo, ones))[:, 0:1]
            s_o_ref[b, h] = jnp.exp(last_col) * s + _dot_tn(kl.astype(BF16), v)
        o_ref[:, dv] = _rms(o, gn_ref[...]) * _silu(r_ref[:, dv])


def _gla_sample(zs, state, wa, ba, gn):
    rows = GS_B * DEC_SEQ
    n = DEC_BATCH * DEC_SEQ
    hk, hv = GLA_HEADS * GLA_DK, GLA_HEADS * GLA_DV
    st_spec = pl.BlockSpec((GS_B, GLA_HEADS, GLA_DK, GLA_DV), lambda i: (i, 0, 0, 0))
    return pl.pallas_call(
        _gla_sample_kernel,
        grid=(DEC_BATCH // GS_B,),
        in_specs=[pl.BlockSpec((rows, hk), lambda i: (i, Z_QG // hk)),
                  pl.BlockSpec((rows, hk), lambda i: (i, Z_KG // hk)),
                  pl.BlockSpec((rows, hv), lambda i: (i, Z_VG // hv)),
                  pl.BlockSpec((rows, hv), lambda i: (i, Z_RG // hv)),
                  pl.BlockSpec((rows, 128), lambda i: (i, Z_MISC // 128)),
                  pl.BlockSpec((128, hk), lambda i: (0, 0)),
                  pl.BlockSpec((1, hk), lambda i: (0, 0)),
                  pl.BlockSpec((1, GLA_DV), lambda i: (0, 0)),
                  st_spec],
        out_specs=[pl.BlockSpec((rows, hv), lambda i: (i, 0)), st_spec],
        out_shape=[jax.ShapeDtypeStruct((n, hv), F32), jax.ShapeDtypeStruct(state.shape, F32)],
        compiler_params=_cparams(("parallel",)),
        name="gla_sample",
    )(zs, zs, zs, zs, zs, wa, ba, gn, state)


def _softmax_rows(s):
    m = jnp.max(s, axis=-1, keepdims=True)
    e = jnp.exp(s - m)
    return e / jnp.sum(e, axis=-1, keepdims=True)


def _mem_prompt_kernel(q_ref, k_ref, v_ref, o_ref):
    for h in range(MEM_HEADS):
        d = slice(h * MEM_HEAD_DIM, (h + 1) * MEM_HEAD_DIM)
        q = (q_ref[:, d] * (MEM_HEAD_DIM ** -0.5)).astype(BF16)
        p = _softmax_rows(_dot_nt(q, k_ref[:, d].astype(BF16)))
        o_ref[:, d] = _dot(p.astype(BF16), v_ref[:, d].astype(BF16))


def _mem_prompt(qm, memkv, tq=256):
    w = MEM_HEADS * MEM_HEAD_DIM
    return pl.pallas_call(
        _mem_prompt_kernel,
        grid=(SEQ // tq,),
        in_specs=[pl.BlockSpec((tq, w), lambda i: (i, 0)),
                  pl.BlockSpec((MEM_TOKENS, w), lambda i: (0, 0)),
                  pl.BlockSpec((MEM_TOKENS, w), lambda i: (0, 1))],
        out_specs=pl.BlockSpec((tq, w), lambda i: (i, 0)),
        out_shape=jax.ShapeDtypeStruct((SEQ, w), F32),
        compiler_params=_cparams(("parallel",)),
        name="mem_prompt",
    )(qm, memkv, memkv)


def _mem_sample_kernel(q_ref, k_ref, v_ref, o_ref):
    rows = MEM_HEADS * DEC_SEQ
    cols = MEM_HEADS * MEM_TOKENS
    row_h = lax.broadcasted_iota(jnp.int32, (rows, cols), 0) // DEC_SEQ
    col_h = lax.broadcasted_iota(jnp.int32, (rows, cols), 1) % MEM_HEADS
    for b in range(MS_B):
        kv_rows = slice(b * cols, (b + 1) * cols)
        q = (q_ref[b] * (MEM_HEAD_DIM ** -0.5)).astype(BF16)
        s = jnp.where(row_h == col_h, _dot_nt(q, k_ref[kv_rows, :].astype(BF16)), NEG)
        o_ref[b] = _dot(_softmax_rows(s).astype(BF16), v_ref[kv_rows, :].astype(BF16))


MS_B = 4


def _mem_sample(q_s, k_mem, v_mem):
    rows = MEM_HEADS * DEC_SEQ
    kv_spec = pl.BlockSpec((MS_B * MEM_HEADS * MEM_TOKENS, MEM_HEAD_DIM), lambda b: (b, 0))
    return pl.pallas_call(
        _mem_sample_kernel,
        grid=(DEC_BATCH // MS_B,),
        in_specs=[pl.BlockSpec((MS_B, rows, MEM_HEAD_DIM), lambda b: (b, 0, 0)), kv_spec, kv_spec],
        out_specs=pl.BlockSpec((MS_B, rows, MEM_HEAD_DIM), lambda b: (b, 0, 0)),
        out_shape=jax.ShapeDtypeStruct((DEC_BATCH, rows, MEM_HEAD_DIM), F32),
        compiler_params=_cparams(("parallel",)),
        name="mem_sample",
    )(q_s, k_mem, v_mem)


W_IN_SEGMENTS = ((0, 1024, Z_QN), (1024, 1536, Z_KV), (2560, 24, Z_MISC + MISC_GATES), (2584, 512, Z_QG),
                 (3096, 512, Z_KG), (3608, 1024, Z_VG), (4632, 1024, Z_RG), (5656, 16, Z_MISC + MISC_A))
W_IN_COLS = 5672


def _permute_w_in_kernel(w_ref, o_ref):
    o_ref[:, Z_MISC:Z_W] = jnp.zeros((o_ref.shape[0], Z_W - Z_MISC), BF16)
    for src, width, dst in W_IN_SEGMENTS:
        o_ref[:, dst:dst + width] = w_ref[:, src:src + width].astype(BF16)


def _permute_w_in(w_in, tr=256):
    d = w_in.shape[0]
    return pl.pallas_call(
        _permute_w_in_kernel,
        grid=(d // tr,),
        in_specs=[pl.BlockSpec((tr, W_IN_COLS), lambda i: (i, 0))],
        out_specs=pl.BlockSpec((tr, Z_W), lambda i: (i, 0)),
        out_shape=jax.ShapeDtypeStruct((d, Z_W), BF16),
        compiler_params=_cparams(("parallel",)),
        name="permute_w_in",
    )(w_in)


def kernel(x_prompt, x_sample, mem_prompt, cache_k_cmp, cache_v_cmp, cache_k_sel, cache_v_sel, cache_k_win,
           cache_v_win, state_gla, cache_k_mem, cache_v_mem, page_table, norm_ffn1, ffn1_w_gate, ffn1_w_up,
           ffn1_w_down, norm_mix, w_in, w_out, cmp_pos_k, cmp_w1_k, cmp_w2_k, cmp_pos_v, cmp_w1_v, cmp_w2_v,
           rel_bias, gla_w_a2, gla_b_a, gla_norm, norm_mem, norm_mem_src, w_mem_q, w_mem_k, w_mem_v, w_mem_o,
           norm_ffn2, ffn2_w_gate, ffn2_w_up, ffn2_w_down, norm_final):
    bf = lambda a: a.astype(BF16)
    row = lambda a: a.reshape(1, -1)
    nb, ns = DEC_BATCH, DEC_SEQ
    kvw = NSA_KV_HEADS * HEAD_DIM

    h1, h1n = _ffn([x_prompt[0], x_sample.reshape(nb * ns, D_MODEL)], row(norm_ffn1[0]), bf(ffn1_w_gate[0]),
                   bf(ffn1_w_up[0]), bf(ffn1_w_down[0]), row(norm_mix[0]), False)
    z, kvb, new_s, *new_p = _proj_in(h1n, _permute_w_in(w_in[0]))
    rows_p = [a.reshape(1, 1, SEQ, NSA_KV_HEADS, HEAD_DIM) for a in new_p]
    rows_s = [new_s[j].reshape(1, nb, ns, NSA_KV_HEADS, HEAD_DIM) for j in range(4)]

    tab_p = _bias_tables(rel_bias, _prompt_bucket_table(), "bias_prompt")
    far = rel_bias[NUM_BUCKETS - 1][:, None, None]
    near = tab_p[:, :3 * Q_TILE]
    tab_p = jnp.concatenate([jnp.where(near > M_INIT, near - far, NEG), tab_p[:, 3 * Q_TILE:]], axis=1)
    tab_p = jnp.where(tab_p > M_INIT, tab_p * LOG2E, NEG)
    tab_p = tab_p.reshape(NSA_KV_HEADS, NSA_GROUP, -1, Q_TILE).transpose(0, 2, 1, 3)
    tab_p = tab_p.reshape(NSA_KV_HEADS, -1, NSA_GROUP * Q_TILE)
    tab_s = _bias_tables(rel_bias, _sample_bucket_table(), "bias_sample")
    tab_s = tab_s.reshape(NSA_KV_HEADS, NSA_GROUP, NSA_KV_HEADS, ns, -1)
    tab_s = jnp.stack([tab_s[h, :, h] for h in range(NSA_KV_HEADS)], axis=1).reshape(S_ROWS, -1)

    cmp_w1 = bf(jnp.stack([cmp_w1_k[0], cmp_w1_v[0]]))
    cmp_w1 = jnp.concatenate([cmp_w1[:, :CMP_STRIDE], cmp_w1[:, CMP_STRIDE:]], axis=-1)
    cmp_w2 = bf(jnp.stack([cmp_w2_k[0], cmp_w2_v[0]]))
    cmp_pos = _pos_term(jnp.stack([cmp_pos_k[0], cmp_pos_v[0]]), cmp_w1)
    kcp = _compress_prompt(z, cmp_pos, cmp_w1, cmp_w2)
    cover_p = jnp.asarray(_cover_np(SLAB, N_SEL, lambda u, j: u - 4 * j - SLAB_OFF + 4 * REL0).T, BF16)
    erel = (np.arange(N_SEL)[None, None, :] == REL0 - 2 * np.arange(N_QT)[:, None, None]
            + (np.arange(Q_TILE)[None, :, None] >= SEL_BLOCK))
    o_nsa_p = _nsa_prompt(z, kvb, kcp, tab_p, cover_p, jnp.asarray(erel, BF16))

    zs = z[SEQ:]
    q_s = zs[:, Z_QN:Z_QN + NSA_HEADS * HEAD_DIM].reshape(nb, ns, NSA_KV_HEADS, NSA_GROUP, HEAD_DIM)
    q_s = q_s.transpose(0, 3, 2, 1, 4).reshape(nb, S_ROWS, HEAD_DIM)
    gate_s = zs[:, Z_MISC + MISC_GATES:Z_MISC + MISC_GATES + 3 * NSA_HEADS]
    gate_s = gate_s.reshape(nb, ns, NSA_KV_HEADS, NSA_GROUP, 3).transpose(0, 3, 2, 1, 4).reshape(nb, S_ROWS, 3)
    as_rows = lambda c: c.reshape(-1, HEAD_DIM)
    pools = ([c.reshape(-1, CHUNK_ROWS, HEAD_DIM) for c in (cache_k_cmp, cache_v_cmp)]
             + [as_rows(c) for c in (cache_k_sel, cache_v_sel)])
    cover_s = _cover_np(S_CMP, 128, lambda c, j: c - 4 * j)
    cover_s = jnp.asarray(np.concatenate([cover_s] * NSA_KV_HEADS, axis=0), BF16)
    expand_s = jnp.asarray(np.arange(128)[:, None]
                           == (np.arange(S_SEL_COLS)[None, :] // (NSA_KV_HEADS * SEL_BLOCK)), BF16)
    o_nsa_s, kwin_s, vwin_s = _nsa_sample(
        page_table, q_s, gate_s, new_s, as_rows(cache_k_win), as_rows(cache_v_win), pools,
        (cmp_pos[0], cmp_w1[0], cmp_w2[0], cmp_pos[1], cmp_w1[1], cmp_w2[1]),
        tab_s, cover_s, expand_s)
    o_nsa_s = o_nsa_s.reshape(nb, NSA_GROUP, NSA_KV_HEADS, ns, HEAD_DIM).transpose(0, 3, 2, 1, 4)
    o_nsa_s = o_nsa_s.reshape(nb * ns, NSA_HEADS * HEAD_DIM)

    wa = bf(jnp.zeros((128, GLA_HEADS * GLA_DK), F32).at[MISC_A:MISC_A + GLA_RANK].set(gla_w_a2[0]))
    ba, gn = row(gla_b_a[0]), row(gla_norm[0])
    o_gla_p, st_p = _gla_prompt(z, wa, ba, gn)
    o_gla_s, st_s = _gla_sample(zs, state_gla[0], wa, ba, gn)

    half = NSA_HEADS * HEAD_DIM
    h2 = _matmul_res(h1, [(o_nsa_p, o_nsa_s), (o_gla_p, o_gla_s)], [bf(w_out[0][:half]), bf(w_out[0][half:])],
                     512, "proj_out")

    memkv = _norm_matmul(mem_prompt[0], row(norm_mem_src[0]),
                         bf(jnp.concatenate([w_mem_k[0], w_mem_v[0]], axis=1)), MEM_TOKENS, 512, "mem_kv")
    mw = MEM_HEADS * MEM_HEAD_DIM
    qm = _norm_matmul(h2, row(norm_mem[0]), bf(w_mem_q[0]), 512, mw, "mem_q")
    om_p = _mem_prompt(qm, memkv)
    qm_s = qm[SEQ:].reshape(nb, ns, MEM_HEADS, MEM_HEAD_DIM).transpose(0, 2, 1, 3)
    om_s = _mem_sample(qm_s.reshape(nb, MEM_HEADS * ns, MEM_HEAD_DIM),
                       as_rows(cache_k_mem), as_rows(cache_v_mem))
    om_s = om_s.reshape(nb, MEM_HEADS, ns, MEM_HEAD_DIM).transpose(0, 2, 1, 3).reshape(nb * ns, mw)
    h3 = _matmul_res(h2, [(om_p, om_s)], [bf(w_mem_o[0])], 512, "mem_out")

    y_p, y_s = _ffn([h3], row(norm_ffn2[0]), bf(ffn2_w_gate[0]), bf(ffn2_w_up[0]), bf(ffn2_w_down[0]),
                    row(norm_final), True)

    mem_shape = (1, 1, MEM_TOKENS, MEM_HEADS, MEM_HEAD_DIM)
    win_shape = (1, nb, WINDOW, NSA_KV_HEADS, HEAD_DIM)
    return (y_p.reshape(1, SEQ, D_MODEL), y_s.reshape(nb, ns, D_MODEL),
            rows_p[0], rows_p[1], rows_p[2], rows_p[3],
            rows_p[4][:, :, SEQ - WINDOW:], rows_p[5][:, :, SEQ - WINDOW:],
            st_p.transpose(0, 2, 1).reshape(1, 1, GLA_HEADS, GLA_DK, GLA_DV),
            memkv[:, :mw].reshape(mem_shape), memkv[:, mw:].reshape(mem_shape),
            rows_s[0], rows_s[1], rows_s[2], rows_s[3],
            kwin_s.reshape(win_shape), vwin_s.reshape(win_shape),
            st_s.reshape(1, nb, GLA_HEADS, GLA_DK, GLA_DV))
```

```python
import functools
import math

import numpy as np
import jax
import jax.numpy as jnp
from jax import lax
from jax.experimental import pallas as pl
from jax.experimental.pallas import tpu as pltpu

F32 = jnp.float32
BF16 = jnp.bfloat16

D_MODEL = 2048
SEQ = 8192
DEC_BATCH = 128
DEC_SEQ = 4
PAST_LEN = 2048
PAGE_SIZE = 128
N_PAGES = PAST_LEN // PAGE_SIZE
HEAD_DIM = 128
NSA_HEADS = 8
NSA_KV_HEADS = 2
NSA_GROUP = 4
CMP_BLOCK = 32
CMP_STRIDE = 16
SEL_BLOCK = 64
SEL_TOPN = 16
WINDOW = 512
FORCED_SCORE = 1.0e4
GLA_HEADS = 4
GLA_DV = 256
GLA_DK = 128
GLA_RANK = 16
GLA_TAU = 16.0
GLA_CHUNK = 32
MEM_TOKENS = 256
MEM_HEADS = 4
MEM_HEAD_DIM = 128
D_FF = 5632
NUM_BUCKETS = 32
MAX_DISTANCE = 128
RMS_EPS = 1e-6

N_TOK = SEQ + DEC_BATCH * DEC_SEQ
Z_KV, Z_QG, Z_VG, Z_RG, Z_QN, Z_KG, Z_MISC = 0, 1536, 2048, 3072, 4096, 5120, 5632
Z_W = 5760
MISC_GATES, MISC_A = 0, 24

NEG = -1e30
LOG2E = math.log2(math.e)
ONES_ROWS = 16
M_INIT = -1e29

Q_TILE = 128
N_QT = SEQ // Q_TILE
N_CMP_PAD = SEQ // CMP_STRIDE
SLAB = N_CMP_PAD + 128
SLAB_OFF = SLAB - 16
KC_ROWS = SLAB_OFF + N_CMP_PAD + 16
N_SEL = SEQ // SEL_BLOCK
REL0 = N_SEL - 2
CMP_GROUP = 4
SWEEP_TILES = 8
SWEEP_PARTS = 4
WIN_PART_TILES = 2
FLASH_SPLIT = 1

VMEM_LIMIT = 56 * 1024 * 1024


def _cparams(sem):
    return pltpu.CompilerParams(dimension_semantics=sem, vmem_limit_bytes=VMEM_LIMIT)


def _dot(a, b):
    return jnp.dot(a, b, preferred_element_type=F32)


def _dot_nt(a, b):
    return lax.dot_general(a, b, (((1,), (1,)), ((), ())), preferred_element_type=F32)


def _dot_tn(a, b):
    return lax.dot_general(a, b, (((0,), (0,)), ((), ())), preferred_element_type=F32)


def _rms(x, g):
    return x * lax.rsqrt(jnp.mean(x * x, axis=-1, keepdims=True) + RMS_EPS) * g


def _silu(x):
    return x * jax.nn.sigmoid(x)


def _ffn_kernel(*refs, n_ff, n_first, two_in, two_out):
    refs = list(refs)
    x_refs = [refs.pop(0) for _ in range(2 if two_in else 1)]
    g_ref, wg_ref, wu_ref, wd_ref, gf_ref = refs[:5]
    o_refs = refs[5:7]
    xn_ref, acc_ref = refs[-2:]
    i = pl.program_id(0)
    j = pl.program_id(1)

    def x_tile():
        return jnp.where(i < n_first, x_refs[0][...], x_refs[1][...]) if two_in else x_refs[0][...]

    @pl.when(j == 0)
    def _():
        xn_ref[...] = _rms(x_tile(), g_ref[...]).astype(BF16)
        acc_ref[...] = jnp.zeros_like(acc_ref)

    xn = xn_ref[...]
    hid = _silu(_dot(xn, wg_ref[...])) * _dot(xn, wu_ref[...])
    acc_ref[...] += _dot(hid.astype(BF16), wd_ref[...])

    @pl.when(j == n_ff - 1)
    def _():
        h = x_tile() + 0.5 * acc_ref[...]
        if two_out:
            h = _rms(h, gf_ref[...])

            @pl.when(i < n_first)
            def _():
                o_refs[0][...] = h

            @pl.when(i >= n_first)
            def _():
                o_refs[1][...] = h
        else:
            o_refs[0][...] = h
            o_refs[1][...] = _rms(h, gf_ref[...]).astype(BF16)


FFN_TM = 512
FFN_TF = 512


def _ffn(xs, g, wg, wu, wd, gf, split_out):
    tm, tf, d = FFN_TM, FFN_TF, D_MODEL
    n_ff = D_FF // tf
    n_first = SEQ // tm
    two_in = len(xs) == 2

    def first(i, j):
        return (jnp.minimum(i, n_first - 1), 0)

    def second(i, j):
        return (jnp.maximum(i - n_first, 0), 0)
    whole = pl.BlockSpec((tm, d), lambda i, j: (i, 0))
    pair = [pl.BlockSpec((tm, d), first), pl.BlockSpec((tm, d), second)]
    n_s = DEC_BATCH * DEC_SEQ
    return pl.pallas_call(
        functools.partial(_ffn_kernel, n_ff=n_ff, n_first=n_first, two_in=two_in, two_out=split_out),
        grid=(N_TOK // tm, n_ff),
        in_specs=(pair if two_in else [whole])
        + [pl.BlockSpec((1, d), lambda i, j: (0, 0)),
           pl.BlockSpec((d, tf), lambda i, j: (0, j)),
           pl.BlockSpec((d, tf), lambda i, j: (0, j)),
           pl.BlockSpec((tf, d), lambda i, j: (j, 0)),
           pl.BlockSpec((1, d), lambda i, j: (0, 0))],
        out_specs=pair if split_out else [whole, whole],
        out_shape=([jax.ShapeDtypeStruct((SEQ, d), F32), jax.ShapeDtypeStruct((n_s, d), F32)] if split_out
                   else [jax.ShapeDtypeStruct((N_TOK, d), F32), jax.ShapeDtypeStruct((N_TOK, d), BF16)]),
        scratch_shapes=[pltpu.VMEM((tm, d), BF16), pltpu.VMEM((tm, d), F32)],
        compiler_params=_cparams(("arbitrary", "arbitrary")),
        name="ffn",
    )(*xs, g, wg, wu, wd, gf)


def _norm_matmul_kernel(x_ref, g_ref, w_ref, o_ref, xn_ref):
    @pl.when(pl.program_id(1) == 0)
    def _():
        xn_ref[...] = _rms(x_ref[...], g_ref[...]).astype(BF16)

    o_ref[...] = _dot(xn_ref[...], w_ref[...])


def _norm_matmul(x, g, w, tm, tn, name):
    n, d = x.shape
    dout = w.shape[1]
    return pl.pallas_call(
        _norm_matmul_kernel,
        grid=(n // tm, dout // tn),
        in_specs=[pl.BlockSpec((tm, d), lambda i, j: (i, 0)),
                  pl.BlockSpec((1, d), lambda i, j: (0, 0)),
                  pl.BlockSpec((d, tn), lambda i, j: (0, j))],
        out_specs=pl.BlockSpec((tm, tn), lambda i, j: (i, j)),
        out_shape=jax.ShapeDtypeStruct((n, dout), F32),
        scratch_shapes=[pltpu.VMEM((tm, d), BF16)],
        compiler_params=_cparams(("parallel", "arbitrary")),
        name=name,
    )(x, g, w)


N_KV_ARRAYS = 6
KV_W = NSA_KV_HEADS * HEAD_DIM


def _proj_in_kernel(x_ref, w_ref, o_ref, kvb_ref, rows_s_ref, *rows_p_refs, n_first):
    j, i = pl.program_id(0), pl.program_id(1)
    res = _dot(x_ref[...], w_ref[...])
    o_ref[...] = res
    tm = res.shape[0]

    @pl.when(j == 0)
    def _():
        kvb_ref[...] = res[:, Z_KV:Z_KV + N_KV_ARRAYS * KV_W].astype(BF16)

        def head_rows(a, h):
            c0 = Z_KV + a * KV_W + h * HEAD_DIM
            return res[:, c0:c0 + HEAD_DIM]

        @pl.when(i < n_first)
        def _():
            for a in range(N_KV_ARRAYS):
                for h in range(NSA_KV_HEADS):
                    rows_p_refs[a][pl.ds(h, tm, stride=NSA_KV_HEADS), :] = head_rows(a, h)

        @pl.when(i >= n_first)
        def _():
            for a in range(N_KV_ARRAYS):
                for h in range(NSA_KV_HEADS):
                    rows_s_ref.at[a][pl.ds(h, tm, stride=NSA_KV_HEADS), :] = head_rows(a, h)


def _proj_in(x, w, tm=512, tn=1920):
    n, d = x.shape
    n_i, n_first = n // tm, SEQ // tm
    n_s = n - SEQ
    assert Z_KV == 0 and N_KV_ARRAYS * KV_W <= tn and n_s == tm

    def once(i_of):
        return lambda j, i: jnp.where(j == 0, i_of(i), i_of(n_i - 1))
    p_idx = once(lambda i: jnp.minimum(i, n_first - 1))
    return pl.pallas_call(
        functools.partial(_proj_in_kernel, n_first=n_first),
        grid=(Z_W // tn, n_i),
        in_specs=[pl.BlockSpec((tm, d), lambda j, i: (i, 0)),
                  pl.BlockSpec((d, tn), lambda j, i: (0, j))],
        out_specs=[pl.BlockSpec((tm, tn), lambda j, i: (i, j)),
                   pl.BlockSpec((tm, N_KV_ARRAYS * KV_W), lambda j, i: (once(lambda i: i)(j, i), 0)),
                   pl.BlockSpec((N_KV_ARRAYS, NSA_KV_HEADS * n_s, HEAD_DIM), lambda j, i: (0, 0, 0))]
        + [pl.BlockSpec((NSA_KV_HEADS * tm, HEAD_DIM), lambda j, i: (p_idx(j, i), 0))] * N_KV_ARRAYS,
        out_shape=[jax.ShapeDtypeStruct((n, Z_W), F32),
                   jax.ShapeDtypeStruct((n, N_KV_ARRAYS * KV_W), BF16),
                   jax.ShapeDtypeStruct((N_KV_ARRAYS, NSA_KV_HEADS * n_s, HEAD_DIM), F32)]
        + [jax.ShapeDtypeStruct((NSA_KV_HEADS * SEQ, HEAD_DIM), F32)] * N_KV_ARRAYS,
        compiler_params=_cparams(("arbitrary", "arbitrary")),
        name="proj_in",
    )(x, w)


def _matmul_res_kernel(*refs, n_lhs, n_first):
    res_ref = refs[0]
    o_ref = refs[1 + 3 * n_lhs]
    first = pl.program_id(0) < n_first
    acc = res_ref[...]
    for k in range(n_lhs):
        lhs = jnp.where(first, refs[1 + 2 * k][...], refs[2 + 2 * k][...])
        acc = acc + _dot(lhs.astype(BF16), refs[1 + 2 * n_lhs + k][...])
    o_ref[...] = acc


def _matmul_res(res, lhs, ws, tm, name):
    n, d = res.shape
    n_first = SEQ // tm

    def pair_specs(width):
        return [pl.BlockSpec((tm, width), lambda i: (jnp.minimum(i, n_first - 1), 0)),
                pl.BlockSpec((tm, width), lambda i: (jnp.maximum(i - n_first, 0), 0))]
    return pl.pallas_call(
        functools.partial(_matmul_res_kernel, n_lhs=len(lhs), n_first=n_first),
        grid=(n // tm,),
        in_specs=([pl.BlockSpec((tm, d), lambda i: (i, 0))]
                  + [spec for a, _ in lhs for spec in pair_specs(a.shape[1])]
                  + [pl.BlockSpec(w.shape, lambda i: (0, 0)) for w in ws]),
        out_specs=pl.BlockSpec((tm, d), lambda i: (i, 0)),
        out_shape=jax.ShapeDtypeStruct((n, d), F32),
        compiler_params=_cparams(("arbitrary",)),
        name=name,
    )(res, *[a for pair in lhs for a in pair], *ws)


def _rel_bucket_np(dist):
    n = np.maximum(dist, 0)
    exact = NUM_BUCKETS // 2
    nf = np.maximum(n, 1).astype(np.float32)
    large = exact + (np.log(nf / np.float32(exact)) / np.float32(math.log(MAX_DISTANCE / exact))
                     * np.float32(NUM_BUCKETS - exact)).astype(np.int32)
    return np.where(n < exact, n, np.minimum(large, NUM_BUCKETS - 1)).astype(np.int32)


def _bucket_or_masked(dist, valid):
    return np.where(valid, _rel_bucket_np(dist), -1).astype(np.int32)


def _prompt_bucket_table():
    i = np.arange(Q_TILE)[:, None]
    j = np.arange(Q_TILE)[None, :]
    u = np.arange(SLAB)[None, :]
    dist_c = i - CMP_STRIDE * u + (CMP_STRIDE * SLAB_OFF - (CMP_BLOCK - 1))
    diag = _bucket_or_masked(i - j, i - j >= 0)
    prev = _bucket_or_masked(Q_TILE + i - j, np.ones((Q_TILE, Q_TILE), bool))
    first = _bucket_or_masked(WINDOW + i - j, j > i)
    cmp_ = _bucket_or_masked(dist_c, dist_c >= 0)
    return np.concatenate([diag, prev, first, cmp_], axis=1).T


S_CMP = PAST_LEN // CMP_STRIDE
S_CMP_COLS = NSA_KV_HEADS * S_CMP
S_NEW_COLS = 128
S_SEL_COLS = NSA_KV_HEADS * PAST_LEN + S_NEW_COLS
S_WIN_COLS = NSA_KV_HEADS * WINDOW


def _sample_bucket_table():
    i = np.arange(DEC_SEQ)[:, None]
    pos = PAST_LEN + i
    rows = []
    for h in range(NSA_KV_HEADS):
        col = np.arange(S_CMP_COLS)[None, :]
        c = col % S_CMP
        dist_c = pos - (c * CMP_STRIDE + CMP_BLOCK - 1)
        cmp_ = _bucket_or_masked(dist_c, (col // S_CMP == h) & (c < S_CMP - 1) & (dist_c >= 0))
        col = np.arange(S_SEL_COLS)[None, :]
        key = col // NSA_KV_HEADS
        sel = _bucket_or_masked(pos - key, (col % NSA_KV_HEADS == h) & (key <= pos))
        col = np.arange(S_WIN_COLS)[None, :]
        dist_w = pos - (PAST_LEN - WINDOW + col // NSA_KV_HEADS)
        win = _bucket_or_masked(dist_w, (col % NSA_KV_HEADS == h) & (dist_w < WINDOW))
        rows.append(np.concatenate([cmp_, sel, win], axis=1))
    return np.concatenate(rows, axis=0)


def _bias_table_kernel(tab_ref, idx_ref, o_ref):
    h = pl.program_id(0)
    idx = idx_ref[...]
    out = jnp.full(idx.shape, NEG, F32)
    for b in range(NUM_BUCKETS):
        out = jnp.where(idx == b, tab_ref[b, h], out)
    o_ref[0] = out


def _bias_tables(rel_bias, idx, name):
    r, c = idx.shape
    return pl.pallas_call(
        _bias_table_kernel,
        grid=(NSA_HEADS,),
        in_specs=[pl.BlockSpec(memory_space=pltpu.SMEM),
                  pl.BlockSpec((r, c), lambda h: (0, 0))],
        out_specs=pl.BlockSpec((1, r, c), lambda h: (h, 0, 0)),
        out_shape=jax.ShapeDtypeStruct((NSA_HEADS, r, c), F32),
        compiler_params=_cparams(("arbitrary",)),
        name=name,
    )(rel_bias, jnp.asarray(idx))


def _cover_np(n_cmp_cols, n_blk_cols, delta_of):
    u = np.arange(n_cmp_cols)[:, None]
    j = np.arange(n_blk_cols)[None, :]
    delta = delta_of(u, j)
    shared = np.minimum(CMP_STRIDE * delta + CMP_BLOCK, SEL_BLOCK) - np.maximum(CMP_STRIDE * delta, 0)
    return (np.maximum(shared, 0) / CMP_STRIDE).astype(np.float32)


def _compress(load_rows, n_chunk, pos_term_ref, w1_ref, w2_ref):
    parts = []
    for s0 in range(0, CMP_STRIDE, CMP_GROUP):
        lhs = jnp.concatenate([load_rows(s0 + k).astype(BF16) for k in range(CMP_GROUP)], axis=1)
        w = w1_ref[s0:s0 + CMP_GROUP].reshape(CMP_GROUP * HEAD_DIM, 2 * HEAD_DIM)
        parts.append(_dot(lhs, w))
    while len(parts) > 1:
        parts = [a + b for a, b in zip(parts[0::2], parts[1::2])]
    acc = parts[0] + pos_term_ref[0:1, :]
    nxt = pltpu.roll(acc[:, HEAD_DIM:], n_chunk - 1, axis=0)
    hid = _silu(acc[:, :HEAD_DIM] + nxt)
    return _dot(hid.astype(BF16), w2_ref[...])


def _pos_term_kernel(pos_ref, w1_ref, o_ref):
    halves = []
    for half in range(2):
        acc = jnp.zeros((8, HEAD_DIM), F32)
        for s in range(CMP_STRIDE):
            row = half * CMP_STRIDE + s
            p = jnp.broadcast_to(pos_ref[0, row:row + 1, :], (8, HEAD_DIM)).astype(BF16)
            acc = acc + _dot(p, w1_ref[0, s][:, half * HEAD_DIM:(half + 1) * HEAD_DIM])
        halves.append(acc)
    o_ref[0] = jnp.concatenate(halves, axis=1)


def _pos_term(pos, w1cat):
    return pl.pallas_call(
        _pos_term_kernel,
        grid=(2,),
        in_specs=[pl.BlockSpec((1, CMP_BLOCK, HEAD_DIM), lambda i: (i, 0, 0)),
                  pl.BlockSpec((1, CMP_STRIDE, HEAD_DIM, 2 * HEAD_DIM), lambda i: (i, 0, 0, 0))],
        out_specs=pl.BlockSpec((1, 8, 2 * HEAD_DIM), lambda i: (i, 0, 0)),
        out_shape=jax.ShapeDtypeStruct((2, 8, 2 * HEAD_DIM), F32),
        compiler_params=_cparams(("arbitrary",)),
        name="pos_term",
    )(pos, w1cat)


def _flash_step(carry, s, v_t):
    outs = []
    w = s.shape[1] // FLASH_SPLIT
    v_ext = jnp.concatenate([v_t, jnp.ones((ONES_ROWS, v_t.shape[1]), BF16)], axis=0)
    for c in range(FLASH_SPLIT):
        m, acc = (x[:, c * w:(c + 1) * w] for x in carry)
        sc = s[:, c * w:(c + 1) * w]
        m_new = jnp.maximum(m, jnp.max(sc, axis=0, keepdims=True))
        acc = jnp.exp2(m - m_new) * acc + _dot(v_ext, jnp.exp2(sc - m_new).astype(BF16))
        outs.append((m_new, acc))
    return tuple(jnp.concatenate([o[i] for o in outs], axis=1) for i in range(2))


def _flash_init(cols):
    return (jnp.full((1, cols), M_INIT, F32), jnp.zeros((HEAD_DIM + ONES_ROWS, cols), F32))


def _flash_finish(carry):
    _, acc = carry
    return acc[:HEAD_DIM] / acc[HEAD_DIM:HEAD_DIM + 1]


def _masked_softmax(s, valid):
    s = jnp.where(valid, s, NEG)
    m = jnp.max(s, axis=-1, keepdims=True)
    e = jnp.where(valid, jnp.exp(s - m), 0.0)
    return e / jnp.maximum(jnp.sum(e, axis=-1, keepdims=True), 1e-30)


def _split_dot(x, w):
    hi = x.astype(BF16)
    lo = (x - hi.astype(F32)).astype(BF16)
    return _dot(hi, w) + _dot(lo, w)


def _top_n_mask(score, index, axis, interleave=()):
    sel = jnp.zeros(score.shape, F32)
    every = SEL_TOPN // (len(interleave) + 1)
    for r in range(SEL_TOPN):
        mx = jnp.max(score, axis=axis, keepdims=True)
        first = jnp.min(jnp.where(score == mx, index, 1e9), axis=axis, keepdims=True)
        hit = index == first
        sel = jnp.where(hit, 1.0, sel)
        score = jnp.where(hit, -jnp.inf, score)
        if (r + 1) % every == 0 and (r + 1) // every <= len(interleave):
            interleave[(r + 1) // every - 1]()
    return sel


def _top_n_mask_by_rank(score, n_cand):
    lane = lax.broadcasted_iota(jnp.int32, score.shape, 1)
    rank = jnp.zeros(score.shape, F32)
    for j in range(n_cand):
        col = score[:, j:j + 1]
        ahead = (col > score) | ((col == score) & (lane > j))
        rank = rank + ahead.astype(F32)
    return ((rank < SEL_TOPN) & (lane < n_cand)).astype(F32)


def _compress_prompt_kernel(rows_ref, pos_ref, w1_ref, w2_ref, o_ref):
    out = _compress(lambda s: rows_ref[pl.ds(s, N_CMP_PAD, stride=CMP_STRIDE), :], N_CMP_PAD,
                    pos_ref.at[0], w1_ref.at[0], w2_ref.at[0])
    real = lax.broadcasted_iota(jnp.int32, (N_CMP_PAD, HEAD_DIM), 0) < N_CMP_PAD - 1
    o_ref[0, 0, 0:SLAB_OFF, :] = jnp.zeros((SLAB_OFF, HEAD_DIM), F32)
    o_ref[0, 0, SLAB_OFF:SLAB_OFF + N_CMP_PAD, :] = jnp.where(real, out, 0.0)
    o_ref[0, 0, SLAB_OFF + N_CMP_PAD:KC_ROWS, :] = jnp.zeros((KC_ROWS - SLAB_OFF - N_CMP_PAD, HEAD_DIM), F32)


def _compress_prompt(z, pos, w1, w2):
    kv_blk = Z_KV // HEAD_DIM
    return pl.pallas_call(
        _compress_prompt_kernel,
        grid=(2, NSA_KV_HEADS),
        in_specs=[pl.BlockSpec((SEQ, HEAD_DIM), lambda i, h: (0, kv_blk + NSA_KV_HEADS * i + h)),
                  pl.BlockSpec((1, 8, 2 * HEAD_DIM), lambda i, h: (i, 0, 0)),
                  pl.BlockSpec((1, CMP_STRIDE, HEAD_DIM, 2 * HEAD_DIM), lambda i, h: (i, 0, 0, 0)),
                  pl.BlockSpec((1, HEAD_DIM, HEAD_DIM), lambda i, h: (i, 0, 0))],
        out_specs=pl.BlockSpec((1, 1, KC_ROWS, HEAD_DIM), lambda i, h: (i, h, 0, 0)),
        out_shape=jax.ShapeDtypeStruct((2, NSA_KV_HEADS, KC_ROWS, HEAD_DIM), F32),
        compiler_params=_cparams(("parallel", "parallel")),
        name="compress_prompt",
    )(z, pos, w1, w2)


def _nsa_prompt_kernel(q_ref, misc_ref, ksel_ref, vsel_rows_ref, kwin_ref, vwin_rows_ref, kc_ref, vc_ref,
                       bias_ref, cover_ref, erel_ref, o_ref, vsel_ref, vwin_ref):
    kvh = pl.program_id(0)
    t = pl.program_id(1)
    cols = NSA_GROUP * Q_TILE

    @pl.when(t == 0)
    def _():
        def body(kt, _):
            rows = pl.ds(pl.multiple_of(kt * Q_TILE, Q_TILE), Q_TILE)
            vsel_ref[:, rows] = vsel_rows_ref[rows, :].astype(F32).T.astype(BF16)
            vwin_ref[:, rows] = vwin_rows_ref[rows, :].astype(F32).T.astype(BF16)
            return 0
        lax.fori_loop(0, N_QT, body, 0)

    q = q_ref[...] * (HEAD_DIM ** -0.5 * LOG2E)
    qt =jnp.concatenate([q[:, g * HEAD_DIM:(g + 1) * HEAD_DIM].T for g in range(NSA_GROUP)], axis=1).astype(BF16)

    def bias_tile(k):
        return bias_ref[0, k * Q_TILE:(k + 1) * Q_TILE, :]
    b_diag, b_prev, b_first = bias_tile(0), bias_tile(1), bias_tile(2)

    def key_tile(ref, kt):
        return ref[pl.ds(pl.multiple_of(kt * Q_TILE, Q_TILE), Q_TILE), :]

    def value_tile(ref, kt):
        return ref[:, pl.ds(pl.multiple_of(kt * Q_TILE, Q_TILE), Q_TILE)]

    n_wt = WINDOW // Q_TILE + 1
    scores, values = [], []
    for w, b_tile in enumerate((b_first,) + (None,) * (n_wt - 3) + (b_prev, b_diag)):
        kt = t - (n_wt - 1) + w
        kc = jnp.maximum(kt, 0)
        s = _dot(key_tile(kwin_ref, kc), qt) + jnp.where(kt >= 0, 0.0, NEG)
        scores.append(s if b_tile is None else s + b_tile)
        values.append(value_tile(vwin_ref, kc))
    win_carry = [_flash_init(cols)]

    def win_part(lo):
        def run():
            win_carry[0] = _flash_step(win_carry[0], jnp.concatenate(scores[lo:lo + WIN_PART_TILES], axis=0),
                                       jnp.concatenate(values[lo:lo + WIN_PART_TILES], axis=1))
        return run
    win_parts = [win_part(lo) for lo in range(0, n_wt, WIN_PART_TILES)]

    start = pl.multiple_of(t * (Q_TILE // CMP_STRIDE), 8)
    kslab = kc_ref[0, 0, pl.ds(start, SLAB), :].astype(BF16)
    vslab = vc_ref[0, 0, pl.ds(start, SLAB), :].astype(BF16)
    b_cmp = bias_ref[0, 3 * Q_TILE:3 * Q_TILE + SLAB, :]
    u = lax.broadcasted_iota(jnp.int32, (SLAB, 1), 0)
    s = _dot(kslab, qt) + b_cmp + jnp.where(u >= SLAB_OFF - (Q_TILE // CMP_STRIDE) * t, 0.0, NEG)
    e = jnp.exp2(s - jnp.maximum(jnp.max(s, axis=0, keepdims=True), M_INIT))
    p_cmp = e / jnp.maximum(jnp.sum(e, axis=0, keepdims=True), 1e-30)
    o_cmp = _dot_tn(vslab, p_cmp.astype(BF16))

    p_sum = p_cmp[:, 0:Q_TILE]
    for g in range(1, NSA_GROUP):
        p_sum = p_sum + p_cmp[:, g * Q_TILE:(g + 1) * Q_TILE]
    hi = p_sum.astype(BF16)
    lo = (p_sum - hi.astype(F32)).astype(BF16)
    imp = _dot(cover_ref[...], hi) + _dot(cover_ref[...], lo)
    jr = lax.broadcasted_iota(jnp.int32, (N_SEL, Q_TILE), 0)
    qi = lax.broadcasted_iota(jnp.int32, (N_SEL, Q_TILE), 1)
    cur = REL0 + (qi >= SEL_BLOCK).astype(jnp.int32)
    first_blk = REL0 - 2 * t
    forced = (jr == first_blk) | (jr == cur) | (jr == cur - 1)
    in_range = (jr <= cur) & (jr >= first_blk)
    score = jnp.where(in_range, jnp.where(forced, FORCED_SCORE, imp), NEG)
    sel = _top_n_mask(score, jr.astype(F32), 0, win_parts)
    o_win = _flash_finish(win_carry[0])

    unsel = ((sel - 1.0) * (-NEG)).astype(BF16)
    q_aug = jnp.concatenate([qt, jnp.concatenate([unsel] * NSA_GROUP, axis=1)], axis=0)

    def sweep_step(i, carry, masked, biases=()):
        scores, values = [], []
        for j in range(SWEEP_TILES):
            kt = t - SWEEP_TILES * i - (SWEEP_TILES - 1) + j
            kc = jnp.maximum(kt, 0) if masked else kt
            tile = jnp.concatenate([key_tile(ksel_ref, kc), erel_ref[t - kc]], axis=1)
            s = _dot(tile, q_aug)
            if masked:
                s = s + jnp.where(kt >= 0, 0.0, NEG)
            if j >= SWEEP_TILES - len(biases):
                s = s + biases[j - (SWEEP_TILES - len(biases))]
            scores.append(s)
            values.append(value_tile(vsel_ref, kc))
        per = SWEEP_TILES // SWEEP_PARTS
        for k in range(SWEEP_PARTS):
            carry = _flash_step(carry, jnp.concatenate(scores[k * per:(k + 1) * per], axis=0),
                                jnp.concatenate(values[k * per:(k + 1) * per], axis=1))
        return carry

    n_full = jnp.maximum(t - (SWEEP_TILES - 1), 0) // SWEEP_TILES
    carry = lax.fori_loop(1, n_full + 1, lambda i, c: sweep_step(i, c, False), _flash_init(cols))
    some_left = (t - SWEEP_TILES * (n_full + 1) >= 0).astype(jnp.int32)
    carry = lax.fori_loop(0, some_left, lambda _, c: sweep_step(n_full + 1, c, True), carry)
    o_sel = _flash_finish(sweep_step(0, carry, True, (b_prev, b_diag)))

    gates = jax.nn.sigmoid(misc_ref[...]).T
    n_g = 3 * NSA_GROUP
    gk = jnp.where(kvh == 0, gates, pltpu.roll(gates, gates.shape[0] - n_g, axis=0))[MISC_GATES:MISC_GATES + n_g]
    for g in range(NSA_GROUP):
        c = slice(g * Q_TILE, (g + 1) * Q_TILE)
        o_g = (gk[3 * g:3 * g + 1] * o_cmp[:, c] + gk[3 * g + 1:3 * g + 2] * o_sel[:, c]
               + gk[3 * g + 2:3 * g + 3] * o_win[:, c])
        o_ref[:, g * HEAD_DIM:(g + 1) * HEAD_DIM] = o_g.T


def _nsa_prompt(z, kvb, kcp, bias, cover, erel):
    gw = NSA_GROUP * HEAD_DIM

    def k_spec(j):
        return pl.BlockSpec((SEQ, HEAD_DIM), lambda h, t, j=j: (0, 2 * j + h))
    return pl.pallas_call(
        _nsa_prompt_kernel,
        grid=(NSA_KV_HEADS, N_QT),
        in_specs=[pl.BlockSpec((Q_TILE, gw), lambda h, t: (t, Z_QN // gw + h)),
                  pl.BlockSpec((Q_TILE, 128), lambda h, t: (t, Z_MISC // 128)),
                  k_spec(2), k_spec(3), k_spec(4), k_spec(5),
                  pl.BlockSpec((1, 1, KC_ROWS, HEAD_DIM), lambda h, t: (0, h, 0, 0)),
                  pl.BlockSpec((1, 1, KC_ROWS, HEAD_DIM), lambda h, t: (1, h, 0, 0)),
                  pl.BlockSpec((1,) + bias.shape[1:], lambda h, t: (h, 0, 0)),
                  pl.BlockSpec(cover.shape, lambda h, t: (0, 0)),
                  pl.BlockSpec(erel.shape, lambda h, t: (0, 0, 0))],
        out_specs=pl.BlockSpec((Q_TILE, gw), lambda h, t: (t, h)),
        out_shape=jax.ShapeDtypeStruct((SEQ, NSA_HEADS * HEAD_DIM), F32),
        scratch_shapes=[pltpu.VMEM((HEAD_DIM, SEQ), BF16), pltpu.VMEM((HEAD_DIM, SEQ), BF16)],
        compiler_params=_cparams(("arbitrary", "arbitrary")),
        name="nsa_prompt",
    )(z, z, kvb, kvb, kvb, kvb, kcp, kcp, bias, cover, erel)


S_ROWS = NSA_HEADS * DEC_SEQ
PAGE_ROWS = NSA_KV_HEADS * PAGE_SIZE
NEW_ROWS = NSA_KV_HEADS * DEC_SEQ
CHUNK_ROWS = NSA_KV_HEADS * CMP_STRIDE
CHUNK_PITCH = CHUNK_ROWS + 8


def _nsa_sample_kernel(pt_ref, q_ref, gate_ref, new_ref, kwin_ref, vwin_ref, kcmp_hbm, vcmp_hbm, ksel_hbm, vsel_hbm,
                       posk_ref, w1k_ref, w2k_ref, posv_ref, w1v_ref, w2v_ref, bias_ref, cover_ref, expand_ref,
                       o_ref, kwin_o_ref, vwin_o_ref, kcmp_buf, vcmp_buf, ksel_buf, vsel_buf, sem):
    n_pg = N_PAGES
    b = pl.program_id(0)
    slot = b % 2
    page_chunks = PAGE_ROWS // CHUNK_ROWS
    hbm = (kcmp_hbm, vcmp_hbm, ksel_hbm, vsel_hbm)
    bufs = (kcmp_buf, vcmp_buf, ksel_buf, vsel_buf)

    def page_copy(k, p, page, into):
        if k < 2:
            src = hbm[k].at[pl.ds(page * page_chunks, page_chunks)]
            dst = bufs[k].at[pl.ds(into * S_CMP + p * page_chunks, page_chunks), pl.ds(0, CHUNK_ROWS), :]
        else:
            src = hbm[k].at[pl.ds(page * PAGE_ROWS, PAGE_ROWS)]
            dst = bufs[k].at[pl.ds(into * (n_pg * PAGE_ROWS) + p * PAGE_ROWS, PAGE_ROWS), :]
        return pltpu.make_async_copy(src, dst, sem.at[into, k])

    def gather(seq, into):
        for p in range(n_pg):
            page = pt_ref[seq, p]
            for k in range(4):
                page_copy(k, p, page, into).start()

    @pl.when(b == 0)
    def _():
        gather(0, 0)

    @pl.when(b + 1 < DEC_BATCH)
    def _():
        gather(b + 1, 1 - slot)

    for p in range(n_pg):
        for k in range(4):
            page_copy(k, p, 0, slot).wait()

    def pages(k, p):
        return bufs[k][pl.ds(pl.multiple_of(slot * (n_pg * PAGE_ROWS), PAGE_ROWS) + p * PAGE_ROWS, PAGE_ROWS), :]

    def compress_pool(k, pos_ref, w1_ref, w2_ref):
        flat = bufs[k].reshape(2 * S_CMP * CHUNK_PITCH, HEAD_DIM)

        def load_rows(s):
            return jnp.concatenate(
                [flat[pl.ds(slot * (S_CMP * CHUNK_PITCH) + NSA_KV_HEADS * s + h, S_CMP, stride=CHUNK_PITCH), :]
                 for h in range(NSA_KV_HEADS)], axis=0)
        return _compress(load_rows, S_CMP_COLS, pos_ref, w1_ref, w2_ref).astype(BF16)

    qs = (q_ref[0] * (HEAD_DIM ** -0.5)).astype(BF16)
    o0, o1 = S_CMP_COLS, S_CMP_COLS + S_SEL_COLS
    b_cmp = bias_ref[:, 0:o0]
    b_new = bias_ref[:, o1 - S_NEW_COLS:o1]

    def attend(scores, bias, values):
        s = jnp.concatenate(scores, axis=1) + bias
        e = jnp.exp(s - jnp.max(s, axis=-1, keepdims=True))
        acc = jnp.zeros((S_ROWS, HEAD_DIM), F32)
        c0 = 0
        for v in values:
            acc = acc + _dot(e[:, c0:c0 + v.shape[0]].astype(BF16), v)
            c0 += v.shape[0]
        return acc / jnp.sum(e, axis=-1, keepdims=True)

    def new_tile(j):
        pad = jnp.zeros((S_NEW_COLS - NEW_ROWS, HEAD_DIM), F32)
        return jnp.concatenate([new_ref[j], pad], axis=0).astype(BF16)

    p_cmp = _masked_softmax(_dot_nt(qs, compress_pool(0, posk_ref, w1k_ref, w2k_ref)) + b_cmp, b_cmp > M_INIT)
    comp_v = compress_pool(1, posv_ref, w1v_ref, w2v_ref)
    o_cmp = _dot(p_cmp.astype(BF16), comp_v)

    imp = _split_dot(p_cmp, cover_ref[...])
    imp = imp + pltpu.roll(imp, 8, axis=0) + pltpu.roll(imp, 16, axis=0) + pltpu.roll(imp, 24, axis=0)
    blk = lax.broadcasted_iota(jnp.int32, (S_ROWS, 128), 1)
    cur = PAST_LEN // SEL_BLOCK
    forced = (blk == 0) | (blk == cur) | (blk == cur - 1)
    score = jnp.where(blk <= cur, jnp.where(forced, FORCED_SCORE, imp), NEG)

    nk, nv = new_tile(4), new_tile(5)
    scores = [_dot_nt(qs, kwin_ref[...].astype(BF16)), _dot_nt(qs, nk)]
    bias_w = jnp.concatenate([bias_ref[:, o1:o1 + S_WIN_COLS], b_new], axis=1)
    o_win = attend(scores, bias_w, [vwin_ref[...].astype(BF16), nv])

    sel = _top_n_mask_by_rank(score, cur + 1).astype(BF16)
    mask_add = (_dot(sel, expand_ref[...]) - 1.0) * (-NEG)

    keep = NSA_KV_HEADS * WINDOW - NEW_ROWS
    kwin_o_ref[0:keep, :] = kwin_ref[NEW_ROWS:NSA_KV_HEADS * WINDOW, :]
    kwin_o_ref[keep:keep + NEW_ROWS, :] = new_ref[4]
    vwin_o_ref[0:keep, :] = vwin_ref[NEW_ROWS:NSA_KV_HEADS * WINDOW, :]
    vwin_o_ref[keep:keep + NEW_ROWS, :] = new_ref[5]

    nk, nv = new_tile(2), new_tile(3)
    scores = [_dot_nt(qs, pages(2, p).astype(BF16)) for p in range(n_pg)] + [_dot_nt(qs, nk)]
    values = [pages(3, p).astype(BF16) for p in range(n_pg)] + [nv]
    o_sel = attend(scores, bias_ref[:, o0:o1] + mask_add, values)

    g = jax.nn.sigmoid(gate_ref[0])
    o_ref[0] = g[:, 0:1] * o_cmp + g[:, 1:2] * o_sel + g[:, 2:3] * o_win


def _nsa_sample(page_table, q_s, gate_s, new_s, kwin, vwin, pools, cmp_w, bias, cover, expand):
    win_rows = NSA_KV_HEADS * WINDOW

    def full(a):
        return pl.BlockSpec(a.shape, lambda b, pt, n=a.ndim: (0,) * n)

    def per_b(a):
        return pl.BlockSpec((1,) + a.shape[1:], lambda b, pt, n=a.ndim: (b,) + (0,) * (n - 1))
    win_spec = pl.BlockSpec((win_rows, HEAD_DIM), lambda b, pt: (b, 0))
    consts = list(cmp_w) + [bias, cover, expand]
    grid_spec = pltpu.PrefetchScalarGridSpec(
        num_scalar_prefetch=1,
        grid=(DEC_BATCH,),
        in_specs=[per_b(q_s), per_b(gate_s),
                  pl.BlockSpec((N_KV_ARRAYS, NEW_ROWS, HEAD_DIM), lambda b, pt: (0, b, 0)), win_spec, win_spec]
        + [pl.BlockSpec(memory_space=pl.ANY)] * 4 + [full(a) for a in consts],
        out_specs=[per_b(q_s), win_spec, win_spec],
        scratch_shapes=[pltpu.VMEM((2 * S_CMP, CHUNK_PITCH, HEAD_DIM), F32)] * 2
        + [pltpu.VMEM((2 * N_PAGES * PAGE_ROWS, HEAD_DIM), F32)] * 2
        + [pltpu.SemaphoreType.DMA((2, 4))],
    )
    return pl.pallas_call(
        _nsa_sample_kernel,
        grid_spec=grid_spec,
        out_shape=[jax.ShapeDtypeStruct(q_s.shape, F32),
                   jax.ShapeDtypeStruct(kwin.shape, F32),
                   jax.ShapeDtypeStruct(vwin.shape, F32)],
        compiler_params=_cparams(("arbitrary",)),
        name="nsa_sample",
    )(page_table, q_s, gate_s, new_s, kwin, vwin, *pools, *consts)


def _log_decay(a_blk, wa_ref, ba_ref):
    x = _dot(a_blk.astype(BF16), wa_ref[...]) + ba_ref[...]
    return (jnp.minimum(x, 0.0) - jnp.log1p(jnp.exp(-jnp.abs(x)))) * (1.0 / GLA_TAU)


def _segment_cumsum(g, seg):
    pos = lax.broadcasted_iota(jnp.int32, g.shape, 0) % seg
    cum = g
    sh = 1
    while sh < seg:
        cum = cum + jnp.where(pos >= sh, pltpu.roll(cum, sh, axis=0), 0.0)
        sh *= 2
    return cum


def _gla_prompt_kernel(q_ref, k_ref, v_ref, r_ref, a_ref, wa_ref, ba_ref, gn_ref, o_ref, st_o_ref, st_ref,
                       *, n_blk, tb):
    tbi = pl.program_id(0)

    @pl.when(tbi == 0)
    def _():
        st_ref[...] = jnp.zeros_like(st_ref)

    c = GLA_CHUNK
    tril = lax.broadcasted_iota(jnp.int32, (c, c), 0) >= lax.broadcasted_iota(jnp.int32, (c, c), 1)
    cum = _segment_cumsum(_log_decay(a_ref[...], wa_ref, ba_ref), c)
    q = q_ref[...] * (GLA_DK ** -0.5)
    k = k_ref[...]
    v = v_ref[...].astype(BF16)
    qe = (q * jnp.exp(cum)).astype(BF16)
    kd = (k * jnp.exp(-cum)).astype(BF16)
    heads = [(slice(h * GLA_DK, (h + 1) * GLA_DK), slice(h * GLA_DV, (h + 1) * GLA_DV)) for h in range(GLA_HEADS)]
    sts = [st_ref[h] for h in range(GLA_HEADS)]
    outs = [[] for _ in range(GLA_HEADS)]
    for ci in range(tb // c):
        r = slice(ci * c, (ci + 1) * c)
        last = cum[ci * c + c - 1:ci * c + c, :]
        kl = (k[r] * jnp.exp(last - cum[r])).astype(BF16)
        decay = jnp.exp(last)
        for h, (dk, dv) in enumerate(heads):
            att = jnp.where(tril, _dot_nt(qe[r, dk], kd[r, dk]), 0.0)
            outs[h].append(_dot_nt(qe[r, dk], sts[h].astype(BF16)) + _dot(att.astype(BF16), v[r, dv]))
            sts[h] = decay[:, dk] * sts[h] + _dot_tn(v[r, dv], kl[:, dk])
    for h, (dk, dv) in enumerate(heads):
        st_ref[h] = sts[h]
        o = jnp.concatenate(outs[h], axis=0)
        o_ref[:, dv] = _rms(o, gn_ref[...]) * _silu(r_ref[:, dv])

    @pl.when(tbi == n_blk - 1)
    def _():
        st_o_ref[...] = st_ref[...]


def _gla_prompt(z, wa, ba, gn, tb=256):
    n_blk = SEQ // tb
    hk, hv = GLA_HEADS * GLA_DK, GLA_HEADS * GLA_DV
    st_shape = (GLA_HEADS, GLA_DV, GLA_DK)
    return pl.pallas_call(
        functools.partial(_gla_prompt_kernel, n_blk=n_blk, tb=tb),
        grid=(n_blk,),
        in_specs=[pl.BlockSpec((tb, hk), lambda i: (i, Z_QG // hk)),
                  pl.BlockSpec((tb, hk), lambda i: (i, Z_KG // hk)),
                  pl.BlockSpec((tb, hv), lambda i: (i, Z_VG // hv)),
                  pl.BlockSpec((tb, hv), lambda i: (i, Z_RG // hv)),
                  pl.BlockSpec((tb, 128), lambda i: (i, Z_MISC // 128)),
                  pl.BlockSpec((128, hk), lambda i: (0, 0)),
                  pl.BlockSpec((1, hk), lambda i: (0, 0)),
                  pl.BlockSpec((1, GLA_DV), lambda i: (0, 0))],
        out_specs=[pl.BlockSpec((tb, hv), lambda i: (i, 0)),
                   pl.BlockSpec(st_shape, lambda i: (0, 0, 0))],
        out_shape=[jax.ShapeDtypeStruct((SEQ, hv), F32), jax.ShapeDtypeStruct(st_shape, F32)],
        scratch_shapes=[pltpu.VMEM(st_shape, F32)],
        compiler_params=_cparams(("arbitrary",)),
        name="gla_prompt",
    )(z, z, z, z, z, wa, ba, gn)


GS_B = 4


def _gla_sample_kernel(q_ref, k_ref, v_ref, r_ref, a_ref, wa_ref, ba_ref, gn_ref, s_ref, o_ref, s_o_ref):
    rows = GS_B * DEC_SEQ
    a = a_ref[...]
    ri = lax.broadcasted_iota(jnp.int32, (rows, rows), 0)
    ci = lax.broadcasted_iota(jnp.int32, (rows, rows), 1)
    same_causal = (ri // DEC_SEQ == ci // DEC_SEQ) & (ri >= ci)
    row_b = lax.broadcasted_iota(jnp.int32, (rows, 1), 0) // DEC_SEQ
    ones = jnp.ones((rows, 128), BF16)
    for h in range(GLA_HEADS):
        dk = slice(h * GLA_DK, (h + 1) * GLA_DK)
        dv = slice(h * GLA_DV, (h + 1) * GLA_DV)
        g = _log_decay(a, wa_ref.at[:, dk], ba_ref.at[:, dk])
        cum = _segment_cumsum(g, DEC_SEQ)
        q = q_ref[:, dk] * (GLA_DK ** -0.5)
        k = k_ref[:, dk]
        v = v_ref[:, dv].astype(BF16)
        qe = (q * jnp.exp(cum)).astype(BF16)
        kd = (k * jnp.exp(-cum)).astype(BF16)
        att = jnp.where(same_causal, _dot_nt(qe, kd), 0.0)
        o = _dot(att.astype(BF16), v)
        for b in range(GS_B):
            mine = row_b == b
            last = cum[b * DEC_SEQ + DEC_SEQ - 1:(b + 1) * DEC_SEQ, :]
            s = s_ref[b, h]
            o = o + jnp.where(mine, _dot(qe, s.astype(BF16)), 0.0)
            kl = jnp.where(mine, k * jnp.exp(last - cum), 0.0)
            hi = jnp.where(mine, g, 0.0).astype(BF16)
            lo = (jnp.where(mine, g, 0.0) - hi.astype(F32)).astype(BF16)
            last_col = (_dot_tn(hi, ones) + _dot_tn(lo, ones))[:, 0:1]
            s_o_ref[b, h] = jnp.exp(last_col) * s + _dot_tn(kl.astype(BF16), v)
        o_ref[:, dv] = _rms(o, gn_ref[...]) * _silu(r_ref[:, dv])


def _gla_sample(zs, state, wa, ba, gn):
    rows = GS_B * DEC_SEQ
    n = DEC_BATCH * DEC_SEQ
    hk, hv = GLA_HEADS * GLA_DK, GLA_HEADS * GLA_DV
    st_spec = pl.BlockSpec((GS_B, GLA_HEADS, GLA_DK, GLA_DV), lambda i: (i, 0, 0, 0))
    return pl.pallas_call(
        _gla_sample_kernel,
        grid=(DEC_BATCH // GS_B,),
        in_specs=[pl.BlockSpec((rows, hk), lambda i: (i, Z_QG // hk)),
                  pl.BlockSpec((rows, hk), lambda i: (i, Z_KG // hk)),
                  pl.BlockSpec((rows, hv), lambda i: (i, Z_VG // hv)),
                  pl.BlockSpec((rows, hv), lambda i: (i, Z_RG // hv)),
                  pl.BlockSpec((rows, 128), lambda i: (i, Z_MISC // 128)),
                  pl.BlockSpec((128, hk), lambda i: (0, 0)),
                  pl.BlockSpec((1, hk), lambda i: (0, 0)),
                  pl.BlockSpec((1, GLA_DV), lambda i: (0, 0)),
                  st_spec],
        out_specs=[pl.BlockSpec((rows, hv), lambda i: (i, 0)), st_spec],
        out_shape=[jax.ShapeDtypeStruct((n, hv), F32), jax.ShapeDtypeStruct(state.shape, F32)],
        compiler_params=_cparams(("parallel",)),
        name="gla_sample",
    )(zs, zs, zs, zs, zs, wa, ba, gn, state)


def _softmax_rows(s):
    m = jnp.max(s, axis=-1, keepdims=True)
    e = jnp.exp(s - m)
    return e / jnp.sum(e, axis=-1, keepdims=True)


def _mem_prompt_kernel(q_ref, k_ref, v_ref, o_ref):
    for h in range(MEM_HEADS):
        d = slice(h * MEM_HEAD_DIM, (h + 1) * MEM_HEAD_DIM)
        q = (q_ref[:, d] * (MEM_HEAD_DIM ** -0.5)).astype(BF16)
        p = _softmax_rows(_dot_nt(q, k_ref[:, d].astype(BF16)))
        o_ref[:, d] = _dot(p.astype(BF16), v_ref[:, d].astype(BF16))


def _mem_prompt(qm, memkv, tq=256):
    w = MEM_HEADS * MEM_HEAD_DIM
    return pl.pallas_call(
        _mem_prompt_kernel,
        grid=(SEQ // tq,),
        in_specs=[pl.BlockSpec((tq, w), lambda i: (i, 0)),
                  pl.BlockSpec((MEM_TOKENS, w), lambda i: (0, 0)),
                  pl.BlockSpec((MEM_TOKENS, w), lambda i: (0, 1))],
        out_specs=pl.BlockSpec((tq, w), lambda i: (i, 0)),
        out_shape=jax.ShapeDtypeStruct((SEQ, w), F32),
        compiler_params=_cparams(("parallel",)),
        name="mem_prompt",
    )(qm, memkv, memkv)


def _mem_sample_kernel(q_ref, k_ref, v_ref, o_ref):
    rows = MEM_HEADS * DEC_SEQ
    cols = MEM_HEADS * MEM_TOKENS
    row_h = lax.broadcasted_iota(jnp.int32, (rows, cols), 0) // DEC_SEQ
    col_h = lax.broadcasted_iota(jnp.int32, (rows, cols), 1) % MEM_HEADS
    for b in range(MS_B):
        kv_rows = slice(b * cols, (b + 1) * cols)
        q = (q_ref[b] * (MEM_HEAD_DIM ** -0.5)).astype(BF16)
        s = jnp.where(row_h == col_h, _dot_nt(q, k_ref[kv_rows, :].astype(BF16)), NEG)
        o_ref[b] = _dot(_softmax_rows(s).astype(BF16), v_ref[kv_rows, :].astype(BF16))


MS_B = 4


def _mem_sample(q_s, k_mem, v_mem):
    rows = MEM_HEADS * DEC_SEQ
    kv_spec = pl.BlockSpec((MS_B * MEM_HEADS * MEM_TOKENS, MEM_HEAD_DIM), lambda b: (b, 0))
    return pl.pallas_call(
        _mem_sample_kernel,
        grid=(DEC_BATCH // MS_B,),
        in_specs=[pl.BlockSpec((MS_B, rows, MEM_HEAD_DIM), lambda b: (b, 0, 0)), kv_spec, kv_spec],
        out_specs=pl.BlockSpec((MS_B, rows, MEM_HEAD_DIM), lambda b: (b, 0, 0)),
        out_shape=jax.ShapeDtypeStruct((DEC_BATCH, rows, MEM_HEAD_DIM), F32),
        compiler_params=_cparams(("parallel",)),
        name="mem_sample",
    )(q_s, k_mem, v_mem)


W_IN_SEGMENTS = ((0, 1024, Z_QN), (1024, 1536, Z_KV), (2560, 24, Z_MISC + MISC_GATES), (2584, 512, Z_QG),
                 (3096, 512, Z_KG), (3608, 1024, Z_VG), (4632, 1024, Z_RG), (5656, 16, Z_MISC + MISC_A))
W_IN_COLS = 5672


def _permute_w_in_kernel(w_ref, o_ref):
    o_ref[:, Z_MISC:Z_W] = jnp.zeros((o_ref.shape[0], Z_W - Z_MISC), BF16)
    for src, width, dst in W_IN_SEGMENTS:
        o_ref[:, dst:dst + width] = w_ref[:, src:src + width].astype(BF16)


def _permute_w_in(w_in, tr=256):
    d = w_in.shape[0]
    return pl.pallas_call(
        _permute_w_in_kernel,
        grid=(d // tr,),
        in_specs=[pl.BlockSpec((tr, W_IN_COLS), lambda i: (i, 0))],
        out_specs=pl.BlockSpec((tr, Z_W), lambda i: (i, 0)),
        out_shape=jax.ShapeDtypeStruct((d, Z_W), BF16),
        compiler_params=_cparams(("parallel",)),
        name="permute_w_in",
    )(w_in)


def kernel(x_prompt, x_sample, mem_prompt, cache_k_cmp, cache_v_cmp, cache_k_sel, cache_v_sel, cache_k_win,
           cache_v_win, state_gla, cache_k_mem, cache_v_mem, page_table, norm_ffn1, ffn1_w_gate, ffn1_w_up,
           ffn1_w_down, norm_mix, w_in, w_out, cmp_pos_k, cmp_w1_k, cmp_w2_k, cmp_pos_v, cmp_w1_v, cmp_w2_v,
           rel_bias, gla_w_a2, gla_b_a, gla_norm, norm_mem, norm_mem_src, w_mem_q, w_mem_k, w_mem_v, w_mem_o,
           norm_ffn2, ffn2_w_gate, ffn2_w_up, ffn2_w_down, norm_final):
    bf = lambda a: a.astype(BF16)
    row = lambda a: a.reshape(1, -1)
    nb, ns = DEC_BATCH, DEC_SEQ
    kvw = NSA_KV_HEADS * HEAD_DIM

    h1, h1n = _ffn([x_prompt[0], x_sample.reshape(nb * ns, D_MODEL)], row(norm_ffn1[0]), bf(ffn1_w_gate[0]),
                   bf(ffn1_w_up[0]), bf(ffn1_w_down[0]), row(norm_mix[0]), False)
    z, kvb, new_s, *new_p = _proj_in(h1n, _permute_w_in(w_in[0]))
    rows_p = [a.reshape(1, 1, SEQ, NSA_KV_HEADS, HEAD_DIM) for a in new_p]
    rows_s = [new_s[j].reshape(1, nb, ns, NSA_KV_HEADS, HEAD_DIM) for j in range(4)]

    tab_p = _bias_tables(rel_bias, _prompt_bucket_table(), "bias_prompt")
    far = rel_bias[NUM_BUCKETS - 1][:, None, None]
    near = tab_p[:, :3 * Q_TILE]
    tab_p = jnp.concatenate([jnp.where(near > M_INIT, near - far, NEG), tab_p[:, 3 * Q_TILE:]], axis=1)
    tab_p = jnp.where(tab_p > M_INIT, tab_p * LOG2E, NEG)
    tab_p = tab_p.reshape(NSA_KV_HEADS, NSA_GROUP, -1, Q_TILE).transpose(0, 2, 1, 3)
    tab_p = tab_p.reshape(NSA_KV_HEADS, -1, NSA_GROUP * Q_TILE)
    tab_s = _bias_tables(rel_bias, _sample_bucket_table(), "bias_sample")
    tab_s = tab_s.reshape(NSA_KV_HEADS, NSA_GROUP, NSA_KV_HEADS, ns, -1)
    tab_s = jnp.stack([tab_s[h, :, h] for h in range(NSA_KV_HEADS)], axis=1).reshape(S_ROWS, -1)

    cmp_w1 = bf(jnp.stack([cmp_w1_k[0], cmp_w1_v[0]]))
    cmp_w1 = jnp.concatenate([cmp_w1[:, :CMP_STRIDE], cmp_w1[:, CMP_STRIDE:]], axis=-1)
    cmp_w2 = bf(jnp.stack([cmp_w2_k[0], cmp_w2_v[0]]))
    cmp_pos = _pos_term(jnp.stack([cmp_pos_k[0], cmp_pos_v[0]]), cmp_w1)
    kcp = _compress_prompt(z, cmp_pos, cmp_w1, cmp_w2)
    cover_p = jnp.asarray(_cover_np(SLAB, N_SEL, lambda u, j: u - 4 * j - SLAB_OFF + 4 * REL0).T, BF16)
    erel = (np.arange(N_SEL)[None, None, :] == REL0 - 2 * np.arange(N_QT)[:, None, None]
            + (np.arange(Q_TILE)[None, :, None] >= SEL_BLOCK))
    o_nsa_p = _nsa_prompt(z, kvb, kcp, tab_p, cover_p, jnp.asarray(erel, BF16))

    zs = z[SEQ:]
    q_s = zs[:, Z_QN:Z_QN + NSA_HEADS * HEAD_DIM].reshape(nb, ns, NSA_KV_HEADS, NSA_GROUP, HEAD_DIM)
    q_s = q_s.transpose(0, 3, 2, 1, 4).reshape(nb, S_ROWS, HEAD_DIM)
    gate_s = zs[:, Z_MISC + MISC_GATES:Z_MISC + MISC_GATES + 3 * NSA_HEADS]
    gate_s = gate_s.reshape(nb, ns, NSA_KV_HEADS, NSA_GROUP, 3).transpose(0, 3, 2, 1, 4).reshape(nb, S_ROWS, 3)
    as_rows = lambda c: c.reshape(-1, HEAD_DIM)
    pools = ([c.reshape(-1, CHUNK_ROWS, HEAD_DIM) for c in (cache_k_cmp, cache_v_cmp)]
             + [as_rows(c) for c in (cache_k_sel, cache_v_sel)])
    cover_s = _cover_np(S_CMP, 128, lambda c, j: c - 4 * j)
    cover_s = jnp.asarray(np.concatenate([cover_s] * NSA_KV_HEADS, axis=0), BF16)
    expand_s = jnp.asarray(np.arange(128)[:, None]
                           == (np.arange(S_SEL_COLS)[None, :] // (NSA_KV_HEADS * SEL_BLOCK)), BF16)
    o_nsa_s, kwin_s, vwin_s = _nsa_sample(
        page_table, q_s, gate_s, new_s, as_rows(cache_k_win), as_rows(cache_v_win), pools,
        (cmp_pos[0], cmp_w1[0], cmp_w2[0], cmp_pos[1], cmp_w1[1], cmp_w2[1]),
        tab_s, cover_s, expand_s)
    o_nsa_s = o_nsa_s.reshape(nb, NSA_GROUP, NSA_KV_HEADS, ns, HEAD_DIM).transpose(0, 3, 2, 1, 4)
    o_nsa_s = o_nsa_s.reshape(nb * ns, NSA_HEADS * HEAD_DIM)

    wa = bf(jnp.zeros((128, GLA_HEADS * GLA_DK), F32).at[MISC_A:MISC_A + GLA_RANK].set(gla_w_a2[0]))
    ba, gn = row(gla_b_a[0]), row(gla_norm[0])
    o_gla_p, st_p = _gla_prompt(z, wa, ba, gn)
    o_gla_s, st_s = _gla_sample(zs, state_gla[0], wa, ba, gn)

    half = NSA_HEADS * HEAD_DIM
    h2 = _matmul_res(h1, [(o_nsa_p, o_nsa_s), (o_gla_p, o_gla_s)], [bf(w_out[0][:half]), bf(w_out[0][half:])],
                     512, "proj_out")

    memkv = _norm_matmul(mem_prompt[0], row(norm_mem_src[0]),
                         bf(jnp.concatenate([w_mem_k[0], w_mem_v[0]], axis=1)), MEM_TOKENS, 512, "mem_kv")
    mw = MEM_HEADS * MEM_HEAD_DIM
    qm = _norm_matmul(h2, row(norm_mem[0]), bf(w_mem_q[0]), 512, mw, "mem_q")
    om_p = _mem_prompt(qm, memkv)
    qm_s = qm[SEQ:].reshape(nb, ns, MEM_HEADS, MEM_HEAD_DIM).transpose(0, 2, 1, 3)
    om_s = _mem_sample(qm_s.reshape(nb, MEM_HEADS * ns, MEM_HEAD_DIM),
                       as_rows(cache_k_mem), as_rows(cache_v_mem))
    om_s = om_s.reshape(nb, MEM_HEADS, ns, MEM_HEAD_DIM).transpose(0, 2, 1, 3).reshape(nb * ns, mw)
    h3 = _matmul_res(h2, [(om_p, om_s)], [bf(w_mem_o[0])], 512, "mem_out")

    y_p, y_s = _ffn([h3], row(norm_ffn2[0]), bf(ffn2_w_gate[0]), bf(ffn2_w_up[0]), bf(ffn2_w_down[0]),
                    row(norm_final), True)

    mem_shape = (1, 1, MEM_TOKENS, MEM_HEADS, MEM_HEAD_DIM)
    win_shape = (1, nb, WINDOW, NSA_KV_HEADS, HEAD_DIM)
    return (y_p.reshape(1, SEQ, D_MODEL), y_s.reshape(nb, ns, D_MODEL),
            rows_p[0], rows_p[1], rows_p[2], rows_p[3],
            rows_p[4][:, :, SEQ - WINDOW:], rows_p[5][:, :, SEQ - WINDOW:],
            st_p.transpose(0, 2, 1).reshape(1, 1, GLA_HEADS, GLA_DK, GLA_DV),
            memkv[:, :mw].reshape(mem_shape), memkv[:, mw:].reshape(mem_shape),
            rows_s[0], rows_s[1], rows_s[2], rows_s[3],
            kwin_s.reshape(win_shape), vwin_s.reshape(win_shape),
            st_s.reshape(1, nb, GLA_HEADS, GLA_DK, GLA_DV))
```

```python
import functools
import math

import numpy as np
import jax
import jax.numpy as jnp
from jax import lax
from jax.experimental import pallas as pl
from jax.experimental.pallas import tpu as pltpu

F32 = jnp.float32
BF16 = jnp.bfloat16

D_MODEL = 2048
SEQ = 8192
DEC_BATCH = 128
DEC_SEQ = 4
PAST_LEN = 2048
PAGE_SIZE = 128
N_PAGES = PAST_LEN // PAGE_SIZE
HEAD_DIM = 128
NSA_HEADS = 8
NSA_KV_HEADS = 2
NSA_GROUP = 4
CMP_BLOCK = 32
CMP_STRIDE = 16
SEL_BLOCK = 64
SEL_TOPN = 16
WINDOW = 512
FORCED_SCORE = 1.0e4
GLA_HEADS = 4
GLA_DV = 256
GLA_DK = 128
GLA_RANK = 16
GLA_TAU = 16.0
GLA_CHUNK = 32
MEM_TOKENS = 256
MEM_HEADS = 4
MEM_HEAD_DIM = 128
D_FF = 5632
NUM_BUCKETS = 32
MAX_DISTANCE = 128
RMS_EPS = 1e-6

N_TOK = SEQ + DEC_BATCH * DEC_SEQ
Z_KV, Z_QG, Z_VG, Z_RG, Z_QN, Z_KG, Z_MISC = 0, 1536, 2048, 3072, 4096, 5120, 5632
Z_W = 5760
MISC_GATES, MISC_A = 0, 24

NEG = -1e30
LOG2E = math.log2(math.e)
ONES_ROWS = 16
M_INIT = -1e29

Q_TILE = 128
N_QT = SEQ // Q_TILE
N_CMP_PAD = SEQ // CMP_STRIDE
SLAB = N_CMP_PAD + 128
SLAB_OFF = SLAB - 16
KC_ROWS = SLAB_OFF + N_CMP_PAD + 16
N_SEL = SEQ // SEL_BLOCK
REL0 = N_SEL - 2
CMP_GROUP = 4
SWEEP_TILES = 8
SWEEP_PARTS = 4
WIN_PART_TILES = 2
FLASH_SPLIT = 1

VMEM_LIMIT = 56 * 1024 * 1024


def _cparams(sem):
    return pltpu.CompilerParams(dimension_semantics=sem, vmem_limit_bytes=VMEM_LIMIT)


def _dot(a, b):
    return jnp.dot(a, b, preferred_element_type=F32)


def _dot_nt(a, b):
    return lax.dot_general(a, b, (((1,), (1,)), ((), ())), preferred_element_type=F32)


def _dot_tn(a, b):
    return lax.dot_general(a, b, (((0,), (0,)), ((), ())), preferred_element_type=F32)


def _rms(x, g):
    return x * lax.rsqrt(jnp.mean(x * x, axis=-1, keepdims=True) + RMS_EPS) * g


def _silu(x):
    return x * jax.nn.sigmoid(x)


def _ffn_kernel(*refs, n_ff, n_first, two_in, two_out):
    refs = list(refs)
    x_refs = [refs.pop(0) for _ in range(2 if two_in else 1)]
    g_ref, wg_ref, wu_ref, wd_ref, gf_ref = refs[:5]
    o_refs = refs[5:7]
    xn_ref, acc_ref = refs[-2:]
    i = pl.program_id(0)
    j = pl.program_id(1)

    def x_tile():
        return jnp.where(i < n_first, x_refs[0][...], x_refs[1][...]) if two_in else x_refs[0][...]

    @pl.when(j == 0)
    def _():
        xn_ref[...] = _rms(x_tile(), g_ref[...]).astype(BF16)
        acc_ref[...] = jnp.zeros_like(acc_ref)

    xn = xn_ref[...]
    hid = _silu(_dot(xn, wg_ref[...])) * _dot(xn, wu_ref[...])
    acc_ref[...] += _dot(hid.astype(BF16), wd_ref[...])

    @pl.when(j == n_ff - 1)
    def _():
        h = x_tile() + 0.5 * acc_ref[...]
        if two_out:
            h = _rms(h, gf_ref[...])

            @pl.when(i < n_first)
            def _():
                o_refs[0][...] = h

            @pl.when(i >= n_first)
            def _():
                o_refs[1][...] = h
        else:
            o_refs[0][...] = h
            o_refs[1][...] = _rms(h, gf_ref[...]).astype(BF16)


FFN_TM = 512
FFN_TF = 512


def _ffn(xs, g, wg, wu, wd, gf, split_out):
    tm, tf, d = FFN_TM, FFN_TF, D_MODEL
    n_ff = D_FF // tf
    n_first = SEQ // tm
    two_in = len(xs) == 2

    def first(i, j):
        return (jnp.minimum(i, n_first - 1), 0)

    def second(i, j):
        return (jnp.maximum(i - n_first, 0), 0)
    whole = pl.BlockSpec((tm, d), lambda i, j: (i, 0))
    pair = [pl.BlockSpec((tm, d), first), pl.BlockSpec((tm, d), second)]
    n_s = DEC_BATCH * DEC_SEQ
    return pl.pallas_call(
        functools.partial(_ffn_kernel, n_ff=n_ff, n_first=n_first, two_in=two_in, two_out=split_out),
        grid=(N_TOK // tm, n_ff),
        in_specs=(pair if two_in else [whole])
        + [pl.BlockSpec((1, d), lambda i, j: (0, 0)),
           pl.BlockSpec((d, tf), lambda i, j: (0, j)),
           pl.BlockSpec((d, tf), lambda i, j: (0, j)),
           pl.BlockSpec((tf, d), lambda i, j: (j, 0)),
           pl.BlockSpec((1, d), lambda i, j: (0, 0))],
        out_specs=pair if split_out else [whole, whole],
        out_shape=([jax.ShapeDtypeStruct((SEQ, d), F32), jax.ShapeDtypeStruct((n_s, d), F32)] if split_out
                   else [jax.ShapeDtypeStruct((N_TOK, d), F32), jax.ShapeDtypeStruct((N_TOK, d), BF16)]),
        scratch_shapes=[pltpu.VMEM((tm, d), BF16), pltpu.VMEM((tm, d), F32)],
        compiler_params=_cparams(("arbitrary", "arbitrary")),
        name="ffn",
    )(*xs, g, wg, wu, wd, gf)


def _norm_matmul_kernel(x_ref, g_ref, w_ref, o_ref, xn_ref):
    @pl.when(pl.program_id(1) == 0)
    def _():
        xn_ref[...] = _rms(x_ref[...], g_ref[...]).astype(BF16)

    o_ref[...] = _dot(xn_ref[...], w_ref[...])


def _norm_matmul(x, g, w, tm, tn, name):
    n, d = x.shape
    dout = w.shape[1]
    return pl.pallas_call(
        _norm_matmul_kernel,
        grid=(n // tm, dout // tn),
        in_specs=[pl.BlockSpec((tm, d), lambda i, j: (i, 0)),
                  pl.BlockSpec((1, d), lambda i, j: (0, 0)),
                  pl.BlockSpec((d, tn), lambda i, j: (0, j))],
        out_specs=pl.BlockSpec((tm, tn), lambda i, j: (i, j)),
        out_shape=jax.ShapeDtypeStruct((n, dout), F32),
        scratch_shapes=[pltpu.VMEM((tm, d), BF16)],
        compiler_params=_cparams(("parallel", "arbitrary")),
        name=name,
    )(x, g, w)


N_KV_ARRAYS = 6
KV_W = NSA_KV_HEADS * HEAD_DIM


def _proj_in_kernel(x_ref, w_ref, o_ref, kvb_ref, rows_s_ref, *rows_p_refs, n_first):
    j, i = pl.program_id(0), pl.program_id(1)
    res = _dot(x_ref[...], w_ref[...])
    o_ref[...] = res
    tm = res.shape[0]

    @pl.when(j == 0)
    def _():
        kvb_ref[...] = res[:, Z_KV:Z_KV + N_KV_ARRAYS * KV_W].astype(BF16)

        def head_rows(a, h):
            c0 = Z_KV + a * KV_W + h * HEAD_DIM
            return res[:, c0:c0 + HEAD_DIM]

        @pl.when(i < n_first)
        def _():
            for a in range(N_KV_ARRAYS):
                for h in range(NSA_KV_HEADS):
                    rows_p_refs[a][pl.ds(h, tm, stride=NSA_KV_HEADS), :] = head_rows(a, h)

        @pl.when(i >= n_first)
        def _():
            for a in range(N_KV_ARRAYS):
                for h in range(NSA_KV_HEADS):
                    rows_s_ref.at[a][pl.ds(h, tm, stride=NSA_KV_HEADS), :] = head_rows(a, h)


def _proj_in(x, w, tm=512, tn=1920):
    n, d = x.shape
    n_i, n_first = n // tm, SEQ // tm
    n_s = n - SEQ
    assert Z_KV == 0 and N_KV_ARRAYS * KV_W <= tn and n_s == tm

    def once(i_of):
        return lambda j, i: jnp.where(j == 0, i_of(i), i_of(n_i - 1))
    p_idx = once(lambda i: jnp.minimum(i, n_first - 1))
    return pl.pallas_call(
        functools.partial(_proj_in_kernel, n_first=n_first),
        grid=(Z_W // tn, n_i),
        in_specs=[pl.BlockSpec((tm, d), lambda j, i: (i, 0)),
                  pl.BlockSpec((d, tn), lambda j, i: (0, j))],
        out_specs=[pl.BlockSpec((tm, tn), lambda j, i: (i, j)),
                   pl.BlockSpec((tm, N_KV_ARRAYS * KV_W), lambda j, i: (once(lambda i: i)(j, i), 0)),
                   pl.BlockSpec((N_KV_ARRAYS, NSA_KV_HEADS * n_s, HEAD_DIM), lambda j, i: (0, 0, 0))]
        + [pl.BlockSpec((NSA_KV_HEADS * tm, HEAD_DIM), lambda j, i: (p_idx(j, i), 0))] * N_KV_ARRAYS,
        out_shape=[jax.ShapeDtypeStruct((n, Z_W), F32),
                   jax.ShapeDtypeStruct((n, N_KV_ARRAYS * KV_W), BF16),
                   jax.ShapeDtypeStruct((N_KV_ARRAYS, NSA_KV_HEADS * n_s, HEAD_DIM), F32)]
        + [jax.ShapeDtypeStruct((NSA_KV_HEADS * SEQ, HEAD_DIM), F32)] * N_KV_ARRAYS,
        compiler_params=_cparams(("arbitrary", "arbitrary")),
        name="proj_in",
    )(x, w)


def _matmul_res_kernel(*refs, n_lhs, n_first):
    res_ref = refs[0]
    o_ref = refs[1 + 3 * n_lhs]
    first = pl.program_id(0) < n_first
    acc = res_ref[...]
    for k in range(n_lhs):
        lhs = jnp.where(first, refs[1 + 2 * k][...], refs[2 + 2 * k][...])
        acc = acc + _dot(lhs.astype(BF16), refs[1 + 2 * n_lhs + k][...])
    o_ref[...] = acc


def _matmul_res(res, lhs, ws, tm, name):
    n, d = res.shape
    n_first = SEQ // tm

    def pair_specs(width):
        return [pl.BlockSpec((tm, width), lambda i: (jnp.minimum(i, n_first - 1), 0)),
                pl.BlockSpec((tm, width), lambda i: (jnp.maximum(i - n_first, 0), 0))]
    return pl.pallas_call(
        functools.partial(_matmul_res_kernel, n_lhs=len(lhs), n_first=n_first),
        grid=(n // tm,),
        in_specs=([pl.BlockSpec((tm, d), lambda i: (i, 0))]
                  + [spec for a, _ in lhs for spec in pair_specs(a.shape[1])]
                  + [pl.BlockSpec(w.shape, lambda i: (0, 0)) for w in ws]),
        out_specs=pl.BlockSpec((tm, d), lambda i: (i, 0)),
        out_shape=jax.ShapeDtypeStruct((n, d), F32),
        compiler_params=_cparams(("arbitrary",)),
        name=name,
    )(res, *[a for pair in lhs for a in pair], *ws)


def _rel_bucket_np(dist):
    n = np.maximum(dist, 0)
    exact = NUM_BUCKETS // 2
    nf = np.maximum(n, 1).astype(np.float32)
    large = exact + (np.log(nf / np.float32(exact)) / np.float32(math.log(MAX_DISTANCE / exact))
                     * np.float32(NUM_BUCKETS - exact)).astype(np.int32)
    return np.where(n < exact, n, np.minimum(large, NUM_BUCKETS - 1)).astype(np.int32)


def _bucket_or_masked(dist, valid):
    return np.where(valid, _rel_bucket_np(dist), -1).astype(np.int32)


def _prompt_bucket_table():
    i = np.arange(Q_TILE)[:, None]
    j = np.arange(Q_TILE)[None, :]
    u = np.arange(SLAB)[None, :]
    dist_c = i - CMP_STRIDE * u + (CMP_STRIDE * SLAB_OFF - (CMP_BLOCK - 1))
    diag = _bucket_or_masked(i - j, i - j >= 0)
    prev = _bucket_or_masked(Q_TILE + i - j, np.ones((Q_TILE, Q_TILE), bool))
    first = _bucket_or_masked(WINDOW + i - j, j > i)
    cmp_ = _bucket_or_masked(dist_c, dist_c >= 0)
    return np.concatenate([diag, prev, first, cmp_], axis=1).T


S_CMP = PAST_LEN // CMP_STRIDE
S_CMP_COLS = NSA_KV_HEADS * S_CMP
S_NEW_COLS = 128
S_SEL_COLS = NSA_KV_HEADS * PAST_LEN + S_NEW_COLS
S_WIN_COLS = NSA_KV_HEADS * WINDOW


def _sample_bucket_table():
    i = np.arange(DEC_SEQ)[:, None]
    pos = PAST_LEN + i
    rows = []
    for h in range(NSA_KV_HEADS):
        col = np.arange(S_CMP_COLS)[None, :]
        c = col % S_CMP
        dist_c = pos - (c * CMP_STRIDE + CMP_BLOCK - 1)
        cmp_ = _bucket_or_masked(dist_c, (col // S_CMP == h) & (c < S_CMP - 1) & (dist_c >= 0))
        col = np.arange(S_SEL_COLS)[None, :]
        key = col // NSA_KV_HEADS
        sel = _bucket_or_masked(pos - key, (col % NSA_KV_HEADS == h) & (key <= pos))
        col = np.arange(S_WIN_COLS)[None, :]
        dist_w = pos - (PAST_LEN - WINDOW + col // NSA_KV_HEADS)
        win = _bucket_or_masked(dist_w, (col % NSA_KV_HEADS == h) & (dist_w < WINDOW))
        rows.append(np.concatenate([cmp_, sel, win], axis=1))
    return np.concatenate(rows, axis=0)


def _bias_table_kernel(tab_ref, idx_ref, o_ref):
    h = pl.program_id(0)
    idx = idx_ref[...]
    out = jnp.full(idx.shape, NEG, F32)
    for b in range(NUM_BUCKETS):
        out = jnp.where(idx == b, tab_ref[b, h], out)
    o_ref[0] = out


def _bias_tables(rel_bias, idx, name):
    r, c = idx.shape
    return pl.pallas_call(
        _bias_table_kernel,
        grid=(NSA_HEADS,),
        in_specs=[pl.BlockSpec(memory_space=pltpu.SMEM),
                  pl.BlockSpec((r, c), lambda h: (0, 0))],
        out_specs=pl.BlockSpec((1, r, c), lambda h: (h, 0, 0)),
        out_shape=jax.ShapeDtypeStruct((NSA_HEADS, r, c), F32),
        compiler_params=_cparams(("arbitrary",)),
        name=name,
    )(rel_bias, jnp.asarray(idx))


def _cover_np(n_cmp_cols, n_blk_cols, delta_of):
    u = np.arange(n_cmp_cols)[:, None]
    j = np.arange(n_blk_cols)[None, :]
    delta = delta_of(u, j)
    shared = np.minimum(CMP_STRIDE * delta + CMP_BLOCK, SEL_BLOCK) - np.maximum(CMP_STRIDE * delta, 0)
    return (np.maximum(shared, 0) / CMP_STRIDE).astype(np.float32)


def _compress(load_rows, n_chunk, pos_term_ref, w1_ref, w2_ref):
    parts = []
    for s0 in range(0, CMP_STRIDE, CMP_GROUP):
        lhs = jnp.concatenate([load_rows(s0 + k).astype(BF16) for k in range(CMP_GROUP)], axis=1)
        w = w1_ref[s0:s0 + CMP_GROUP].reshape(CMP_GROUP * HEAD_DIM, 2 * HEAD_DIM)
        parts.append(_dot(lhs, w))
    while len(parts) > 1:
        parts = [a + b for a, b in zip(parts[0::2], parts[1::2])]
    acc = parts[0] + pos_term_ref[0:1, :]
    nxt = pltpu.roll(acc[:, HEAD_DIM:], n_chunk - 1, axis=0)
    hid = _silu(acc[:, :HEAD_DIM] + nxt)
    return _dot(hid.astype(BF16), w2_ref[...])


def _pos_term_kernel(pos_ref, w1_ref, o_ref):
    halves = []
    for half in range(2):
        acc = jnp.zeros((8, HEAD_DIM), F32)
        for s in range(CMP_STRIDE):
            row = half * CMP_STRIDE + s
            p = jnp.broadcast_to(pos_ref[0, row:row + 1, :], (8, HEAD_DIM)).astype(BF16)
            acc = acc + _dot(p, w1_ref[0, s][:, half * HEAD_DIM:(half + 1) * HEAD_DIM])
        halves.append(acc)
    o_ref[0] = jnp.concatenate(halves, axis=1)


def _pos_term(pos, w1cat):
    return pl.pallas_call(
        _pos_term_kernel,
        grid=(2,),
        in_specs=[pl.BlockSpec((1, CMP_BLOCK, HEAD_DIM), lambda i: (i, 0, 0)),
                  pl.BlockSpec((1, CMP_STRIDE, HEAD_DIM, 2 * HEAD_DIM), lambda i: (i, 0, 0, 0))],
        out_specs=pl.BlockSpec((1, 8, 2 * HEAD_DIM), lambda i: (i, 0, 0)),
        out_shape=jax.ShapeDtypeStruct((2, 8, 2 * HEAD_DIM), F32),
        compiler_params=_cparams(("arbitrary",)),
        name="pos_term",
    )(pos, w1cat)


def _flash_step(carry, s, v_t):
    outs = []
    w = s.shape[1] // FLASH_SPLIT
    v_ext = jnp.concatenate([v_t, jnp.ones((ONES_ROWS, v_t.shape[1]), BF16)], axis=0)
    for c in range(FLASH_SPLIT):
        m, acc = (x[:, c * w:(c + 1) * w] for x in carry)
        sc = s[:, c * w:(c + 1) * w]
        m_new = jnp.maximum(m, jnp.max(sc, axis=0, keepdims=True))
        acc = jnp.exp2(m - m_new) * acc + _dot(v_ext, jnp.exp2(sc - m_new).astype(BF16))
        outs.append((m_new, acc))
    return tuple(jnp.concatenate([o[i] for o in outs], axis=1) for i in range(2))


def _flash_init(cols):
    return (jnp.full((1, cols), M_INIT, F32), jnp.zeros((HEAD_DIM + ONES_ROWS, cols), F32))


def _flash_finish(carry):
    _, acc = carry
    return acc[:HEAD_DIM] / acc[HEAD_DIM:HEAD_DIM + 1]


def _masked_softmax(s, valid):
    s = jnp.where(valid, s, NEG)
    m = jnp.max(s, axis=-1, keepdims=True)
    e = jnp.where(valid, jnp.exp(s - m), 0.0)
    return e / jnp.maximum(jnp.sum(e, axis=-1, keepdims=True), 1e-30)


def _split_dot(x, w):
    hi = x.astype(BF16)
    lo = (x - hi.astype(F32)).astype(BF16)
    return _dot(hi, w) + _dot(lo, w)


def _top_n_mask(score, index, axis, interleave=()):
    sel = jnp.zeros(score.shape, F32)
    every = SEL_TOPN // (len(interleave) + 1)
    for r in range(SEL_TOPN):
        mx = jnp.max(score, axis=axis, keepdims=True)
        first = jnp.min(jnp.where(score == mx, index, 1e9), axis=axis, keepdims=True)
        hit = index == first
        sel = jnp.where(hit, 1.0, sel)
        score = jnp.where(hit, -jnp.inf, score)
        if (r + 1) % every == 0 and (r + 1) // every <= len(interleave):
            interleave[(r + 1) // every - 1]()
    return sel


def _top_n_mask_by_rank(score, n_cand):
    lane = lax.broadcasted_iota(jnp.int32, score.shape, 1)
    rank = jnp.zeros(score.shape, F32)
    for j in range(n_cand):
        col = score[:, j:j + 1]
        ahead = (col > score) | ((col == score) & (lane > j))
        rank = rank + ahead.astype(F32)
    return ((rank < SEL_TOPN) & (lane < n_cand)).astype(F32)


def _compress_prompt_kernel(rows_ref, pos_ref, w1_ref, w2_ref, o_ref):
    out = _compress(lambda s: rows_ref[pl.ds(s, N_CMP_PAD, stride=CMP_STRIDE), :], N_CMP_PAD,
                    pos_ref.at[0], w1_ref.at[0], w2_ref.at[0])
    real = lax.broadcasted_iota(jnp.int32, (N_CMP_PAD, HEAD_DIM), 0) < N_CMP_PAD - 1
    o_ref[0, 0, 0:SLAB_OFF, :] = jnp.zeros((SLAB_OFF, HEAD_DIM), F32)
    o_ref[0, 0, SLAB_OFF:SLAB_OFF + N_CMP_PAD, :] = jnp.where(real, out, 0.0)
    o_ref[0, 0, SLAB_OFF + N_CMP_PAD:KC_ROWS, :] = jnp.zeros((KC_ROWS - SLAB_OFF - N_CMP_PAD, HEAD_DIM), F32)


def _compress_prompt(z, pos, w1, w2):
    kv_blk = Z_KV // HEAD_DIM
    return pl.pallas_call(
        _compress_prompt_kernel,
        grid=(2, NSA_KV_HEADS),
        in_specs=[pl.BlockSpec((SEQ, HEAD_DIM), lambda i, h: (0, kv_blk + NSA_KV_HEADS * i + h)),
                  pl.BlockSpec((1, 8, 2 * HEAD_DIM), lambda i, h: (i, 0, 0)),
                  pl.BlockSpec((1, CMP_STRIDE, HEAD_DIM, 2 * HEAD_DIM), lambda i, h: (i, 0, 0, 0)),
                  pl.BlockSpec((1, HEAD_DIM, HEAD_DIM), lambda i, h: (i, 0, 0))],
        out_specs=pl.BlockSpec((1, 1, KC_ROWS, HEAD_DIM), lambda i, h: (i, h, 0, 0)),
        out_shape=jax.ShapeDtypeStruct((2, NSA_KV_HEADS, KC_ROWS, HEAD_DIM), F32),
        compiler_params=_cparams(("parallel", "parallel")),
        name="compress_prompt",
    )(z, pos, w1, w2)


def _nsa_prompt_kernel(q_ref, misc_ref, ksel_ref, vsel_rows_ref, kwin_ref, vwin_rows_ref, kc_ref, vc_ref,
                       bias_ref, cover_ref, erel_ref, o_ref, vsel_ref, vwin_ref):
    kvh = pl.program_id(0)
    t = pl.program_id(1)
    cols = NSA_GROUP * Q_TILE

    @pl.when(t == 0)
    def _():
        def body(kt, _):
            rows = pl.ds(pl.multiple_of(kt * Q_TILE, Q_TILE), Q_TILE)
            vsel_ref[:, rows] = vsel_rows_ref[rows, :].astype(F32).T.astype(BF16)
            vwin_ref[:, rows] = vwin_rows_ref[rows, :].astype(F32).T.astype(BF16)
            return 0
        lax.fori_loop(0, N_QT, body, 0)

    q = q_ref[...] * (HEAD_DIM ** -0.5 * LOG2E)
    qt =jnp.concatenate([q[:, g * HEAD_DIM:(g + 1) * HEAD_DIM].T for g in range(NSA_GROUP)], axis=1).astype(BF16)

    def bias_tile(k):
        return bias_ref[0, k * Q_TILE:(k + 1) * Q_TILE, :]
    b_diag, b_prev, b_first = bias_tile(0), bias_tile(1), bias_tile(2)

    def key_tile(ref, kt):
        return ref[pl.ds(pl.multiple_of(kt * Q_TILE, Q_TILE), Q_TILE), :]

    def value_tile(ref, kt):
        return ref[:, pl.ds(pl.multiple_of(kt * Q_TILE, Q_TILE), Q_TILE)]

    n_wt = WINDOW // Q_TILE + 1
    scores, values = [], []
    for w, b_tile in enumerate((b_first,) + (None,) * (n_wt - 3) + (b_prev, b_diag)):
        kt = t - (n_wt - 1) + w
        kc = jnp.maximum(kt, 0)
        s = _dot(key_tile(kwin_ref, kc), qt) + jnp.where(kt >= 0, 0.0, NEG)
        scores.append(s if b_tile is None else s + b_tile)
        values.append(value_tile(vwin_ref, kc))
    win_carry = [_flash_init(cols)]

    def win_part(lo):
        def run():
            win_carry[0] = _flash_step(win_carry[0], jnp.concatenate(scores[lo:lo + WIN_PART_TILES], axis=0),
                                       jnp.concatenate(values[lo:lo + WIN_PART_TILES], axis=1))
        return run
    win_parts = [win_part(lo) for lo in range(0, n_wt, WIN_PART_TILES)]

    start = pl.multiple_of(t * (Q_TILE // CMP_STRIDE), 8)
    kslab = kc_ref[0, 0, pl.ds(start, SLAB), :].astype(BF16)
    vslab = vc_ref[0, 0, pl.ds(start, SLAB), :].astype(BF16)
    b_cmp = bias_ref[0, 3 * Q_TILE:3 * Q_TILE + SLAB, :]
    u = lax.broadcasted_iota(jnp.int32, (SLAB, 1), 0)
    s = _dot(kslab, qt) + b_cmp + jnp.where(u >= SLAB_OFF - (Q_TILE // CMP_STRIDE) * t, 0.0, NEG)
    e = jnp.exp2(s - jnp.maximum(jnp.max(s, axis=0, keepdims=True), M_INIT))
    p_cmp = e / jnp.maximum(jnp.sum(e, axis=0, keepdims=True), 1e-30)
    o_cmp = _dot_tn(vslab, p_cmp.astype(BF16))

    p_sum = p_cmp[:, 0:Q_TILE]
    for g in range(1, NSA_GROUP):
        p_sum = p_sum + p_cmp[:, g * Q_TILE:(g + 1) * Q_TILE]
    hi = p_sum.astype(BF16)
    lo = (p_sum - hi.astype(F32)).astype(BF16)
    imp = _dot(cover_ref[...], hi) + _dot(cover_ref[...], lo)
    jr = lax.broadcasted_iota(jnp.int32, (N_SEL, Q_TILE), 0)
    qi = lax.broadcasted_iota(jnp.int32, (N_SEL, Q_TILE), 1)
    cur = REL0 + (qi >= SEL_BLOCK).astype(jnp.int32)
    first_blk = REL0 - 2 * t
    forced = (jr == first_blk) | (jr == cur) | (jr == cur - 1)
    in_range = (jr <= cur) & (jr >= first_blk)
    score = jnp.where(in_range, jnp.where(forced, FORCED_SCORE, imp), NEG)
    sel = _top_n_mask(score, jr.astype(F32), 0, win_parts)
    o_win = _flash_finish(win_carry[0])

    unsel = ((sel - 1.0) * (-NEG)).astype(BF16)
    q_aug = jnp.concatenate([qt, jnp.concatenate([unsel] * NSA_GROUP, axis=1)], axis=0)

    def sweep_step(i, carry, masked, biases=()):
        scores, values = [], []
        for j in range(SWEEP_TILES):
            kt = t - SWEEP_TILES * i - (SWEEP_TILES - 1) + j
            kc = jnp.maximum(kt, 0) if masked else kt
            tile = jnp.concatenate([key_tile(ksel_ref, kc), erel_ref[t - kc]], axis=1)
            s = _dot(tile, q_aug)
            if masked:
                s = s + jnp.where(kt >= 0, 0.0, NEG)
            if j >= SWEEP_TILES - len(biases):
                s = s + biases[j - (SWEEP_TILES - len(biases))]
            scores.append(s)
            values.append(value_tile(vsel_ref, kc))
        per = SWEEP_TILES // SWEEP_PARTS
        for k in range(SWEEP_PARTS):
            carry = _flash_step(carry, jnp.concatenate(scores[k * per:(k + 1) * per], axis=0),
                                jnp.concatenate(values[k * per:(k + 1) * per], axis=1))
        return carry

    n_full = jnp.maximum(t - (SWEEP_TILES - 1), 0) // SWEEP_TILES
    carry = lax.fori_loop(1, n_full + 1, lambda i, c: sweep_step(i, c, False), _flash_init(cols))
    some_left = (t - SWEEP_TILES * (n_full + 1) >= 0).astype(jnp.int32)
    carry = lax.fori_loop(0, some_left, lambda _, c: sweep_step(n_full + 1, c, True), carry)
    o_sel = _flash_finish(sweep_step(0, carry, True, (b_prev, b_diag)))

    gates = jax.nn.sigmoid(misc_ref[...]).T
    n_g = 3 * NSA_GROUP
    gk = jnp.where(kvh == 0, gates, pltpu.roll(gates, gates.shape[0] - n_g, axis=0))[MISC_GATES:MISC_GATES + n_g]
    for g in range(NSA_GROUP):
        c = slice(g * Q_TILE, (g + 1) * Q_TILE)
        o_g = (gk[3 * g:3 * g + 1] * o_cmp[:, c] + gk[3 * g + 1:3 * g + 2] * o_sel[:, c]
               + gk[3 * g + 2:3 * g + 3] * o_win[:, c])
        o_ref[:, g * HEAD_DIM:(g + 1) * HEAD_DIM] = o_g.T


def _nsa_prompt(z, kvb, kcp, bias, cover, erel):
    gw = NSA_GROUP * HEAD_DIM

    def k_spec(j):
        return pl.BlockSpec((SEQ, HEAD_DIM), lambda h, t, j=j: (0, 2 * j + h))
    return pl.pallas_call(
        _nsa_prompt_kernel,
        grid=(NSA_KV_HEADS, N_QT),
        in_specs=[pl.BlockSpec((Q_TILE, gw), lambda h, t: (t, Z_QN // gw + h)),
                  pl.BlockSpec((Q_TILE, 128), lambda h, t: (t, Z_MISC // 128)),
                  k_spec(2), k_spec(3), k_spec(4), k_spec(5),
                  pl.BlockSpec((1, 1, KC_ROWS, HEAD_DIM), lambda h, t: (0, h, 0, 0)),
                  pl.BlockSpec((1, 1, KC_ROWS, HEAD_DIM), lambda h, t: (1, h, 0, 0)),
                  pl.BlockSpec((1,) + bias.shape[1:], lambda h, t: (h, 0, 0)),
                  pl.BlockSpec(cover.shape, lambda h, t: (0, 0)),
                  pl.BlockSpec(erel.shape, lambda h, t: (0, 0, 0))],
        out_specs=pl.BlockSpec((Q_TILE, gw), lambda h, t: (t, h)),
        out_shape=jax.ShapeDtypeStruct((SEQ, NSA_HEADS * HEAD_DIM), F32),
        scratch_shapes=[pltpu.VMEM((HEAD_DIM, SEQ), BF16), pltpu.VMEM((HEAD_DIM, SEQ), BF16)],
        compiler_params=_cparams(("arbitrary", "arbitrary")),
        name="nsa_prompt",
    )(z, z, kvb, kvb, kvb, kvb, kcp, kcp, bias, cover, erel)


S_ROWS = NSA_HEADS * DEC_SEQ
PAGE_ROWS = NSA_KV_HEADS * PAGE_SIZE
NEW_ROWS = NSA_KV_HEADS * DEC_SEQ
CHUNK_ROWS = NSA_KV_HEADS * CMP_STRIDE
CHUNK_PITCH = CHUNK_ROWS + 8


def _nsa_sample_kernel(pt_ref, q_ref, gate_ref, new_ref, kwin_ref, vwin_ref, kcmp_hbm, vcmp_hbm, ksel_hbm, vsel_hbm,
                       posk_ref, w1k_ref, w2k_ref, posv_ref, w1v_ref, w2v_ref, bias_ref, cover_ref, expand_ref,
                       o_ref, kwin_o_ref, vwin_o_ref, kcmp_buf, vcmp_buf, ksel_buf, vsel_buf, sem):
    n_pg = N_PAGES
    b = pl.program_id(0)
    slot = b % 2
    page_chunks = PAGE_ROWS // CHUNK_ROWS
    hbm = (kcmp_hbm, vcmp_hbm, ksel_hbm, vsel_hbm)
    bufs = (kcmp_buf, vcmp_buf, ksel_buf, vsel_buf)

    def page_copy(k, p, page, into):
        if k < 2:
            src = hbm[k].at[pl.ds(page * page_chunks, page_chunks)]
            dst = bufs[k].at[pl.ds(into * S_CMP + p * page_chunks, page_chunks), pl.ds(0, CHUNK_ROWS), :]
        else:
            src = hbm[k].at[pl.ds(page * PAGE_ROWS, PAGE_ROWS)]
            dst = bufs[k].at[pl.ds(into * (n_pg * PAGE_ROWS) + p * PAGE_ROWS, PAGE_ROWS), :]
        return pltpu.make_async_copy(src, dst, sem.at[into, k])

    def gather(seq, into):
        for p in range(n_pg):
            page = pt_ref[seq, p]
            for k in range(4):
                page_copy(k, p, page, into).start()

    @pl.when(b == 0)
    def _():
        gather(0, 0)

    @pl.when(b + 1 < DEC_BATCH)
    def _():
        gather(b + 1, 1 - slot)

    for p in range(n_pg):
        for k in range(4):
            page_copy(k, p, 0, slot).wait()

    def pages(k, p):
        return bufs[k][pl.ds(pl.multiple_of(slot * (n_pg * PAGE_ROWS), PAGE_ROWS) + p * PAGE_ROWS, PAGE_ROWS), :]

    def compress_pool(k, pos_ref, w1_ref, w2_ref):
        flat = bufs[k].reshape(2 * S_CMP * CHUNK_PITCH, HEAD_DIM)

        def load_rows(s):
            return jnp.concatenate(
                [flat[pl.ds(slot * (S_CMP * CHUNK_PITCH) + NSA_KV_HEADS * s + h, S_CMP, stride=CHUNK_PITCH), :]
                 for h in range(NSA_KV_HEADS)], axis=0)
        return _compress(load_rows, S_CMP_COLS, pos_ref, w1_ref, w2_ref).astype(BF16)

    qs = (q_ref[0] * (HEAD_DIM ** -0.5)).astype(BF16)
    o0, o1 = S_CMP_COLS, S_CMP_COLS + S_SEL_COLS
    b_cmp = bias_ref[:, 0:o0]
    b_new = bias_ref[:, o1 - S_NEW_COLS:o1]

    def attend(scores, bias, values):
        s = jnp.concatenate(scores, axis=1) + bias
        e = jnp.exp(s - jnp.max(s, axis=-1, keepdims=True))
        acc = jnp.zeros((S_ROWS, HEAD_DIM), F32)
        c0 = 0
        for v in values:
            acc = acc + _dot(e[:, c0:c0 + v.shape[0]].astype(BF16), v)
            c0 += v.shape[0]
        return acc / jnp.sum(e, axis=-1, keepdims=True)

    def new_tile(j):
        pad = jnp.zeros((S_NEW_COLS - NEW_ROWS, HEAD_DIM), F32)
        return jnp.concatenate([new_ref[j], pad], axis=0).astype(BF16)

    p_cmp = _masked_softmax(_dot_nt(qs, compress_pool(0, posk_ref, w1k_ref, w2k_ref)) + b_cmp, b_cmp > M_INIT)
    comp_v = compress_pool(1, posv_ref, w1v_ref, w2v_ref)
    o_cmp = _dot(p_cmp.astype(BF16), comp_v)

    imp = _split_dot(p_cmp, cover_ref[...])
    imp = imp + pltpu.roll(imp, 8, axis=0) + pltpu.roll(imp, 16, axis=0) + pltpu.roll(imp, 24, axis=0)
    blk = lax.broadcasted_iota(jnp.int32, (S_ROWS, 128), 1)
    cur = PAST_LEN // SEL_BLOCK
    forced = (blk == 0) | (blk == cur) | (blk == cur - 1)
    score = jnp.where(blk <= cur, jnp.where(forced, FORCED_SCORE, imp), NEG)

    nk, nv = new_tile(4), new_tile(5)
    scores = [_dot_nt(qs, kwin_ref[...].astype(BF16)), _dot_nt(qs, nk)]
    bias_w = jnp.concatenate([bias_ref[:, o1:o1 + S_WIN_COLS], b_new], axis=1)
    o_win = attend(scores, bias_w, [vwin_ref[...].astype(BF16), nv])

    sel = _top_n_mask_by_rank(score, cur + 1).astype(BF16)
    mask_add = (_dot(sel, expand_ref[...]) - 1.0) * (-NEG)

    keep = NSA_KV_HEADS * WINDOW - NEW_ROWS
    kwin_o_ref[0:keep, :] = kwin_ref[NEW_ROWS:NSA_KV_HEADS * WINDOW, :]
    kwin_o_ref[keep:keep + NEW_ROWS, :] = new_ref[4]
    vwin_o_ref[0:keep, :] = vwin_ref[NEW_ROWS:NSA_KV_HEADS * WINDOW, :]
    vwin_o_ref[keep:keep + NEW_ROWS, :] = new_ref[5]

    nk, nv = new_tile(2), new_tile(3)
    scores = [_dot_nt(qs, pages(2, p).astype(BF16)) for p in range(n_pg)] + [_dot_nt(qs, nk)]
    values = [pages(3, p).astype(BF16) for p in range(n_pg)] + [nv]
    o_sel = attend(scores, bias_ref[:, o0:o1] + mask_add, values)

    g = jax.nn.sigmoid(gate_ref[0])
    o_ref[0] = g[:, 0:1] * o_cmp + g[:, 1:2] * o_sel + g[:, 2:3] * o_win


def _nsa_sample(page_table, q_s, gate_s, new_s, kwin, vwin, pools, cmp_w, bias, cover, expand):
    win_rows = NSA_KV_HEADS * WINDOW

    def full(a):
        return pl.BlockSpec(a.shape, lambda b, pt, n=a.ndim: (0,) * n)

    def per_b(a):
        return pl.BlockSpec((1,) + a.shape[1:], lambda b, pt, n=a.ndim: (b,) + (0,) * (n - 1))
    win_spec = pl.BlockSpec((win_rows, HEAD_DIM), lambda b, pt: (b, 0))
    consts = list(cmp_w) + [bias, cover, expand]
    grid_spec = pltpu.PrefetchScalarGridSpec(
        num_scalar_prefetch=1,
        grid=(DEC_BATCH,),
        in_specs=[per_b(q_s), per_b(gate_s),
                  pl.BlockSpec((N_KV_ARRAYS, NEW_ROWS, HEAD_DIM), lambda b, pt: (0, b, 0)), win_spec, win_spec]
        + [pl.BlockSpec(memory_space=pl.ANY)] * 4 + [full(a) for a in consts],
        out_specs=[per_b(q_s), win_spec, win_spec],
        scratch_shapes=[pltpu.VMEM((2 * S_CMP, CHUNK_PITCH, HEAD_DIM), F32)] * 2
        + [pltpu.VMEM((2 * N_PAGES * PAGE_ROWS, HEAD_DIM), F32)] * 2
        + [pltpu.SemaphoreType.DMA((2, 4))],
    )
    return pl.pallas_call(
        _nsa_sample_kernel,
        grid_spec=grid_spec,
        out_shape=[jax.ShapeDtypeStruct(q_s.shape, F32),
                   jax.ShapeDtypeStruct(kwin.shape, F32),
                   jax.ShapeDtypeStruct(vwin.shape, F32)],
        compiler_params=_cparams(("arbitrary",)),
        name="nsa_sample",
    )(page_table, q_s, gate_s, new_s, kwin, vwin, *pools, *consts)


def _log_decay(a_blk, wa_ref, ba_ref):
    x = _dot(a_blk.astype(BF16), wa_ref[...]) + ba_ref[...]
    return (jnp.minimum(x, 0.0) - jnp.log1p(jnp.exp(-jnp.abs(x)))) * (1.0 / GLA_TAU)


def _segment_cumsum(g, seg):
    pos = lax.broadcasted_iota(jnp.int32, g.shape, 0) % seg
    cum = g
    sh = 1
    while sh < seg:
        cum = cum + jnp.where(pos >= sh, pltpu.roll(cum, sh, axis=0), 0.0)
        sh *= 2
    return cum


def _gla_prompt_kernel(q_ref, k_ref, v_ref, r_ref, a_ref, wa_ref, ba_ref, gn_ref, o_ref, st_o_ref, st_ref,
                       *, n_blk, tb):
    tbi = pl.program_id(0)

    @pl.when(tbi == 0)
    def _():
        st_ref[...] = jnp.zeros_like(st_ref)

    c = GLA_CHUNK
    tril = lax.broadcasted_iota(jnp.int32, (c, c), 0) >= lax.broadcasted_iota(jnp.int32, (c, c), 1)
    cum = _segment_cumsum(_log_decay(a_ref[...], wa_ref, ba_ref), c)
    q = q_ref[...] * (GLA_DK ** -0.5)
    k = k_ref[...]
    v = v_ref[...].astype(BF16)
    qe = (q * jnp.exp(cum)).astype(BF16)
    kd = (k * jnp.exp(-cum)).astype(BF16)
    heads = [(slice(h * GLA_DK, (h + 1) * GLA_DK), slice(h * GLA_DV, (h + 1) * GLA_DV)) for h in range(GLA_HEADS)]
    sts = [st_ref[h] for h in range(GLA_HEADS)]
    outs = [[] for _ in range(GLA_HEADS)]
    for ci in range(tb // c):
        r = slice(ci * c, (ci + 1) * c)
        last = cum[ci * c + c - 1:ci * c + c, :]
        kl = (k[r] * jnp.exp(last - cum[r])).astype(BF16)
        decay = jnp.exp(last)
        for h, (dk, dv) in enumerate(heads):
            att = jnp.where(tril, _dot_nt(qe[r, dk], kd[r, dk]), 0.0)
            outs[h].append(_dot_nt(qe[r, dk], sts[h].astype(BF16)) + _dot(att.astype(BF16), v[r, dv]))
            sts[h] = decay[:, dk] * sts[h] + _dot_tn(v[r, dv], kl[:, dk])
    for h, (dk, dv) in enumerate(heads):
        st_ref[h] = sts[h]
        o = jnp.concatenate(outs[h], axis=0)
        o_ref[:, dv] = _rms(o, gn_ref[...]) * _silu(r_ref[:, dv])

    @pl.when(tbi == n_blk - 1)
    def _():
        st_o_ref[...] = st_ref[...]


def _gla_prompt(z, wa, ba, gn, tb=256):
    n_blk = SEQ // tb
    hk, hv = GLA_HEADS * GLA_DK, GLA_HEADS * GLA_DV
    st_shape = (GLA_HEADS, GLA_DV, GLA_DK)
    return pl.pallas_call(
        functools.partial(_gla_prompt_kernel, n_blk=n_blk, tb=tb),
        grid=(n_blk,),
        in_specs=[pl.BlockSpec((tb, hk), lambda i: (i, Z_QG // hk)),
                  pl.BlockSpec((tb, hk), lambda i: (i, Z_KG // hk)),
                  pl.BlockSpec((tb, hv), lambda i: (i, Z_VG // hv)),
                  pl.BlockSpec((tb, hv), lambda i: (i, Z_RG // hv)),
                  pl.BlockSpec((tb, 128), lambda i: (i, Z_MISC // 128)),
                  pl.BlockSpec((128, hk), lambda i: (0, 0)),
                  pl.BlockSpec((1, hk), lambda i: (0, 0)),
                  pl.BlockSpec((1, GLA_DV), lambda i: (0, 0))],
        out_specs=[pl.BlockSpec((tb, hv), lambda i: (i, 0)),
                   pl.BlockSpec(st_shape, lambda i: (0, 0, 0))],
        out_shape=[jax.ShapeDtypeStruct((SEQ, hv), F32), jax.ShapeDtypeStruct(st_shape, F32)],
        scratch_shapes=[pltpu.VMEM(st_shape, F32)],
        compiler_params=_cparams(("arbitrary",)),
        name="gla_prompt",
    )(z, z, z, z, z, wa, ba, gn)


GS_B = 4


def _gla_sample_kernel(q_ref, k_ref, v_ref, r_ref, a_ref, wa_ref, ba_ref, gn_ref, s_ref, o_ref, s_o_ref):
    rows = GS_B * DEC_SEQ
    a = a_ref[...]
    ri = lax.broadcasted_iota(jnp.int32, (rows, rows), 0)
    ci = lax.broadcasted_iota(jnp.int32, (rows, rows), 1)
    same_causal = (ri // DEC_SEQ == ci // DEC_SEQ) & (ri >= ci)
    row_b = lax.broadcasted_iota(jnp.int32, (rows, 1), 0) // DEC_SEQ
    ones = jnp.ones((rows, 128), BF16)
    for h in range(GLA_HEADS):
        dk = slice(h * GLA_DK, (h + 1) * GLA_DK)
        dv = slice(h * GLA_DV, (h + 1) * GLA_DV)
        g = _log_decay(a, wa_ref.at[:, dk], ba_ref.at[:, dk])
        cum = _segment_cumsum(g, DEC_SEQ)
        q = q_ref[:, dk] * (GLA_DK ** -0.5)
        k = k_ref[:, dk]
        v = v_ref[:, dv].astype(BF16)
        qe = (q * jnp.exp(cum)).astype(BF16)
        kd = (k * jnp.exp(-cum)).astype(BF16)
        att = jnp.where(same_causal, _dot_nt(qe, kd), 0.0)
        o = _dot(att.astype(BF16), v)
        for b in range(GS_B):
            mine = row_b == b
            last = cum[b * DEC_SEQ + DEC_SEQ - 1:(b + 1) * DEC_SEQ, :]
            s = s_ref[b, h]
            o = o + jnp.where(mine, _dot(qe, s.astype(BF16)), 0.0)
            kl = jnp.where(mine, k * jnp.exp(last - cum), 0.0)
            hi = jnp.where(mine, g, 0.0).astype(BF16)
            lo = (jnp.where(mine, g, 0.0) - hi.astype(F32)).astype(BF16)
            last_col = (_dot_tn(hi, ones) + _dot_tn(lo, ones))[:, 0:1]
            s_o_ref[b, h] = jnp.exp(last_col) * s + _dot_tn(kl.astype(BF16), v)
        o_ref[:, dv] = _rms(o, gn_ref[...]) * _silu(r_ref[:, dv])


def _gla_sample(zs, state, wa, ba, gn):
    rows = GS_B * DEC_SEQ
    n = DEC_BATCH * DEC_SEQ
    hk, hv = GLA_HEADS * GLA_DK, GLA_HEADS * GLA_DV
    st_spec = pl.BlockSpec((GS_B, GLA_HEADS, GLA_DK, GLA_DV), lambda i: (i, 0, 0, 0))
    return pl.pallas_call(
        _gla_sample_kernel,
        grid=(DEC_BATCH // GS_B,),
        in_specs=[pl.BlockSpec((rows, hk), lambda i: (i, Z_QG // hk)),
                  pl.BlockSpec((rows, hk), lambda i: (i, Z_KG // hk)),
                  pl.BlockSpec((rows, hv), lambda i: (i, Z_VG // hv)),
                  pl.BlockSpec((rows, hv), lambda i: (i, Z_RG // hv)),
                  pl.BlockSpec((rows, 128), lambda i: (i, Z_MISC // 128)),
                  pl.BlockSpec((128, hk), lambda i: (0, 0)),
                  pl.BlockSpec((1, hk), lambda i: (0, 0)),
                  pl.BlockSpec((1, GLA_DV), lambda i: (0, 0)),
                  st_spec],
        out_specs=[pl.BlockSpec((rows, hv), lambda i: (i, 0)), st_spec],
        out_shape=[jax.ShapeDtypeStruct((n, hv), F32), jax.ShapeDtypeStruct(state.shape, F32)],
        compiler_params=_cparams(("parallel",)),
        name="gla_sample",
    )(zs, zs, zs, zs, zs, wa, ba, gn, state)


def _softmax_rows(s):
    m = jnp.max(s, axis=-1, keepdims=True)
    e = jnp.exp(s - m)
    return e / jnp.sum(e, axis=-1, keepdims=True)


def _mem_prompt_kernel(q_ref, k_ref, v_ref, o_ref):
    for h in range(MEM_HEADS):
        d = slice(h * MEM_HEAD_DIM, (h + 1) * MEM_HEAD_DIM)
        q = (q_ref[:, d] * (MEM_HEAD_DIM ** -0.5)).astype(BF16)
        p = _softmax_rows(_dot_nt(q, k_ref[:, d].astype(BF16)))
        o_ref[:, d] = _dot(p.astype(BF16), v_ref[:, d].astype(BF16))


def _mem_prompt(qm, memkv, tq=256):
    w = MEM_HEADS * MEM_HEAD_DIM
    return pl.pallas_call(
        _mem_prompt_kernel,
        grid=(SEQ // tq,),
        in_specs=[pl.BlockSpec((tq, w), lambda i: (i, 0)),
                  pl.BlockSpec((MEM_TOKENS, w), lambda i: (0, 0)),
                  pl.BlockSpec((MEM_TOKENS, w), lambda i: (0, 1))],
        out_specs=pl.BlockSpec((tq, w), lambda i: (i, 0)),
        out_shape=jax.ShapeDtypeStruct((SEQ, w), F32),
        compiler_params=_cparams(("parallel",)),
        name="mem_prompt",
    )(qm, memkv, memkv)


def _mem_sample_kernel(q_ref, k_ref, v_ref, o_ref):
    rows = MEM_HEADS * DEC_SEQ
    cols = MEM_HEADS * MEM_TOKENS
    row_h = lax.broadcasted_iota(jnp.int32, (rows, cols), 0) // DEC_SEQ
    col_h = lax.broadcasted_iota(jnp.int32, (rows, cols), 1) % MEM_HEADS
    for b in range(MS_B):
        kv_rows = slice(b * cols, (b + 1) * cols)
        q = (q_ref[b] * (MEM_HEAD_DIM ** -0.5)).astype(BF16)
        s = jnp.where(row_h == col_h, _dot_nt(q, k_ref[kv_rows, :].astype(BF16)), NEG)
        o_ref[b] = _dot(_softmax_rows(s).astype(BF16), v_ref[kv_rows, :].astype(BF16))


MS_B = 4


def _mem_sample(q_s, k_mem, v_mem):
    rows = MEM_HEADS * DEC_SEQ
    kv_spec = pl.BlockSpec((MS_B * MEM_HEADS * MEM_TOKENS, MEM_HEAD_DIM), lambda b: (b, 0))
    return pl.pallas_call(
        _mem_sample_kernel,
        grid=(DEC_BATCH // MS_B,),
        in_specs=[pl.BlockSpec((MS_B, rows, MEM_HEAD_DIM), lambda b: (b, 0, 0)), kv_spec, kv_spec],
        out_specs=pl.BlockSpec((MS_B, rows, MEM_HEAD_DIM), lambda b: (b, 0, 0)),
        out_shape=jax.ShapeDtypeStruct((DEC_BATCH, rows, MEM_HEAD_DIM), F32),
        compiler_params=_cparams(("parallel",)),
        name="mem_sample",
    )(q_s, k_mem, v_mem)


W_IN_SEGMENTS = ((0, 1024, Z_QN), (1024, 1536, Z_KV), (2560, 24, Z_MISC + MISC_GATES), (2584, 512, Z_QG),
                 (3096, 512, Z_KG), (3608, 1024, Z_VG), (4632, 1024, Z_RG), (5656, 16, Z_MISC + MISC_A))
W_IN_COLS = 5672


def _permute_w_in_kernel(wt_ref, o_ref):
    tc = wt_ref.shape[1]

    def piece(row):
        return wt_ref[row:row + 128, :].T

    for src, width, dst in W_IN_SEGMENTS:
        for c in range(0, width - width % 128, 128):
            o_ref[:, dst + c:dst + c + 128] = piece(src + c).astype(BF16)
    (g_src, g_w, _), (a_src, a_w, _) = [seg for seg in W_IN_SEGMENTS if seg[1] % 128]
    lane = lax.broadcasted_iota(jnp.int32, (tc, 128), 1)
    gates = piece(g_src)
    low_rank = pltpu.roll(piece(a_src + a_w - 128), MISC_A + a_w, axis=1)
    misc = jnp.where(lane < MISC_GATES + g_w, gates, jnp.where(lane < MISC_A + a_w, low_rank, 0.0))
    o_ref[:, Z_MISC:Z_W] = misc.astype(BF16)


def _permute_w_in(w_in_t, tc=256):
    d = w_in_t.shape[1]
    return pl.pallas_call(
        _permute_w_in_kernel,
        grid=(d // tc,),
        in_specs=[pl.BlockSpec((W_IN_COLS, tc), lambda i: (0, i))],
        out_specs=pl.BlockSpec((tc, Z_W), lambda i: (i, 0)),
        out_shape=jax.ShapeDtypeStruct((d, Z_W), BF16),
        compiler_params=_cparams(("parallel",)),
        name="permute_w_in",
    )(w_in_t)


def kernel(x_prompt, x_sample, mem_prompt, cache_k_cmp, cache_v_cmp, cache_k_sel, cache_v_sel, cache_k_win,
           cache_v_win, state_gla, cache_k_mem, cache_v_mem, page_table, norm_ffn1, ffn1_w_gate, ffn1_w_up,
           ffn1_w_down, norm_mix, w_in, w_out, cmp_pos_k, cmp_w1_k, cmp_w2_k, cmp_pos_v, cmp_w1_v, cmp_w2_v,
           rel_bias, gla_w_a2, gla_b_a, gla_norm, norm_mem, norm_mem_src, w_mem_q, w_mem_k, w_mem_v, w_mem_o,
           norm_ffn2, ffn2_w_gate, ffn2_w_up, ffn2_w_down, norm_final):
    bf = lambda a: a.astype(BF16)
    row = lambda a: a.reshape(1, -1)
    nb, ns = DEC_BATCH, DEC_SEQ
    kvw = NSA_KV_HEADS * HEAD_DIM

    h1, h1n = _ffn([x_prompt[0], x_sample.reshape(nb * ns, D_MODEL)], row(norm_ffn1[0]), bf(ffn1_w_gate[0]),
                   bf(ffn1_w_up[0]), bf(ffn1_w_down[0]), row(norm_mix[0]), False)
    z, kvb, new_s, *new_p = _proj_in(h1n, _permute_w_in(w_in[0].T))
    rows_p = [a.reshape(1, 1, SEQ, NSA_KV_HEADS, HEAD_DIM) for a in new_p]
    rows_s = [new_s[j].reshape(1, nb, ns, NSA_KV_HEADS, HEAD_DIM) for j in range(4)]

    tab_p = _bias_tables(rel_bias, _prompt_bucket_table(), "bias_prompt")
    far = rel_bias[NUM_BUCKETS - 1][:, None, None]
    near = tab_p[:, :3 * Q_TILE]
    tab_p = jnp.concatenate([jnp.where(near > M_INIT, near - far, NEG), tab_p[:, 3 * Q_TILE:]], axis=1)
    tab_p = jnp.where(tab_p > M_INIT, tab_p * LOG2E, NEG)
    tab_p = tab_p.reshape(NSA_KV_HEADS, NSA_GROUP, -1, Q_TILE).transpose(0, 2, 1, 3)
    tab_p = tab_p.reshape(NSA_KV_HEADS, -1, NSA_GROUP * Q_TILE)
    tab_s = _bias_tables(rel_bias, _sample_bucket_table(), "bias_sample")
    tab_s = tab_s.reshape(NSA_KV_HEADS, NSA_GROUP, NSA_KV_HEADS, ns, -1)
    tab_s = jnp.stack([tab_s[h, :, h] for h in range(NSA_KV_HEADS)], axis=1).reshape(S_ROWS, -1)

    cmp_w1 = bf(jnp.stack([cmp_w1_k[0], cmp_w1_v[0]]))
    cmp_w1 = jnp.concatenate([cmp_w1[:, :CMP_STRIDE], cmp_w1[:, CMP_STRIDE:]], axis=-1)
    cmp_w2 = bf(jnp.stack([cmp_w2_k[0], cmp_w2_v[0]]))
    cmp_pos = _pos_term(jnp.stack([cmp_pos_k[0], cmp_pos_v[0]]), cmp_w1)
    kcp = _compress_prompt(z, cmp_pos, cmp_w1, cmp_w2)
    cover_p = jnp.asarray(_cover_np(SLAB, N_SEL, lambda u, j: u - 4 * j - SLAB_OFF + 4 * REL0).T, BF16)
    erel = (np.arange(N_SEL)[None, None, :] == REL0 - 2 * np.arange(N_QT)[:, None, None]
            + (np.arange(Q_TILE)[None, :, None] >= SEL_BLOCK))
    o_nsa_p = _nsa_prompt(z, kvb, kcp, tab_p, cover_p, jnp.asarray(erel, BF16))

    zs = z[SEQ:]
    q_s = zs[:, Z_QN:Z_QN + NSA_HEADS * HEAD_DIM].reshape(nb, ns, NSA_KV_HEADS, NSA_GROUP, HEAD_DIM)
    q_s = q_s.transpose(0, 3, 2, 1, 4).reshape(nb, S_ROWS, HEAD_DIM)
    gate_s = zs[:, Z_MISC + MISC_GATES:Z_MISC + MISC_GATES + 3 * NSA_HEADS]
    gate_s = gate_s.reshape(nb, ns, NSA_KV_HEADS, NSA_GROUP, 3).transpose(0, 3, 2, 1, 4).reshape(nb, S_ROWS, 3)
    as_rows = lambda c: c.reshape(-1, HEAD_DIM)
    pools = ([c.reshape(-1, CHUNK_ROWS, HEAD_DIM) for c in (cache_k_cmp, cache_v_cmp)]
             + [as_rows(c) for c in (cache_k_sel, cache_v_sel)])
    cover_s = _cover_np(S_CMP, 128, lambda c, j: c - 4 * j)
    cover_s = jnp.asarray(np.concatenate([cover_s] * NSA_KV_HEADS, axis=0), BF16)
    expand_s = jnp.asarray(np.arange(128)[:, None]
                           == (np.arange(S_SEL_COLS)[None, :] // (NSA_KV_HEADS * SEL_BLOCK)), BF16)
    o_nsa_s, kwin_s, vwin_s = _nsa_sample(
        page_table, q_s, gate_s, new_s, as_rows(cache_k_win), as_rows(cache_v_win), pools,
        (cmp_pos[0], cmp_w1[0], cmp_w2[0], cmp_pos[1], cmp_w1[1], cmp_w2[1]),
        tab_s, cover_s, expand_s)
    o_nsa_s = o_nsa_s.reshape(nb, NSA_GROUP, NSA_KV_HEADS, ns, HEAD_DIM).transpose(0, 3, 2, 1, 4)
    o_nsa_s = o_nsa_s.reshape(nb * ns, NSA_HEADS * HEAD_DIM)

    wa = bf(jnp.zeros((128, GLA_HEADS * GLA_DK), F32).at[MISC_A:MISC_A + GLA_RANK].set(gla_w_a2[0]))
    ba, gn = row(gla_b_a[0]), row(gla_norm[0])
    o_gla_p, st_p = _gla_prompt(z, wa, ba, gn)
    o_gla_s, st_s = _gla_sample(zs, state_gla[0], wa, ba, gn)

    half = NSA_HEADS * HEAD_DIM
    h2 = _matmul_res(h1, [(o_nsa_p, o_nsa_s), (o_gla_p, o_gla_s)], [bf(w_out[0][:half]), bf(w_out[0][half:])],
                     512, "proj_out")

    memkv = _norm_matmul(mem_prompt[0], row(norm_mem_src[0]),
                         bf(jnp.concatenate([w_mem_k[0], w_mem_v[0]], axis=1)), MEM_TOKENS, 512, "mem_kv")
    mw = MEM_HEADS * MEM_HEAD_DIM
    qm = _norm_matmul(h2, row(norm_mem[0]), bf(w_mem_q[0]), 512, mw, "mem_q")
    om_p = _mem_prompt(qm, memkv)
    qm_s = qm[SEQ:].reshape(nb, ns, MEM_HEADS, MEM_HEAD_DIM).transpose(0, 2, 1, 3)
    om_s = _mem_sample(qm_s.reshape(nb, MEM_HEADS * ns, MEM_HEAD_DIM),
                       as_rows(cache_k_mem), as_rows(cache_v_mem))
    om_s = om_s.reshape(nb, MEM_HEADS, ns, MEM_HEAD_DIM).transpose(0, 2, 1, 3).reshape(nb * ns, mw)
    h3 = _matmul_res(h2, [(om_p, om_s)], [bf(w_mem_o[0])], 512, "mem_out")

    y_p, y_s = _ffn([h3], row(norm_ffn2[0]), bf(ffn2_w_gate[0]), bf(ffn2_w_up[0]), bf(ffn2_w_down[0]),
                    row(norm_final), True)

    mem_shape = (1, 1, MEM_TOKENS, MEM_HEADS, MEM_HEAD_DIM)
    win_shape = (1, nb, WINDOW, NSA_KV_HEADS, HEAD_DIM)
    return (y_p.reshape(1, SEQ, D_MODEL), y_s.reshape(nb, ns, D_MODEL),
            rows_p[0], rows_p[1], rows_p[2], rows_p[3],
            rows_p[4][:, :, SEQ - WINDOW:], rows_p[5][:, :, SEQ - WINDOW:],
            st_p.transpose(0, 2, 1).reshape(1, 1, GLA_HEADS, GLA_DK, GLA_DV),
            memkv[:, :mw].reshape(mem_shape), memkv[:, mw:].reshape(mem_shape),
            rows_s[0], rows_s[1], rows_s[2], rows_s[3],
            kwin_s.reshape(win_shape), vwin_s.reshape(win_shape),
            st_s.reshape(1, nb, GLA_HEADS, GLA_DK, GLA_DV))
```

```python
import functools
import math

import numpy as np
import jax
import jax.numpy as jnp
from jax import lax
from jax.experimental import pallas as pl
from jax.experimental.pallas import tpu as pltpu

F32 = jnp.float32
BF16 = jnp.bfloat16

D_MODEL = 2048
SEQ = 8192
DEC_BATCH = 128
DEC_SEQ = 4
PAST_LEN = 2048
PAGE_SIZE = 128
N_PAGES = PAST_LEN // PAGE_SIZE
HEAD_DIM = 128
NSA_HEADS = 8
NSA_KV_HEADS = 2
NSA_GROUP = 4
CMP_BLOCK = 32
CMP_STRIDE = 16
SEL_BLOCK = 64
SEL_TOPN = 16
WINDOW = 512
FORCED_SCORE = 1.0e4
GLA_HEADS = 4
GLA_DV = 256
GLA_DK = 128
GLA_RANK = 16
GLA_TAU = 16.0
GLA_CHUNK = 32
MEM_TOKENS = 256
MEM_HEADS = 4
MEM_HEAD_DIM = 128
D_FF = 5632
NUM_BUCKETS = 32
MAX_DISTANCE = 128
RMS_EPS = 1e-6

N_TOK = SEQ + DEC_BATCH * DEC_SEQ
Z_KV, Z_QG, Z_VG, Z_RG, Z_QN, Z_KG, Z_MISC = 0, 1536, 2048, 3072, 4096, 5120, 5632
Z_W = 5760
MISC_GATES, MISC_A = 0, 24

NEG = -1e30
LOG2E = math.log2(math.e)
ONES_ROWS = 16
M_INIT = -1e29

Q_TILE = 128
N_QT = SEQ // Q_TILE
N_CMP_PAD = SEQ // CMP_STRIDE
SLAB = N_CMP_PAD + 128
SLAB_OFF = SLAB - 16
KC_ROWS = SLAB_OFF + N_CMP_PAD + 16
N_SEL = SEQ // SEL_BLOCK
REL0 = N_SEL - 2
CMP_GROUP = 4
SWEEP_TILES = 8
SWEEP_PARTS = 4
WIN_PART_TILES = 2
FLASH_SPLIT = 1

VMEM_LIMIT = 56 * 1024 * 1024


def _cparams(sem):
    return pltpu.CompilerParams(dimension_semantics=sem, vmem_limit_bytes=VMEM_LIMIT)


def _dot(a, b):
    return jnp.dot(a, b, preferred_element_type=F32)


def _dot_nt(a, b):
    return lax.dot_general(a, b, (((1,), (1,)), ((), ())), preferred_element_type=F32)


def _dot_tn(a, b):
    return lax.dot_general(a, b, (((0,), (0,)), ((), ())), preferred_element_type=F32)


def _rms(x, g):
    return x * lax.rsqrt(jnp.mean(x * x, axis=-1, keepdims=True) + RMS_EPS) * g


def _silu(x):
    return x * jax.nn.sigmoid(x)


def _ffn_kernel(*refs, n_ff, n_first, two_in, two_out):
    refs = list(refs)
    x_refs = [refs.pop(0) for _ in range(2 if two_in else 1)]
    g_ref, wg_ref, wu_ref, wd_ref, gf_ref = refs[:5]
    o_refs = refs[5:7]
    xn_ref, acc_ref = refs[-2:]
    i = pl.program_id(0)
    j = pl.program_id(1)

    def x_tile():
        return jnp.where(i < n_first, x_refs[0][...], x_refs[1][...]) if two_in else x_refs[0][...]

    @pl.when(j == 0)
    def _():
        xn_ref[...] = _rms(x_tile(), g_ref[...]).astype(BF16)
        acc_ref[...] = jnp.zeros_like(acc_ref)

    xn = xn_ref[...]
    hid = _silu(_dot(xn, wg_ref[...])) * _dot(xn, wu_ref[...])
    acc_ref[...] += _dot(hid.astype(BF16), wd_ref[...])

    @pl.when(j == n_ff - 1)
    def _():
        h = x_tile() + 0.5 * acc_ref[...]
        if two_out:
            h = _rms(h, gf_ref[...])

            @pl.when(i < n_first)
            def _():
                o_refs[0][...] = h

            @pl.when(i >= n_first)
            def _():
                o_refs[1][...] = h
        else:
            o_refs[0][...] = h
            o_refs[1][...] = _rms(h, gf_ref[...]).astype(BF16)


FFN_TM = 512
FFN_TF = 512


def _ffn(xs, g, wg, wu, wd, gf, split_out):
    tm, tf, d = FFN_TM, FFN_TF, D_MODEL
    n_ff = D_FF // tf
    n_first = SEQ // tm
    two_in = len(xs) == 2

    def first(i, j):
        return (jnp.minimum(i, n_first - 1), 0)

    def second(i, j):
        return (jnp.maximum(i - n_first, 0), 0)
    whole = pl.BlockSpec((tm, d), lambda i, j: (i, 0))
    pair = [pl.BlockSpec((tm, d), first), pl.BlockSpec((tm, d), second)]
    n_s = DEC_BATCH * DEC_SEQ
    return pl.pallas_call(
        functools.partial(_ffn_kernel, n_ff=n_ff, n_first=n_first, two_in=two_in, two_out=split_out),
        grid=(N_TOK // tm, n_ff),
        in_specs=(pair if two_in else [whole])
        + [pl.BlockSpec((1, d), lambda i, j: (0, 0)),
           pl.BlockSpec((d, tf), lambda i, j: (0, j)),
           pl.BlockSpec((d, tf), lambda i, j: (0, j)),
           pl.BlockSpec((tf, d), lambda i, j: (j, 0)),
           pl.BlockSpec((1, d), lambda i, j: (0, 0))],
        out_specs=pair if split_out else [whole, whole],
        out_shape=([jax.ShapeDtypeStruct((SEQ, d), F32), jax.ShapeDtypeStruct((n_s, d), F32)] if split_out
                   else [jax.ShapeDtypeStruct((N_TOK, d), F32), jax.ShapeDtypeStruct((N_TOK, d), BF16)]),
        scratch_shapes=[pltpu.VMEM((tm, d), BF16), pltpu.VMEM((tm, d), F32)],
        compiler_params=_cparams(("arbitrary", "arbitrary")),
        name="ffn",
    )(*xs, g, wg, wu, wd, gf)


def _norm_matmul_kernel(x_ref, g_ref, w_ref, o_ref, xn_ref):
    @pl.when(pl.program_id(1) == 0)
    def _():
        xn_ref[...] = _rms(x_ref[...], g_ref[...]).astype(BF16)

    o_ref[...] = _dot(xn_ref[...], w_ref[...])


def _norm_matmul(x, g, w, tm, tn, name):
    n, d = x.shape
    dout = w.shape[1]
    return pl.pallas_call(
        _norm_matmul_kernel,
        grid=(n // tm, dout // tn),
        in_specs=[pl.BlockSpec((tm, d), lambda i, j: (i, 0)),
                  pl.BlockSpec((1, d), lambda i, j: (0, 0)),
                  pl.BlockSpec((d, tn), lambda i, j: (0, j))],
        out_specs=pl.BlockSpec((tm, tn), lambda i, j: (i, j)),
        out_shape=jax.ShapeDtypeStruct((n, dout), F32),
        scratch_shapes=[pltpu.VMEM((tm, d), BF16)],
        compiler_params=_cparams(("parallel", "arbitrary")),
        name=name,
    )(x, g, w)


N_KV_ARRAYS = 6
KV_W = NSA_KV_HEADS * HEAD_DIM


def _proj_in_kernel(x_ref, w_ref, o_ref, kvb_ref, rows_s_ref, *rows_p_refs, n_first):
    j, i = pl.program_id(0), pl.program_id(1)
    res = _dot(x_ref[...], w_ref[...])
    o_ref[...] = res
    tm = res.shape[0]

    @pl.when(j == 0)
    def _():
        kvb_ref[...] = res[:, Z_KV:Z_KV + N_KV_ARRAYS * KV_W].astype(BF16)

        def head_rows(a, h):
            c0 = Z_KV + a * KV_W + h * HEAD_DIM
            return res[:, c0:c0 + HEAD_DIM]

        @pl.when(i < n_first)
        def _():
            for a in range(N_KV_ARRAYS):
                for h in range(NSA_KV_HEADS):
                    rows_p_refs[a][pl.ds(h, tm, stride=NSA_KV_HEADS), :] = head_rows(a, h)

        @pl.when(i >= n_first)
        def _():
            for a in range(N_KV_ARRAYS):
                for h in range(NSA_KV_HEADS):
                    rows_s_ref.at[a][pl.ds(h, tm, stride=NSA_KV_HEADS), :] = head_rows(a, h)


def _proj_in(x, w, tm=512, tn=1920):
    n, d = x.shape
    n_i, n_first = n // tm, SEQ // tm
    n_s = n - SEQ
    assert Z_KV == 0 and N_KV_ARRAYS * KV_W <= tn and n_s == tm

    def once(i_of):
        return lambda j, i: jnp.where(j == 0, i_of(i), i_of(n_i - 1))
    p_idx = once(lambda i: jnp.minimum(i, n_first - 1))
    return pl.pallas_call(
        functools.partial(_proj_in_kernel, n_first=n_first),
        grid=(Z_W // tn, n_i),
        in_specs=[pl.BlockSpec((tm, d), lambda j, i: (i, 0)),
                  pl.BlockSpec((d, tn), lambda j, i: (0, j))],
        out_specs=[pl.BlockSpec((tm, tn), lambda j, i: (i, j)),
                   pl.BlockSpec((tm, N_KV_ARRAYS * KV_W), lambda j, i: (once(lambda i: i)(j, i), 0)),
                   pl.BlockSpec((N_KV_ARRAYS, NSA_KV_HEADS * n_s, HEAD_DIM), lambda j, i: (0, 0, 0))]
        + [pl.BlockSpec((NSA_KV_HEADS * tm, HEAD_DIM), lambda j, i: (p_idx(j, i), 0))] * N_KV_ARRAYS,
        out_shape=[jax.ShapeDtypeStruct((n, Z_W), F32),
                   jax.ShapeDtypeStruct((n, N_KV_ARRAYS * KV_W), BF16),
                   jax.ShapeDtypeStruct((N_KV_ARRAYS, NSA_KV_HEADS * n_s, HEAD_DIM), F32)]
        + [jax.ShapeDtypeStruct((NSA_KV_HEADS * SEQ, HEAD_DIM), F32)] * N_KV_ARRAYS,
        compiler_params=_cparams(("arbitrary", "arbitrary")),
        name="proj_in",
    )(x, w)


def _matmul_res_kernel(*refs, n_lhs, n_first, follow):
    res_ref = refs[0]
    n_in = 1 + 3 * n_lhs + (2 if follow else 0)
    o_ref = refs[n_in]
    first = pl.program_id(0) < n_first
    acc = res_ref[...]
    for k in range(n_lhs):
        lhs = jnp.where(first, refs[1 + 2 * k][...], refs[2 + 2 * k][...])
        acc = acc + _dot(lhs.astype(BF16), refs[1 + 2 * n_lhs + k][...])
    o_ref[...] = acc
    if follow:
        g_ref, w_ref = refs[n_in - 2:n_in]
        refs[n_in + 1][...] = _dot(_rms(acc, g_ref[...]).astype(BF16), w_ref[...])


def _matmul_res(res, lhs, ws, tm, name, follow=None):
    n, d = res.shape
    n_first = SEQ // tm

    def pair_specs(width):
        return [pl.BlockSpec((tm, width), lambda i: (jnp.minimum(i, n_first - 1), 0)),
                pl.BlockSpec((tm, width), lambda i: (jnp.maximum(i - n_first, 0), 0))]
    extra = list(follow) if follow else []
    out_specs = [pl.BlockSpec((tm, d), lambda i: (i, 0))]
    out_shape = [jax.ShapeDtypeStruct((n, d), F32)]
    if follow:
        out_specs.append(pl.BlockSpec((tm, follow[1].shape[1]), lambda i: (i, 0)))
        out_shape.append(jax.ShapeDtypeStruct((n, follow[1].shape[1]), F32))
    out = pl.pallas_call(
        functools.partial(_matmul_res_kernel, n_lhs=len(lhs), n_first=n_first, follow=bool(follow)),
        grid=(n // tm,),
        in_specs=([pl.BlockSpec((tm, d), lambda i: (i, 0))]
                  + [spec for a, _ in lhs for spec in pair_specs(a.shape[1])]
                  + [pl.BlockSpec(w.shape, lambda i: (0, 0)) for w in ws + extra]),
        out_specs=out_specs,
        out_shape=out_shape,
        compiler_params=_cparams(("arbitrary",)),
        name=name,
    )(res, *[a for pair in lhs for a in pair], *ws, *extra)
    return out if follow else out[0]


def _rel_bucket_np(dist):
    n = np.maximum(dist, 0)
    exact = NUM_BUCKETS // 2
    nf = np.maximum(n, 1).astype(np.float32)
    large = exact + (np.log(nf / np.float32(exact)) / np.float32(math.log(MAX_DISTANCE / exact))
                     * np.float32(NUM_BUCKETS - exact)).astype(np.int32)
    return np.where(n < exact, n, np.minimum(large, NUM_BUCKETS - 1)).astype(np.int32)


def _bucket_or_masked(dist, valid):
    return np.where(valid, _rel_bucket_np(dist), -1).astype(np.int32)


def _prompt_bucket_table():
    i = np.arange(Q_TILE)[:, None]
    j = np.arange(Q_TILE)[None, :]
    u = np.arange(SLAB)[None, :]
    dist_c = i - CMP_STRIDE * u + (CMP_STRIDE * SLAB_OFF - (CMP_BLOCK - 1))
    diag = _bucket_or_masked(i - j, i - j >= 0)
    prev = _bucket_or_masked(Q_TILE + i - j, np.ones((Q_TILE, Q_TILE), bool))
    first = _bucket_or_masked(WINDOW + i - j, j > i)
    cmp_ = _bucket_or_masked(dist_c, dist_c >= 0)
    return np.concatenate([diag, prev, first, cmp_], axis=1).T


S_CMP = PAST_LEN // CMP_STRIDE
S_CMP_COLS = NSA_KV_HEADS * S_CMP
S_NEW_COLS = 128
S_SEL_COLS = NSA_KV_HEADS * PAST_LEN + S_NEW_COLS
S_WIN_COLS = NSA_KV_HEADS * WINDOW


def _sample_bucket_table():
    i = np.arange(DEC_SEQ)[:, None]
    pos = PAST_LEN + i
    rows = []
    for h in range(NSA_KV_HEADS):
        col = np.arange(S_CMP_COLS)[None, :]
        c = col % S_CMP
        dist_c = pos - (c * CMP_STRIDE + CMP_BLOCK - 1)
        cmp_ = _bucket_or_masked(dist_c, (col // S_CMP == h) & (c < S_CMP - 1) & (dist_c >= 0))
        col = np.arange(S_SEL_COLS)[None, :]
        key = col // NSA_KV_HEADS
        sel = _bucket_or_masked(pos - key, (col % NSA_KV_HEADS == h) & (key <= pos))
        col = np.arange(S_WIN_COLS)[None, :]
        dist_w = pos - (PAST_LEN - WINDOW + col // NSA_KV_HEADS)
        win = _bucket_or_masked(dist_w, (col % NSA_KV_HEADS == h) & (dist_w < WINDOW))
        rows.append(np.concatenate([cmp_, sel, win], axis=1))
    return np.concatenate(rows, axis=0)


def _bias_table_kernel(tab_ref, idx_ref, o_ref):
    h = pl.program_id(0)
    idx = idx_ref[...]
    out = jnp.full(idx.shape, NEG, F32)
    for b in range(NUM_BUCKETS):
        out = jnp.where(idx == b, tab_ref[b, h], out)
    o_ref[0] = out


def _bias_tables(rel_bias, idx, name):
    r, c = idx.shape
    return pl.pallas_call(
        _bias_table_kernel,
        grid=(NSA_HEADS,),
        in_specs=[pl.BlockSpec(memory_space=pltpu.SMEM),
                  pl.BlockSpec((r, c), lambda h: (0, 0))],
        out_specs=pl.BlockSpec((1, r, c), lambda h: (h, 0, 0)),
        out_shape=jax.ShapeDtypeStruct((NSA_HEADS, r, c), F32),
        compiler_params=_cparams(("arbitrary",)),
        name=name,
    )(rel_bias, jnp.asarray(idx))


def _cover_np(n_cmp_cols, n_blk_cols, delta_of):
    u = np.arange(n_cmp_cols)[:, None]
    j = np.arange(n_blk_cols)[None, :]
    delta = delta_of(u, j)
    shared = np.minimum(CMP_STRIDE * delta + CMP_BLOCK, SEL_BLOCK) - np.maximum(CMP_STRIDE * delta, 0)
    return (np.maximum(shared, 0) / CMP_STRIDE).astype(np.float32)


def _compress(load_rows, n_chunk, pos_term_ref, w1_ref, w2_ref):
    parts = []
    for s0 in range(0, CMP_STRIDE, CMP_GROUP):
        lhs = jnp.concatenate([load_rows(s0 + k).astype(BF16) for k in range(CMP_GROUP)], axis=1)
        w = w1_ref[s0:s0 + CMP_GROUP].reshape(CMP_GROUP * HEAD_DIM, 2 * HEAD_DIM)
        parts.append(_dot(lhs, w))
    while len(parts) > 1:
        parts = [a + b for a, b in zip(parts[0::2], parts[1::2])]
    acc = parts[0] + pos_term_ref[0:1, :]
    nxt = pltpu.roll(acc[:, HEAD_DIM:], n_chunk - 1, axis=0)
    hid = _silu(acc[:, :HEAD_DIM] + nxt)
    return _dot(hid.astype(BF16), w2_ref[...])


def _pos_term_kernel(pos_ref, w1_ref, o_ref):
    halves = []
    for half in range(2):
        acc = jnp.zeros((8, HEAD_DIM), F32)
        for s in range(CMP_STRIDE):
            row = half * CMP_STRIDE + s
            p = jnp.broadcast_to(pos_ref[0, row:row + 1, :], (8, HEAD_DIM)).astype(BF16)
            acc = acc + _dot(p, w1_ref[0, s][:, half * HEAD_DIM:(half + 1) * HEAD_DIM])
        halves.append(acc)
    o_ref[0] = jnp.concatenate(halves, axis=1)


def _pos_term(pos, w1cat):
    return pl.pallas_call(
        _pos_term_kernel,
        grid=(2,),
        in_specs=[pl.BlockSpec((1, CMP_BLOCK, HEAD_DIM), lambda i: (i, 0, 0)),
                  pl.BlockSpec((1, CMP_STRIDE, HEAD_DIM, 2 * HEAD_DIM), lambda i: (i, 0, 0, 0))],
        out_specs=pl.BlockSpec((1, 8, 2 * HEAD_DIM), lambda i: (i, 0, 0)),
        out_shape=jax.ShapeDtypeStruct((2, 8, 2 * HEAD_DIM), F32),
        compiler_params=_cparams(("arbitrary",)),
        name="pos_term",
    )(pos, w1cat)


def _flash_step(carry, s, v_t):
    outs = []
    w = s.shape[1] // FLASH_SPLIT
    v_ext = jnp.concatenate([v_t, jnp.ones((ONES_ROWS, v_t.shape[1]), BF16)], axis=0)
    for c in range(FLASH_SPLIT):
        m, acc = (x[:, c * w:(c + 1) * w] for x in carry)
        sc = s[:, c * w:(c + 1) * w]
        m_new = jnp.maximum(m, jnp.max(sc, axis=0, keepdims=True))
        acc = jnp.exp2(m - m_new) * acc + _dot(v_ext, jnp.exp2(sc - m_new).astype(BF16))
        outs.append((m_new, acc))
    return tuple(jnp.concatenate([o[i] for o in outs], axis=1) for i in range(2))


def _flash_init(cols):
    return (jnp.full((1, cols), M_INIT, F32), jnp.zeros((HEAD_DIM + ONES_ROWS, cols), F32))


def _flash_finish(carry):
    _, acc = carry
    return acc[:HEAD_DIM] / acc[HEAD_DIM:HEAD_DIM + 1]


def _masked_softmax(s, valid):
    s = jnp.where(valid, s, NEG)
    m = jnp.max(s, axis=-1, keepdims=True)
    e = jnp.where(valid, jnp.exp(s - m), 0.0)
    return e / jnp.maximum(jnp.sum(e, axis=-1, keepdims=True), 1e-30)


def _split_dot(x, w):
    hi = x.astype(BF16)
    lo = (x - hi.astype(F32)).astype(BF16)
    return _dot(hi, w) + _dot(lo, w)


def _top_n_mask(score, index, axis, interleave=()):
    sel = jnp.zeros(score.shape, F32)
    every = SEL_TOPN // (len(interleave) + 1)
    for r in range(SEL_TOPN):
        mx = jnp.max(score, axis=axis, keepdims=True)
        first = jnp.min(jnp.where(score == mx, index, 1e9), axis=axis, keepdims=True)
        hit = index == first
        sel = jnp.where(hit, 1.0, sel)
        score = jnp.where(hit, -jnp.inf, score)
        if (r + 1) % every == 0 and (r + 1) // every <= len(interleave):
            interleave[(r + 1) // every - 1]()
    return sel


def _top_n_mask_by_rank(score, n_cand):
    lane = lax.broadcasted_iota(jnp.int32, score.shape, 1)
    rank = jnp.zeros(score.shape, F32)
    for j in range(n_cand):
        col = score[:, j:j + 1]
        ahead = (col > score) | ((col == score) & (lane > j))
        rank = rank + ahead.astype(F32)
    return ((rank < SEL_TOPN) & (lane < n_cand)).astype(F32)


def _compress_prompt_kernel(rows_ref, pos_ref, w1_ref, w2_ref, o_ref):
    out = _compress(lambda s: rows_ref[pl.ds(s, N_CMP_PAD, stride=CMP_STRIDE), :], N_CMP_PAD,
                    pos_ref.at[0], w1_ref.at[0], w2_ref.at[0])
    real = lax.broadcasted_iota(jnp.int32, (N_CMP_PAD, HEAD_DIM), 0) < N_CMP_PAD - 1
    o_ref[0, 0, 0:SLAB_OFF, :] = jnp.zeros((SLAB_OFF, HEAD_DIM), F32)
    o_ref[0, 0, SLAB_OFF:SLAB_OFF + N_CMP_PAD, :] = jnp.where(real, out, 0.0)
    o_ref[0, 0, SLAB_OFF + N_CMP_PAD:KC_ROWS, :] = jnp.zeros((KC_ROWS - SLAB_OFF - N_CMP_PAD, HEAD_DIM), F32)


def _compress_prompt(z, pos, w1, w2):
    kv_blk = Z_KV // HEAD_DIM
    return pl.pallas_call(
        _compress_prompt_kernel,
        grid=(2, NSA_KV_HEADS),
        in_specs=[pl.BlockSpec((SEQ, HEAD_DIM), lambda i, h: (0, kv_blk + NSA_KV_HEADS * i + h)),
                  pl.BlockSpec((1, 8, 2 * HEAD_DIM), lambda i, h: (i, 0, 0)),
                  pl.BlockSpec((1, CMP_STRIDE, HEAD_DIM, 2 * HEAD_DIM), lambda i, h: (i, 0, 0, 0)),
                  pl.BlockSpec((1, HEAD_DIM, HEAD_DIM), lambda i, h: (i, 0, 0))],
        out_specs=pl.BlockSpec((1, 1, KC_ROWS, HEAD_DIM), lambda i, h: (i, h, 0, 0)),
        out_shape=jax.ShapeDtypeStruct((2, NSA_KV_HEADS, KC_ROWS, HEAD_DIM), F32),
        compiler_params=_cparams(("parallel", "parallel")),
        name="compress_prompt",
    )(z, pos, w1, w2)


def _nsa_prompt_kernel(q_ref, misc_ref, ksel_ref, vsel_rows_ref, kwin_ref, vwin_rows_ref, kc_ref, vc_ref,
                       bias_ref, cover_ref, erel_ref, o_ref, vsel_ref, vwin_ref):
    kvh = pl.program_id(0)
    t = pl.program_id(1)
    cols = NSA_GROUP * Q_TILE

    @pl.when(t == 0)
    def _():
        def body(kt, _):
            rows = pl.ds(pl.multiple_of(kt * Q_TILE, Q_TILE), Q_TILE)
            vsel_ref[:, rows] = vsel_rows_ref[rows, :].astype(F32).T.astype(BF16)
            vwin_ref[:, rows] = vwin_rows_ref[rows, :].astype(F32).T.astype(BF16)
            return 0
        lax.fori_loop(0, N_QT, body, 0)

    q = q_ref[...] * (HEAD_DIM ** -0.5 * LOG2E)
    qt =jnp.concatenate([q[:, g * HEAD_DIM:(g + 1) * HEAD_DIM].T for g in range(NSA_GROUP)], axis=1).astype(BF16)

    def bias_tile(k):
        return bias_ref[0, k * Q_TILE:(k + 1) * Q_TILE, :]
    b_diag, b_prev, b_first = bias_tile(0), bias_tile(1), bias_tile(2)

    def key_tile(ref, kt):
        return ref[pl.ds(pl.multiple_of(kt * Q_TILE, Q_TILE), Q_TILE), :]

    def value_tile(ref, kt):
        return ref[:, pl.ds(pl.multiple_of(kt * Q_TILE, Q_TILE), Q_TILE)]

    n_wt = WINDOW // Q_TILE + 1
    scores, values = [], []
    for w, b_tile in enumerate((b_first,) + (None,) * (n_wt - 3) + (b_prev, b_diag)):
        kt = t - (n_wt - 1) + w
        kc = jnp.maximum(kt, 0)
        s = _dot(key_tile(kwin_ref, kc), qt) + jnp.where(kt >= 0, 0.0, NEG)
        scores.append(s if b_tile is None else s + b_tile)
        values.append(value_tile(vwin_ref, kc))
    win_carry = [_flash_init(cols)]

    def win_part(lo):
        def run():
            win_carry[0] = _flash_step(win_carry[0], jnp.concatenate(scores[lo:lo + WIN_PART_TILES], axis=0),
                                       jnp.concatenate(values[lo:lo + WIN_PART_TILES], axis=1))
        return run
    win_parts = [win_part(lo) for lo in range(0, n_wt, WIN_PART_TILES)]

    start = pl.multiple_of(t * (Q_TILE // CMP_STRIDE), 8)
    kslab = kc_ref[0, 0, pl.ds(start, SLAB), :].astype(BF16)
    vslab = vc_ref[0, 0, pl.ds(start, SLAB), :].astype(BF16)
    b_cmp = bias_ref[0, 3 * Q_TILE:3 * Q_TILE + SLAB, :]
    u = lax.broadcasted_iota(jnp.int32, (SLAB, 1), 0)
    s = _dot(kslab, qt) + b_cmp + jnp.where(u >= SLAB_OFF - (Q_TILE // CMP_STRIDE) * t, 0.0, NEG)
    e = jnp.exp2(s - jnp.maximum(jnp.max(s, axis=0, keepdims=True), M_INIT))
    p_cmp = e / jnp.maximum(jnp.sum(e, axis=0, keepdims=True), 1e-30)
    o_cmp = _dot_tn(vslab, p_cmp.astype(BF16))

    p_sum = p_cmp[:, 0:Q_TILE]
    for g in range(1, NSA_GROUP):
        p_sum = p_sum + p_cmp[:, g * Q_TILE:(g + 1) * Q_TILE]
    hi = p_sum.astype(BF16)
    lo = (p_sum - hi.astype(F32)).astype(BF16)
    imp = _dot(cover_ref[...], hi) + _dot(cover_ref[...], lo)
    jr = lax.broadcasted_iota(jnp.int32, (N_SEL, Q_TILE), 0)
    qi = lax.broadcasted_iota(jnp.int32, (N_SEL, Q_TILE), 1)
    cur = REL0 + (qi >= SEL_BLOCK).astype(jnp.int32)
    first_blk = REL0 - 2 * t
    forced = (jr == first_blk) | (jr == cur) | (jr == cur - 1)
    in_range = (jr <= cur) & (jr >= first_blk)
    score = jnp.where(in_range, jnp.where(forced, FORCED_SCORE, imp), NEG)
    sel = _top_n_mask(score, jr.astype(F32), 0, win_parts)
    o_win = _flash_finish(win_carry[0])

    unsel = ((sel - 1.0) * (-NEG)).astype(BF16)
    q_aug = jnp.concatenate([qt, jnp.concatenate([unsel] * NSA_GROUP, axis=1)], axis=0)

    def sweep_step(i, carry, masked, biases=()):
        scores, values = [], []
        for j in range(SWEEP_TILES):
            kt = t - SWEEP_TILES * i - (SWEEP_TILES - 1) + j
            kc = jnp.maximum(kt, 0) if masked else kt
            tile = jnp.concatenate([key_tile(ksel_ref, kc), erel_ref[t - kc]], axis=1)
            s = _dot(tile, q_aug)
            if masked:
                s = s + jnp.where(kt >= 0, 0.0, NEG)
            if j >= SWEEP_TILES - len(biases):
                s = s + biases[j - (SWEEP_TILES - len(biases))]
            scores.append(s)
            values.append(value_tile(vsel_ref, kc))
        per = SWEEP_TILES // SWEEP_PARTS
        for k in range(SWEEP_PARTS):
            carry = _flash_step(carry, jnp.concatenate(scores[k * per:(k + 1) * per], axis=0),
                                jnp.concatenate(values[k * per:(k + 1) * per], axis=1))
        return carry

    n_full = jnp.maximum(t - (SWEEP_TILES - 1), 0) // SWEEP_TILES
    carry = lax.fori_loop(1, n_full + 1, lambda i, c: sweep_step(i, c, False), _flash_init(cols))
    some_left = (t - SWEEP_TILES * (n_full + 1) >= 0).astype(jnp.int32)
    carry = lax.fori_loop(0, some_left, lambda _, c: sweep_step(n_full + 1, c, True), carry)
    o_sel = _flash_finish(sweep_step(0, carry, True, (b_prev, b_diag)))

    gates = jax.nn.sigmoid(misc_ref[...]).T
    n_g = 3 * NSA_GROUP
    gk = jnp.where(kvh == 0, gates, pltpu.roll(gates, gates.shape[0] - n_g, axis=0))[MISC_GATES:MISC_GATES + n_g]
    for g in range(NSA_GROUP):
        c = slice(g * Q_TILE, (g + 1) * Q_TILE)
        o_g = (gk[3 * g:3 * g + 1] * o_cmp[:, c] + gk[3 * g + 1:3 * g + 2] * o_sel[:, c]
               + gk[3 * g + 2:3 * g + 3] * o_win[:, c])
        o_ref[:, g * HEAD_DIM:(g + 1) * HEAD_DIM] = o_g.T


def _nsa_prompt(z, kvb, kcp, bias, cover, erel):
    gw = NSA_GROUP * HEAD_DIM

    def k_spec(j):
        return pl.BlockSpec((SEQ, HEAD_DIM), lambda h, t, j=j: (0, 2 * j + h))
    return pl.pallas_call(
        _nsa_prompt_kernel,
        grid=(NSA_KV_HEADS, N_QT),
        in_specs=[pl.BlockSpec((Q_TILE, gw), lambda h, t: (t, Z_QN // gw + h)),
                  pl.BlockSpec((Q_TILE, 128), lambda h, t: (t, Z_MISC // 128)),
                  k_spec(2), k_spec(3), k_spec(4), k_spec(5),
                  pl.BlockSpec((1, 1, KC_ROWS, HEAD_DIM), lambda h, t: (0, h, 0, 0)),
                  pl.BlockSpec((1, 1, KC_ROWS, HEAD_DIM), lambda h, t: (1, h, 0, 0)),
                  pl.BlockSpec((1,) + bias.shape[1:], lambda h, t: (h, 0, 0)),
                  pl.BlockSpec(cover.shape, lambda h, t: (0, 0)),
                  pl.BlockSpec(erel.shape, lambda h, t: (0, 0, 0))],
        out_specs=pl.BlockSpec((Q_TILE, gw), lambda h, t: (t, h)),
        out_shape=jax.ShapeDtypeStruct((SEQ, NSA_HEADS * HEAD_DIM), F32),
        scratch_shapes=[pltpu.VMEM((HEAD_DIM, SEQ), BF16), pltpu.VMEM((HEAD_DIM, SEQ), BF16)],
        compiler_params=_cparams(("arbitrary", "arbitrary")),
        name="nsa_prompt",
    )(z, z, kvb, kvb, kvb, kvb, kcp, kcp, bias, cover, erel)


S_ROWS = NSA_HEADS * DEC_SEQ
PAGE_ROWS = NSA_KV_HEADS * PAGE_SIZE
NEW_ROWS = NSA_KV_HEADS * DEC_SEQ
CHUNK_ROWS = NSA_KV_HEADS * CMP_STRIDE
CHUNK_PITCH = CHUNK_ROWS + 8


def _nsa_sample_kernel(pt_ref, q_ref, gate_ref, new_ref, kwin_ref, vwin_ref, kcmp_hbm, vcmp_hbm, ksel_hbm, vsel_hbm,
                       posk_ref, w1k_ref, w2k_ref, posv_ref, w1v_ref, w2v_ref, bias_ref, cover_ref, expand_ref,
                       o_ref, kwin_o_ref, vwin_o_ref, kcmp_buf, vcmp_buf, ksel_buf, vsel_buf, sem):
    n_pg = N_PAGES
    b = pl.program_id(0)
    slot = b % 2
    page_chunks = PAGE_ROWS // CHUNK_ROWS
    hbm = (kcmp_hbm, vcmp_hbm, ksel_hbm, vsel_hbm)
    bufs = (kcmp_buf, vcmp_buf, ksel_buf, vsel_buf)

    def page_copy(k, p, page, into):
        if k < 2:
            src = hbm[k].at[pl.ds(page * page_chunks, page_chunks)]
            dst = bufs[k].at[pl.ds(into * S_CMP + p * page_chunks, page_chunks), pl.ds(0, CHUNK_ROWS), :]
        else:
            src = hbm[k].at[pl.ds(page * PAGE_ROWS, PAGE_ROWS)]
            dst = bufs[k].at[pl.ds(into * (n_pg * PAGE_ROWS) + p * PAGE_ROWS, PAGE_ROWS), :]
        return pltpu.make_async_copy(src, dst, sem.at[into, k])

    def gather(seq, into):
        for p in range(n_pg):
            page = pt_ref[seq, p]
            for k in range(4):
                page_copy(k, p, page, into).start()

    @pl.when(b == 0)
    def _():
        gather(0, 0)

    @pl.when(b + 1 < DEC_BATCH)
    def _():
        gather(b + 1, 1 - slot)

    for p in range(n_pg):
        for k in range(4):
            page_copy(k, p, 0, slot).wait()

    def pages(k, p):
        return bufs[k][pl.ds(pl.multiple_of(slot * (n_pg * PAGE_ROWS), PAGE_ROWS) + p * PAGE_ROWS, PAGE_ROWS), :]

    def compress_pool(k, pos_ref, w1_ref, w2_ref):
        flat = bufs[k].reshape(2 * S_CMP * CHUNK_PITCH, HEAD_DIM)

        def load_rows(s):
            return jnp.concatenate(
                [flat[pl.ds(slot * (S_CMP * CHUNK_PITCH) + NSA_KV_HEADS * s + h, S_CMP, stride=CHUNK_PITCH), :]
                 for h in range(NSA_KV_HEADS)], axis=0)
        return _compress(load_rows, S_CMP_COLS, pos_ref, w1_ref, w2_ref).astype(BF16)

    qs = (q_ref[0] * (HEAD_DIM ** -0.5)).astype(BF16)
    o0, o1 = S_CMP_COLS, S_CMP_COLS + S_SEL_COLS
    b_cmp = bias_ref[:, 0:o0]
    b_new = bias_ref[:, o1 - S_NEW_COLS:o1]

    def attend(scores, bias, values):
        s = jnp.concatenate(scores, axis=1) + bias
        e = jnp.exp(s - jnp.max(s, axis=-1, keepdims=True))
        acc = jnp.zeros((S_ROWS, HEAD_DIM), F32)
        c0 = 0
        for v in values:
            acc = acc + _dot(e[:, c0:c0 + v.shape[0]].astype(BF16), v)
            c0 += v.shape[0]
        return acc / jnp.sum(e, axis=-1, keepdims=True)

    def new_tile(j):
        pad = jnp.zeros((S_NEW_COLS - NEW_ROWS, HEAD_DIM), F32)
        return jnp.concatenate([new_ref[j], pad], axis=0).astype(BF16)

    p_cmp = _masked_softmax(_dot_nt(qs, compress_pool(0, posk_ref, w1k_ref, w2k_ref)) + b_cmp, b_cmp > M_INIT)
    comp_v = compress_pool(1, posv_ref, w1v_ref, w2v_ref)
    o_cmp = _dot(p_cmp.astype(BF16), comp_v)

    imp = _split_dot(p_cmp, cover_ref[...])
    imp = imp + pltpu.roll(imp, 8, axis=0) + pltpu.roll(imp, 16, axis=0) + pltpu.roll(imp, 24, axis=0)
    blk = lax.broadcasted_iota(jnp.int32, (S_ROWS, 128), 1)
    cur = PAST_LEN // SEL_BLOCK
    forced = (blk == 0) | (blk == cur) | (blk == cur - 1)
    score = jnp.where(blk <= cur, jnp.where(forced, FORCED_SCORE, imp), NEG)

    nk, nv = new_tile(4), new_tile(5)
    scores = [_dot_nt(qs, kwin_ref[...].astype(BF16)), _dot_nt(qs, nk)]
    bias_w = jnp.concatenate([bias_ref[:, o1:o1 + S_WIN_COLS], b_new], axis=1)
    o_win = attend(scores, bias_w, [vwin_ref[...].astype(BF16), nv])

    sel = _top_n_mask_by_rank(score, cur + 1).astype(BF16)
    mask_add = (_dot(sel, expand_ref[...]) - 1.0) * (-NEG)

    keep = NSA_KV_HEADS * WINDOW - NEW_ROWS
    kwin_o_ref[0:keep, :] = kwin_ref[NEW_ROWS:NSA_KV_HEADS * WINDOW, :]
    kwin_o_ref[keep:keep + NEW_ROWS, :] = new_ref[4]
    vwin_o_ref[0:keep, :] = vwin_ref[NEW_ROWS:NSA_KV_HEADS * WINDOW, :]
    vwin_o_ref[keep:keep + NEW_ROWS, :] = new_ref[5]

    nk, nv = new_tile(2), new_tile(3)
    scores = [_dot_nt(qs, pages(2, p).astype(BF16)) for p in range(n_pg)] + [_dot_nt(qs, nk)]
    values = [pages(3, p).astype(BF16) for p in range(n_pg)] + [nv]
    o_sel = attend(scores, bias_ref[:, o0:o1] + mask_add, values)

    g = jax.nn.sigmoid(gate_ref[0])
    o_ref[0] = g[:, 0:1] * o_cmp + g[:, 1:2] * o_sel + g[:, 2:3] * o_win


def _nsa_sample(page_table, q_s, gate_s, new_s, kwin, vwin, pools, cmp_w, bias, cover, expand):
    win_rows = NSA_KV_HEADS * WINDOW

    def full(a):
        return pl.BlockSpec(a.shape, lambda b, pt, n=a.ndim: (0,) * n)

    def per_b(a):
        return pl.BlockSpec((1,) + a.shape[1:], lambda b, pt, n=a.ndim: (b,) + (0,) * (n - 1))
    win_spec = pl.BlockSpec((win_rows, HEAD_DIM), lambda b, pt: (b, 0))
    consts = list(cmp_w) + [bias, cover, expand]
    grid_spec = pltpu.PrefetchScalarGridSpec(
        num_scalar_prefetch=1,
        grid=(DEC_BATCH,),
        in_specs=[per_b(q_s), per_b(gate_s),
                  pl.BlockSpec((N_KV_ARRAYS, NEW_ROWS, HEAD_DIM), lambda b, pt: (0, b, 0)), win_spec, win_spec]
        + [pl.BlockSpec(memory_space=pl.ANY)] * 4 + [full(a) for a in consts],
        out_specs=[per_b(q_s), win_spec, win_spec],
        scratch_shapes=[pltpu.VMEM((2 * S_CMP, CHUNK_PITCH, HEAD_DIM), F32)] * 2
        + [pltpu.VMEM((2 * N_PAGES * PAGE_ROWS, HEAD_DIM), F32)] * 2
        + [pltpu.SemaphoreType.DMA((2, 4))],
    )
    return pl.pallas_call(
        _nsa_sample_kernel,
        grid_spec=grid_spec,
        out_shape=[jax.ShapeDtypeStruct(q_s.shape, F32),
                   jax.ShapeDtypeStruct(kwin.shape, F32),
                   jax.ShapeDtypeStruct(vwin.shape, F32)],
        compiler_params=_cparams(("arbitrary",)),
        name="nsa_sample",
    )(page_table, q_s, gate_s, new_s, kwin, vwin, *pools, *consts)


def _log_decay(a_blk, wa_ref, ba_ref):
    x = _dot(a_blk.astype(BF16), wa_ref[...]) + ba_ref[...]
    return (jnp.minimum(x, 0.0) - jnp.log1p(jnp.exp(-jnp.abs(x)))) * (1.0 / GLA_TAU)


def _segment_cumsum(g, seg):
    pos = lax.broadcasted_iota(jnp.int32, g.shape, 0) % seg
    cum = g
    sh = 1
    while sh < seg:
        cum = cum + jnp.where(pos >= sh, pltpu.roll(cum, sh, axis=0), 0.0)
        sh *= 2
    return cum


def _gla_prompt_kernel(q_ref, k_ref, v_ref, r_ref, a_ref, wa_ref, ba_ref, gn_ref, o_ref, st_o_ref, st_ref,
                       *, n_blk, tb):
    tbi = pl.program_id(0)

    @pl.when(tbi == 0)
    def _():
        st_ref[...] = jnp.zeros_like(st_ref)

    c = GLA_CHUNK
    tril = lax.broadcasted_iota(jnp.int32, (c, c), 0) >= lax.broadcasted_iota(jnp.int32, (c, c), 1)
    cum = _segment_cumsum(_log_decay(a_ref[...], wa_ref, ba_ref), c)
    q = q_ref[...] * (GLA_DK ** -0.5)
    k = k_ref[...]
    v = v_ref[...].astype(BF16)
    qe = (q * jnp.exp(cum)).astype(BF16)
    kd = (k * jnp.exp(-cum)).astype(BF16)
    heads = [(slice(h * GLA_DK, (h + 1) * GLA_DK), slice(h * GLA_DV, (h + 1) * GLA_DV)) for h in range(GLA_HEADS)]
    sts = [st_ref[h] for h in range(GLA_HEADS)]
    outs = [[] for _ in range(GLA_HEADS)]
    for ci in range(tb // c):
        r = slice(ci * c, (ci + 1) * c)
        last = cum[ci * c + c - 1:ci * c + c, :]
        kl = (k[r] * jnp.exp(last - cum[r])).astype(BF16)
        decay = jnp.exp(last)
        for h, (dk, dv) in enumerate(heads):
            att = jnp.where(tril, _dot_nt(qe[r, dk], kd[r, dk]), 0.0)
            outs[h].append(_dot_nt(qe[r, dk], sts[h].astype(BF16)) + _dot(att.astype(BF16), v[r, dv]))
            sts[h] = decay[:, dk] * sts[h] + _dot_tn(v[r, dv], kl[:, dk])
    for h, (dk, dv) in enumerate(heads):
        st_ref[h] = sts[h]
        o = jnp.concatenate(outs[h], axis=0)
        o_ref[:, dv] = _rms(o, gn_ref[...]) * _silu(r_ref[:, dv])

    @pl.when(tbi == n_blk - 1)
    def _():
        st_o_ref[...] = st_ref[...]


def _gla_prompt(z, wa, ba, gn, tb=256):
    n_blk = SEQ // tb
    hk, hv = GLA_HEADS * GLA_DK, GLA_HEADS * GLA_DV
    st_shape = (GLA_HEADS, GLA_DV, GLA_DK)
    return pl.pallas_call(
        functools.partial(_gla_prompt_kernel, n_blk=n_blk, tb=tb),
        grid=(n_blk,),
        in_specs=[pl.BlockSpec((tb, hk), lambda i: (i, Z_QG // hk)),
                  pl.BlockSpec((tb, hk), lambda i: (i, Z_KG // hk)),
                  pl.BlockSpec((tb, hv), lambda i: (i, Z_VG // hv)),
                  pl.BlockSpec((tb, hv), lambda i: (i, Z_RG // hv)),
                  pl.BlockSpec((tb, 128), lambda i: (i, Z_MISC // 128)),
                  pl.BlockSpec((128, hk), lambda i: (0, 0)),
                  pl.BlockSpec((1, hk), lambda i: (0, 0)),
                  pl.BlockSpec((1, GLA_DV), lambda i: (0, 0))],
        out_specs=[pl.BlockSpec((tb, hv), lambda i: (i, 0)),
                   pl.BlockSpec(st_shape, lambda i: (0, 0, 0))],
        out_shape=[jax.ShapeDtypeStruct((SEQ, hv), F32), jax.ShapeDtypeStruct(st_shape, F32)],
        scratch_shapes=[pltpu.VMEM(st_shape, F32)],
        compiler_params=_cparams(("arbitrary",)),
        name="gla_prompt",
    )(z, z, z, z, z, wa, ba, gn)


GS_B = 4


def _gla_sample_kernel(q_ref, k_ref, v_ref, r_ref, a_ref, wa_ref, ba_ref, gn_ref, s_ref, o_ref, s_o_ref):
    rows = GS_B * DEC_SEQ
    a = a_ref[...]
    ri = lax.broadcasted_iota(jnp.int32, (rows, rows), 0)
    ci = lax.broadcasted_iota(jnp.int32, (rows, rows), 1)
    same_causal = (ri // DEC_SEQ == ci // DEC_SEQ) & (ri >= ci)
    row_b = lax.broadcasted_iota(jnp.int32, (rows, 1), 0) // DEC_SEQ
    ones = jnp.ones((rows, 128), BF16)
    for h in range(GLA_HEADS):
        dk = slice(h * GLA_DK, (h + 1) * GLA_DK)
        dv = slice(h * GLA_DV, (h + 1) * GLA_DV)
        g = _log_decay(a, wa_ref.at[:, dk], ba_ref.at[:, dk])
        cum = _segment_cumsum(g, DEC_SEQ)
        q = q_ref[:, dk] * (GLA_DK ** -0.5)
        k = k_ref[:, dk]
        v = v_ref[:, dv].astype(BF16)
        qe = (q * jnp.exp(cum)).astype(BF16)
        kd = (k * jnp.exp(-cum)).astype(BF16)
        att = jnp.where(same_causal, _dot_nt(qe, kd), 0.0)
        o = _dot(att.astype(BF16), v)
        for b in range(GS_B):
            mine = row_b == b
            last = cum[b * DEC_SEQ + DEC_SEQ - 1:(b + 1) * DEC_SEQ, :]
            s = s_ref[b, h]
            o = o + jnp.where(mine, _dot(qe, s.astype(BF16)), 0.0)
            kl = jnp.where(mine, k * jnp.exp(last - cum), 0.0)
            hi = jnp.where(mine, g, 0.0).astype(BF16)
            lo = (jnp.where(mine, g, 0.0) - hi.astype(F32)).astype(BF16)
            last_col = (_dot_tn(hi, ones) + _dot_tn(lo, ones))[:, 0:1]
            s_o_ref[b, h] = jnp.exp(last_col) * s + _dot_tn(kl.astype(BF16), v)
        o_ref[:, dv] = _rms(o, gn_ref[...]) * _silu(r_ref[:, dv])


def _gla_sample(zs, state, wa, ba, gn):
    rows = GS_B * DEC_SEQ
    n = DEC_BATCH * DEC_SEQ
    hk, hv = GLA_HEADS * GLA_DK, GLA_HEADS * GLA_DV
    st_spec = pl.BlockSpec((GS_B, GLA_HEADS, GLA_DK, GLA_DV), lambda i: (i, 0, 0, 0))
    return pl.pallas_call(
        _gla_sample_kernel,
        grid=(DEC_BATCH // GS_B,),
        in_specs=[pl.BlockSpec((rows, hk), lambda i: (i, Z_QG // hk)),
                  pl.BlockSpec((rows, hk), lambda i: (i, Z_KG // hk)),
                  pl.BlockSpec((rows, hv), lambda i: (i, Z_VG // hv)),
                  pl.BlockSpec((rows, hv), lambda i: (i, Z_RG // hv)),
                  pl.BlockSpec((rows, 128), lambda i: (i, Z_MISC // 128)),
                  pl.BlockSpec((128, hk), lambda i: (0, 0)),
                  pl.BlockSpec((1, hk), lambda i: (0, 0)),
                  pl.BlockSpec((1, GLA_DV), lambda i: (0, 0)),
                  st_spec],
        out_specs=[pl.BlockSpec((rows, hv), lambda i: (i, 0)), st_spec],
        out_shape=[jax.ShapeDtypeStruct((n, hv), F32), jax.ShapeDtypeStruct(state.shape, F32)],
        compiler_params=_cparams(("parallel",)),
        name="gla_sample",
    )(zs, zs, zs, zs, zs, wa, ba, gn, state)


def _softmax_rows(s):
    m = jnp.max(s, axis=-1, keepdims=True)
    e = jnp.exp(s - m)
    return e / jnp.sum(e, axis=-1, keepdims=True)


def _mem_prompt_kernel(q_ref, k_ref, v_ref, o_ref):
    for h in range(MEM_HEADS):
        d = slice(h * MEM_HEAD_DIM, (h + 1) * MEM_HEAD_DIM)
        q = (q_ref[:, d] * (MEM_HEAD_DIM ** -0.5)).astype(BF16)
        p = _softmax_rows(_dot_nt(q, k_ref[:, d].astype(BF16)))
        o_ref[:, d] = _dot(p.astype(BF16), v_ref[:, d].astype(BF16))


def _mem_prompt(qm, memkv, tq=256):
    w = MEM_HEADS * MEM_HEAD_DIM
    return pl.pallas_call(
        _mem_prompt_kernel,
        grid=(SEQ // tq,),
        in_specs=[pl.BlockSpec((tq, w), lambda i: (i, 0)),
                  pl.BlockSpec((MEM_TOKENS, w), lambda i: (0, 0)),
                  pl.BlockSpec((MEM_TOKENS, w), lambda i: (0, 1))],
        out_specs=pl.BlockSpec((tq, w), lambda i: (i, 0)),
        out_shape=jax.ShapeDtypeStruct((SEQ, w), F32),
        compiler_params=_cparams(("parallel",)),
        name="mem_prompt",
    )(qm, memkv, memkv)


def _mem_sample_kernel(q_ref, k_ref, v_ref, o_ref):
    rows = MEM_HEADS * DEC_SEQ
    cols = MEM_HEADS * MEM_TOKENS
    row_h = lax.broadcasted_iota(jnp.int32, (rows, cols), 0) // DEC_SEQ
    col_h = lax.broadcasted_iota(jnp.int32, (rows, cols), 1) % MEM_HEADS
    for b in range(MS_B):
        kv_rows = slice(b * cols, (b + 1) * cols)
        q = (q_ref[b] * (MEM_HEAD_DIM ** -0.5)).astype(BF16)
        s = jnp.where(row_h == col_h, _dot_nt(q, k_ref[kv_rows, :].astype(BF16)), NEG)
        o_ref[b] = _dot(_softmax_rows(s).astype(BF16), v_ref[kv_rows, :].astype(BF16))


MS_B = 4


def _mem_sample(q_s, k_mem, v_mem):
    rows = MEM_HEADS * DEC_SEQ
    kv_spec = pl.BlockSpec((MS_B * MEM_HEADS * MEM_TOKENS, MEM_HEAD_DIM), lambda b: (b, 0))
    return pl.pallas_call(
        _mem_sample_kernel,
        grid=(DEC_BATCH // MS_B,),
        in_specs=[pl.BlockSpec((MS_B, rows, MEM_HEAD_DIM), lambda b: (b, 0, 0)), kv_spec, kv_spec],
        out_specs=pl.BlockSpec((MS_B, rows, MEM_HEAD_DIM), lambda b: (b, 0, 0)),
        out_shape=jax.ShapeDtypeStruct((DEC_BATCH, rows, MEM_HEAD_DIM), F32),
        compiler_params=_cparams(("parallel",)),
        name="mem_sample",
    )(q_s, k_mem, v_mem)


W_IN_SEGMENTS = ((0, 1024, Z_QN), (1024, 1536, Z_KV), (2560, 24, Z_MISC + MISC_GATES), (2584, 512, Z_QG),
                 (3096, 512, Z_KG), (3608, 1024, Z_VG), (4632, 1024, Z_RG), (5656, 16, Z_MISC + MISC_A))
W_IN_COLS = 5672


def _permute_w_in_kernel(wt_ref, o_ref):
    tc = wt_ref.shape[1]

    def piece(row):
        return wt_ref[row:row + 128, :].T

    for src, width, dst in W_IN_SEGMENTS:
        for c in range(0, width - width % 128, 128):
            o_ref[:, dst + c:dst + c + 128] = piece(src + c).astype(BF16)
    (g_src, g_w, _), (a_src, a_w, _) = [seg for seg in W_IN_SEGMENTS if seg[1] % 128]
    lane = lax.broadcasted_iota(jnp.int32, (tc, 128), 1)
    gates = piece(g_src)
    low_rank = pltpu.roll(piece(a_src + a_w - 128), MISC_A + a_w, axis=1)
    misc = jnp.where(lane < MISC_GATES + g_w, gates, jnp.where(lane < MISC_A + a_w, low_rank, 0.0))
    o_ref[:, Z_MISC:Z_W] = misc.astype(BF16)


def _permute_w_in(w_in_t, tc=256):
    d = w_in_t.shape[1]
    return pl.pallas_call(
        _permute_w_in_kernel,
        grid=(d // tc,),
        in_specs=[pl.BlockSpec((W_IN_COLS, tc), lambda i: (0, i))],
        out_specs=pl.BlockSpec((tc, Z_W), lambda i: (i, 0)),
        out_shape=jax.ShapeDtypeStruct((d, Z_W), BF16),
        compiler_params=_cparams(("parallel",)),
        name="permute_w_in",
    )(w_in_t)


def kernel(x_prompt, x_sample, mem_prompt, cache_k_cmp, cache_v_cmp, cache_k_sel, cache_v_sel, cache_k_win,
           cache_v_win, state_gla, cache_k_mem, cache_v_mem, page_table, norm_ffn1, ffn1_w_gate, ffn1_w_up,
           ffn1_w_down, norm_mix, w_in, w_out, cmp_pos_k, cmp_w1_k, cmp_w2_k, cmp_pos_v, cmp_w1_v, cmp_w2_v,
           rel_bias, gla_w_a2, gla_b_a, gla_norm, norm_mem, norm_mem_src, w_mem_q, w_mem_k, w_mem_v, w_mem_o,
           norm_ffn2, ffn2_w_gate, ffn2_w_up, ffn2_w_down, norm_final):
    bf = lambda a: a.astype(BF16)
    row = lambda a: a.reshape(1, -1)
    nb, ns = DEC_BATCH, DEC_SEQ
    kvw = NSA_KV_HEADS * HEAD_DIM

    h1, h1n = _ffn([x_prompt[0], x_sample.reshape(nb * ns, D_MODEL)], row(norm_ffn1[0]), bf(ffn1_w_gate[0]),
                   bf(ffn1_w_up[0]), bf(ffn1_w_down[0]), row(norm_mix[0]), False)
    z, kvb, new_s, *new_p = _proj_in(h1n, _permute_w_in(w_in[0].T))
    rows_p = [a.reshape(1, 1, SEQ, NSA_KV_HEADS, HEAD_DIM) for a in new_p]
    rows_s = [new_s[j].reshape(1, nb, ns, NSA_KV_HEADS, HEAD_DIM) for j in range(4)]

    tab_p = _bias_tables(rel_bias, _prompt_bucket_table(), "bias_prompt")
    far = rel_bias[NUM_BUCKETS - 1][:, None, None]
    near = tab_p[:, :3 * Q_TILE]
    tab_p = jnp.concatenate([jnp.where(near > M_INIT, near - far, NEG), tab_p[:, 3 * Q_TILE:]], axis=1)
    tab_p = jnp.where(tab_p > M_INIT, tab_p * LOG2E, NEG)
    tab_p = tab_p.reshape(NSA_KV_HEADS, NSA_GROUP, -1, Q_TILE).transpose(0, 2, 1, 3)
    tab_p = tab_p.reshape(NSA_KV_HEADS, -1, NSA_GROUP * Q_TILE)
    tab_s = _bias_tables(rel_bias, _sample_bucket_table(), "bias_sample")
    tab_s = tab_s.reshape(NSA_KV_HEADS, NSA_GROUP, NSA_KV_HEADS, ns, -1)
    tab_s = jnp.stack([tab_s[h, :, h] for h in range(NSA_KV_HEADS)], axis=1).reshape(S_ROWS, -1)

    cmp_w1 = bf(jnp.stack([cmp_w1_k[0], cmp_w1_v[0]]))
    cmp_w1 = jnp.concatenate([cmp_w1[:, :CMP_STRIDE], cmp_w1[:, CMP_STRIDE:]], axis=-1)
    cmp_w2 = bf(jnp.stack([cmp_w2_k[0], cmp_w2_v[0]]))
    cmp_pos = _pos_term(jnp.stack([cmp_pos_k[0], cmp_pos_v[0]]), cmp_w1)
    kcp = _compress_prompt(z, cmp_pos, cmp_w1, cmp_w2)
    cover_p = jnp.asarray(_cover_np(SLAB, N_SEL, lambda u, j: u - 4 * j - SLAB_OFF + 4 * REL0).T, BF16)
    erel = (np.arange(N_SEL)[None, None, :] == REL0 - 2 * np.arange(N_QT)[:, None, None]
            + (np.arange(Q_TILE)[None, :, None] >= SEL_BLOCK))
    o_nsa_p = _nsa_prompt(z, kvb, kcp, tab_p, cover_p, jnp.asarray(erel, BF16))

    zs = z[SEQ:]
    q_s = zs[:, Z_QN:Z_QN + NSA_HEADS * HEAD_DIM].reshape(nb, ns, NSA_KV_HEADS, NSA_GROUP, HEAD_DIM)
    q_s = q_s.transpose(0, 3, 2, 1, 4).reshape(nb, S_ROWS, HEAD_DIM)
    gate_s = zs[:, Z_MISC + MISC_GATES:Z_MISC + MISC_GATES + 3 * NSA_HEADS]
    gate_s = gate_s.reshape(nb, ns, NSA_KV_HEADS, NSA_GROUP, 3).transpose(0, 3, 2, 1, 4).reshape(nb, S_ROWS, 3)
    as_rows = lambda c: c.reshape(-1, HEAD_DIM)
    pools = ([c.reshape(-1, CHUNK_ROWS, HEAD_DIM) for c in (cache_k_cmp, cache_v_cmp)]
             + [as_rows(c) for c in (cache_k_sel, cache_v_sel)])
    cover_s = _cover_np(S_CMP, 128, lambda c, j: c - 4 * j)
    cover_s = jnp.asarray(np.concatenate([cover_s] * NSA_KV_HEADS, axis=0), BF16)
    expand_s = jnp.asarray(np.arange(128)[:, None]
                           == (np.arange(S_SEL_COLS)[None, :] // (NSA_KV_HEADS * SEL_BLOCK)), BF16)
    o_nsa_s, kwin_s, vwin_s = _nsa_sample(
        page_table, q_s, gate_s, new_s, as_rows(cache_k_win), as_rows(cache_v_win), pools,
        (cmp_pos[0], cmp_w1[0], cmp_w2[0], cmp_pos[1], cmp_w1[1], cmp_w2[1]),
        tab_s, cover_s, expand_s)
    o_nsa_s = o_nsa_s.reshape(nb, NSA_GROUP, NSA_KV_HEADS, ns, HEAD_DIM).transpose(0, 3, 2, 1, 4)
    o_nsa_s = o_nsa_s.reshape(nb * ns, NSA_HEADS * HEAD_DIM)

    wa = bf(jnp.zeros((128, GLA_HEADS * GLA_DK), F32).at[MISC_A:MISC_A + GLA_RANK].set(gla_w_a2[0]))
    ba, gn = row(gla_b_a[0]), row(gla_norm[0])
    o_gla_p, st_p = _gla_prompt(z, wa, ba, gn)
    o_gla_s, st_s = _gla_sample(zs, state_gla[0], wa, ba, gn)

    half = NSA_HEADS * HEAD_DIM
    h2, qm = _matmul_res(h1, [(o_nsa_p, o_nsa_s), (o_gla_p, o_gla_s)], [bf(w_out[0][:half]), bf(w_out[0][half:])],
                         512, "proj_out", follow=(row(norm_mem[0]), bf(w_mem_q[0])))

    memkv = _norm_matmul(mem_prompt[0], row(norm_mem_src[0]),
                         bf(jnp.concatenate([w_mem_k[0], w_mem_v[0]], axis=1)), MEM_TOKENS, 512, "mem_kv")
    mw = MEM_HEADS * MEM_HEAD_DIM
    om_p = _mem_prompt(qm, memkv)
    qm_s = qm[SEQ:].reshape(nb, ns, MEM_HEADS, MEM_HEAD_DIM).transpose(0, 2, 1, 3)
    om_s = _mem_sample(qm_s.reshape(nb, MEM_HEADS * ns, MEM_HEAD_DIM),
                       as_rows(cache_k_mem), as_rows(cache_v_mem))
    om_s = om_s.reshape(nb, MEM_HEADS, ns, MEM_HEAD_DIM).transpose(0, 2, 1, 3).reshape(nb * ns, mw)
    h3 = _matmul_res(h2, [(om_p, om_s)], [bf(w_mem_o[0])], 512, "mem_out")

    y_p, y_s = _ffn([h3], row(norm_ffn2[0]), bf(ffn2_w_gate[0]), bf(ffn2_w_up[0]), bf(ffn2_w_down[0]),
                    row(norm_final), True)

    mem_shape = (1, 1, MEM_TOKENS, MEM_HEADS, MEM_HEAD_DIM)
    win_shape = (1, nb, WINDOW, NSA_KV_HEADS, HEAD_DIM)
    return (y_p.reshape(1, SEQ, D_MODEL), y_s.reshape(nb, ns, D_MODEL),
            rows_p[0], rows_p[1], rows_p[2], rows_p[3],
            rows_p[4][:, :, SEQ - WINDOW:], rows_p[5][:, :, SEQ - WINDOW:],
            st_p.transpose(0, 2, 1).reshape(1, 1, GLA_HEADS, GLA_DK, GLA_DV),
            memkv[:, :mw].reshape(mem_shape), memkv[:, mw:].reshape(mem_shape),
            rows_s[0], rows_s[1], rows_s[2], rows_s[3],
            kwin_s.reshape(win_shape), vwin_s.reshape(win_shape),
            st_s.reshape(1, nb, GLA_HEADS, GLA_DK, GLA_DV))
```
